```python
import math
import jax, jax.numpy as jnp
from jax import lax
import numpy as np

D_MODEL = 2048
BATCH = 2
SEQ = 8192
DEPTH = 1

N_META = 16
D_MIX = D_MODEL
D_LRU = D_MIX // 2
D_ATT = D_MIX - D_LRU
LRU_BLOCKS = 16
LRU_BLOCK_DIM = D_LRU // LRU_BLOCKS
CONV_WIDTH = 4
LRU_C = 8.0
N_HEADS = 8
V_HEAD_DIM = D_ATT // N_HEADS
QK_HEAD_DIM = V_HEAD_DIM // 2
N_BUCKETS = 32
MAX_DISTANCE = 128
Q_BLOCK = 128
N_GROUPS = 4
EXPERTS_PER_GROUP = 8
N_EXPERTS = N_GROUPS * EXPERTS_PER_GROUP
TOP_K = 2
D_EXPERT = D_MODEL // 2
DISPATCH_BLOCK = 128
D_IN_PROJ = 2 * D_LRU + 3 * D_ATT
EPS = 1e-6

kernel_name = "hymba_rglru_diffattn_hiermoe"


def rmsnorm(x, g):
    xf = x.astype(jnp.float32)
    y = xf * lax.rsqrt(jnp.mean(xf * xf, axis=-1, keepdims=True) + EPS)
    return (y * g.astype(jnp.float32)).astype(x.dtype)


def causal_depthwise_conv(x, w, b):
    T = x.shape[1]
    xp = jnp.pad(x, ((0, 0), (CONV_WIDTH - 1, 0), (0, 0)))
    return sum((xp[:, k:k + T] * w[k] for k in range(CONV_WIDTH)), b)


def rglru(x, w_r, b_r, w_i, b_i, lam):
    B, T, C = x.shape
    xb = x.reshape(B, T, LRU_BLOCKS, LRU_BLOCK_DIM)
    r = jax.nn.sigmoid(jnp.einsum('bthi,hij->bthj', xb, w_r) + b_r).reshape(B, T, C)
    i = jax.nn.sigmoid(jnp.einsum('bthi,hij->bthj', xb, w_i) + b_i).reshape(B, T, C)
    log_a = (LRU_C * r.astype(jnp.float32)) * jax.nn.log_sigmoid(lam.astype(jnp.float32))
    a = jnp.exp(log_a)
    bx = jnp.sqrt(-jnp.expm1(2.0 * log_a)) * (i * x).astype(jnp.float32)

    def combine(left, right):
        a_l, b_l = left
        a_r, b_r2 = right
        return a_l * a_r, a_r * b_l + b_r2

    _, h = lax.associative_scan(combine, (a, bx), axis=1)
    return h.astype(x.dtype)


def t5_bucket(rel):
    n = jnp.maximum(rel, 0)
    max_exact = N_BUCKETS // 2
    nf = jnp.maximum(n, 1).astype(jnp.float32)
    large = max_exact + (jnp.log(nf / max_exact) / math.log(MAX_DISTANCE / max_exact)
                         * (N_BUCKETS - max_exact)).astype(jnp.int32)
    large = jnp.minimum(large, N_BUCKETS - 1)
    return jnp.where(n < max_exact, n, large)


def diff_attention(q, k, v, rel_bias, lq1, lk1, lq2, lk2, subln_g, lambda_init):
    B, T, _ = q.shape
    T_pad = -(-T // Q_BLOCK) * Q_BLOCK
    n_blk = T_pad // Q_BLOCK
    pad = lambda t: jnp.pad(t, ((0, 0), (0, T_pad - T), (0, 0)))
    q = pad(q).reshape(B, T_pad, N_HEADS, 2, QK_HEAD_DIM).transpose(3, 0, 2, 1, 4)
    k = pad(k).reshape(B, T_pad, N_HEADS, 2, QK_HEAD_DIM).transpose(3, 0, 2, 1, 4)
    v = pad(v).reshape(B, T_pad, N_HEADS, V_HEAD_DIM).transpose(0, 2, 1, 3)
    q_blocks = q.reshape(2, B, N_HEADS, n_blk, Q_BLOCK, QK_HEAD_DIM).transpose(3, 0, 1, 2, 4, 5)
    lam = (jnp.exp(jnp.sum(lq1.astype(jnp.float32) * lk1.astype(jnp.float32)))
           - jnp.exp(jnp.sum(lq2.astype(jnp.float32) * lk2.astype(jnp.float32))) + lambda_init)
    k_pos = jnp.arange(T_pad, dtype=jnp.int32)
    scale = QK_HEAD_DIM ** -0.5

    def one_block(args):
        qb, blk = args
        q_pos = blk * Q_BLOCK + jnp.arange(Q_BLOCK, dtype=jnp.int32)
        rel = q_pos[:, None] - k_pos[None, :]
        bias = rel_bias[t5_bucket(rel)].astype(jnp.float32).transpose(2, 0, 1)
        s = jnp.einsum('mbhqd,mbhkd->mbhqk', qb, k).astype(jnp.float32) * scale + bias
        s = jnp.where(rel >= 0, s, -jnp.inf)
        p = jax.nn.softmax(s, axis=-1)
        w = p[0] - lam * p[1]
        return jnp.einsum('bhqk,bhkd->bhqd', w.astype(v.dtype), v)

    o = lax.map(one_block, (q_blocks, jnp.arange(n_blk, dtype=jnp.int32)))
    o = o.transpose(1, 2, 0, 3, 4).reshape(B, N_HEADS, T_pad, V_HEAD_DIM)[:, :, :T]
    o = rmsnorm(o, subln_g) * (1.0 - lambda_init)
    return o.transpose(0, 2, 1, 3).reshape(B, T, D_ATT)


def hier_moe(u, w_group, b_group, w_router, b_router, w_gate, w_up, w_down):
    B, T, D = u.shape
    n_tok = B * T
    xt = u.reshape(n_tok, D)
    g_logits = (xt @ w_group).astype(jnp.float32) + b_group.astype(jnp.float32)
    g_prob = jax.nn.softmax(g_logits, axis=-1)
    grp = jnp.argmax(g_logits, axis=-1).astype(jnp.int32)
    p_grp = jnp.take_along_axis(g_prob, grp[:, None], axis=-1)
    e_logits = ((xt @ w_router).astype(jnp.float32) + b_router.astype(jnp.float32)).reshape(
        n_tok, N_GROUPS, EXPERTS_PER_GROUP)
    e_in = jnp.take_along_axis(e_logits, grp[:, None, None], axis=1)[:, 0]
    top_v, top_i = lax.top_k(e_in, TOP_K)
    gates = (p_grp * jax.nn.softmax(top_v, axis=-1)).reshape(-1)
    expert = (grp[:, None] * EXPERTS_PER_GROUP + top_i.astype(jnp.int32)).reshape(-1)

    n_assign = n_tok * TOP_K
    tok = jnp.repeat(jnp.arange(n_tok, dtype=jnp.int32), TOP_K)
    order = jnp.argsort(expert)
    e_s, tok_s, gate_s = expert[order], tok[order], gates[order]
    counts = jax.ops.segment_sum(jnp.ones_like(expert), expert, num_segments=N_EXPERTS)
    starts = jnp.cumsum(counts) - counts
    padded = (counts + DISPATCH_BLOCK - 1) // DISPATCH_BLOCK * DISPATCH_BLOCK
    pends = jnp.cumsum(padded)
    pstarts = pends - padded
    dest = pstarts[e_s] + jnp.arange(n_assign, dtype=jnp.int32) - starts[e_s]
    n_blocks = -(-(n_assign + N_EXPERTS * (DISPATCH_BLOCK - 1)) // DISPATCH_BLOCK)
    rows = n_blocks * DISPATCH_BLOCK
    buf_tok = jnp.full((rows,), n_tok, dtype=jnp.int32).at[dest].set(tok_s)
    x_buf = jnp.concatenate([xt, jnp.zeros((1, D), xt.dtype)], axis=0)[buf_tok]
    x_buf = x_buf.reshape(n_blocks, DISPATCH_BLOCK, D)
    blk_start = jnp.arange(n_blocks, dtype=jnp.int32) * DISPATCH_BLOCK
    blk_expert = jnp.minimum(jnp.searchsorted(pends, blk_start, side='right'), N_EXPERTS - 1)

    def expert_block(args):
        xb, e = args
        return (jax.nn.silu(xb @ w_gate[e]) * (xb @ w_up[e])) @ w_down[e]

    y_buf = lax.map(expert_block, (x_buf, blk_expert)).reshape(rows, D)
    y_assign = y_buf[dest] * gate_s[:, None].astype(u.dtype)
    y = jax.ops.segment_sum(y_assign, tok_s, num_segments=n_tok)
    return y.reshape(B, T, D)


def setup_inputs(seed: int = 0) -> dict:
    key = jax.random.key(seed)
    ks = jax.random.split(key, 27)
    f32 = jnp.float32
    nrm = lambda k, shape, s: jax.random.normal(k, shape, f32) * s
    a0 = jax.random.uniform(ks[10], (DEPTH, D_LRU), f32, 0.9, 0.999)
    sig = a0 ** (1.0 / LRU_C)
    lru_L = jnp.log(sig) - jnp.log1p(-sig)
    return {
        "x": nrm(ks[0], (BATCH, SEQ, D_MODEL), 1.0),
        "meta_tokens": nrm(ks[1], (N_META, D_MODEL), 1.0),
        "g_mix": 1.0 + nrm(ks[2], (DEPTH, D_MODEL), 0.01),
        "w_in": nrm(ks[3], (DEPTH, D_MODEL, D_IN_PROJ), D_MODEL ** -0.5),
        "conv_w": nrm(ks[4], (DEPTH, CONV_WIDTH, D_LRU), CONV_WIDTH ** -0.5),
        "conv_b": nrm(ks[5], (DEPTH, D_LRU), 0.01),
        "w_rgate": nrm(ks[6], (DEPTH, LRU_BLOCKS, LRU_BLOCK_DIM, LRU_BLOCK_DIM), LRU_BLOCK_DIM ** -0.5),
        "b_rgate": nrm(ks[7], (DEPTH, LRU_BLOCKS, LRU_BLOCK_DIM), 0.01),
        "w_igate": nrm(ks[8], (DEPTH, LRU_BLOCKS, LRU_BLOCK_DIM, LRU_BLOCK_DIM), LRU_BLOCK_DIM ** -0.5),
        "b_igate": nrm(ks[9], (DEPTH, LRU_BLOCKS, LRU_BLOCK_DIM), 0.01),
        "lru_L": lru_L,
        "lambda_q1": nrm(ks[11], (DEPTH, QK_HEAD_DIM), 0.1),
        "lambda_k1": nrm(ks[12], (DEPTH, QK_HEAD_DIM), 0.1),
        "lambda_q2": nrm(ks[13], (DEPTH, QK_HEAD_DIM), 0.1),
        "lambda_k2": nrm(ks[14], (DEPTH, QK_HEAD_DIM), 0.1),
        "subln_g": 1.0 + nrm(ks[15], (DEPTH, V_HEAD_DIM), 0.01),
        "rel_bias": nrm(ks[16], (N_BUCKETS, N_HEADS), 0.5),
        "w_out": nrm(ks[17], (DEPTH, D_MIX, D_MODEL), D_MIX ** -0.5),
        "g_ffn": 1.0 + nrm(ks[18], (DEPTH, D_MODEL), 0.01),
        "w_group": nrm(ks[19], (DEPTH, D_MODEL, N_GROUPS), D_MODEL ** -0.5),
        "b_group": nrm(ks[20], (DEPTH, N_GROUPS), 0.01),
        "w_router": nrm(ks[21], (DEPTH, D_MODEL, N_EXPERTS), D_MODEL ** -0.5),
        "b_router": nrm(ks[22], (DEPTH, N_EXPERTS), 0.01),
        "w_gate": nrm(ks[23], (DEPTH, N_EXPERTS, D_MODEL, D_EXPERT), D_MODEL ** -0.5),
        "w_up": nrm(ks[24], (DEPTH, N_EXPERTS, D_MODEL, D_EXPERT), D_MODEL ** -0.5),
        "w_down": nrm(ks[25], (DEPTH, N_EXPERTS, D_EXPERT, D_MODEL), D_EXPERT ** -0.5),
        "g_final": 1.0 + nrm(ks[26], (D_MODEL,), 0.01),
    }


def reference(x, meta_tokens, g_mix, w_in, conv_w, conv_b, w_rgate, b_rgate, w_igate, b_igate,
              lru_L, lambda_q1, lambda_k1, lambda_q2, lambda_k2, subln_g, rel_bias, w_out,
              g_ffn, w_group, b_group, w_router, b_router, w_gate, w_up, w_down, g_final):
    B = x.shape[0]
    meta = jnp.broadcast_to(meta_tokens[None].astype(x.dtype), (B, N_META, D_MODEL))
    h = jnp.concatenate([meta, x], axis=1)
    splits = [D_LRU, 2 * D_LRU, 2 * D_LRU + D_ATT, 2 * D_LRU + 2 * D_ATT]
    for l in range(DEPTH):
        lambda_init = 0.8 - 0.6 * math.exp(-0.3 * l)
        u = rmsnorm(h, g_mix[l])
        proj = u @ w_in[l]
        x_lru, g_lru, q, k, v = jnp.split(proj, splits, axis=-1)
        y_lru = rglru(causal_depthwise_conv(x_lru, conv_w[l], conv_b[l]),
                      w_rgate[l], b_rgate[l], w_igate[l], b_igate[l], lru_L[l]) * jax.nn.gelu(g_lru)
        y_att = diff_attention(q, k, v, rel_bias, lambda_q1[l], lambda_k1[l], lambda_q2[l],
                               lambda_k2[l], subln_g[l], lambda_init)
        h = h + jnp.concatenate([y_lru, y_att], axis=-1) @ w_out[l]
        h = h + hier_moe(rmsnorm(h, g_ffn[l]), w_group[l], b_group[l], w_router[l], b_router[l],
                         w_gate[l], w_up[l], w_down[l])
    h = rmsnorm(h, g_final)
    return h[:, N_META:]
```

```python
import functools
import math

import numpy as np
import jax
import jax.numpy as jnp
from jax import lax
from jax.experimental import pallas as pl
from jax.experimental.pallas import tpu as pltpu

D_MODEL = 2048
N_META = 16
D_LRU = 1024
D_ATT = 1024
LRU_BLOCKS = 16
LRU_BLOCK_DIM = 64
CONV_WIDTH = 4
LRU_C = 8.0
N_HEADS = 8
V_HEAD_DIM = 128
QK_HEAD_DIM = 64
N_BUCKETS = 32
MAX_DISTANCE = 128
N_GROUPS = 4
EXPERTS_PER_GROUP = 8
N_EXPERTS = 32
D_EXPERT = 1024
D_IN_PROJ = 2 * D_LRU + 3 * D_ATT
EPS = 1e-6
LAMBDA_INIT = 0.8 - 0.6 * math.exp(-0.3 * 0)

F32 = jnp.float32
BF16 = jnp.bfloat16

LANES = 128
MXU_DIM = 256
TT = 768
PROJ_TN = 1024
OUT_TM = TT // 2
LRU_CB = MXU_DIM
DISPATCH_BLK = 256
VMEM_LIMIT = 56 * 1024 * 1024
NEG_BIG = -1e30


def _round_up(a, b):
    return -(-a // b) * b


def _rmsnorm(x, g):
    ms = jnp.mean(x * x, axis=-1, keepdims=True)
    return x * lax.rsqrt(ms + EPS) * g


def _inproj_kernel(x_ref, g_ref, w_ref, lru_ref, qkv_ref, u_scr):
    n = pl.program_id(1)

    @pl.when(n == 0)
    def _():
        u_scr[...] = _rmsnorm(x_ref[...], g_ref[...]).astype(BF16)

    y = jnp.dot(u_scr[...], w_ref[...], preferred_element_type=F32)

    @pl.when(n < 2)
    def _():
        lru_ref[...] = y

    @pl.when(n >= 2)
    def _():
        qkv_ref[...] = y.astype(BF16)


def _in_proj(h0, g_mix, w_in_bf16):
    n_pad = h0.shape[0]
    n_col = D_IN_PROJ // PROJ_TN
    return pl.pallas_call(
        _inproj_kernel,
        grid=(n_pad // TT, n_col),
        in_specs=[
            pl.BlockSpec((TT, D_MODEL), lambda m, n: (m, 0)),
            pl.BlockSpec((1, D_MODEL), lambda m, n: (0, 0)),
            pl.BlockSpec((D_MODEL, PROJ_TN), lambda m, n: (0, n)),
        ],
        out_specs=[
            pl.BlockSpec((TT, PROJ_TN), lambda m, n: (m, jnp.minimum(n, 1))),
            pl.BlockSpec((TT, PROJ_TN), lambda m, n: (m, jnp.maximum(n - 2, 0))),
        ],
        out_shape=[
            jax.ShapeDtypeStruct((n_pad, 2 * D_LRU), F32),
            jax.ShapeDtypeStruct((n_pad, 3 * D_ATT), BF16),
        ],
        scratch_shapes=[pltpu.VMEM((TT, D_MODEL), BF16)],
        compiler_params=pltpu.CompilerParams(
            dimension_semantics=("arbitrary", "arbitrary"), vmem_limit_bytes=VMEM_LIMIT),
        name="in_proj",
    )(h0, g_mix, w_in_bf16)


def _shift_rows(x, tail, s):
    r = pltpu.roll(x, s, 0)
    row8 = lax.broadcasted_iota(jnp.int32, tail.shape, 0)
    head = jnp.where(row8 < s, pltpu.roll(tail, s, 0), r[0:8])
    return jnp.concatenate([head, r[8:]], axis=0)


def _lru_kernel(x_ref, gate_ref, cw_ref, cb_ref, wr_ref, br_ref, wi_ref, bi_ref, lam_ref,
                y_ref, tail_scr, h_scr):
    t = pl.program_id(2)

    @pl.when(t == 0)
    def _():
        tail_scr[...] = jnp.zeros_like(tail_scr)
        h_scr[...] = jnp.zeros_like(h_scr)

    x = x_ref[...]
    tail = tail_scr[...]
    cw = cw_ref[...]
    xc = cb_ref[...] + x * cw[CONV_WIDTH - 1:CONV_WIDTH]
    for k in range(CONV_WIDTH - 1):
        xc = xc + _shift_rows(x, tail, CONV_WIDTH - 1 - k) * cw[k:k + 1]
    tail_scr[...] = x[TT - 8:TT]

    xb = xc.astype(BF16)
    r = jax.nn.sigmoid(jnp.dot(xb, wr_ref[...], preferred_element_type=F32) + br_ref[...])
    i = jax.nn.sigmoid(jnp.dot(xb, wi_ref[...], preferred_element_type=F32) + bi_ref[...])
    lam = lam_ref[...]
    log_sig = jnp.minimum(lam, 0.0) - jnp.log(1.0 + jnp.exp(-jnp.abs(lam)))
    log_a = (LRU_C * r) * log_sig
    a = jnp.exp(log_a)
    b = jnp.sqrt(1.0 - a * a) * (i * xc)

    row = lax.broadcasted_iota(jnp.int32, a.shape, 0)
    s = 1
    while s < TT:
        keep = row >= s
        a_sh = jnp.where(keep, pltpu.roll(a, s, 0), 1.0)
        b_sh = jnp.where(keep, pltpu.roll(b, s, 0), 0.0)
        b = a * b_sh + b
        a = a * a_sh
        s *= 2
    h = b + a * h_scr[0:1]
    h_scr[...] = jnp.broadcast_to(h[TT - 1:TT], h_scr.shape)
    y_ref[...] = (h * jax.nn.gelu(gate_ref[...])).astype(BF16)


def _lru(proj_lru, conv_w, conv_b, wr_bd, b_r, wi_bd, b_i, lru_l, batch, t_pad):
    n_pad = proj_lru.shape[0]
    n_t = t_pad // TT
    n_c = D_LRU // LRU_CB
    vec = lambda: pl.BlockSpec((1, LRU_CB), lambda b, c, t: (0, c))
    return pl.pallas_call(
        _lru_kernel,
        grid=(batch, n_c, n_t),
        in_specs=[
            pl.BlockSpec((TT, LRU_CB), lambda b, c, t: (b * n_t + t, c)),
            pl.BlockSpec((TT, LRU_CB), lambda b, c, t: (b * n_t + t, n_c + c)),
            pl.BlockSpec((CONV_WIDTH, LRU_CB), lambda b, c, t: (0, c)),
            vec(),
            pl.BlockSpec((None, LRU_CB, LRU_CB), lambda b, c, t: (c, 0, 0)),
            vec(),
            pl.BlockSpec((None, LRU_CB, LRU_CB), lambda b, c, t: (c, 0, 0)),
            vec(),
            vec(),
        ],
        out_specs=pl.BlockSpec((TT, LRU_CB), lambda b, c, t: (b * n_t + t, c)),
        out_shape=jax.ShapeDtypeStruct((n_pad, D_LRU), BF16),
        scratch_shapes=[pltpu.VMEM((8, LRU_CB), F32), pltpu.VMEM((8, LRU_CB), F32)],
        compiler_params=pltpu.CompilerParams(
            dimension_semantics=("arbitrary", "arbitrary", "arbitrary"),
            vmem_limit_bytes=VMEM_LIMIT),
        name="rglru",
    )(proj_lru, proj_lru, conv_w, conv_b, wr_bd, b_r, wi_bd, b_i, lru_l)


def _bucket_thresholds():
    max_exact = N_BUCKETS // 2
    n = np.arange(0, MAX_DISTANCE + 1)
    nf = np.maximum(n, 1).astype(np.float32)
    large = max_exact + (np.log(nf / np.float32(max_exact)) / np.float32(math.log(MAX_DISTANCE / max_exact))
                         * np.float32(N_BUCKETS - max_exact)).astype(np.int32)
    large = np.minimum(large, N_BUCKETS - 1)
    bucket = np.where(n < max_exact, n, large)
    return [int(np.argmax(bucket >= k)) for k in range(1, N_BUCKETS)]


_BUCKET_THR = _bucket_thresholds()


def _attn_kernel(rb_ref, q_ref, k_ref, v_ref, lam_ref, g_ref, o_ref, bias_scr, m_scr, l_scr, acc_scr):
    h = pl.program_id(0)
    b = pl.program_id(1)
    qi = pl.program_id(2)
    tk = TT

    @pl.when(jnp.logical_and(b == 0, qi == 0))
    def _():
        ii = lax.broadcasted_iota(jnp.int32, (tk, tk), 0)
        jj = lax.broadcasted_iota(jnp.int32, (tk, tk), 1)
        far = rb_ref[N_BUCKETS - 1, h]
        for d in range(2):
            rel = ii - jj + d * tk
            val = jnp.full((tk, tk), rb_ref[0, h] - far, F32)
            for kk, thr in enumerate(_BUCKET_THR):
                val = jnp.where(rel >= thr, rb_ref[kk + 1, h] - far, val)
            if d == 0:
                val = jnp.where(rel >= 0, val, -jnp.inf)
            bias_scr[d] = val

    q = q_ref[...]
    lane = lax.broadcasted_iota(jnp.int32, q.shape, 1)
    scale = QK_HEAD_DIM ** -0.5
    zero = jnp.zeros_like(q)
    qs = jnp.concatenate([jnp.where(lane < QK_HEAD_DIM, q, zero),
                          jnp.where(lane >= QK_HEAD_DIM, q, zero)], axis=0) * scale

    m_scr[...] = jnp.full(m_scr.shape, NEG_BIG, F32)
    l_scr[...] = jnp.zeros_like(l_scr)
    acc_scr[...] = jnp.zeros_like(acc_scr)

    def tile(j, bias):
        off = pl.multiple_of(j * tk, tk)
        kt = k_ref[pl.ds(off, tk), :]
        vt = v_ref[pl.ds(off, tk), :]
        s = lax.dot_general(qs, kt, (((1,), (1,)), ((), ())), preferred_element_type=F32)
        if bias is not None:
            s = s + jnp.concatenate([bias, bias], axis=0)
        m_prev = m_scr[...]
        m_new = jnp.maximum(m_prev, jnp.max(s, axis=-1, keepdims=True))
        alpha = jnp.exp(m_prev - m_new)
        p = jnp.exp(s - m_new)
        l_scr[...] = alpha * l_scr[...] + jnp.sum(p, axis=-1, keepdims=True)
        acc_scr[...] = alpha * acc_scr[...] + jnp.dot(p.astype(BF16), vt, preferred_element_type=F32)
        m_scr[...] = m_new

    def far_body(j, carry):
        tile(j, None)
        return carry

    lax.fori_loop(0, jnp.maximum(qi - 1, 0), far_body, 0)

    @pl.when(qi >= 1)
    def _():
        tile(qi - 1, bias_scr[1])

    tile(qi, bias_scr[0])

    lam_p = lam_ref[...]
    lam = (jnp.exp(jnp.sum(lam_p[0:1] * lam_p[1:2], axis=-1, keepdims=True))
           - jnp.exp(jnp.sum(lam_p[2:3] * lam_p[3:4], axis=-1, keepdims=True)) + LAMBDA_INIT)
    acc = acc_scr[...]
    l = l_scr[...]
    o = acc[:tk] / l[:tk] - lam * (acc[tk:] / l[tk:])
    o_ref[...] = (_rmsnorm(o, g_ref[...]) * (1.0 - LAMBDA_INIT)).astype(BF16)


def _attention(qkv, rel_bias, lam_params, subln_g, batch, t_pad):
    n_q = t_pad // TT
    qkv3 = qkv.reshape(batch, t_pad, 3 * D_ATT)
    out = pl.pallas_call(
        _attn_kernel,
        grid_spec=pltpu.PrefetchScalarGridSpec(
            num_scalar_prefetch=0,
            grid=(N_HEADS, batch, n_q),
            in_specs=[
                pl.BlockSpec(memory_space=pltpu.SMEM),
                pl.BlockSpec((None, TT, V_HEAD_DIM), lambda h, b, q: (b, q, h)),
                pl.BlockSpec((None, t_pad, V_HEAD_DIM), lambda h, b, q: (b, 0, N_HEADS + h)),
                pl.BlockSpec((None, t_pad, V_HEAD_DIM), lambda h, b, q: (b, 0, 2 * N_HEADS + h)),
                pl.BlockSpec((4, QK_HEAD_DIM), lambda h, b, q: (0, 0)),
                pl.BlockSpec((1, V_HEAD_DIM), lambda h, b, q: (0, 0)),
            ],
            out_specs=pl.BlockSpec((None, TT, V_HEAD_DIM), lambda h, b, q: (b, q, h)),
            scratch_shapes=[
                pltpu.VMEM((2, TT, TT), F32),
                pltpu.VMEM((2 * TT, 1), F32),
                pltpu.VMEM((2 * TT, 1), F32),
                pltpu.VMEM((2 * TT, V_HEAD_DIM), F32),
            ],
        ),
        out_shape=jax.ShapeDtypeStruct((batch, t_pad, D_ATT), BF16),
        compiler_params=pltpu.CompilerParams(
            dimension_semantics=("arbitrary", "arbitrary", "arbitrary"),
            vmem_limit_bytes=VMEM_LIMIT),
        name="diff_attention",
    )(rel_bias, qkv3, qkv3, qkv3, lam_params, subln_g)
    return out.reshape(batch * t_pad, D_ATT)


def _outproj_kernel(yl_ref, ya_ref, h0_ref, w_ref, g_ref, wrt_ref, brt_ref, h1_ref, u_ref, lg_ref):
    acc = jnp.dot(yl_ref[...], w_ref[0:D_LRU, :], preferred_element_type=F32)
    acc = acc + jnp.dot(ya_ref[...], w_ref[D_LRU:, :], preferred_element_type=F32)
    h1 = h0_ref[...] + acc
    h1_ref[...] = h1
    u = _rmsnorm(h1, g_ref[...])
    u_ref[...] = u
    u_hi = u.astype(BF16)
    u_lo = (u - u_hi.astype(F32)).astype(BF16)
    w = wrt_ref[...]
    w_hi = w.astype(BF16)
    w_lo = (w - w_hi.astype(F32)).astype(BF16)
    lg = jnp.dot(u_hi, w_hi, preferred_element_type=F32)
    lg = lg + jnp.dot(u_lo, w_hi, preferred_element_type=F32)
    lg = lg + jnp.dot(u_hi, w_lo, preferred_element_type=F32)
    lg_ref[...] = lg + brt_ref[...]


def _out_proj(y_lru, y_att, h0, w_out_bf16, g_ffn, w_rt, b_rt):
    n_pad = h0.shape[0]
    row = lambda w: pl.BlockSpec((OUT_TM, w), lambda m: (m, 0))
    full = lambda a, b: pl.BlockSpec((a, b), lambda m: (0, 0))
    return pl.pallas_call(
        _outproj_kernel,
        grid=(n_pad // OUT_TM,),
        in_specs=[row(D_LRU), row(D_ATT), row(D_MODEL), full(D_MODEL, D_MODEL), full(1, D_MODEL),
                  full(D_MODEL, LANES), full(1, LANES)],
        out_specs=[row(D_MODEL), row(D_MODEL), row(LANES)],
        out_shape=[
            jax.ShapeDtypeStruct((n_pad, D_MODEL), F32),
            jax.ShapeDtypeStruct((n_pad, D_MODEL), F32),
            jax.ShapeDtypeStruct((n_pad, LANES), F32),
        ],
        compiler_params=pltpu.CompilerParams(
            dimension_semantics=("arbitrary",), vmem_limit_bytes=VMEM_LIMIT),
        name="out_proj",
    )(y_lru, y_att, h0, w_out_bf16, g_ffn, w_rt, b_rt)


def _route_kernel(lg_ref, idx_ref, gate_ref, cnt_ref, carry_scr):
    step = pl.program_id(0)

    @pl.when(step == 0)
    def _():
        carry_scr[...] = jnp.zeros_like(carry_scr)

    lg = lg_ref[...]
    lane = lax.broadcasted_iota(jnp.int32, lg.shape, 1)
    first = lambda mask: jnp.min(jnp.where(mask, lane, LANES), axis=-1, keepdims=True)

    is_g = lane < N_GROUPS
    gl = jnp.where(is_g, lg, -jnp.inf)
    gmax = jnp.max(gl, axis=-1, keepdims=True)
    grp = first(gl == gmax)
    gsum = jnp.sum(jnp.where(is_g, jnp.exp(gl - gmax), 0.0), axis=-1, keepdims=True)
    p_grp = 1.0 / gsum

    lane_e = lane - N_GROUPS
    in_grp = (lane_e >= 0) & (lane_e < N_EXPERTS) & ((lane_e // EXPERTS_PER_GROUP) == grp)
    el = jnp.where(in_grp, lg, -jnp.inf)
    v1 = jnp.max(el, axis=-1, keepdims=True)
    i1 = first(el == v1)
    el2 = jnp.where(lane == i1, -jnp.inf, el)
    v2 = jnp.max(el2, axis=-1, keepdims=True)
    i2 = first(el2 == v2)
    e2 = jnp.exp(v2 - v1)
    den = 1.0 + e2
    g0 = p_grp * (1.0 / den)
    g1 = p_grp * (e2 / den)

    hit1 = lane == i1
    hit2 = lane == i2
    onehot = jnp.where(hit1 | hit2, 1.0, 0.0)
    rr = lax.broadcasted_iota(jnp.int32, (TT, TT), 0)
    cc = lax.broadcasted_iota(jnp.int32, (TT, TT), 1)
    tri = jnp.where(cc < rr, 1.0, 0.0).astype(BF16)
    rank = jnp.dot(tri, onehot.astype(BF16), preferred_element_type=F32) + carry_scr[...]
    r0 = jnp.sum(jnp.where(hit1, rank, 0.0), axis=-1, keepdims=True).astype(jnp.int32)
    r1 = jnp.sum(jnp.where(hit2, rank, 0.0), axis=-1, keepdims=True).astype(jnp.int32)
    carry_scr[...] = carry_scr[...] + jnp.sum(onehot, axis=0, keepdims=True)

    idx_ref[...] = jnp.where(lane == 0, i1 - N_GROUPS,
                             jnp.where(lane == 1, i2 - N_GROUPS,
                                       jnp.where(lane == 2, r0, jnp.where(lane == 3, r1, 0))))
    gate_ref[...] = jnp.where(lane == 0, g0, jnp.where(lane == 1, g1, 0.0))
    cnt_ref[...] = carry_scr[...]


def _route(logits):
    n_pad = logits.shape[0]
    row = pl.BlockSpec((TT, LANES), lambda m: (m, 0))
    return pl.pallas_call(
        _route_kernel,
        grid=(n_pad // TT,),
        in_specs=[row],
        out_specs=[row, row, pl.BlockSpec((1, LANES), lambda m: (0, 0))],
        out_shape=[
            jax.ShapeDtypeStruct((n_pad, LANES), jnp.int32),
            jax.ShapeDtypeStruct((n_pad, LANES), F32),
            jax.ShapeDtypeStruct((1, LANES), F32),
        ],
        scratch_shapes=[pltpu.VMEM((1, LANES), F32)],
        compiler_params=pltpu.CompilerParams(
            dimension_semantics=("arbitrary",), vmem_limit_bytes=VMEM_LIMIT),
        name="route",
    )(logits)


def _row_copy(src, s_row, dst, d_row, sem):
    return pltpu.make_async_copy(src.at[pl.ds(s_row, 1)], dst.at[pl.ds(d_row, 1)], sem)


def _dispatch_kernel(d0_ref, d1_ref, u_ref, xz_ref, xb_ref, sem):
    del xz_ref
    base = pl.program_id(0) * TT

    def issue(r, carry):
        _row_copy(u_ref, r, xb_ref, d0_ref[base + r], sem.at[0]).start()
        _row_copy(u_ref, r, xb_ref, d1_ref[base + r], sem.at[1]).start()
        return carry

    lax.fori_loop(0, TT, issue, 0)

    def drain(r, carry):
        _row_copy(u_ref, r, xb_ref, d0_ref[base + r], sem.at[0]).wait()
        _row_copy(u_ref, r, xb_ref, d1_ref[base + r], sem.at[1]).wait()
        return carry

    lax.fori_loop(0, TT, drain, 0)


def _dispatch(d0, d1, u2, rows):
    n_pad = u2.shape[0]
    zeros = jnp.zeros((rows, D_MODEL), F32)
    return pl.pallas_call(
        _dispatch_kernel,
        grid_spec=pltpu.PrefetchScalarGridSpec(
            num_scalar_prefetch=2,
            grid=(n_pad // TT,),
            in_specs=[
                pl.BlockSpec((TT, D_MODEL), lambda m, d0, d1: (m, 0)),
                pl.BlockSpec(memory_space=pl.ANY),
            ],
            out_specs=pl.BlockSpec(memory_space=pl.ANY),
            scratch_shapes=[pltpu.SemaphoreType.DMA((2,))],
        ),
        out_shape=jax.ShapeDtypeStruct((rows, D_MODEL), F32),
        input_output_aliases={3: 0},
        compiler_params=pltpu.CompilerParams(
            dimension_semantics=("arbitrary",), vmem_limit_bytes=VMEM_LIMIT),
        name="dispatch",
    )(d0, d1, u2, zeros)


def _expert_kernel(be_ref, nu_ref, x_ref, wg_ref, wu_ref, wd_ref, y_ref):
    del be_ref
    i = pl.program_id(0)

    @pl.when(i < nu_ref[0])
    def _():
        x = x_ref[...].astype(BF16)
        g = jnp.dot(x, wg_ref[...], preferred_element_type=F32)
        u = jnp.dot(x, wu_ref[...], preferred_element_type=F32)
        a = (jax.nn.silu(g) * u).astype(BF16)
        y_ref[...] = jnp.dot(a, wd_ref[...], preferred_element_type=F32)

    @pl.when(i >= nu_ref[0])
    def _():
        y_ref[...] = jnp.zeros_like(y_ref)


def _experts(blk_expert, n_used, x_buf, w_gate, w_up, w_down):
    rows = x_buf.shape[0]
    return pl.pallas_call(
        _expert_kernel,
        grid_spec=pltpu.PrefetchScalarGridSpec(
            num_scalar_prefetch=2,
            grid=(rows // DISPATCH_BLK,),
            in_specs=[
                pl.BlockSpec((DISPATCH_BLK, D_MODEL), lambda i, be, nu: (i, 0)),
                pl.BlockSpec((None, D_MODEL, D_EXPERT), lambda i, be, nu: (be[i], 0, 0)),
                pl.BlockSpec((None, D_MODEL, D_EXPERT), lambda i, be, nu: (be[i], 0, 0)),
                pl.BlockSpec((None, D_EXPERT, D_MODEL), lambda i, be, nu: (be[i], 0, 0)),
            ],
            out_specs=pl.BlockSpec((DISPATCH_BLK, D_MODEL), lambda i, be, nu: (i, 0)),
        ),
        out_shape=jax.ShapeDtypeStruct((rows, D_MODEL), F32),
        compiler_params=pltpu.CompilerParams(
            dimension_semantics=("arbitrary",), vmem_limit_bytes=VMEM_LIMIT),
        name="experts",
    )(blk_expert, n_used, x_buf, w_gate, w_up, w_down)


def _combine_kernel(d0_ref, d1_ref, h_ref, gate_ref, g_ref, yb_ref, o_ref, y0_scr, y1_scr, sem):
    base = pl.program_id(0) * TT

    def issue(r, carry):
        _row_copy(yb_ref, d0_ref[base + r], y0_scr, r, sem.at[0]).start()
        _row_copy(yb_ref, d1_ref[base + r], y1_scr, r, sem.at[1]).start()
        return carry

    lax.fori_loop(0, TT, issue, 0)

    def drain(r, carry):
        _row_copy(yb_ref, d0_ref[base + r], y0_scr, r, sem.at[0]).wait()
        _row_copy(yb_ref, d1_ref[base + r], y1_scr, r, sem.at[1]).wait()
        return carry

    lax.fori_loop(0, TT, drain, 0)

    gate = gate_ref[...]
    h = h_ref[...] + gate[:, 0:1] * y0_scr[...] + gate[:, 1:2] * y1_scr[...]
    o_ref[...] = _rmsnorm(h, g_ref[...])


def _combine(d0, d1, h1, gates, g_final, y_buf):
    n_pad = h1.shape[0]
    return pl.pallas_call(
        _combine_kernel,
        grid_spec=pltpu.PrefetchScalarGridSpec(
            num_scalar_prefetch=2,
            grid=(n_pad // TT,),
            in_specs=[
                pl.BlockSpec((TT, D_MODEL), lambda m, d0, d1: (m, 0)),
                pl.BlockSpec((TT, LANES), lambda m, d0, d1: (m, 0)),
                pl.BlockSpec((1, D_MODEL), lambda m, d0, d1: (0, 0)),
                pl.BlockSpec(memory_space=pl.ANY),
            ],
            out_specs=pl.BlockSpec((TT, D_MODEL), lambda m, d0, d1: (m, 0)),
            scratch_shapes=[
                pltpu.VMEM((TT, D_MODEL), F32),
                pltpu.VMEM((TT, D_MODEL), F32),
                pltpu.SemaphoreType.DMA((2,)),
            ],
        ),
        out_shape=jax.ShapeDtypeStruct((n_pad, D_MODEL), F32),
        compiler_params=pltpu.CompilerParams(
            dimension_semantics=("arbitrary",), vmem_limit_bytes=VMEM_LIMIT),
        name="combine",
    )(d0, d1, h1, gates, g_final, y_buf)


def _block_diag(w):
    per = LRU_CB // LRU_BLOCK_DIM
    w4 = w.reshape(LRU_BLOCKS // per, per, LRU_BLOCK_DIM, LRU_BLOCK_DIM)
    eye = jnp.eye(per, dtype=w.dtype)
    bd = jnp.einsum('cpij,pq->cpiqj', w4, eye)
    return bd.reshape(LRU_BLOCKS // per, LRU_CB, LRU_CB)


def kernel(x, meta_tokens, g_mix, w_in, conv_w, conv_b, w_rgate, b_rgate, w_igate, b_igate, lru_L, lambda_q1, lambda_k1, lambda_q2, lambda_k2, subln_g, rel_bias, w_out, g_ffn, w_group, b_group, w_router, b_router, w_gate, w_up, w_down, g_final):
    batch, seq, _ = x.shape
    t_real = N_META + seq
    t_pad = _round_up(t_real, TT)
    n_pad = batch * t_pad

    meta = jnp.broadcast_to(meta_tokens[None].astype(x.dtype), (batch, N_META, D_MODEL))
    h0 = jnp.concatenate([meta, x, jnp.zeros((batch, t_pad - t_real, D_MODEL), x.dtype)], axis=1)
    h0 = h0.reshape(n_pad, D_MODEL)

    proj_lru, proj_qkv = _in_proj(h0, g_mix[0][None], w_in[0].astype(BF16))

    y_lru = _lru(proj_lru, conv_w[0], conv_b[0][None],
                 _block_diag(w_rgate[0]).astype(BF16), b_rgate[0].reshape(1, D_LRU),
                 _block_diag(w_igate[0]).astype(BF16), b_igate[0].reshape(1, D_LRU),
                 lru_L[0][None], batch, t_pad)

    lam_params = jnp.stack([lambda_q1[0], lambda_k1[0], lambda_q2[0], lambda_k2[0]])
    y_att = _attention(proj_qkv, rel_bias, lam_params, subln_g[0][None], batch, t_pad)

    w_rt = jnp.concatenate([w_group[0], w_router[0],
                            jnp.zeros((D_MODEL, LANES - N_GROUPS - N_EXPERTS), F32)], axis=1)
    b_rt = jnp.concatenate([b_group[0], b_router[0],
                            jnp.zeros((LANES - N_GROUPS - N_EXPERTS,), F32)])[None]
    h1, u2, logits = _out_proj(y_lru, y_att, h0, w_out[0].astype(BF16), g_ffn[0][None], w_rt, b_rt)

    idx, gates, counts = _route(logits)

    cnt = counts[0, N_GROUPS:N_GROUPS + N_EXPERTS].astype(jnp.int32)
    padded = (cnt + DISPATCH_BLK - 1) // DISPATCH_BLK * DISPATCH_BLK
    pends = jnp.cumsum(padded)
    pstarts = pends - padded
    d0 = pstarts[idx[:, 0]] + idx[:, 2]
    d1 = pstarts[idx[:, 1]] + idx[:, 3]
    n_blocks = -(-(2 * n_pad + N_EXPERTS * (DISPATCH_BLK - 1)) // DISPATCH_BLK)
    rows = n_blocks * DISPATCH_BLK
    n_used = pends[-1] // DISPATCH_BLK
    blk = jnp.arange(n_blocks, dtype=jnp.int32)
    blk_expert = jnp.minimum(jnp.searchsorted(pends, blk * DISPATCH_BLK, side='right'),
                             N_EXPERTS - 1).astype(jnp.int32)
    blk_expert = jnp.where(blk < n_used, blk_expert, blk_expert[jnp.maximum(n_used - 1, 0)])

    x_buf = _dispatch(d0, d1, u2, rows)
    y_buf = _experts(blk_expert, n_used[None].astype(jnp.int32), x_buf,
                     w_gate[0].astype(BF16), w_up[0].astype(BF16), w_down[0].astype(BF16))
    out = _combine(d0, d1, h1, gates, g_final[None], y_buf)
    return out.reshape(batch, t_pad, D_MODEL)[:, N_META:t_real]
```

```python
import functools
import math

import numpy as np
import jax
import jax.numpy as jnp
from jax import lax
from jax.experimental import pallas as pl
from jax.experimental.pallas import tpu as pltpu

D_MODEL = 2048
N_META = 16
D_LRU = 1024
D_ATT = 1024
LRU_BLOCKS = 16
LRU_BLOCK_DIM = 64
CONV_WIDTH = 4
LRU_C = 8.0
N_HEADS = 8
V_HEAD_DIM = 128
QK_HEAD_DIM = 64
N_BUCKETS = 32
MAX_DISTANCE = 128
N_GROUPS = 4
EXPERTS_PER_GROUP = 8
N_EXPERTS = 32
D_EXPERT = 1024
D_IN_PROJ = 2 * D_LRU + 3 * D_ATT
EPS = 1e-6
LAMBDA_INIT = 0.8 - 0.6 * math.exp(-0.3 * 0)

F32 = jnp.float32
BF16 = jnp.bfloat16

LANES = 128
MXU_DIM = 256
TT = 768
PROJ_TN = 1024
OUT_TM = TT // 2
ATT_SUB = MXU_DIM
VT_ROWS = V_HEAD_DIM + 16
LRU_CB = MXU_DIM
DISPATCH_BLK = 256
VMEM_LIMIT = 56 * 1024 * 1024
NEG_BIG = -1e30


def _round_up(a, b):
    return -(-a // b) * b


def _rmsnorm(x, g):
    ms = jnp.mean(x * x, axis=-1, keepdims=True)
    return x * lax.rsqrt(ms + EPS) * g


def _inproj_kernel(x_ref, g_ref, w_ref, lru_ref, qkv_ref, u_scr):
    n = pl.program_id(1)

    @pl.when(n == 0)
    def _():
        u_scr[...] = _rmsnorm(x_ref[...], g_ref[...]).astype(BF16)

    y = jnp.dot(u_scr[...], w_ref[...], preferred_element_type=F32)

    @pl.when(n < 2)
    def _():
        lru_ref[...] = y

    @pl.when(n >= 2)
    def _():
        qkv_ref[...] = y.astype(BF16)


def _in_proj(h0, g_mix, w_in_bf16):
    n_pad = h0.shape[0]
    n_col = D_IN_PROJ // PROJ_TN
    return pl.pallas_call(
        _inproj_kernel,
        grid=(n_pad // TT, n_col),
        in_specs=[
            pl.BlockSpec((TT, D_MODEL), lambda m, n: (m, 0)),
            pl.BlockSpec((1, D_MODEL), lambda m, n: (0, 0)),
            pl.BlockSpec((D_MODEL, PROJ_TN), lambda m, n: (0, n)),
        ],
        out_specs=[
            pl.BlockSpec((TT, PROJ_TN), lambda m, n: (m, jnp.minimum(n, 1))),
            pl.BlockSpec((TT, PROJ_TN), lambda m, n: (m, jnp.maximum(n - 2, 0))),
        ],
        out_shape=[
            jax.ShapeDtypeStruct((n_pad, 2 * D_LRU), F32),
            jax.ShapeDtypeStruct((n_pad, 3 * D_ATT), BF16),
        ],
        scratch_shapes=[pltpu.VMEM((TT, D_MODEL), BF16)],
        compiler_params=pltpu.CompilerParams(
            dimension_semantics=("arbitrary", "arbitrary"), vmem_limit_bytes=VMEM_LIMIT),
        name="in_proj",
    )(h0, g_mix, w_in_bf16)


def _shift_rows(x, tail, s):
    r = pltpu.roll(x, s, 0)
    row8 = lax.broadcasted_iota(jnp.int32, tail.shape, 0)
    head = jnp.where(row8 < s, pltpu.roll(tail, s, 0), r[0:8])
    return jnp.concatenate([head, r[8:]], axis=0)


def _lru_kernel(x_ref, gate_ref, cw_ref, cb_ref, wr_ref, br_ref, wi_ref, bi_ref, lam_ref,
                y_ref, tail_scr, h_scr):
    t = pl.program_id(2)

    @pl.when(t == 0)
    def _():
        tail_scr[...] = jnp.zeros_like(tail_scr)
        h_scr[...] = jnp.zeros_like(h_scr)

    x = x_ref[...]
    tail = tail_scr[...]
    cw = cw_ref[...]
    xc = cb_ref[...] + x * cw[CONV_WIDTH - 1:CONV_WIDTH]
    for k in range(CONV_WIDTH - 1):
        xc = xc + _shift_rows(x, tail, CONV_WIDTH - 1 - k) * cw[k:k + 1]
    tail_scr[...] = x[TT - 8:TT]

    xb = xc.astype(BF16)
    r = jax.nn.sigmoid(jnp.dot(xb, wr_ref[...], preferred_element_type=F32) + br_ref[...])
    i = jax.nn.sigmoid(jnp.dot(xb, wi_ref[...], preferred_element_type=F32) + bi_ref[...])
    lam = lam_ref[...]
    log_sig = jnp.minimum(lam, 0.0) - jnp.log(1.0 + jnp.exp(-jnp.abs(lam)))
    log_a = (LRU_C * r) * log_sig
    a = jnp.exp(log_a)
    b = jnp.sqrt(1.0 - a * a) * (i * xc)

    row = lax.broadcasted_iota(jnp.int32, a.shape, 0)
    s = 1
    while s < TT:
        keep = row >= s
        a_sh = jnp.where(keep, pltpu.roll(a, s, 0), 1.0)
        b_sh = jnp.where(keep, pltpu.roll(b, s, 0), 0.0)
        b = a * b_sh + b
        a = a * a_sh
        s *= 2
    h = b + a * h_scr[0:1]
    h_scr[...] = jnp.broadcast_to(h[TT - 1:TT], h_scr.shape)
    y_ref[...] = (h * jax.nn.gelu(gate_ref[...])).astype(BF16)


def _lru(proj_lru, conv_w, conv_b, wr_bd, b_r, wi_bd, b_i, lru_l, batch, t_pad):
    n_pad = proj_lru.shape[0]
    n_t = t_pad // TT
    n_c = D_LRU // LRU_CB
    vec = lambda: pl.BlockSpec((1, LRU_CB), lambda b, c, t: (0, c))
    return pl.pallas_call(
        _lru_kernel,
        grid=(batch, n_c, n_t),
        in_specs=[
            pl.BlockSpec((TT, LRU_CB), lambda b, c, t: (b * n_t + t, c)),
            pl.BlockSpec((TT, LRU_CB), lambda b, c, t: (b * n_t + t, n_c + c)),
            pl.BlockSpec((CONV_WIDTH, LRU_CB), lambda b, c, t: (0, c)),
            vec(),
            pl.BlockSpec((None, LRU_CB, LRU_CB), lambda b, c, t: (c, 0, 0)),
            vec(),
            pl.BlockSpec((None, LRU_CB, LRU_CB), lambda b, c, t: (c, 0, 0)),
            vec(),
            vec(),
        ],
        out_specs=pl.BlockSpec((TT, LRU_CB), lambda b, c, t: (b * n_t + t, c)),
        out_shape=jax.ShapeDtypeStruct((n_pad, D_LRU), BF16),
        scratch_shapes=[pltpu.VMEM((8, LRU_CB), F32), pltpu.VMEM((8, LRU_CB), F32)],
        compiler_params=pltpu.CompilerParams(
            dimension_semantics=("arbitrary", "arbitrary", "arbitrary"),
            vmem_limit_bytes=VMEM_LIMIT),
        name="rglru",
    )(proj_lru, proj_lru, conv_w, conv_b, wr_bd, b_r, wi_bd, b_i, lru_l)


def _bucket_thresholds():
    max_exact = N_BUCKETS // 2
    n = np.arange(0, MAX_DISTANCE + 1)
    nf = np.maximum(n, 1).astype(np.float32)
    large = max_exact + (np.log(nf / np.float32(max_exact)) / np.float32(math.log(MAX_DISTANCE / max_exact))
                         * np.float32(N_BUCKETS - max_exact)).astype(np.int32)
    large = np.minimum(large, N_BUCKETS - 1)
    bucket = np.where(n < max_exact, n, large)
    return [int(np.argmax(bucket >= k)) for k in range(1, N_BUCKETS)]


_BUCKET_THR = _bucket_thresholds()


def _attn_kernel(rb_ref, q_ref, k_ref, v_ref, lam_ref, g_ref, o_ref, bias_scr, vt_scr, m_scr, acc_scr):
    h = pl.program_id(0)
    b = pl.program_id(1)
    qi = pl.program_id(2)
    tk = TT
    n_kv = vt_scr.shape[0]

    @pl.when(jnp.logical_and(b == 0, qi == 0))
    def _():
        k_pos = lax.broadcasted_iota(jnp.int32, (tk, tk), 0)
        q_pos = lax.broadcasted_iota(jnp.int32, (tk, tk), 1)
        far = rb_ref[N_BUCKETS - 1, h]
        for d in range(2):
            rel = q_pos - k_pos + d * tk
            val = jnp.full((tk, tk), rb_ref[0, h] - far, F32)
            for kk, thr in enumerate(_BUCKET_THR):
                val = jnp.where(rel >= thr, rb_ref[kk + 1, h] - far, val)
            if d == 0:
                val = jnp.where(rel >= 0, val, -jnp.inf)
            bias_scr[d] = val

    @pl.when(qi == 0)
    def _():
        for t in range(n_kv):
            vt_scr[t, 0:V_HEAD_DIM, :] = v_ref[t * tk:(t + 1) * tk, :].astype(F32).T.astype(BF16)
            vt_scr[t, V_HEAD_DIM:VT_ROWS, :] = jnp.ones((VT_ROWS - V_HEAD_DIM, tk), BF16)

    q = q_ref[...]
    lane = lax.broadcasted_iota(jnp.int32, q.shape, 1)
    scale = QK_HEAD_DIM ** -0.5
    zero = jnp.zeros_like(q)
    qs = jnp.concatenate([jnp.where(lane < QK_HEAD_DIM, q, zero),
                          jnp.where(lane >= QK_HEAD_DIM, q, zero)], axis=0) * scale

    m_scr[...] = jnp.full(m_scr.shape, NEG_BIG, F32)
    acc_scr[...] = jnp.zeros_like(acc_scr)

    def tile(j, bias):
        off = pl.multiple_of(j * tk, tk)
        kt = k_ref[pl.ds(off, tk), :]
        vt = vt_scr[j]
        n_sub = 2 * tk // ATT_SUB

        def scores(c):
            cols = slice(c * ATT_SUB, (c + 1) * ATT_SUB)
            s = lax.dot_general(kt, qs[cols], (((1,), (1,)), ((), ())), preferred_element_type=F32)
            if bias is not None:
                q0 = (c * ATT_SUB) % tk
                s = s + bias[:, q0:q0 + ATT_SUB]
            return s

        s_next = scores(0)
        for c in range(n_sub):
            cols = slice(c * ATT_SUB, (c + 1) * ATT_SUB)
            s = s_next
            if c + 1 < n_sub:
                s_next = scores(c + 1)
            m_prev = m_scr[:, cols]
            m_new = jnp.maximum(m_prev, jnp.max(s, axis=0, keepdims=True))
            alpha = jnp.exp(m_prev - m_new)
            p = jnp.exp(s - m_new).astype(BF16)
            acc_scr[:, cols] = alpha * acc_scr[:, cols] + jnp.dot(vt, p, preferred_element_type=F32)
            m_scr[:, cols] = m_new

    def far_body(j, carry):
        tile(j, None)
        return carry

    lax.fori_loop(0, jnp.maximum(qi - 1, 0), far_body, 0)

    @pl.when(qi >= 1)
    def _():
        tile(qi - 1, bias_scr[1])

    tile(qi, bias_scr[0])

    lam_p = lam_ref[...]
    lam = (jnp.exp(jnp.sum(lam_p[0:1] * lam_p[1:2], axis=-1, keepdims=True))
           - jnp.exp(jnp.sum(lam_p[2:3] * lam_p[3:4], axis=-1, keepdims=True)) + LAMBDA_INIT)
    acc = acc_scr[...]
    num = acc[0:V_HEAD_DIM]
    den = acc[V_HEAD_DIM:V_HEAD_DIM + 1]
    o_t = num[:, :tk] / den[:, :tk] - lam * (num[:, tk:] / den[:, tk:])
    o_ref[...] = (_rmsnorm(o_t.T, g_ref[...]) * (1.0 - LAMBDA_INIT)).astype(BF16)


def _attention(qkv, rel_bias, lam_params, subln_g, batch, t_pad):
    n_q = t_pad // TT
    qkv3 = qkv.reshape(batch, t_pad, 3 * D_ATT)
    out = pl.pallas_call(
        _attn_kernel,
        grid_spec=pltpu.PrefetchScalarGridSpec(
            num_scalar_prefetch=0,
            grid=(N_HEADS, batch, n_q),
            in_specs=[
                pl.BlockSpec(memory_space=pltpu.SMEM),
                pl.BlockSpec((None, TT, V_HEAD_DIM), lambda h, b, q: (b, q, h)),
                pl.BlockSpec((None, t_pad, V_HEAD_DIM), lambda h, b, q: (b, 0, N_HEADS + h)),
                pl.BlockSpec((None, t_pad, V_HEAD_DIM), lambda h, b, q: (b, 0, 2 * N_HEADS + h)),
                pl.BlockSpec((4, QK_HEAD_DIM), lambda h, b, q: (0, 0)),
                pl.BlockSpec((1, V_HEAD_DIM), lambda h, b, q: (0, 0)),
            ],
            out_specs=pl.BlockSpec((None, TT, V_HEAD_DIM), lambda h, b, q: (b, q, h)),
            scratch_shapes=[
                pltpu.VMEM((2, TT, TT), F32),
                pltpu.VMEM((n_q, VT_ROWS, TT), BF16),
                pltpu.VMEM((1, 2 * TT), F32),
                pltpu.VMEM((VT_ROWS, 2 * TT), F32),
            ],
        ),
        out_shape=jax.ShapeDtypeStruct((batch, t_pad, D_ATT), BF16),
        compiler_params=pltpu.CompilerParams(
            dimension_semantics=("arbitrary", "arbitrary", "arbitrary"),
            vmem_limit_bytes=VMEM_LIMIT),
        name="diff_attention",
    )(rel_bias, qkv3, qkv3, qkv3, lam_params, subln_g)
    return out.reshape(batch * t_pad, D_ATT)


def _outproj_kernel(yl_ref, ya_ref, h0_ref, w_ref, g_ref, wrt_ref, brt_ref, h1_ref, u_ref, lg_ref):
    acc = jnp.dot(yl_ref[...], w_ref[0:D_LRU, :], preferred_element_type=F32)
    acc = acc + jnp.dot(ya_ref[...], w_ref[D_LRU:, :], preferred_element_type=F32)
    h1 = h0_ref[...] + acc
    h1_ref[...] = h1
    u = _rmsnorm(h1, g_ref[...])
    u_ref[...] = u
    u_hi = u.astype(BF16)
    u_lo = (u - u_hi.astype(F32)).astype(BF16)
    w = wrt_ref[...]
    w_hi = w.astype(BF16)
    w_lo = (w - w_hi.astype(F32)).astype(BF16)
    lg = jnp.dot(u_hi, w_hi, preferred_element_type=F32)
    lg = lg + jnp.dot(u_lo, w_hi, preferred_element_type=F32)
    lg = lg + jnp.dot(u_hi, w_lo, preferred_element_type=F32)
    lg_ref[...] = lg + brt_ref[...]


def _out_proj(y_lru, y_att, h0, w_out_bf16, g_ffn, w_rt, b_rt):
    n_pad = h0.shape[0]
    row = lambda w: pl.BlockSpec((OUT_TM, w), lambda m: (m, 0))
    full = lambda a, b: pl.BlockSpec((a, b), lambda m: (0, 0))
    return pl.pallas_call(
        _outproj_kernel,
        grid=(n_pad // OUT_TM,),
        in_specs=[row(D_LRU), row(D_ATT), row(D_MODEL), full(D_MODEL, D_MODEL), full(1, D_MODEL),
                  full(D_MODEL, LANES), full(1, LANES)],
        out_specs=[row(D_MODEL), row(D_MODEL), row(LANES)],
        out_shape=[
            jax.ShapeDtypeStruct((n_pad, D_MODEL), F32),
            jax.ShapeDtypeStruct((n_pad, D_MODEL), F32),
            jax.ShapeDtypeStruct((n_pad, LANES), F32),
        ],
        compiler_params=pltpu.CompilerParams(
            dimension_semantics=("arbitrary",), vmem_limit_bytes=VMEM_LIMIT),
        name="out_proj",
    )(y_lru, y_att, h0, w_out_bf16, g_ffn, w_rt, b_rt)


def _route_kernel(lg_ref, idx_ref, gate_ref, cnt_ref, carry_scr):
    step = pl.program_id(0)

    @pl.when(step == 0)
    def _():
        carry_scr[...] = jnp.zeros_like(carry_scr)

    lg = lg_ref[...]
    lane = lax.broadcasted_iota(jnp.int32, lg.shape, 1)
    first = lambda mask: jnp.min(jnp.where(mask, lane, LANES), axis=-1, keepdims=True)

    is_g = lane < N_GROUPS
    gl = jnp.where(is_g, lg, -jnp.inf)
    gmax = jnp.max(gl, axis=-1, keepdims=True)
    grp = first(gl == gmax)
    gsum = jnp.sum(jnp.where(is_g, jnp.exp(gl - gmax), 0.0), axis=-1, keepdims=True)
    p_grp = 1.0 / gsum

    lane_e = lane - N_GROUPS
    in_grp = (lane_e >= 0) & (lane_e < N_EXPERTS) & ((lane_e // EXPERTS_PER_GROUP) == grp)
    el = jnp.where(in_grp, lg, -jnp.inf)
    v1 = jnp.max(el, axis=-1, keepdims=True)
    i1 = first(el == v1)
    el2 = jnp.where(lane == i1, -jnp.inf, el)
    v2 = jnp.max(el2, axis=-1, keepdims=True)
    i2 = first(el2 == v2)
    e2 = jnp.exp(v2 - v1)
    den = 1.0 + e2
    g0 = p_grp * (1.0 / den)
    g1 = p_grp * (e2 / den)

    hit1 = lane == i1
    hit2 = lane == i2
    onehot = jnp.where(hit1 | hit2, 1.0, 0.0)
    rr = lax.broadcasted_iota(jnp.int32, (TT, TT), 0)
    cc = lax.broadcasted_iota(jnp.int32, (TT, TT), 1)
    tri = jnp.where(cc < rr, 1.0, 0.0).astype(BF16)
    rank = jnp.dot(tri, onehot.astype(BF16), preferred_element_type=F32) + carry_scr[...]
    r0 = jnp.sum(jnp.where(hit1, rank, 0.0), axis=-1, keepdims=True).astype(jnp.int32)
    r1 = jnp.sum(jnp.where(hit2, rank, 0.0), axis=-1, keepdims=True).astype(jnp.int32)
    carry_scr[...] = carry_scr[...] + jnp.sum(onehot, axis=0, keepdims=True)

    idx_ref[...] = jnp.where(lane == 0, i1 - N_GROUPS,
                             jnp.where(lane == 1, i2 - N_GROUPS,
                                       jnp.where(lane == 2, r0, jnp.where(lane == 3, r1, 0))))
    gate_ref[...] = jnp.where(lane == 0, g0, jnp.where(lane == 1, g1, 0.0))
    cnt_ref[...] = carry_scr[...]


def _route(logits):
    n_pad = logits.shape[0]
    row = pl.BlockSpec((TT, LANES), lambda m: (m, 0))
    return pl.pallas_call(
        _route_kernel,
        grid=(n_pad // TT,),
        in_specs=[row],
        out_specs=[row, row, pl.BlockSpec((1, LANES), lambda m: (0, 0))],
        out_shape=[
            jax.ShapeDtypeStruct((n_pad, LANES), jnp.int32),
            jax.ShapeDtypeStruct((n_pad, LANES), F32),
            jax.ShapeDtypeStruct((1, LANES), F32),
        ],
        scratch_shapes=[pltpu.VMEM((1, LANES), F32)],
        compiler_params=pltpu.CompilerParams(
            dimension_semantics=("arbitrary",), vmem_limit_bytes=VMEM_LIMIT),
        name="route",
    )(logits)


def _row_copy(src, s_row, dst, d_row, sem):
    return pltpu.make_async_copy(src.at[pl.ds(s_row, 1)], dst.at[pl.ds(d_row, 1)], sem)


def _dispatch_kernel(d0_ref, d1_ref, u_ref, xz_ref, xb_ref, sem):
    del xz_ref
    base = pl.program_id(0) * TT

    def issue(r, carry):
        _row_copy(u_ref, r, xb_ref, d0_ref[base + r], sem.at[0]).start()
        _row_copy(u_ref, r, xb_ref, d1_ref[base + r], sem.at[1]).start()
        return carry

    lax.fori_loop(0, TT, issue, 0)

    def drain(r, carry):
        _row_copy(u_ref, r, xb_ref, d0_ref[base + r], sem.at[0]).wait()
        _row_copy(u_ref, r, xb_ref, d1_ref[base + r], sem.at[1]).wait()
        return carry

    lax.fori_loop(0, TT, drain, 0)


def _dispatch(d0, d1, u2, rows):
    n_pad = u2.shape[0]
    zeros = jnp.zeros((rows, D_MODEL), F32)
    return pl.pallas_call(
        _dispatch_kernel,
        grid_spec=pltpu.PrefetchScalarGridSpec(
            num_scalar_prefetch=2,
            grid=(n_pad // TT,),
            in_specs=[
                pl.BlockSpec((TT, D_MODEL), lambda m, d0, d1: (m, 0)),
                pl.BlockSpec(memory_space=pl.ANY),
            ],
            out_specs=pl.BlockSpec(memory_space=pl.ANY),
            scratch_shapes=[pltpu.SemaphoreType.DMA((2,))],
        ),
        out_shape=jax.ShapeDtypeStruct((rows, D_MODEL), F32),
        input_output_aliases={3: 0},
        compiler_params=pltpu.CompilerParams(
            dimension_semantics=("arbitrary",), vmem_limit_bytes=VMEM_LIMIT),
        name="dispatch",
    )(d0, d1, u2, zeros)


def _expert_kernel(be_ref, nu_ref, x_ref, wg_ref, wu_ref, wd_ref, y_ref):
    del be_ref
    i = pl.program_id(0)

    @pl.when(i < nu_ref[0])
    def _():
        x = x_ref[...].astype(BF16)
        g = jnp.dot(x, wg_ref[...], preferred_element_type=F32)
        u = jnp.dot(x, wu_ref[...], preferred_element_type=F32)
        a = (jax.nn.silu(g) * u).astype(BF16)
        y_ref[...] = jnp.dot(a, wd_ref[...], preferred_element_type=F32)

    @pl.when(i >= nu_ref[0])
    def _():
        y_ref[...] = jnp.zeros_like(y_ref)


def _experts(blk_expert, n_used, x_buf, w_gate, w_up, w_down):
    rows = x_buf.shape[0]
    return pl.pallas_call(
        _expert_kernel,
        grid_spec=pltpu.PrefetchScalarGridSpec(
            num_scalar_prefetch=2,
            grid=(rows // DISPATCH_BLK,),
            in_specs=[
                pl.BlockSpec((DISPATCH_BLK, D_MODEL), lambda i, be, nu: (i, 0)),
                pl.BlockSpec((None, D_MODEL, D_EXPERT), lambda i, be, nu: (be[i], 0, 0)),
                pl.BlockSpec((None, D_MODEL, D_EXPERT), lambda i, be, nu: (be[i], 0, 0)),
                pl.BlockSpec((None, D_EXPERT, D_MODEL), lambda i, be, nu: (be[i], 0, 0)),
            ],
            out_specs=pl.BlockSpec((DISPATCH_BLK, D_MODEL), lambda i, be, nu: (i, 0)),
        ),
        out_shape=jax.ShapeDtypeStruct((rows, D_MODEL), F32),
        compiler_params=pltpu.CompilerParams(
            dimension_semantics=("arbitrary",), vmem_limit_bytes=VMEM_LIMIT),
        name="experts",
    )(blk_expert, n_used, x_buf, w_gate, w_up, w_down)


def _combine_kernel(d0_ref, d1_ref, h_ref, gate_ref, g_ref, yb_ref, o_ref, y0_scr, y1_scr, sem):
    base = pl.program_id(0) * TT

    def issue(r, carry):
        _row_copy(yb_ref, d0_ref[base + r], y0_scr, r, sem.at[0]).start()
        _row_copy(yb_ref, d1_ref[base + r], y1_scr, r, sem.at[1]).start()
        return carry

    lax.fori_loop(0, TT, issue, 0)

    def drain(r, carry):
        _row_copy(yb_ref, d0_ref[base + r], y0_scr, r, sem.at[0]).wait()
        _row_copy(yb_ref, d1_ref[base + r], y1_scr, r, sem.at[1]).wait()
        return carry

    lax.fori_loop(0, TT, drain, 0)

    gate = gate_ref[...]
    h = h_ref[...] + gate[:, 0:1] * y0_scr[...] + gate[:, 1:2] * y1_scr[...]
    o_ref[...] = _rmsnorm(h, g_ref[...])


def _combine(d0, d1, h1, gates, g_final, y_buf):
    n_pad = h1.shape[0]
    return pl.pallas_call(
        _combine_kernel,
        grid_spec=pltpu.PrefetchScalarGridSpec(
            num_scalar_prefetch=2,
            grid=(n_pad // TT,),
            in_specs=[
                pl.BlockSpec((TT, D_MODEL), lambda m, d0, d1: (m, 0)),
                pl.BlockSpec((TT, LANES), lambda m, d0, d1: (m, 0)),
                pl.BlockSpec((1, D_MODEL), lambda m, d0, d1: (0, 0)),
                pl.BlockSpec(memory_space=pl.ANY),
            ],
            out_specs=pl.BlockSpec((TT, D_MODEL), lambda m, d0, d1: (m, 0)),
            scratch_shapes=[
                pltpu.VMEM((TT, D_MODEL), F32),
                pltpu.VMEM((TT, D_MODEL), F32),
                pltpu.SemaphoreType.DMA((2,)),
            ],
        ),
        out_shape=jax.ShapeDtypeStruct((n_pad, D_MODEL), F32),
        compiler_params=pltpu.CompilerParams(
            dimension_semantics=("arbitrary",), vmem_limit_bytes=VMEM_LIMIT),
        name="combine",
    )(d0, d1, h1, gates, g_final, y_buf)


def _block_diag(w):
    per = LRU_CB // LRU_BLOCK_DIM
    w4 = w.reshape(LRU_BLOCKS // per, per, LRU_BLOCK_DIM, LRU_BLOCK_DIM)
    eye = jnp.eye(per, dtype=w.dtype)
    bd = jnp.einsum('cpij,pq->cpiqj', w4, eye)
    return bd.reshape(LRU_BLOCKS // per, LRU_CB, LRU_CB)


def kernel(x, meta_tokens, g_mix, w_in, conv_w, conv_b, w_rgate, b_rgate, w_igate, b_igate, lru_L, lambda_q1, lambda_k1, lambda_q2, lambda_k2, subln_g, rel_bias, w_out, g_ffn, w_group, b_group, w_router, b_router, w_gate, w_up, w_down, g_final):
    batch, seq, _ = x.shape
    t_real = N_META + seq
    t_pad = _round_up(t_real, TT)
    n_pad = batch * t_pad

    meta = jnp.broadcast_to(meta_tokens[None].astype(x.dtype), (batch, N_META, D_MODEL))
    h0 = jnp.concatenate([meta, x, jnp.zeros((batch, t_pad - t_real, D_MODEL), x.dtype)], axis=1)
    h0 = h0.reshape(n_pad, D_MODEL)

    proj_lru, proj_qkv = _in_proj(h0, g_mix[0][None], w_in[0].astype(BF16))

    y_lru = _lru(proj_lru, conv_w[0], conv_b[0][None],
                 _block_diag(w_rgate[0]).astype(BF16), b_rgate[0].reshape(1, D_LRU),
                 _block_diag(w_igate[0]).astype(BF16), b_igate[0].reshape(1, D_LRU),
                 lru_L[0][None], batch, t_pad)

    lam_params = jnp.stack([lambda_q1[0], lambda_k1[0], lambda_q2[0], lambda_k2[0]])
    y_att = _attention(proj_qkv, rel_bias, lam_params, subln_g[0][None], batch, t_pad)

    w_rt = jnp.concatenate([w_group[0], w_router[0],
                            jnp.zeros((D_MODEL, LANES - N_GROUPS - N_EXPERTS), F32)], axis=1)
    b_rt = jnp.concatenate([b_group[0], b_router[0],
                            jnp.zeros((LANES - N_GROUPS - N_EXPERTS,), F32)])[None]
    h1, u2, logits = _out_proj(y_lru, y_att, h0, w_out[0].astype(BF16), g_ffn[0][None], w_rt, b_rt)

    idx, gates, counts = _route(logits)

    cnt = counts[0, N_GROUPS:N_GROUPS + N_EXPERTS].astype(jnp.int32)
    padded = (cnt + DISPATCH_BLK - 1) // DISPATCH_BLK * DISPATCH_BLK
    pends = jnp.cumsum(padded)
    pstarts = pends - padded
    d0 = pstarts[idx[:, 0]] + idx[:, 2]
    d1 = pstarts[idx[:, 1]] + idx[:, 3]
    n_blocks = -(-(2 * n_pad + N_EXPERTS * (DISPATCH_BLK - 1)) // DISPATCH_BLK)
    rows = n_blocks * DISPATCH_BLK
    n_used = pends[-1] // DISPATCH_BLK
    blk = jnp.arange(n_blocks, dtype=jnp.int32)
    blk_expert = jnp.minimum(jnp.searchsorted(pends, blk * DISPATCH_BLK, side='right'),
                             N_EXPERTS - 1).astype(jnp.int32)
    blk_expert = jnp.where(blk < n_used, blk_expert, blk_expert[jnp.maximum(n_used - 1, 0)])

    x_buf = _dispatch(d0, d1, u2, rows)
    y_buf = _experts(blk_expert, n_used[None].astype(jnp.int32), x_buf,
                     w_gate[0].astype(BF16), w_up[0].astype(BF16), w_down[0].astype(BF16))
    out = _combine(d0, d1, h1, gates, g_final[None], y_buf)
    return out.reshape(batch, t_pad, D_MODEL)[:, N_META:t_real]
```

```python
import functools
import math

import numpy as np
import jax
import jax.numpy as jnp
from jax import lax
from jax.experimental import pallas as pl
from jax.experimental.pallas import tpu as pltpu

D_MODEL = 2048
N_META = 16
D_LRU = 1024
D_ATT = 1024
LRU_BLOCKS = 16
LRU_BLOCK_DIM = 64
CONV_WIDTH = 4
LRU_C = 8.0
N_HEADS = 8
V_HEAD_DIM = 128
QK_HEAD_DIM = 64
N_BUCKETS = 32
MAX_DISTANCE = 128
N_GROUPS = 4
EXPERTS_PER_GROUP = 8
N_EXPERTS = 32
D_EXPERT = 1024
D_IN_PROJ = 2 * D_LRU + 3 * D_ATT
EPS = 1e-6
LAMBDA_INIT = 0.8 - 0.6 * math.exp(-0.3 * 0)
LOG2_E = math.log2(math.e)

F32 = jnp.float32
BF16 = jnp.bfloat16

LANES = 128
MXU_DIM = 256
TT = 768
PROJ_TN = 1024
OUT_TM = TT // 2
ATT_SUB = MXU_DIM
ATT_LOOKAHEAD = 3
VT_ROWS = V_HEAD_DIM + 16
LRU_CB = MXU_DIM
DISPATCH_BLK = 256
VMEM_LIMIT = 56 * 1024 * 1024
NEG_BIG = -1e30


def _round_up(a, b):
    return -(-a // b) * b


def _rmsnorm(x, g):
    ms = jnp.mean(x * x, axis=-1, keepdims=True)
    return x * lax.rsqrt(ms + EPS) * g


def _inproj_kernel(x_ref, g_ref, w_ref, lru_ref, qkv_ref, u_scr):
    n = pl.program_id(1)

    @pl.when(n == 0)
    def _():
        u_scr[...] = _rmsnorm(x_ref[...], g_ref[...]).astype(BF16)

    y = jnp.dot(u_scr[...], w_ref[...], preferred_element_type=F32)

    @pl.when(n < 2)
    def _():
        lru_ref[...] = y

    @pl.when(n >= 2)
    def _():
        qkv_ref[...] = y.astype(BF16)


def _in_proj(h0, g_mix, w_in_bf16):
    n_pad = h0.shape[0]
    n_col = D_IN_PROJ // PROJ_TN
    return pl.pallas_call(
        _inproj_kernel,
        grid=(n_pad // TT, n_col),
        in_specs=[
            pl.BlockSpec((TT, D_MODEL), lambda m, n: (m, 0)),
            pl.BlockSpec((1, D_MODEL), lambda m, n: (0, 0)),
            pl.BlockSpec((D_MODEL, PROJ_TN), lambda m, n: (0, n)),
        ],
        out_specs=[
            pl.BlockSpec((TT, PROJ_TN), lambda m, n: (m, jnp.minimum(n, 1))),
            pl.BlockSpec((TT, PROJ_TN), lambda m, n: (m, jnp.maximum(n - 2, 0))),
        ],
        out_shape=[
            jax.ShapeDtypeStruct((n_pad, 2 * D_LRU), F32),
            jax.ShapeDtypeStruct((n_pad, 3 * D_ATT), BF16),
        ],
        scratch_shapes=[pltpu.VMEM((TT, D_MODEL), BF16)],
        compiler_params=pltpu.CompilerParams(
            dimension_semantics=("arbitrary", "arbitrary"), vmem_limit_bytes=VMEM_LIMIT),
        name="in_proj",
    )(h0, g_mix, w_in_bf16)


def _shift_rows(x, tail, s):
    r = pltpu.roll(x, s, 0)
    row8 = lax.broadcasted_iota(jnp.int32, tail.shape, 0)
    head = jnp.where(row8 < s, pltpu.roll(tail, s, 0), r[0:8])
    return jnp.concatenate([head, r[8:]], axis=0)


def _lru_kernel(x_ref, gate_ref, cw_ref, cb_ref, wr_ref, br_ref, wi_ref, bi_ref, lam_ref,
                y_ref, tail_scr, h_scr):
    t = pl.program_id(2)

    @pl.when(t == 0)
    def _():
        tail_scr[...] = jnp.zeros_like(tail_scr)
        h_scr[...] = jnp.zeros_like(h_scr)

    x = x_ref[...]
    tail = tail_scr[...]
    cw = cw_ref[...]
    xc = cb_ref[...] + x * cw[CONV_WIDTH - 1:CONV_WIDTH]
    for k in range(CONV_WIDTH - 1):
        xc = xc + _shift_rows(x, tail, CONV_WIDTH - 1 - k) * cw[k:k + 1]
    tail_scr[...] = x[TT - 8:TT]

    xb = xc.astype(BF16)
    r = jax.nn.sigmoid(jnp.dot(xb, wr_ref[...], preferred_element_type=F32) + br_ref[...])
    i = jax.nn.sigmoid(jnp.dot(xb, wi_ref[...], preferred_element_type=F32) + bi_ref[...])
    lam = lam_ref[...]
    log_sig = jnp.minimum(lam, 0.0) - jnp.log(1.0 + jnp.exp(-jnp.abs(lam)))
    log_a = (LRU_C * r) * log_sig
    a = jnp.exp(log_a)
    b = jnp.sqrt(1.0 - a * a) * (i * xc)

    row = lax.broadcasted_iota(jnp.int32, a.shape, 0)
    s = 1
    while s < TT:
        keep = row >= s
        a_sh = jnp.where(keep, pltpu.roll(a, s, 0), 1.0)
        b_sh = jnp.where(keep, pltpu.roll(b, s, 0), 0.0)
        b = a * b_sh + b
        a = a * a_sh
        s *= 2
    h = b + a * h_scr[0:1]
    h_scr[...] = jnp.broadcast_to(h[TT - 1:TT], h_scr.shape)
    y_ref[...] = (h * jax.nn.gelu(gate_ref[...])).astype(BF16)


def _lru(proj_lru, conv_w, conv_b, wr_bd, b_r, wi_bd, b_i, lru_l, batch, t_pad):
    n_pad = proj_lru.shape[0]
    n_t = t_pad // TT
    n_c = D_LRU // LRU_CB
    vec = lambda: pl.BlockSpec((1, LRU_CB), lambda b, c, t: (0, c))
    return pl.pallas_call(
        _lru_kernel,
        grid=(batch, n_c, n_t),
        in_specs=[
            pl.BlockSpec((TT, LRU_CB), lambda b, c, t: (b * n_t + t, c)),
            pl.BlockSpec((TT, LRU_CB), lambda b, c, t: (b * n_t + t, n_c + c)),
            pl.BlockSpec((CONV_WIDTH, LRU_CB), lambda b, c, t: (0, c)),
            vec(),
            pl.BlockSpec((None, LRU_CB, LRU_CB), lambda b, c, t: (c, 0, 0)),
            vec(),
            pl.BlockSpec((None, LRU_CB, LRU_CB), lambda b, c, t: (c, 0, 0)),
            vec(),
            vec(),
        ],
        out_specs=pl.BlockSpec((TT, LRU_CB), lambda b, c, t: (b * n_t + t, c)),
        out_shape=jax.ShapeDtypeStruct((n_pad, D_LRU), BF16),
        scratch_shapes=[pltpu.VMEM((8, LRU_CB), F32), pltpu.VMEM((8, LRU_CB), F32)],
        compiler_params=pltpu.CompilerParams(
            dimension_semantics=("arbitrary", "arbitrary", "arbitrary"),
            vmem_limit_bytes=VMEM_LIMIT),
        name="rglru",
    )(proj_lru, proj_lru, conv_w, conv_b, wr_bd, b_r, wi_bd, b_i, lru_l)


def _bucket_thresholds():
    max_exact = N_BUCKETS // 2
    n = np.arange(0, MAX_DISTANCE + 1)
    nf = np.maximum(n, 1).astype(np.float32)
    large = max_exact + (np.log(nf / np.float32(max_exact)) / np.float32(math.log(MAX_DISTANCE / max_exact))
                         * np.float32(N_BUCKETS - max_exact)).astype(np.int32)
    large = np.minimum(large, N_BUCKETS - 1)
    bucket = np.where(n < max_exact, n, large)
    return [int(np.argmax(bucket >= k)) for k in range(1, N_BUCKETS)]


_BUCKET_THR = _bucket_thresholds()


def _attn_kernel(rb_ref, q_ref, k_ref, v_ref, lam_ref, g_ref, o_ref, bias_scr, vt_scr, m_scr, acc_scr):
    h = pl.program_id(0)
    b = pl.program_id(1)
    qi = pl.program_id(2)
    tk = TT
    n_kv = vt_scr.shape[0]

    @pl.when(jnp.logical_and(b == 0, qi == 0))
    def _():
        k_pos = lax.broadcasted_iota(jnp.int32, (tk, tk), 0)
        q_pos = lax.broadcasted_iota(jnp.int32, (tk, tk), 1)
        far = rb_ref[N_BUCKETS - 1, h]
        for d in range(2):
            rel = q_pos - k_pos + d * tk
            val = jnp.full((tk, tk), rb_ref[0, h] - far, F32)
            for kk, thr in enumerate(_BUCKET_THR):
                val = jnp.where(rel >= thr, rb_ref[kk + 1, h] - far, val)
            val = val * LOG2_E
            if d == 0:
                val = jnp.where(rel >= 0, val, -jnp.inf)
            bias_scr[d] = val

    @pl.when(qi == 0)
    def _():
        for t in range(n_kv):
            vt_scr[t, 0:V_HEAD_DIM, :] = v_ref[t * tk:(t + 1) * tk, :].astype(F32).T.astype(BF16)
            vt_scr[t, V_HEAD_DIM:VT_ROWS, :] = jnp.ones((VT_ROWS - V_HEAD_DIM, tk), BF16)

    q = q_ref[...]
    lane = lax.broadcasted_iota(jnp.int32, q.shape, 1)
    scale = (QK_HEAD_DIM ** -0.5) * LOG2_E
    qf = q.astype(F32) * scale
    qs = jnp.concatenate([jnp.where(lane < QK_HEAD_DIM, qf, 0.0),
                          jnp.where(lane >= QK_HEAD_DIM, qf, 0.0)], axis=0).astype(BF16)

    m_scr[...] = jnp.full(m_scr.shape, NEG_BIG, F32)
    acc_scr[...] = jnp.zeros_like(acc_scr)

    def tile_group(tiles):
        n_sub = 2 * tk // ATT_SUB
        stages = [(t, c) for t in range(len(tiles)) for c in range(n_sub)]

        def scores(t, c):
            j, bias_idx = tiles[t]
            kt = k_ref[pl.ds(pl.multiple_of(j * tk, tk), tk), :]
            cols = slice(c * ATT_SUB, (c + 1) * ATT_SUB)
            s = lax.dot_general(kt, qs[cols], (((1,), (1,)), ((), ())), preferred_element_type=F32)
            if bias_idx is not None:
                q0 = (c * ATT_SUB) % tk
                s = s + bias_scr[bias_idx, :, q0:q0 + ATT_SUB]
            return s

        pending = [scores(*st) for st in stages[:ATT_LOOKAHEAD]]
        for n, (t, c) in enumerate(stages):
            cols = slice(c * ATT_SUB, (c + 1) * ATT_SUB)
            s = pending.pop(0)
            if n + ATT_LOOKAHEAD < len(stages):
                pending.append(scores(*stages[n + ATT_LOOKAHEAD]))
            m_prev = m_scr[:, cols]
            m_new = jnp.maximum(m_prev, jnp.max(s, axis=0, keepdims=True))
            alpha = jnp.exp2(m_prev - m_new)
            p = jnp.exp2(s - m_new).astype(BF16)
            vt = vt_scr[tiles[t][0]]
            acc_scr[:, cols] = alpha * acc_scr[:, cols] + jnp.dot(vt, p, preferred_element_type=F32)
            m_scr[:, cols] = m_new

    n_far = jnp.maximum(qi - 1, 0)

    def far_pair(i, carry):
        tile_group([(2 * i, None), (2 * i + 1, None)])
        return carry

    lax.fori_loop(0, n_far // 2, far_pair, 0)

    @pl.when(n_far % 2 == 1)
    def _():
        tile_group([(n_far - 1, None)])

    @pl.when(qi >= 1)
    def _():
        tile_group([(qi - 1, 1), (qi, 0)])

    @pl.when(qi == 0)
    def _():
        tile_group([(qi, 0)])

    lam_p = lam_ref[...]
    lam = (jnp.exp(jnp.sum(lam_p[0:1] * lam_p[1:2], axis=-1, keepdims=True))
           - jnp.exp(jnp.sum(lam_p[2:3] * lam_p[3:4], axis=-1, keepdims=True)) + LAMBDA_INIT)
    acc = acc_scr[...]
    num = acc[0:V_HEAD_DIM]
    den = acc[V_HEAD_DIM:V_HEAD_DIM + 1]
    o_t = num[:, :tk] / den[:, :tk] - lam * (num[:, tk:] / den[:, tk:])
    o_ref[...] = (_rmsnorm(o_t.T, g_ref[...]) * (1.0 - LAMBDA_INIT)).astype(BF16)


def _attention(qkv, rel_bias, lam_params, subln_g, batch, t_pad):
    n_q = t_pad // TT
    qkv3 = qkv.reshape(batch, t_pad, 3 * D_ATT)
    out = pl.pallas_call(
        _attn_kernel,
        grid_spec=pltpu.PrefetchScalarGridSpec(
            num_scalar_prefetch=0,
            grid=(N_HEADS, batch, n_q),
            in_specs=[
                pl.BlockSpec(memory_space=pltpu.SMEM),
                pl.BlockSpec((None, TT, V_HEAD_DIM), lambda h, b, q: (b, q, h)),
                pl.BlockSpec((None, t_pad, V_HEAD_DIM), lambda h, b, q: (b, 0, N_HEADS + h)),
                pl.BlockSpec((None, t_pad, V_HEAD_DIM), lambda h, b, q: (b, 0, 2 * N_HEADS + h)),
                pl.BlockSpec((4, QK_HEAD_DIM), lambda h, b, q: (0, 0)),
                pl.BlockSpec((1, V_HEAD_DIM), lambda h, b, q: (0, 0)),
            ],
            out_specs=pl.BlockSpec((None, TT, V_HEAD_DIM), lambda h, b, q: (b, q, h)),
            scratch_shapes=[
                pltpu.VMEM((2, TT, TT), F32),
                pltpu.VMEM((n_q, VT_ROWS, TT), BF16),
                pltpu.VMEM((1, 2 * TT), F32),
                pltpu.VMEM((VT_ROWS, 2 * TT), F32),
            ],
        ),
        out_shape=jax.ShapeDtypeStruct((batch, t_pad, D_ATT), BF16),
        compiler_params=pltpu.CompilerParams(
            dimension_semantics=("arbitrary", "arbitrary", "arbitrary"),
            vmem_limit_bytes=VMEM_LIMIT),
        name="diff_attention",
    )(rel_bias, qkv3, qkv3, qkv3, lam_params, subln_g)
    return out.reshape(batch * t_pad, D_ATT)


def _outproj_kernel(yl_ref, ya_ref, h0_ref, w_ref, g_ref, wrt_ref, brt_ref, h1_ref, u_ref, lg_ref):
    acc = jnp.dot(yl_ref[...], w_ref[0:D_LRU, :], preferred_element_type=F32)
    acc = acc + jnp.dot(ya_ref[...], w_ref[D_LRU:, :], preferred_element_type=F32)
    h1 = h0_ref[...] + acc
    h1_ref[...] = h1
    u = _rmsnorm(h1, g_ref[...])
    u_ref[...] = u
    u_hi = u.astype(BF16)
    u_lo = (u - u_hi.astype(F32)).astype(BF16)
    w = wrt_ref[...]
    w_hi = w.astype(BF16)
    w_lo = (w - w_hi.astype(F32)).astype(BF16)
    lg = jnp.dot(u_hi, w_hi, preferred_element_type=F32)
    lg = lg + jnp.dot(u_lo, w_hi, preferred_element_type=F32)
    lg = lg + jnp.dot(u_hi, w_lo, preferred_element_type=F32)
    lg_ref[...] = lg + brt_ref[...]


def _out_proj(y_lru, y_att, h0, w_out_bf16, g_ffn, w_rt, b_rt):
    n_pad = h0.shape[0]
    row = lambda w: pl.BlockSpec((OUT_TM, w), lambda m: (m, 0))
    full = lambda a, b: pl.BlockSpec((a, b), lambda m: (0, 0))
    return pl.pallas_call(
        _outproj_kernel,
        grid=(n_pad // OUT_TM,),
        in_specs=[row(D_LRU), row(D_ATT), row(D_MODEL), full(D_MODEL, D_MODEL), full(1, D_MODEL),
                  full(D_MODEL, LANES), full(1, LANES)],
        out_specs=[row(D_MODEL), row(D_MODEL), row(LANES)],
        out_shape=[
            jax.ShapeDtypeStruct((n_pad, D_MODEL), F32),
            jax.ShapeDtypeStruct((n_pad, D_MODEL), F32),
            jax.ShapeDtypeStruct((n_pad, LANES), F32),
        ],
        compiler_params=pltpu.CompilerParams(
            dimension_semantics=("arbitrary",), vmem_limit_bytes=VMEM_LIMIT),
        name="out_proj",
    )(y_lru, y_att, h0, w_out_bf16, g_ffn, w_rt, b_rt)


def _route_kernel(lg_ref, idx_ref, gate_ref, cnt_ref, carry_scr):
    step = pl.program_id(0)

    @pl.when(step == 0)
    def _():
        carry_scr[...] = jnp.zeros_like(carry_scr)

    lg = lg_ref[...]
    lane = lax.broadcasted_iota(jnp.int32, lg.shape, 1)
    first = lambda mask: jnp.min(jnp.where(mask, lane, LANES), axis=-1, keepdims=True)

    is_g = lane < N_GROUPS
    gl = jnp.where(is_g, lg, -jnp.inf)
    gmax = jnp.max(gl, axis=-1, keepdims=True)
    grp = first(gl == gmax)
    gsum = jnp.sum(jnp.where(is_g, jnp.exp(gl - gmax), 0.0), axis=-1, keepdims=True)
    p_grp = 1.0 / gsum

    lane_e = lane - N_GROUPS
    in_grp = (lane_e >= 0) & (lane_e < N_EXPERTS) & ((lane_e // EXPERTS_PER_GROUP) == grp)
    el = jnp.where(in_grp, lg, -jnp.inf)
    v1 = jnp.max(el, axis=-1, keepdims=True)
    i1 = first(el == v1)
    el2 = jnp.where(lane == i1, -jnp.inf, el)
    v2 = jnp.max(el2, axis=-1, keepdims=True)
    i2 = first(el2 == v2)
    e2 = jnp.exp(v2 - v1)
    den = 1.0 + e2
    g0 = p_grp * (1.0 / den)
    g1 = p_grp * (e2 / den)

    hit1 = lane == i1
    hit2 = lane == i2
    onehot = jnp.where(hit1 | hit2, 1.0, 0.0)
    rr = lax.broadcasted_iota(jnp.int32, (TT, TT), 0)
    cc = lax.broadcasted_iota(jnp.int32, (TT, TT), 1)
    tri = jnp.where(cc < rr, 1.0, 0.0).astype(BF16)
    rank = jnp.dot(tri, onehot.astype(BF16), preferred_element_type=F32) + carry_scr[...]
    r0 = jnp.sum(jnp.where(hit1, rank, 0.0), axis=-1, keepdims=True).astype(jnp.int32)
    r1 = jnp.sum(jnp.where(hit2, rank, 0.0), axis=-1, keepdims=True).astype(jnp.int32)
    carry_scr[...] = carry_scr[...] + jnp.sum(onehot, axis=0, keepdims=True)

    idx_ref[...] = jnp.where(lane == 0, i1 - N_GROUPS,
                             jnp.where(lane == 1, i2 - N_GROUPS,
                                       jnp.where(lane == 2, r0, jnp.where(lane == 3, r1, 0))))
    gate_ref[...] = jnp.where(lane == 0, g0, jnp.where(lane == 1, g1, 0.0))
    cnt_ref[...] = carry_scr[...]


def _route(logits):
    n_pad = logits.shape[0]
    row = pl.BlockSpec((TT, LANES), lambda m: (m, 0))
    return pl.pallas_call(
        _route_kernel,
        grid=(n_pad // TT,),
        in_specs=[row],
        out_specs=[row, row, pl.BlockSpec((1, LANES), lambda m: (0, 0))],
        out_shape=[
            jax.ShapeDtypeStruct((n_pad, LANES), jnp.int32),
            jax.ShapeDtypeStruct((n_pad, LANES), F32),
            jax.ShapeDtypeStruct((1, LANES), F32),
        ],
        scratch_shapes=[pltpu.VMEM((1, LANES), F32)],
        compiler_params=pltpu.CompilerParams(
            dimension_semantics=("arbitrary",), vmem_limit_bytes=VMEM_LIMIT),
        name="route",
    )(logits)


def _dest_kernel(idx_ref, ps_ref, o_ref):
    idx = idx_ref[...]
    lane = lax.broadcasted_iota(jnp.int32, idx.shape, 1)
    ps = ps_ref[...]
    pick = lambda e: jnp.sum(jnp.where(lane == e, ps, 0.0), axis=-1, keepdims=True).astype(jnp.int32)
    d0 = pick(idx[:, 0:1]) + idx[:, 2:3]
    d1 = pick(idx[:, 1:2]) + idx[:, 3:4]
    o_ref[...] = jnp.where(lane == 0, d0, jnp.where(lane == 1, d1, 0))


def _dest(idx, pstart_row):
    n_pad = idx.shape[0]
    row = pl.BlockSpec((TT, LANES), lambda m: (m, 0))
    return pl.pallas_call(
        _dest_kernel,
        grid=(n_pad // TT,),
        in_specs=[row, pl.BlockSpec((1, LANES), lambda m: (0, 0))],
        out_specs=row,
        out_shape=jax.ShapeDtypeStruct((n_pad, LANES), jnp.int32),
        compiler_params=pltpu.CompilerParams(
            dimension_semantics=("arbitrary",), vmem_limit_bytes=VMEM_LIMIT),
        name="dest",
    )(idx, pstart_row)


def _row_copy(src, s_row, dst, d_row, sem):
    return pltpu.make_async_copy(src.at[pl.ds(s_row, 1)], dst.at[pl.ds(d_row, 1)], sem)


def _dispatch_kernel(d0_ref, d1_ref, u_ref, xz_ref, xb_ref, sem):
    del xz_ref
    base = pl.program_id(0) * TT

    def issue(r, carry):
        _row_copy(u_ref, r, xb_ref, d0_ref[base + r], sem.at[0]).start()
        _row_copy(u_ref, r, xb_ref, d1_ref[base + r], sem.at[1]).start()
        return carry

    lax.fori_loop(0, TT, issue, 0)

    def drain(r, carry):
        _row_copy(u_ref, r, xb_ref, d0_ref[base + r], sem.at[0]).wait()
        _row_copy(u_ref, r, xb_ref, d1_ref[base + r], sem.at[1]).wait()
        return carry

    lax.fori_loop(0, TT, drain, 0)


def _dispatch(d0, d1, u2, rows):
    n_pad = u2.shape[0]
    zeros = jnp.zeros((rows, D_MODEL), F32)
    return pl.pallas_call(
        _dispatch_kernel,
        grid_spec=pltpu.PrefetchScalarGridSpec(
            num_scalar_prefetch=2,
            grid=(n_pad // TT,),
            in_specs=[
                pl.BlockSpec((TT, D_MODEL), lambda m, d0, d1: (m, 0)),
                pl.BlockSpec(memory_space=pl.ANY),
            ],
            out_specs=pl.BlockSpec(memory_space=pl.ANY),
            scratch_shapes=[pltpu.SemaphoreType.DMA((2,))],
        ),
        out_shape=jax.ShapeDtypeStruct((rows, D_MODEL), F32),
        input_output_aliases={3: 0},
        compiler_params=pltpu.CompilerParams(
            dimension_semantics=("arbitrary",), vmem_limit_bytes=VMEM_LIMIT),
        name="dispatch",
    )(d0, d1, u2, zeros)


def _expert_kernel(be_ref, nu_ref, x_ref, wg_ref, wu_ref, wd_ref, y_ref):
    del be_ref
    i = pl.program_id(0)

    @pl.when(i < nu_ref[0])
    def _():
        x = x_ref[...].astype(BF16)
        g = jnp.dot(x, wg_ref[...], preferred_element_type=F32)
        u = jnp.dot(x, wu_ref[...], preferred_element_type=F32)
        a = (jax.nn.silu(g) * u).astype(BF16)
        y_ref[...] = jnp.dot(a, wd_ref[...], preferred_element_type=F32)

    @pl.when(i >= nu_ref[0])
    def _():
        y_ref[...] = jnp.zeros_like(y_ref)


def _experts(blk_expert, n_used, x_buf, w_gate, w_up, w_down):
    rows = x_buf.shape[0]
    return pl.pallas_call(
        _expert_kernel,
        grid_spec=pltpu.PrefetchScalarGridSpec(
            num_scalar_prefetch=2,
            grid=(rows // DISPATCH_BLK,),
            in_specs=[
                pl.BlockSpec((DISPATCH_BLK, D_MODEL), lambda i, be, nu: (i, 0)),
                pl.BlockSpec((None, D_MODEL, D_EXPERT), lambda i, be, nu: (be[i], 0, 0)),
                pl.BlockSpec((None, D_MODEL, D_EXPERT), lambda i, be, nu: (be[i], 0, 0)),
                pl.BlockSpec((None, D_EXPERT, D_MODEL), lambda i, be, nu: (be[i], 0, 0)),
            ],
            out_specs=pl.BlockSpec((DISPATCH_BLK, D_MODEL), lambda i, be, nu: (i, 0)),
        ),
        out_shape=jax.ShapeDtypeStruct((rows, D_MODEL), F32),
        compiler_params=pltpu.CompilerParams(
            dimension_semantics=("arbitrary",), vmem_limit_bytes=VMEM_LIMIT),
        name="experts",
    )(blk_expert, n_used, x_buf, w_gate, w_up, w_down)


def _combine_kernel(d0_ref, d1_ref, h_ref, gate_ref, g_ref, yb_ref, o_ref, y0_scr, y1_scr, sem):
    base = pl.program_id(0) * TT

    def issue(r, carry):
        _row_copy(yb_ref, d0_ref[base + r], y0_scr, r, sem.at[0]).start()
        _row_copy(yb_ref, d1_ref[base + r], y1_scr, r, sem.at[1]).start()
        return carry

    lax.fori_loop(0, TT, issue, 0)

    def drain(r, carry):
        _row_copy(yb_ref, d0_ref[base + r], y0_scr, r, sem.at[0]).wait()
        _row_copy(yb_ref, d1_ref[base + r], y1_scr, r, sem.at[1]).wait()
        return carry

    lax.fori_loop(0, TT, drain, 0)

    gate = gate_ref[...]
    h = h_ref[...] + gate[:, 0:1] * y0_scr[...] + gate[:, 1:2] * y1_scr[...]
    o_ref[...] = _rmsnorm(h, g_ref[...])


def _combine(d0, d1, h1, gates, g_final, y_buf):
    n_pad = h1.shape[0]
    return pl.pallas_call(
        _combine_kernel,
        grid_spec=pltpu.PrefetchScalarGridSpec(
            num_scalar_prefetch=2,
            grid=(n_pad // TT,),
            in_specs=[
                pl.BlockSpec((TT, D_MODEL), lambda m, d0, d1: (m, 0)),
                pl.BlockSpec((TT, LANES), lambda m, d0, d1: (m, 0)),
                pl.BlockSpec((1, D_MODEL), lambda m, d0, d1: (0, 0)),
                pl.BlockSpec(memory_space=pl.ANY),
            ],
            out_specs=pl.BlockSpec((TT, D_MODEL), lambda m, d0, d1: (m, 0)),
            scratch_shapes=[
                pltpu.VMEM((TT, D_MODEL), F32),
                pltpu.VMEM((TT, D_MODEL), F32),
                pltpu.SemaphoreType.DMA((2,)),
            ],
        ),
        out_shape=jax.ShapeDtypeStruct((n_pad, D_MODEL), F32),
        compiler_params=pltpu.CompilerParams(
            dimension_semantics=("arbitrary",), vmem_limit_bytes=VMEM_LIMIT),
        name="combine",
    )(d0, d1, h1, gates, g_final, y_buf)


def _block_diag(w):
    per = LRU_CB // LRU_BLOCK_DIM
    w4 = w.reshape(LRU_BLOCKS // per, per, LRU_BLOCK_DIM, LRU_BLOCK_DIM)
    eye = jnp.eye(per, dtype=w.dtype)
    bd = jnp.einsum('cpij,pq->cpiqj', w4, eye)
    return bd.reshape(LRU_BLOCKS // per, LRU_CB, LRU_CB)


def kernel(x, meta_tokens, g_mix, w_in, conv_w, conv_b, w_rgate, b_rgate, w_igate, b_igate, lru_L, lambda_q1, lambda_k1, lambda_q2, lambda_k2, subln_g, rel_bias, w_out, g_ffn, w_group, b_group, w_router, b_router, w_gate, w_up, w_down, g_final):
    batch, seq, _ = x.shape
    t_real = N_META + seq
    t_pad = _round_up(t_real, TT)
    n_pad = batch * t_pad

    meta = jnp.broadcast_to(meta_tokens[None].astype(x.dtype), (batch, N_META, D_MODEL))
    h0 = jnp.concatenate([meta, x, jnp.zeros((batch, t_pad - t_real, D_MODEL), x.dtype)], axis=1)
    h0 = h0.reshape(n_pad, D_MODEL)

    proj_lru, proj_qkv = _in_proj(h0, g_mix[0][None], w_in[0].astype(BF16))

    y_lru = _lru(proj_lru, conv_w[0], conv_b[0][None],
                 _block_diag(w_rgate[0]).astype(BF16), b_rgate[0].reshape(1, D_LRU),
                 _block_diag(w_igate[0]).astype(BF16), b_igate[0].reshape(1, D_LRU),
                 lru_L[0][None], batch, t_pad)

    lam_params = jnp.stack([lambda_q1[0], lambda_k1[0], lambda_q2[0], lambda_k2[0]])
    y_att = _attention(proj_qkv, rel_bias, lam_params, subln_g[0][None], batch, t_pad)

    w_rt = jnp.concatenate([w_group[0], w_router[0],
                            jnp.zeros((D_MODEL, LANES - N_GROUPS - N_EXPERTS), F32)], axis=1)
    b_rt = jnp.concatenate([b_group[0], b_router[0],
                            jnp.zeros((LANES - N_GROUPS - N_EXPERTS,), F32)])[None]
    h1, u2, logits = _out_proj(y_lru, y_att, h0, w_out[0].astype(BF16), g_ffn[0][None], w_rt, b_rt)

    idx, gates, counts = _route(logits)

    cnt = counts[0, N_GROUPS:N_GROUPS + N_EXPERTS].astype(jnp.int32)
    padded = (cnt + DISPATCH_BLK - 1) // DISPATCH_BLK * DISPATCH_BLK
    pends = jnp.cumsum(padded)
    pstarts = pends - padded
    dest = _dest(idx, jnp.pad(pstarts.astype(F32), (0, LANES - N_EXPERTS))[None])
    d0 = dest[:, 0]
    d1 = dest[:, 1]
    n_blocks = -(-(2 * n_pad + N_EXPERTS * (DISPATCH_BLK - 1)) // DISPATCH_BLK)
    rows = n_blocks * DISPATCH_BLK
    n_used = pends[-1] // DISPATCH_BLK
    blk = jnp.arange(n_blocks, dtype=jnp.int32)
    blk_c = jnp.minimum(blk, jnp.maximum(n_used - 1, 0))
    blk_expert = jnp.sum((pends[None, :] <= (blk_c * DISPATCH_BLK)[:, None]).astype(jnp.int32), axis=1)
    blk_expert = jnp.minimum(blk_expert, N_EXPERTS - 1)

    x_buf = _dispatch(d0, d1, u2, rows)
    y_buf = _experts(blk_expert, n_used[None].astype(jnp.int32), x_buf,
                     w_gate[0].astype(BF16), w_up[0].astype(BF16), w_down[0].astype(BF16))
    out = _combine(d0, d1, h1, gates, g_final[None], y_buf)
    return out.reshape(batch, t_pad, D_MODEL)[:, N_META:t_real]
```

```python
import functools
import math

import numpy as np
import jax
import jax.numpy as jnp
from jax import lax
from jax.experimental import pallas as pl
from jax.experimental.pallas import tpu as pltpu

D_MODEL = 2048
N_META = 16
D_LRU = 1024
D_ATT = 1024
LRU_BLOCKS = 16
LRU_BLOCK_DIM = 64
CONV_WIDTH = 4
LRU_C = 8.0
N_HEADS = 8
V_HEAD_DIM = 128
QK_HEAD_DIM = 64
N_BUCKETS = 32
MAX_DISTANCE = 128
N_GROUPS = 4
EXPERTS_PER_GROUP = 8
N_EXPERTS = 32
D_EXPERT = 1024
D_IN_PROJ = 2 * D_LRU + 3 * D_ATT
EPS = 1e-6
LAMBDA_INIT = 0.8 - 0.6 * math.exp(-0.3 * 0)
LOG2_E = math.log2(math.e)

F32 = jnp.float32
BF16 = jnp.bfloat16

LANES = 128
MXU_DIM = 256
TT = 768
PROJ_TN = 1024
OUT_TM = TT // 2
ATT_SUB = MXU_DIM
ATT_LOOKAHEAD = 3
VT_ROWS = V_HEAD_DIM + 16
LRU_CB = MXU_DIM
DISPATCH_BLK = 256
VMEM_LIMIT = 56 * 1024 * 1024
NEG_BIG = -1e30


def _round_up(a, b):
    return -(-a // b) * b


def _rmsnorm(x, g):
    ms = jnp.mean(x * x, axis=-1, keepdims=True)
    return x * lax.rsqrt(ms + EPS) * g


def _inproj_kernel(x_ref, g_ref, w_ref, lru_ref, qkv_ref, u_scr):
    n = pl.program_id(1)

    @pl.when(n == 0)
    def _():
        u_scr[...] = _rmsnorm(x_ref[...], g_ref[...]).astype(BF16)

    y = jnp.dot(u_scr[...], w_ref[...], preferred_element_type=F32)

    @pl.when(n < 2)
    def _():
        lru_ref[...] = y

    @pl.when(n >= 2)
    def _():
        qkv_ref[...] = y.astype(BF16)


def _in_proj(h0, g_mix, w_in_bf16):
    n_pad = h0.shape[0]
    n_col = D_IN_PROJ // PROJ_TN
    return pl.pallas_call(
        _inproj_kernel,
        grid=(n_pad // TT, n_col),
        in_specs=[
            pl.BlockSpec((TT, D_MODEL), lambda m, n: (m, 0)),
            pl.BlockSpec((1, D_MODEL), lambda m, n: (0, 0)),
            pl.BlockSpec((D_MODEL, PROJ_TN), lambda m, n: (0, n)),
        ],
        out_specs=[
            pl.BlockSpec((TT, PROJ_TN), lambda m, n: (m, jnp.minimum(n, 1))),
            pl.BlockSpec((TT, PROJ_TN), lambda m, n: (m, jnp.maximum(n - 2, 0))),
        ],
        out_shape=[
            jax.ShapeDtypeStruct((n_pad, 2 * D_LRU), F32),
            jax.ShapeDtypeStruct((n_pad, 3 * D_ATT), BF16),
        ],
        scratch_shapes=[pltpu.VMEM((TT, D_MODEL), BF16)],
        compiler_params=pltpu.CompilerParams(
            dimension_semantics=("arbitrary", "arbitrary"), vmem_limit_bytes=VMEM_LIMIT),
        name="in_proj",
    )(h0, g_mix, w_in_bf16)


def _shift_rows(x, tail, s):
    r = pltpu.roll(x, s, 0)
    row8 = lax.broadcasted_iota(jnp.int32, tail.shape, 0)
    head = jnp.where(row8 < s, pltpu.roll(tail, s, 0), r[0:8])
    return jnp.concatenate([head, r[8:]], axis=0)


def _lru_kernel(x_ref, gate_ref, cw_ref, cb_ref, wr_ref, br_ref, wi_ref, bi_ref, lam_ref,
                y_ref, tail_scr, h_scr):
    t = pl.program_id(2)

    @pl.when(t == 0)
    def _():
        tail_scr[...] = jnp.zeros_like(tail_scr)
        h_scr[...] = jnp.zeros_like(h_scr)

    x = x_ref[...]
    tail = tail_scr[...]
    cw = cw_ref[...]
    xc = cb_ref[...] + x * cw[CONV_WIDTH - 1:CONV_WIDTH]
    for k in range(CONV_WIDTH - 1):
        xc = xc + _shift_rows(x, tail, CONV_WIDTH - 1 - k) * cw[k:k + 1]
    tail_scr[...] = x[TT - 8:TT]

    xb = xc.astype(BF16)
    r = jax.nn.sigmoid(jnp.dot(xb, wr_ref[...], preferred_element_type=F32) + br_ref[...])
    i = jax.nn.sigmoid(jnp.dot(xb, wi_ref[...], preferred_element_type=F32) + bi_ref[...])
    lam = lam_ref[...]
    log_sig = jnp.minimum(lam, 0.0) - jnp.log(1.0 + jnp.exp(-jnp.abs(lam)))
    log_a = (LRU_C * r) * log_sig
    a = jnp.exp(log_a)
    b = jnp.sqrt(1.0 - a * a) * (i * xc)

    row = lax.broadcasted_iota(jnp.int32, a.shape, 0)
    s = 1
    while s < TT:
        keep = row >= s
        a_sh = jnp.where(keep, pltpu.roll(a, s, 0), 1.0)
        b_sh = jnp.where(keep, pltpu.roll(b, s, 0), 0.0)
        b = a * b_sh + b
        a = a * a_sh
        s *= 2
    h = b + a * h_scr[0:1]
    h_scr[...] = jnp.broadcast_to(h[TT - 1:TT], h_scr.shape)
    y_ref[...] = (h * jax.nn.gelu(gate_ref[...])).astype(BF16)


def _lru(proj_lru, conv_w, conv_b, wr_bd, b_r, wi_bd, b_i, lru_l, batch, t_pad):
    n_pad = proj_lru.shape[0]
    n_t = t_pad // TT
    n_c = D_LRU // LRU_CB
    vec = lambda: pl.BlockSpec((1, LRU_CB), lambda b, c, t: (0, c))
    return pl.pallas_call(
        _lru_kernel,
        grid=(batch, n_c, n_t),
        in_specs=[
            pl.BlockSpec((TT, LRU_CB), lambda b, c, t: (b * n_t + t, c)),
            pl.BlockSpec((TT, LRU_CB), lambda b, c, t: (b * n_t + t, n_c + c)),
            pl.BlockSpec((CONV_WIDTH, LRU_CB), lambda b, c, t: (0, c)),
            vec(),
            pl.BlockSpec((None, LRU_CB, LRU_CB), lambda b, c, t: (c, 0, 0)),
            vec(),
            pl.BlockSpec((None, LRU_CB, LRU_CB), lambda b, c, t: (c, 0, 0)),
            vec(),
            vec(),
        ],
        out_specs=pl.BlockSpec((TT, LRU_CB), lambda b, c, t: (b * n_t + t, c)),
        out_shape=jax.ShapeDtypeStruct((n_pad, D_LRU), BF16),
        scratch_shapes=[pltpu.VMEM((8, LRU_CB), F32), pltpu.VMEM((8, LRU_CB), F32)],
        compiler_params=pltpu.CompilerParams(
            dimension_semantics=("arbitrary", "arbitrary", "arbitrary"),
            vmem_limit_bytes=VMEM_LIMIT),
        name="rglru",
    )(proj_lru, proj_lru, conv_w, conv_b, wr_bd, b_r, wi_bd, b_i, lru_l)


def _bucket_thresholds():
    max_exact = N_BUCKETS // 2
    n = np.arange(0, MAX_DISTANCE + 1)
    nf = np.maximum(n, 1).astype(np.float32)
    large = max_exact + (np.log(nf / np.float32(max_exact)) / np.float32(math.log(MAX_DISTANCE / max_exact))
                         * np.float32(N_BUCKETS - max_exact)).astype(np.int32)
    large = np.minimum(large, N_BUCKETS - 1)
    bucket = np.where(n < max_exact, n, large)
    return [int(np.argmax(bucket >= k)) for k in range(1, N_BUCKETS)]


_BUCKET_THR = _bucket_thresholds()


def _attn_kernel(rb_ref, q_ref, k_ref, v_ref, lam_ref, g_ref, o_ref, bias_scr, vt_scr, m_scr, acc_scr):
    h = pl.program_id(0)
    b = pl.program_id(1)
    qi = pl.program_id(2)
    tk = TT
    n_kv = vt_scr.shape[0]

    @pl.when(jnp.logical_and(b == 0, qi == 0))
    def _():
        k_pos = lax.broadcasted_iota(jnp.int32, (tk, tk), 0)
        q_pos = lax.broadcasted_iota(jnp.int32, (tk, tk), 1)
        far = rb_ref[N_BUCKETS - 1, h]
        for d in range(2):
            rel = q_pos - k_pos + d * tk
            val = jnp.full((tk, tk), rb_ref[0, h] - far, F32)
            for kk, thr in enumerate(_BUCKET_THR):
                val = jnp.where(rel >= thr, rb_ref[kk + 1, h] - far, val)
            val = val * LOG2_E
            if d == 0:
                val = jnp.where(rel >= 0, val, -jnp.inf)
            bias_scr[d] = val

    @pl.when(qi == 0)
    def _():
        for t in range(n_kv):
            vt_scr[t, 0:V_HEAD_DIM, :] = v_ref[t * tk:(t + 1) * tk, :].astype(F32).T.astype(BF16)
            vt_scr[t, V_HEAD_DIM:VT_ROWS, :] = jnp.ones((VT_ROWS - V_HEAD_DIM, tk), BF16)

    q = q_ref[...]
    lane = lax.broadcasted_iota(jnp.int32, q.shape, 1)
    scale = (QK_HEAD_DIM ** -0.5) * LOG2_E
    qf = q.astype(F32) * scale
    qs = jnp.concatenate([jnp.where(lane < QK_HEAD_DIM, qf, 0.0),
                          jnp.where(lane >= QK_HEAD_DIM, qf, 0.0)], axis=0).astype(BF16)

    m_scr[...] = jnp.full(m_scr.shape, NEG_BIG, F32)
    acc_scr[...] = jnp.zeros_like(acc_scr)

    def tile_group(tiles):
        n_sub = 2 * tk // ATT_SUB
        stages = [(t, c) for t in range(len(tiles)) for c in range(n_sub)]

        def scores(t, c):
            j, bias_idx = tiles[t]
            kt = k_ref[pl.ds(pl.multiple_of(j * tk, tk), tk), :]
            cols = slice(c * ATT_SUB, (c + 1) * ATT_SUB)
            s = lax.dot_general(kt, qs[cols], (((1,), (1,)), ((), ())), preferred_element_type=F32)
            if bias_idx is not None:
                q0 = (c * ATT_SUB) % tk
                s = s + bias_scr[bias_idx, :, q0:q0 + ATT_SUB]
            return s

        pending = [scores(*st) for st in stages[:ATT_LOOKAHEAD]]
        for n, (t, c) in enumerate(stages):
            cols = slice(c * ATT_SUB, (c + 1) * ATT_SUB)
            s = pending.pop(0)
            if n + ATT_LOOKAHEAD < len(stages):
                pending.append(scores(*stages[n + ATT_LOOKAHEAD]))
            m_prev = m_scr[:, cols]
            m_new = jnp.maximum(m_prev, jnp.max(s, axis=0, keepdims=True))
            alpha = jnp.exp2(m_prev - m_new)
            p = jnp.exp2(s - m_new).astype(BF16)
            vt = vt_scr[tiles[t][0]]
            acc_scr[:, cols] = alpha * acc_scr[:, cols] + jnp.dot(vt, p, preferred_element_type=F32)
            m_scr[:, cols] = m_new

    n_far = jnp.maximum(qi - 1, 0)

    def far_pair(i, carry):
        tile_group([(2 * i, None), (2 * i + 1, None)])
        return carry

    lax.fori_loop(0, n_far // 2, far_pair, 0)

    @pl.when(n_far % 2 == 1)
    def _():
        tile_group([(n_far - 1, None)])

    @pl.when(qi >= 1)
    def _():
        tile_group([(qi - 1, 1), (qi, 0)])

    @pl.when(qi == 0)
    def _():
        tile_group([(qi, 0)])

    lam_p = lam_ref[...]
    lam = (jnp.exp(jnp.sum(lam_p[0:1] * lam_p[1:2], axis=-1, keepdims=True))
           - jnp.exp(jnp.sum(lam_p[2:3] * lam_p[3:4], axis=-1, keepdims=True)) + LAMBDA_INIT)
    acc = acc_scr[...]
    num = acc[0:V_HEAD_DIM]
    den = acc[V_HEAD_DIM:V_HEAD_DIM + 1]
    o_t = num[:, :tk] / den[:, :tk] - lam * (num[:, tk:] / den[:, tk:])
    o_ref[...] = (_rmsnorm(o_t.T, g_ref[...]) * (1.0 - LAMBDA_INIT)).astype(BF16)


def _attention(qkv, rel_bias, lam_params, subln_g, batch, t_pad):
    n_q = t_pad // TT
    qkv3 = qkv.reshape(batch, t_pad, 3 * D_ATT)
    out = pl.pallas_call(
        _attn_kernel,
        grid_spec=pltpu.PrefetchScalarGridSpec(
            num_scalar_prefetch=0,
            grid=(N_HEADS, batch, n_q),
            in_specs=[
                pl.BlockSpec(memory_space=pltpu.SMEM),
                pl.BlockSpec((None, TT, V_HEAD_DIM), lambda h, b, q: (b, q, h)),
                pl.BlockSpec((None, t_pad, V_HEAD_DIM), lambda h, b, q: (b, 0, N_HEADS + h)),
                pl.BlockSpec((None, t_pad, V_HEAD_DIM), lambda h, b, q: (b, 0, 2 * N_HEADS + h)),
                pl.BlockSpec((4, QK_HEAD_DIM), lambda h, b, q: (0, 0)),
                pl.BlockSpec((1, V_HEAD_DIM), lambda h, b, q: (0, 0)),
            ],
            out_specs=pl.BlockSpec((None, TT, V_HEAD_DIM), lambda h, b, q: (b, q, h)),
            scratch_shapes=[
                pltpu.VMEM((2, TT, TT), F32),
                pltpu.VMEM((n_q, VT_ROWS, TT), BF16),
                pltpu.VMEM((1, 2 * TT), F32),
                pltpu.VMEM((VT_ROWS, 2 * TT), F32),
            ],
        ),
        out_shape=jax.ShapeDtypeStruct((batch, t_pad, D_ATT), BF16),
        compiler_params=pltpu.CompilerParams(
            dimension_semantics=("arbitrary", "arbitrary", "arbitrary"),
            vmem_limit_bytes=VMEM_LIMIT),
        name="diff_attention",
    )(rel_bias, qkv3, qkv3, qkv3, lam_params, subln_g)
    return out.reshape(batch * t_pad, D_ATT)


def _outproj_kernel(yl_ref, ya_ref, h0_ref, w_ref, g_ref, wrt_ref, brt_ref, h1_ref, u_ref, lg_ref):
    acc = jnp.dot(yl_ref[...], w_ref[0:D_LRU, :], preferred_element_type=F32)
    acc = acc + jnp.dot(ya_ref[...], w_ref[D_LRU:, :], preferred_element_type=F32)
    h1 = h0_ref[...] + acc
    h1_ref[...] = h1
    u = _rmsnorm(h1, g_ref[...])
    u_ref[...] = u
    u_hi = u.astype(BF16)
    u_lo = (u - u_hi.astype(F32)).astype(BF16)
    w = wrt_ref[...]
    w_hi = w.astype(BF16)
    w_lo = (w - w_hi.astype(F32)).astype(BF16)
    lg = jnp.dot(u_hi, w_hi, preferred_element_type=F32)
    lg = lg + jnp.dot(u_lo, w_hi, preferred_element_type=F32)
    lg = lg + jnp.dot(u_hi, w_lo, preferred_element_type=F32)
    lg_ref[...] = lg + brt_ref[...]


def _out_proj(y_lru, y_att, h0, w_out_bf16, g_ffn, w_rt, b_rt):
    n_pad = h0.shape[0]
    row = lambda w: pl.BlockSpec((OUT_TM, w), lambda m: (m, 0))
    full = lambda a, b: pl.BlockSpec((a, b), lambda m: (0, 0))
    return pl.pallas_call(
        _outproj_kernel,
        grid=(n_pad // OUT_TM,),
        in_specs=[row(D_LRU), row(D_ATT), row(D_MODEL), full(D_MODEL, D_MODEL), full(1, D_MODEL),
                  full(D_MODEL, LANES), full(1, LANES)],
        out_specs=[row(D_MODEL), row(D_MODEL), row(LANES)],
        out_shape=[
            jax.ShapeDtypeStruct((n_pad, D_MODEL), F32),
            jax.ShapeDtypeStruct((n_pad, D_MODEL), F32),
            jax.ShapeDtypeStruct((n_pad, LANES), F32),
        ],
        compiler_params=pltpu.CompilerParams(
            dimension_semantics=("arbitrary",), vmem_limit_bytes=VMEM_LIMIT),
        name="out_proj",
    )(y_lru, y_att, h0, w_out_bf16, g_ffn, w_rt, b_rt)


def _route_kernel(lg_ref, idx_ref, gate_ref, cnt_ref, carry_scr):
    step = pl.program_id(0)

    @pl.when(step == 0)
    def _():
        carry_scr[...] = jnp.zeros_like(carry_scr)

    lg = lg_ref[...]
    lane = lax.broadcasted_iota(jnp.int32, lg.shape, 1)
    first = lambda mask: jnp.min(jnp.where(mask, lane, LANES), axis=-1, keepdims=True)

    is_g = lane < N_GROUPS
    gl = jnp.where(is_g, lg, -jnp.inf)
    gmax = jnp.max(gl, axis=-1, keepdims=True)
    grp = first(gl == gmax)
    gsum = jnp.sum(jnp.where(is_g, jnp.exp(gl - gmax), 0.0), axis=-1, keepdims=True)
    p_grp = 1.0 / gsum

    lane_e = lane - N_GROUPS
    in_grp = (lane_e >= 0) & (lane_e < N_EXPERTS) & ((lane_e // EXPERTS_PER_GROUP) == grp)
    el = jnp.where(in_grp, lg, -jnp.inf)
    v1 = jnp.max(el, axis=-1, keepdims=True)
    i1 = first(el == v1)
    el2 = jnp.where(lane == i1, -jnp.inf, el)
    v2 = jnp.max(el2, axis=-1, keepdims=True)
    i2 = first(el2 == v2)
    e2 = jnp.exp(v2 - v1)
    den = 1.0 + e2
    g0 = p_grp * (1.0 / den)
    g1 = p_grp * (e2 / den)

    hit1 = lane == i1
    hit2 = lane == i2
    onehot = jnp.where(hit1 | hit2, 1.0, 0.0)
    rr = lax.broadcasted_iota(jnp.int32, (TT, TT), 0)
    cc = lax.broadcasted_iota(jnp.int32, (TT, TT), 1)
    tri = jnp.where(cc < rr, 1.0, 0.0).astype(BF16)
    rank = jnp.dot(tri, onehot.astype(BF16), preferred_element_type=F32) + carry_scr[...]
    r0 = jnp.sum(jnp.where(hit1, rank, 0.0), axis=-1, keepdims=True).astype(jnp.int32)
    r1 = jnp.sum(jnp.where(hit2, rank, 0.0), axis=-1, keepdims=True).astype(jnp.int32)
    carry_scr[...] = carry_scr[...] + jnp.sum(onehot, axis=0, keepdims=True)

    idx_ref[...] = jnp.where(lane == 0, i1 - N_GROUPS,
                             jnp.where(lane == 1, i2 - N_GROUPS,
                                       jnp.where(lane == 2, r0, jnp.where(lane == 3, r1, 0))))
    gate_ref[...] = jnp.where(lane == 0, g0, jnp.where(lane == 1, g1, 0.0))
    cnt_ref[...] = carry_scr[...]


def _route(logits):
    n_pad = logits.shape[0]
    row = pl.BlockSpec((TT, LANES), lambda m: (m, 0))
    return pl.pallas_call(
        _route_kernel,
        grid=(n_pad // TT,),
        in_specs=[row],
        out_specs=[row, row, pl.BlockSpec((1, LANES), lambda m: (0, 0))],
        out_shape=[
            jax.ShapeDtypeStruct((n_pad, LANES), jnp.int32),
            jax.ShapeDtypeStruct((n_pad, LANES), F32),
            jax.ShapeDtypeStruct((1, LANES), F32),
        ],
        scratch_shapes=[pltpu.VMEM((1, LANES), F32)],
        compiler_params=pltpu.CompilerParams(
            dimension_semantics=("arbitrary",), vmem_limit_bytes=VMEM_LIMIT),
        name="route",
    )(logits)


def _dest_kernel(idx_ref, ps_ref, o_ref):
    idx = idx_ref[...]
    lane = lax.broadcasted_iota(jnp.int32, idx.shape, 1)
    ps = ps_ref[...]
    pick = lambda e: jnp.sum(jnp.where(lane == e, ps, 0.0), axis=-1, keepdims=True).astype(jnp.int32)
    d0 = pick(idx[:, 0:1]) + idx[:, 2:3]
    d1 = pick(idx[:, 1:2]) + idx[:, 3:4]
    o_ref[...] = jnp.where(lane == 0, d0, jnp.where(lane == 1, d1, 0))


def _dest(idx, pstart_row):
    n_pad = idx.shape[0]
    row = pl.BlockSpec((TT, LANES), lambda m: (m, 0))
    return pl.pallas_call(
        _dest_kernel,
        grid=(n_pad // TT,),
        in_specs=[row, pl.BlockSpec((1, LANES), lambda m: (0, 0))],
        out_specs=row,
        out_shape=jax.ShapeDtypeStruct((n_pad, LANES), jnp.int32),
        compiler_params=pltpu.CompilerParams(
            dimension_semantics=("arbitrary",), vmem_limit_bytes=VMEM_LIMIT),
        name="dest",
    )(idx, pstart_row)


ROW_UNROLL = 8


def _row_copy(src, s_row, dst, d_row, sem):
    return pltpu.make_async_copy(src.at[pl.ds(s_row, 1)], dst.at[pl.ds(d_row, 1)], sem)


def _dispatch_kernel(d0_ref, d1_ref, pe_ref, u_ref, xb_ref, zero_scr, sem, zsem):
    step = pl.program_id(0)
    base = step * TT

    def zero_block(e):
        start = pl.multiple_of(pe_ref[e] - DISPATCH_BLK, DISPATCH_BLK)
        return pltpu.make_async_copy(zero_scr, xb_ref.at[pl.ds(start, DISPATCH_BLK)], zsem)

    @pl.when(step == 0)
    def _():
        zero_scr[...] = jnp.zeros_like(zero_scr)

        def nonempty(e):
            return pe_ref[e] > jnp.where(e == 0, 0, pe_ref[jnp.maximum(e - 1, 0)])

        def zstart(e, carry):
            @pl.when(nonempty(e))
            def _():
                zero_block(e).start()
            return carry

        def zwait(e, carry):
            @pl.when(nonempty(e))
            def _():
                zero_block(e).wait()
            return carry

        lax.fori_loop(0, N_EXPERTS, zstart, 0)
        lax.fori_loop(0, N_EXPERTS, zwait, 0)

        def tail_block(i):
            start = pl.multiple_of(i * DISPATCH_BLK, DISPATCH_BLK)
            return pltpu.make_async_copy(zero_scr, xb_ref.at[pl.ds(start, DISPATCH_BLK)], zsem)

        def tstart(i, carry):
            tail_block(i).start()
            return carry

        def twait(i, carry):
            tail_block(i).wait()
            return carry

        n_used = pe_ref[N_EXPERTS - 1] // DISPATCH_BLK
        n_blocks = xb_ref.shape[0] // DISPATCH_BLK
        lax.fori_loop(n_used, n_blocks, tstart, 0)
        lax.fori_loop(n_used, n_blocks, twait, 0)

    def issue(r, carry):
        _row_copy(u_ref, r, xb_ref, d0_ref[base + r], sem.at[0]).start()
        _row_copy(u_ref, r, xb_ref, d1_ref[base + r], sem.at[1]).start()
        return carry

    lax.fori_loop(0, TT, issue, 0, unroll=ROW_UNROLL)
    for k in range(2):
        pltpu.make_async_copy(u_ref, xb_ref.at[pl.ds(0, TT)], sem.at[k]).wait()


def _dispatch(d0, d1, pends, u2, rows):
    n_pad = u2.shape[0]
    return pl.pallas_call(
        _dispatch_kernel,
        grid_spec=pltpu.PrefetchScalarGridSpec(
            num_scalar_prefetch=3,
            grid=(n_pad // TT,),
            in_specs=[pl.BlockSpec((TT, D_MODEL), lambda m, d0, d1, pe: (m, 0))],
            out_specs=pl.BlockSpec(memory_space=pl.ANY),
            scratch_shapes=[
                pltpu.VMEM((DISPATCH_BLK, D_MODEL), F32),
                pltpu.SemaphoreType.DMA((2,)),
                pltpu.SemaphoreType.DMA(()),
            ],
        ),
        out_shape=jax.ShapeDtypeStruct((rows, D_MODEL), F32),
        compiler_params=pltpu.CompilerParams(
            dimension_semantics=("arbitrary",), vmem_limit_bytes=VMEM_LIMIT),
        name="dispatch",
    )(d0, d1, pends, u2)


def _expert_up_kernel(be_ref, first_ref, nu_ref, x_ref, wg_ref, wu_ref, a_ref, wg_scr, wu_scr):
    del be_ref
    i = pl.program_id(0)
    used = i < nu_ref[0]

    @pl.when(jnp.logical_and(used, first_ref[i] == 1))
    def _():
        wg_scr[...] = wg_ref[...].astype(BF16)
        wu_scr[...] = wu_ref[...].astype(BF16)

    @pl.when(used)
    def _():
        x = x_ref[...].astype(BF16)
        g = jnp.dot(x, wg_scr[...], preferred_element_type=F32)
        u = jnp.dot(x, wu_scr[...], preferred_element_type=F32)
        a_ref[...] = (jax.nn.silu(g) * u).astype(BF16)

    @pl.when(jnp.logical_not(used))
    def _():
        a_ref[...] = jnp.zeros_like(a_ref)


def _expert_down_kernel(be_ref, first_ref, nu_ref, a_ref, wd_ref, y_ref, wd_scr):
    del be_ref
    i = pl.program_id(0)
    used = i < nu_ref[0]

    @pl.when(jnp.logical_and(used, first_ref[i] == 1))
    def _():
        wd_scr[...] = wd_ref[...].astype(BF16)

    @pl.when(used)
    def _():
        y_ref[...] = jnp.dot(a_ref[...], wd_scr[...], preferred_element_type=F32)

    @pl.when(jnp.logical_not(used))
    def _():
        y_ref[...] = jnp.zeros_like(y_ref)


def _experts(blk_expert, blk_first, n_used, x_buf, w_gate, w_up, w_down):
    rows = x_buf.shape[0]
    n_blocks = rows // DISPATCH_BLK
    blk_map = lambda i, be, fi, nu: (jnp.minimum(i, jnp.maximum(nu[0] - 1, 0)), 0)
    out_map = lambda i, be, fi, nu: (i, 0)
    w_map = lambda i, be, fi, nu: (be[i], 0, 0)
    params = pltpu.CompilerParams(dimension_semantics=("arbitrary",), vmem_limit_bytes=VMEM_LIMIT)
    act = pl.pallas_call(
        _expert_up_kernel,
        grid_spec=pltpu.PrefetchScalarGridSpec(
            num_scalar_prefetch=3,
            grid=(n_blocks,),
            in_specs=[
                pl.BlockSpec((DISPATCH_BLK, D_MODEL), blk_map),
                pl.BlockSpec((None, D_MODEL, D_EXPERT), w_map),
                pl.BlockSpec((None, D_MODEL, D_EXPERT), w_map),
            ],
            out_specs=pl.BlockSpec((DISPATCH_BLK, D_EXPERT), out_map),
            scratch_shapes=[pltpu.VMEM((D_MODEL, D_EXPERT), BF16), pltpu.VMEM((D_MODEL, D_EXPERT), BF16)],
        ),
        out_shape=jax.ShapeDtypeStruct((rows, D_EXPERT), BF16),
        compiler_params=params,
        name="experts_up",
    )(blk_expert, blk_first, n_used, x_buf, w_gate, w_up)
    return pl.pallas_call(
        _expert_down_kernel,
        grid_spec=pltpu.PrefetchScalarGridSpec(
            num_scalar_prefetch=3,
            grid=(n_blocks,),
            in_specs=[
                pl.BlockSpec((DISPATCH_BLK, D_EXPERT), blk_map),
                pl.BlockSpec((None, D_EXPERT, D_MODEL), w_map),
            ],
            out_specs=pl.BlockSpec((DISPATCH_BLK, D_MODEL), out_map),
            scratch_shapes=[pltpu.VMEM((D_EXPERT, D_MODEL), BF16)],
        ),
        out_shape=jax.ShapeDtypeStruct((rows, D_MODEL), F32),
        compiler_params=params,
        name="experts_down",
    )(blk_expert, blk_first, n_used, act, w_down)


def _combine_kernel(d0_ref, d1_ref, h_ref, gate_ref, g_ref, yb_ref, o_ref, y0_scr, y1_scr, sem):
    base = pl.program_id(0) * TT

    def issue(r, carry):
        _row_copy(yb_ref, d0_ref[base + r], y0_scr, r, sem.at[0]).start()
        _row_copy(yb_ref, d1_ref[base + r], y1_scr, r, sem.at[1]).start()
        return carry

    lax.fori_loop(0, TT, issue, 0, unroll=ROW_UNROLL)
    pltpu.make_async_copy(yb_ref.at[pl.ds(0, TT)], y0_scr, sem.at[0]).wait()
    pltpu.make_async_copy(yb_ref.at[pl.ds(0, TT)], y1_scr, sem.at[1]).wait()

    gate = gate_ref[...]
    h = h_ref[...] + gate[:, 0:1] * y0_scr[...] + gate[:, 1:2] * y1_scr[...]
    o_ref[...] = _rmsnorm(h, g_ref[...])


def _combine(d0, d1, h1, gates, g_final, y_buf):
    n_pad = h1.shape[0]
    return pl.pallas_call(
        _combine_kernel,
        grid_spec=pltpu.PrefetchScalarGridSpec(
            num_scalar_prefetch=2,
            grid=(n_pad // TT,),
            in_specs=[
                pl.BlockSpec((TT, D_MODEL), lambda m, d0, d1: (m, 0)),
                pl.BlockSpec((TT, LANES), lambda m, d0, d1: (m, 0)),
                pl.BlockSpec((1, D_MODEL), lambda m, d0, d1: (0, 0)),
                pl.BlockSpec(memory_space=pl.ANY),
            ],
            out_specs=pl.BlockSpec((TT, D_MODEL), lambda m, d0, d1: (m, 0)),
            scratch_shapes=[
                pltpu.VMEM((TT, D_MODEL), F32),
                pltpu.VMEM((TT, D_MODEL), F32),
                pltpu.SemaphoreType.DMA((2,)),
            ],
        ),
        out_shape=jax.ShapeDtypeStruct((n_pad, D_MODEL), F32),
        compiler_params=pltpu.CompilerParams(
            dimension_semantics=("arbitrary",), vmem_limit_bytes=VMEM_LIMIT),
        name="combine",
    )(d0, d1, h1, gates, g_final, y_buf)


def _block_diag(w):
    per = LRU_CB // LRU_BLOCK_DIM
    w4 = w.reshape(LRU_BLOCKS // per, per, LRU_BLOCK_DIM, LRU_BLOCK_DIM)
    eye = jnp.eye(per, dtype=w.dtype)
    bd = jnp.einsum('cpij,pq->cpiqj', w4, eye)
    return bd.reshape(LRU_BLOCKS // per, LRU_CB, LRU_CB)


def kernel(x, meta_tokens, g_mix, w_in, conv_w, conv_b, w_rgate, b_rgate, w_igate, b_igate, lru_L, lambda_q1, lambda_k1, lambda_q2, lambda_k2, subln_g, rel_bias, w_out, g_ffn, w_group, b_group, w_router, b_router, w_gate, w_up, w_down, g_final):
    batch, seq, _ = x.shape
    t_real = N_META + seq
    t_pad = _round_up(t_real, TT)
    n_pad = batch * t_pad

    meta = jnp.broadcast_to(meta_tokens[None].astype(x.dtype), (batch, N_META, D_MODEL))
    h0 = jnp.concatenate([meta, x, jnp.zeros((batch, t_pad - t_real, D_MODEL), x.dtype)], axis=1)
    h0 = h0.reshape(n_pad, D_MODEL)

    proj_lru, proj_qkv = _in_proj(h0, g_mix[0][None], w_in[0].astype(BF16))

    y_lru = _lru(proj_lru, conv_w[0], conv_b[0][None],
                 _block_diag(w_rgate[0]).astype(BF16), b_rgate[0].reshape(1, D_LRU),
                 _block_diag(w_igate[0]).astype(BF16), b_igate[0].reshape(1, D_LRU),
                 lru_L[0][None], batch, t_pad)

    lam_params = jnp.stack([lambda_q1[0], lambda_k1[0], lambda_q2[0], lambda_k2[0]])
    y_att = _attention(proj_qkv, rel_bias, lam_params, subln_g[0][None], batch, t_pad)

    w_rt = jnp.concatenate([w_group[0], w_router[0],
                            jnp.zeros((D_MODEL, LANES - N_GROUPS - N_EXPERTS), F32)], axis=1)
    b_rt = jnp.concatenate([b_group[0], b_router[0],
                            jnp.zeros((LANES - N_GROUPS - N_EXPERTS,), F32)])[None]
    h1, u2, logits = _out_proj(y_lru, y_att, h0, w_out[0].astype(BF16), g_ffn[0][None], w_rt, b_rt)

    idx, gates, counts = _route(logits)

    cnt = counts[0, N_GROUPS:N_GROUPS + N_EXPERTS].astype(jnp.int32)
    padded = (cnt + DISPATCH_BLK - 1) // DISPATCH_BLK * DISPATCH_BLK
    pends = jnp.cumsum(padded)
    pstarts = pends - padded
    dest = _dest(idx, jnp.pad(pstarts.astype(F32), (0, LANES - N_EXPERTS))[None])
    d0 = dest[:, 0]
    d1 = dest[:, 1]
    n_blocks = -(-(2 * n_pad + N_EXPERTS * (DISPATCH_BLK - 1)) // DISPATCH_BLK)
    rows = n_blocks * DISPATCH_BLK
    n_used = pends[-1] // DISPATCH_BLK
    blk = jnp.arange(n_blocks, dtype=jnp.int32)
    blk_c = jnp.minimum(blk, jnp.maximum(n_used - 1, 0))
    blk_expert = jnp.sum((pends[None, :] <= (blk_c * DISPATCH_BLK)[:, None]).astype(jnp.int32), axis=1)
    blk_expert = jnp.minimum(blk_expert, N_EXPERTS - 1)

    blk_first = jnp.concatenate([jnp.ones((1,), jnp.int32),
                                 (blk_expert[1:] != blk_expert[:-1]).astype(jnp.int32)])

    x_buf = _dispatch(d0, d1, pends.astype(jnp.int32), u2, rows)
    y_buf = _experts(blk_expert, blk_first, n_used[None].astype(jnp.int32), x_buf,
                     w_gate[0], w_up[0], w_down[0])
    out = _combine(d0, d1, h1, gates, g_final[None], y_buf)
    return out.reshape(batch, t_pad, D_MODEL)[:, N_META:t_real]
```

```python
import functools
import math

import numpy as np
import jax
import jax.numpy as jnp
from jax import lax
from jax.experimental import pallas as pl
from jax.experimental.pallas import tpu as pltpu

D_MODEL = 2048
N_META = 16
D_LRU = 1024
D_ATT = 1024
LRU_BLOCKS = 16
LRU_BLOCK_DIM = 64
CONV_WIDTH = 4
LRU_C = 8.0
N_HEADS = 8
V_HEAD_DIM = 128
QK_HEAD_DIM = 64
N_BUCKETS = 32
MAX_DISTANCE = 128
N_GROUPS = 4
EXPERTS_PER_GROUP = 8
N_EXPERTS = 32
D_EXPERT = 1024
D_IN_PROJ = 2 * D_LRU + 3 * D_ATT
EPS = 1e-6
LAMBDA_INIT = 0.8 - 0.6 * math.exp(-0.3 * 0)
LOG2_E = math.log2(math.e)

F32 = jnp.float32
BF16 = jnp.bfloat16

LANES = 128
MXU_DIM = 256
TT = 768
PROJ_TN = 1024
OUT_TM = TT // 2
ATT_SUB = MXU_DIM
ATT_LOOKAHEAD = 3
VT_ROWS = V_HEAD_DIM + 16
LRU_CB = MXU_DIM
DISPATCH_BLK = 256
VMEM_LIMIT = 56 * 1024 * 1024
NEG_BIG = -1e30


def _round_up(a, b):
    return -(-a // b) * b


def _rmsnorm(x, g):
    ms = jnp.mean(x * x, axis=-1, keepdims=True)
    return x * lax.rsqrt(ms + EPS) * g


def _inproj_kernel(x_ref, g_ref, w_ref, lru_ref, qkv_ref, u_scr):
    n = pl.program_id(1)

    @pl.when(n == 0)
    def _():
        u_scr[...] = _rmsnorm(x_ref[...], g_ref[...]).astype(BF16)

    y = jnp.dot(u_scr[...], w_ref[...], preferred_element_type=F32)

    @pl.when(n < 2)
    def _():
        lru_ref[...] = y

    @pl.when(n >= 2)
    def _():
        qkv_ref[...] = y.astype(BF16)


def _in_proj(h0, g_mix, w_in_bf16):
    n_pad = h0.shape[0]
    n_col = D_IN_PROJ // PROJ_TN
    return pl.pallas_call(
        _inproj_kernel,
        grid=(n_pad // TT, n_col),
        in_specs=[
            pl.BlockSpec((TT, D_MODEL), lambda m, n: (m, 0)),
            pl.BlockSpec((1, D_MODEL), lambda m, n: (0, 0)),
            pl.BlockSpec((D_MODEL, PROJ_TN), lambda m, n: (0, n)),
        ],
        out_specs=[
            pl.BlockSpec((TT, PROJ_TN), lambda m, n: (m, jnp.minimum(n, 1))),
            pl.BlockSpec((TT, PROJ_TN), lambda m, n: (m, jnp.maximum(n - 2, 0))),
        ],
        out_shape=[
            jax.ShapeDtypeStruct((n_pad, 2 * D_LRU), F32),
            jax.ShapeDtypeStruct((n_pad, 3 * D_ATT), BF16),
        ],
        scratch_shapes=[pltpu.VMEM((TT, D_MODEL), BF16)],
        compiler_params=pltpu.CompilerParams(
            dimension_semantics=("arbitrary", "arbitrary"), vmem_limit_bytes=VMEM_LIMIT),
        name="in_proj",
    )(h0, g_mix, w_in_bf16)


def _shift_rows(x, tail, s):
    r = pltpu.roll(x, s, 0)
    row8 = lax.broadcasted_iota(jnp.int32, tail.shape, 0)
    head = jnp.where(row8 < s, pltpu.roll(tail, s, 0), r[0:8])
    return jnp.concatenate([head, r[8:]], axis=0)


def _lru_kernel(x_ref, gate_ref, cw_ref, cb_ref, wr_ref, br_ref, wi_ref, bi_ref, lam_ref,
                y_ref, tail_scr, h_scr):
    t = pl.program_id(2)

    @pl.when(t == 0)
    def _():
        tail_scr[...] = jnp.zeros_like(tail_scr)
        h_scr[...] = jnp.zeros_like(h_scr)

    x = x_ref[...]
    tail = tail_scr[...]
    cw = cw_ref[...]
    xc = cb_ref[...] + x * cw[CONV_WIDTH - 1:CONV_WIDTH]
    for k in range(CONV_WIDTH - 1):
        xc = xc + _shift_rows(x, tail, CONV_WIDTH - 1 - k) * cw[k:k + 1]
    tail_scr[...] = x[TT - 8:TT]

    xb = xc.astype(BF16)
    r = jax.nn.sigmoid(jnp.dot(xb, wr_ref[...], preferred_element_type=F32) + br_ref[...])
    i = jax.nn.sigmoid(jnp.dot(xb, wi_ref[...], preferred_element_type=F32) + bi_ref[...])
    lam = lam_ref[...]
    log_sig = jnp.minimum(lam, 0.0) - jnp.log(1.0 + jnp.exp(-jnp.abs(lam)))
    log_a = (LRU_C * r) * log_sig
    a = jnp.exp(log_a)
    b = jnp.sqrt(1.0 - a * a) * (i * xc)

    row = lax.broadcasted_iota(jnp.int32, a.shape, 0)
    s = 1
    while s < TT:
        keep = row >= s
        a_sh = jnp.where(keep, pltpu.roll(a, s, 0), 1.0)
        b_sh = jnp.where(keep, pltpu.roll(b, s, 0), 0.0)
        b = a * b_sh + b
        a = a * a_sh
        s *= 2
    h = b + a * h_scr[0:1]
    h_scr[...] = jnp.broadcast_to(h[TT - 1:TT], h_scr.shape)
    y_ref[...] = (h * jax.nn.gelu(gate_ref[...])).astype(BF16)


def _lru(proj_lru, conv_w, conv_b, wr_bd, b_r, wi_bd, b_i, lru_l, batch, t_pad):
    n_pad = proj_lru.shape[0]
    n_t = t_pad // TT
    n_c = D_LRU // LRU_CB
    vec = lambda: pl.BlockSpec((1, LRU_CB), lambda b, c, t: (0, c))
    return pl.pallas_call(
        _lru_kernel,
        grid=(batch, n_c, n_t),
        in_specs=[
            pl.BlockSpec((TT, LRU_CB), lambda b, c, t: (b * n_t + t, c)),
            pl.BlockSpec((TT, LRU_CB), lambda b, c, t: (b * n_t + t, n_c + c)),
            pl.BlockSpec((CONV_WIDTH, LRU_CB), lambda b, c, t: (0, c)),
            vec(),
            pl.BlockSpec((None, LRU_CB, LRU_CB), lambda b, c, t: (c, 0, 0)),
            vec(),
            pl.BlockSpec((None, LRU_CB, LRU_CB), lambda b, c, t: (c, 0, 0)),
            vec(),
            vec(),
        ],
        out_specs=pl.BlockSpec((TT, LRU_CB), lambda b, c, t: (b * n_t + t, c)),
        out_shape=jax.ShapeDtypeStruct((n_pad, D_LRU), BF16),
        scratch_shapes=[pltpu.VMEM((8, LRU_CB), F32), pltpu.VMEM((8, LRU_CB), F32)],
        compiler_params=pltpu.CompilerParams(
            dimension_semantics=("arbitrary", "arbitrary", "arbitrary"),
            vmem_limit_bytes=VMEM_LIMIT),
        name="rglru",
    )(proj_lru, proj_lru, conv_w, conv_b, wr_bd, b_r, wi_bd, b_i, lru_l)


def _bucket_thresholds():
    max_exact = N_BUCKETS // 2
    n = np.arange(0, MAX_DISTANCE + 1)
    nf = np.maximum(n, 1).astype(np.float32)
    large = max_exact + (np.log(nf / np.float32(max_exact)) / np.float32(math.log(MAX_DISTANCE / max_exact))
                         * np.float32(N_BUCKETS - max_exact)).astype(np.int32)
    large = np.minimum(large, N_BUCKETS - 1)
    bucket = np.where(n < max_exact, n, large)
    return [int(np.argmax(bucket >= k)) for k in range(1, N_BUCKETS)]


_BUCKET_THR = _bucket_thresholds()


def _attn_kernel(rb_ref, q_ref, k_ref, v_ref, lam_ref, g_ref, o_ref, bias_scr, vt_scr, m_scr, acc_scr):
    h = pl.program_id(0)
    b = pl.program_id(1)
    qi = pl.program_id(2)
    tk = TT
    n_kv = vt_scr.shape[0]

    @pl.when(jnp.logical_and(b == 0, qi == 0))
    def _():
        k_pos = lax.broadcasted_iota(jnp.int32, (tk, tk), 0)
        q_pos = lax.broadcasted_iota(jnp.int32, (tk, tk), 1)
        far = rb_ref[N_BUCKETS - 1, h]
        for d in range(2):
            rel = q_pos - k_pos + d * tk
            val = jnp.full((tk, tk), rb_ref[0, h] - far, F32)
            for kk, thr in enumerate(_BUCKET_THR):
                val = jnp.where(rel >= thr, rb_ref[kk + 1, h] - far, val)
            val = val * LOG2_E
            if d == 0:
                val = jnp.where(rel >= 0, val, -jnp.inf)
            bias_scr[d] = val

    @pl.when(qi == 0)
    def _():
        for t in range(n_kv):
            vt_scr[t, 0:V_HEAD_DIM, :] = v_ref[t * tk:(t + 1) * tk, :].astype(F32).T.astype(BF16)
            vt_scr[t, V_HEAD_DIM:VT_ROWS, :] = jnp.ones((VT_ROWS - V_HEAD_DIM, tk), BF16)

    q = q_ref[...]
    lane = lax.broadcasted_iota(jnp.int32, q.shape, 1)
    scale = (QK_HEAD_DIM ** -0.5) * LOG2_E
    qf = q.astype(F32) * scale
    qs = jnp.concatenate([jnp.where(lane < QK_HEAD_DIM, qf, 0.0),
                          jnp.where(lane >= QK_HEAD_DIM, qf, 0.0)], axis=0).astype(BF16)

    m_scr[...] = jnp.full(m_scr.shape, NEG_BIG, F32)
    acc_scr[...] = jnp.zeros_like(acc_scr)

    def tile_group(tiles):
        n_sub = 2 * tk // ATT_SUB
        stages = [(t, c) for t in range(len(tiles)) for c in range(n_sub)]

        def scores(t, c):
            j, bias_idx = tiles[t]
            kt = k_ref[pl.ds(pl.multiple_of(j * tk, tk), tk), :]
            cols = slice(c * ATT_SUB, (c + 1) * ATT_SUB)
            s = lax.dot_general(kt, qs[cols], (((1,), (1,)), ((), ())), preferred_element_type=F32)
            if bias_idx is not None:
                q0 = (c * ATT_SUB) % tk
                s = s + bias_scr[bias_idx, :, q0:q0 + ATT_SUB]
            return s

        pending = [scores(*st) for st in stages[:ATT_LOOKAHEAD]]
        for n, (t, c) in enumerate(stages):
            cols = slice(c * ATT_SUB, (c + 1) * ATT_SUB)
            s = pending.pop(0)
            if n + ATT_LOOKAHEAD < len(stages):
                pending.append(scores(*stages[n + ATT_LOOKAHEAD]))
            m_prev = m_scr[:, cols]
            m_new = jnp.maximum(m_prev, jnp.max(s, axis=0, keepdims=True))
            alpha = jnp.exp2(m_prev - m_new)
            p = jnp.exp2(s - m_new).astype(BF16)
            vt = vt_scr[tiles[t][0]]
            acc_scr[:, cols] = alpha * acc_scr[:, cols] + jnp.dot(vt, p, preferred_element_type=F32)
            m_scr[:, cols] = m_new

    n_far = jnp.maximum(qi - 1, 0)

    def far_pair(i, carry):
        tile_group([(2 * i, None), (2 * i + 1, None)])
        return carry

    lax.fori_loop(0, n_far // 2, far_pair, 0)

    @pl.when(n_far % 2 == 1)
    def _():
        tile_group([(n_far - 1, None)])

    @pl.when(qi >= 1)
    def _():
        tile_group([(qi - 1, 1), (qi, 0)])

    @pl.when(qi == 0)
    def _():
        tile_group([(qi, 0)])

    lam_p = lam_ref[...]
    lam = (jnp.exp(jnp.sum(lam_p[0:1] * lam_p[1:2], axis=-1, keepdims=True))
           - jnp.exp(jnp.sum(lam_p[2:3] * lam_p[3:4], axis=-1, keepdims=True)) + LAMBDA_INIT)
    acc = acc_scr[...]
    num = acc[0:V_HEAD_DIM]
    den = acc[V_HEAD_DIM:V_HEAD_DIM + 1]
    o_t = num[:, :tk] / den[:, :tk] - lam * (num[:, tk:] / den[:, tk:])
    o_ref[...] = (_rmsnorm(o_t.T, g_ref[...]) * (1.0 - LAMBDA_INIT)).astype(BF16)


def _attention(qkv, rel_bias, lam_params, subln_g, batch, t_pad):
    n_q = t_pad // TT
    qkv3 = qkv.reshape(batch, t_pad, 3 * D_ATT)
    out = pl.pallas_call(
        _attn_kernel,
        grid_spec=pltpu.PrefetchScalarGridSpec(
            num_scalar_prefetch=0,
            grid=(N_HEADS, batch, n_q),
            in_specs=[
                pl.BlockSpec(memory_space=pltpu.SMEM),
                pl.BlockSpec((None, TT, V_HEAD_DIM), lambda h, b, q: (b, q, h)),
                pl.BlockSpec((None, t_pad, V_HEAD_DIM), lambda h, b, q: (b, 0, N_HEADS + h)),
                pl.BlockSpec((None, t_pad, V_HEAD_DIM), lambda h, b, q: (b, 0, 2 * N_HEADS + h)),
                pl.BlockSpec((4, QK_HEAD_DIM), lambda h, b, q: (0, 0)),
                pl.BlockSpec((1, V_HEAD_DIM), lambda h, b, q: (0, 0)),
            ],
            out_specs=pl.BlockSpec((None, TT, V_HEAD_DIM), lambda h, b, q: (b, q, h)),
            scratch_shapes=[
                pltpu.VMEM((2, TT, TT), F32),
                pltpu.VMEM((n_q, VT_ROWS, TT), BF16),
                pltpu.VMEM((1, 2 * TT), F32),
                pltpu.VMEM((VT_ROWS, 2 * TT), F32),
            ],
        ),
        out_shape=jax.ShapeDtypeStruct((batch, t_pad, D_ATT), BF16),
        compiler_params=pltpu.CompilerParams(
            dimension_semantics=("arbitrary", "arbitrary", "arbitrary"),
            vmem_limit_bytes=VMEM_LIMIT),
        name="diff_attention",
    )(rel_bias, qkv3, qkv3, qkv3, lam_params, subln_g)
    return out.reshape(batch * t_pad, D_ATT)


def _outproj_kernel(yl_ref, ya_ref, h0_ref, w_ref, g_ref, wrt_ref, brt_ref, h1_ref, u_ref, lg_ref):
    acc = jnp.dot(yl_ref[...], w_ref[0:D_LRU, :], preferred_element_type=F32)
    acc = acc + jnp.dot(ya_ref[...], w_ref[D_LRU:, :], preferred_element_type=F32)
    h1 = h0_ref[...] + acc
    h1_ref[...] = h1
    u = _rmsnorm(h1, g_ref[...])
    u_ref[...] = u
    u_hi = u.astype(BF16)
    u_lo = (u - u_hi.astype(F32)).astype(BF16)
    w = wrt_ref[...]
    w_hi = w.astype(BF16)
    w_lo = (w - w_hi.astype(F32)).astype(BF16)
    lg = jnp.dot(u_hi, w_hi, preferred_element_type=F32)
    lg = lg + jnp.dot(u_lo, w_hi, preferred_element_type=F32)
    lg = lg + jnp.dot(u_hi, w_lo, preferred_element_type=F32)
    lg_ref[...] = lg + brt_ref[...]


def _out_proj(y_lru, y_att, h0, w_out_bf16, g_ffn, w_rt, b_rt):
    n_pad = h0.shape[0]
    row = lambda w: pl.BlockSpec((OUT_TM, w), lambda m: (m, 0))
    full = lambda a, b: pl.BlockSpec((a, b), lambda m: (0, 0))
    return pl.pallas_call(
        _outproj_kernel,
        grid=(n_pad // OUT_TM,),
        in_specs=[row(D_LRU), row(D_ATT), row(D_MODEL), full(D_MODEL, D_MODEL), full(1, D_MODEL),
                  full(D_MODEL, LANES), full(1, LANES)],
        out_specs=[row(D_MODEL), row(D_MODEL), row(LANES)],
        out_shape=[
            jax.ShapeDtypeStruct((n_pad, D_MODEL), F32),
            jax.ShapeDtypeStruct((n_pad, D_MODEL), F32),
            jax.ShapeDtypeStruct((n_pad, LANES), F32),
        ],
        compiler_params=pltpu.CompilerParams(
            dimension_semantics=("arbitrary",), vmem_limit_bytes=VMEM_LIMIT),
        name="out_proj",
    )(y_lru, y_att, h0, w_out_bf16, g_ffn, w_rt, b_rt)


def _route_kernel(lg_ref, idx_ref, gate_ref, cnt_ref, carry_scr):
    step = pl.program_id(0)

    @pl.when(step == 0)
    def _():
        carry_scr[...] = jnp.zeros_like(carry_scr)

    lg = lg_ref[...]
    lane = lax.broadcasted_iota(jnp.int32, lg.shape, 1)
    first = lambda mask: jnp.min(jnp.where(mask, lane, LANES), axis=-1, keepdims=True)

    is_g = lane < N_GROUPS
    gl = jnp.where(is_g, lg, -jnp.inf)
    gmax = jnp.max(gl, axis=-1, keepdims=True)
    grp = first(gl == gmax)
    gsum = jnp.sum(jnp.where(is_g, jnp.exp(gl - gmax), 0.0), axis=-1, keepdims=True)
    p_grp = 1.0 / gsum

    lane_e = lane - N_GROUPS
    in_grp = (lane_e >= 0) & (lane_e < N_EXPERTS) & ((lane_e // EXPERTS_PER_GROUP) == grp)
    el = jnp.where(in_grp, lg, -jnp.inf)
    v1 = jnp.max(el, axis=-1, keepdims=True)
    i1 = first(el == v1)
    el2 = jnp.where(lane == i1, -jnp.inf, el)
    v2 = jnp.max(el2, axis=-1, keepdims=True)
    i2 = first(el2 == v2)
    e2 = jnp.exp(v2 - v1)
    den = 1.0 + e2
    g0 = p_grp * (1.0 / den)
    g1 = p_grp * (e2 / den)

    hit1 = lane == i1
    hit2 = lane == i2
    onehot = jnp.where(hit1 | hit2, 1.0, 0.0)
    rr = lax.broadcasted_iota(jnp.int32, (TT, TT), 0)
    cc = lax.broadcasted_iota(jnp.int32, (TT, TT), 1)
    tri = jnp.where(cc < rr, 1.0, 0.0).astype(BF16)
    rank = jnp.dot(tri, onehot.astype(BF16), preferred_element_type=F32) + carry_scr[...]
    r0 = jnp.sum(jnp.where(hit1, rank, 0.0), axis=-1, keepdims=True).astype(jnp.int32)
    r1 = jnp.sum(jnp.where(hit2, rank, 0.0), axis=-1, keepdims=True).astype(jnp.int32)
    carry_scr[...] = carry_scr[...] + jnp.sum(onehot, axis=0, keepdims=True)

    idx_ref[...] = jnp.where(lane == 0, i1 - N_GROUPS,
                             jnp.where(lane == 1, i2 - N_GROUPS,
                                       jnp.where(lane == 2, r0, jnp.where(lane == 3, r1, 0))))
    gate_ref[...] = jnp.where(lane == 0, g0, jnp.where(lane == 1, g1, 0.0))
    cnt_ref[...] = carry_scr[...]


def _route(logits):
    n_pad = logits.shape[0]
    row = pl.BlockSpec((TT, LANES), lambda m: (m, 0))
    return pl.pallas_call(
        _route_kernel,
        grid=(n_pad // TT,),
        in_specs=[row],
        out_specs=[row, row, pl.BlockSpec((1, LANES), lambda m: (0, 0))],
        out_shape=[
            jax.ShapeDtypeStruct((n_pad, LANES), jnp.int32),
            jax.ShapeDtypeStruct((n_pad, LANES), F32),
            jax.ShapeDtypeStruct((1, LANES), F32),
        ],
        scratch_shapes=[pltpu.VMEM((1, LANES), F32)],
        compiler_params=pltpu.CompilerParams(
            dimension_semantics=("arbitrary",), vmem_limit_bytes=VMEM_LIMIT),
        name="route",
    )(logits)


def _dest_kernel(idx_ref, ps_ref, o_ref):
    idx = idx_ref[...]
    lane = lax.broadcasted_iota(jnp.int32, idx.shape, 1)
    ps = ps_ref[...]
    pick = lambda e: jnp.sum(jnp.where(lane == e, ps, 0.0), axis=-1, keepdims=True).astype(jnp.int32)
    d0 = pick(idx[:, 0:1]) + idx[:, 2:3]
    d1 = pick(idx[:, 1:2]) + idx[:, 3:4]
    o_ref[...] = jnp.where(lane == 0, d0, jnp.where(lane == 1, d1, 0))


def _dest(idx, pstart_row):
    n_pad = idx.shape[0]
    row = pl.BlockSpec((TT, LANES), lambda m: (m, 0))
    return pl.pallas_call(
        _dest_kernel,
        grid=(n_pad // TT,),
        in_specs=[row, pl.BlockSpec((1, LANES), lambda m: (0, 0))],
        out_specs=row,
        out_shape=jax.ShapeDtypeStruct((n_pad, LANES), jnp.int32),
        compiler_params=pltpu.CompilerParams(
            dimension_semantics=("arbitrary",), vmem_limit_bytes=VMEM_LIMIT),
        name="dest",
    )(idx, pstart_row)


ROW_UNROLL = 8


def _row_copy(src, s_row, dst, d_row, sem):
    return pltpu.make_async_copy(src.at[pl.ds(s_row, 1)], dst.at[pl.ds(d_row, 1)], sem)


def _dispatch_kernel(d0_ref, d1_ref, pe_ref, u_ref, xb_ref, zero_scr, sem, zsem):
    step = pl.program_id(0)
    base = step * TT

    def zero_block(e):
        start = pl.multiple_of(pe_ref[e] - DISPATCH_BLK, DISPATCH_BLK)
        return pltpu.make_async_copy(zero_scr, xb_ref.at[pl.ds(start, DISPATCH_BLK)], zsem)

    @pl.when(step == 0)
    def _():
        zero_scr[...] = jnp.zeros_like(zero_scr)

        def nonempty(e):
            return pe_ref[e] > jnp.where(e == 0, 0, pe_ref[jnp.maximum(e - 1, 0)])

        def zstart(e, carry):
            @pl.when(nonempty(e))
            def _():
                zero_block(e).start()
            return carry

        def zwait(e, carry):
            @pl.when(nonempty(e))
            def _():
                zero_block(e).wait()
            return carry

        lax.fori_loop(0, N_EXPERTS, zstart, 0)
        lax.fori_loop(0, N_EXPERTS, zwait, 0)

        def tail_block(i):
            start = pl.multiple_of(i * DISPATCH_BLK, DISPATCH_BLK)
            return pltpu.make_async_copy(zero_scr, xb_ref.at[pl.ds(start, DISPATCH_BLK)], zsem)

        def tstart(i, carry):
            tail_block(i).start()
            return carry

        def twait(i, carry):
            tail_block(i).wait()
            return carry

        n_used = pe_ref[N_EXPERTS - 1] // DISPATCH_BLK
        n_blocks = xb_ref.shape[0] // DISPATCH_BLK
        lax.fori_loop(n_used, n_blocks, tstart, 0)
        lax.fori_loop(n_used, n_blocks, twait, 0)

    def issue(r, carry):
        _row_copy(u_ref, r, xb_ref, d0_ref[base + r], sem.at[0]).start()
        _row_copy(u_ref, r, xb_ref, d1_ref[base + r], sem.at[1]).start()
        return carry

    lax.fori_loop(0, TT, issue, 0, unroll=ROW_UNROLL)
    for k in range(2):
        pltpu.make_async_copy(u_ref, xb_ref.at[pl.ds(0, TT)], sem.at[k]).wait()


def _dispatch(d0, d1, pends, u2, rows):
    n_pad = u2.shape[0]
    return pl.pallas_call(
        _dispatch_kernel,
        grid_spec=pltpu.PrefetchScalarGridSpec(
            num_scalar_prefetch=3,
            grid=(n_pad // TT,),
            in_specs=[pl.BlockSpec((TT, D_MODEL), lambda m, d0, d1, pe: (m, 0))],
            out_specs=pl.BlockSpec(memory_space=pl.ANY),
            scratch_shapes=[
                pltpu.VMEM((DISPATCH_BLK, D_MODEL), F32),
                pltpu.SemaphoreType.DMA((2,)),
                pltpu.SemaphoreType.DMA(()),
            ],
        ),
        out_shape=jax.ShapeDtypeStruct((rows, D_MODEL), F32),
        compiler_params=pltpu.CompilerParams(
            dimension_semantics=("arbitrary",), vmem_limit_bytes=VMEM_LIMIT),
        name="dispatch",
    )(d0, d1, pends, u2)


WEIGHT_UNITS = 8


def _stream_weights(i, tgt_ref, eseq_ref, ntot_ref, w_hbm, stage, wbf, sem, done_scr):
    unit_rows = stage.shape[2]

    def unit_copies(u):
        e = eseq_ref[u // WEIGHT_UNITS]
        r = pl.multiple_of((u % WEIGHT_UNITS) * unit_rows, unit_rows)
        b = u % 2
        return [pltpu.make_async_copy(w.at[e, pl.ds(r, unit_rows), :], stage.at[b, k], sem.at[b, k])
                for k, w in enumerate(w_hbm)]

    @pl.when(i == 0)
    def _():
        done_scr[0] = 0
        for u in range(2):
            for c in unit_copies(u):
                c.start()

    def body(u, carry):
        for c in unit_copies(u):
            c.wait()
        slot = (u // WEIGHT_UNITS) % 2
        r = pl.multiple_of((u % WEIGHT_UNITS) * unit_rows, unit_rows)
        b = u % 2
        for k in range(len(w_hbm)):
            wbf[slot, k, pl.ds(r, unit_rows), :] = stage[b, k].astype(BF16)

        @pl.when(u + 2 < ntot_ref[0])
        def _():
            for c in unit_copies(u + 2):
                c.start()

        return carry

    lax.fori_loop(done_scr[0], tgt_ref[i], body, 0)
    done_scr[0] = tgt_ref[i]


def _expert_up_kernel(tgt_ref, eseq_ref, ntot_ref, slot_ref, nu_ref, x_ref, wg_hbm, wu_hbm, a_ref,
                      stage, wbf, sem, done_scr):
    i = pl.program_id(0)
    _stream_weights(i, tgt_ref, eseq_ref, ntot_ref, (wg_hbm, wu_hbm), stage, wbf, sem, done_scr)
    used = i < nu_ref[0]

    @pl.when(used)
    def _():
        slot = slot_ref[i]
        x = x_ref[...].astype(BF16)
        g = jnp.dot(x, wbf[slot, 0], preferred_element_type=F32)
        u = jnp.dot(x, wbf[slot, 1], preferred_element_type=F32)
        a_ref[...] = (jax.nn.silu(g) * u).astype(BF16)

    @pl.when(jnp.logical_not(used))
    def _():
        a_ref[...] = jnp.zeros_like(a_ref)


def _expert_down_kernel(tgt_ref, eseq_ref, ntot_ref, slot_ref, nu_ref, a_ref, wd_hbm, y_ref,
                        stage, wbf, sem, done_scr):
    i = pl.program_id(0)
    _stream_weights(i, tgt_ref, eseq_ref, ntot_ref, (wd_hbm,), stage, wbf, sem, done_scr)
    used = i < nu_ref[0]

    @pl.when(used)
    def _():
        y_ref[...] = jnp.dot(a_ref[...], wbf[slot_ref[i], 0], preferred_element_type=F32)

    @pl.when(jnp.logical_not(used))
    def _():
        y_ref[...] = jnp.zeros_like(y_ref)


def _experts(sched, x_buf, w_gate, w_up, w_down):
    rows = x_buf.shape[0]
    n_blocks = rows // DISPATCH_BLK
    blk_map = lambda i, tg, es, nt, sl, nu: (jnp.minimum(i, jnp.maximum(nu[0] - 1, 0)), 0)
    out_map = lambda i, tg, es, nt, sl, nu: (i, 0)
    params = pltpu.CompilerParams(dimension_semantics=("arbitrary",), vmem_limit_bytes=VMEM_LIMIT)
    hbm = pl.BlockSpec(memory_space=pl.ANY)

    def scratch(n_mats, k_dim, n_dim):
        return [
            pltpu.VMEM((2, n_mats, k_dim // WEIGHT_UNITS, n_dim), F32),
            pltpu.VMEM((2, n_mats, k_dim, n_dim), BF16),
            pltpu.SemaphoreType.DMA((2, n_mats)),
            pltpu.SMEM((1,), jnp.int32),
        ]

    act = pl.pallas_call(
        _expert_up_kernel,
        grid_spec=pltpu.PrefetchScalarGridSpec(
            num_scalar_prefetch=5,
            grid=(n_blocks,),
            in_specs=[pl.BlockSpec((DISPATCH_BLK, D_MODEL), blk_map), hbm, hbm],
            out_specs=pl.BlockSpec((DISPATCH_BLK, D_EXPERT), out_map),
            scratch_shapes=scratch(2, D_MODEL, D_EXPERT),
        ),
        out_shape=jax.ShapeDtypeStruct((rows, D_EXPERT), BF16),
        compiler_params=params,
        name="experts_up",
    )(*sched, x_buf, w_gate, w_up)
    return pl.pallas_call(
        _expert_down_kernel,
        grid_spec=pltpu.PrefetchScalarGridSpec(
            num_scalar_prefetch=5,
            grid=(n_blocks,),
            in_specs=[pl.BlockSpec((DISPATCH_BLK, D_EXPERT), blk_map), hbm],
            out_specs=pl.BlockSpec((DISPATCH_BLK, D_MODEL), out_map),
            scratch_shapes=scratch(1, D_EXPERT, D_MODEL),
        ),
        out_shape=jax.ShapeDtypeStruct((rows, D_MODEL), F32),
        compiler_params=params,
        name="experts_down",
    )(*sched, act, w_down)


def _expert_schedule(padded, pstarts, n_blocks):
    nonempty = padded > 0
    seq_of_expert = jnp.cumsum(nonempty.astype(jnp.int32)) - 1
    n_seq = jnp.sum(nonempty.astype(jnp.int32))
    experts = jnp.arange(N_EXPERTS, dtype=jnp.int32)
    is_kth = jnp.logical_and(nonempty[None, :], seq_of_expert[None, :] == experts[:, None])
    eseq = jnp.sum(jnp.where(is_kth, experts[None, :], 0), axis=1)
    pends = pstarts + padded
    n_used = pends[-1] // DISPATCH_BLK
    blk = jnp.arange(n_blocks, dtype=jnp.int32)
    blk_c = jnp.minimum(blk, jnp.maximum(n_used - 1, 0))
    blk_expert = jnp.minimum(
        jnp.sum((pends[None, :] <= (blk_c * DISPATCH_BLK)[:, None]).astype(jnp.int32), axis=1), N_EXPERTS - 1)
    onehot = (blk_expert[:, None] == experts[None, :]).astype(jnp.int32)
    pick = lambda v: jnp.sum(onehot * v[None, :], axis=1)
    q = pick(seq_of_expert)
    j = blk_c - pick(pstarts) // DISPATCH_BLK
    n = jnp.maximum(pick(padded) // DISPATCH_BLK, 1)
    n_total = n_seq * WEIGHT_UNITS
    tgt = jnp.minimum(WEIGHT_UNITS * (q + 1) + (WEIGHT_UNITS * j) // n, n_total)
    tgt = jnp.where(blk < n_used, tgt, n_total)
    i32 = lambda a: a.astype(jnp.int32)
    return i32(tgt), eseq, i32(n_total)[None], i32(q % 2), i32(n_used)[None]


COMBINE_TM = 512


def _combine_kernel(t_pad, d0_ref, d1_ref, h_hbm, gate_hbm, g_ref, yb_ref, o_ref,
                    h_scr, gate_scr, y0_scr, y1_scr, sem):
    b = pl.program_id(0)
    j = pl.program_id(1)
    base = pl.multiple_of(b * t_pad + N_META + j * COMBINE_TM, 8)

    tile_copies = [
        pltpu.make_async_copy(h_hbm.at[pl.ds(base, COMBINE_TM)], h_scr, sem.at[2]),
        pltpu.make_async_copy(gate_hbm.at[pl.ds(base, COMBINE_TM)], gate_scr, sem.at[3]),
    ]
    for c in tile_copies:
        c.start()

    def issue(r, carry):
        _row_copy(yb_ref, d0_ref[base + r], y0_scr, r, sem.at[0]).start()
        _row_copy(yb_ref, d1_ref[base + r], y1_scr, r, sem.at[1]).start()
        return carry

    lax.fori_loop(0, COMBINE_TM, issue, 0, unroll=ROW_UNROLL)
    pltpu.make_async_copy(yb_ref.at[pl.ds(0, COMBINE_TM)], y0_scr, sem.at[0]).wait()
    pltpu.make_async_copy(yb_ref.at[pl.ds(0, COMBINE_TM)], y1_scr, sem.at[1]).wait()
    for c in tile_copies:
        c.wait()

    gate = gate_scr[...]
    h = h_scr[...] + gate[:, 0:1] * y0_scr[...] + gate[:, 1:2] * y1_scr[...]
    o_ref[...] = _rmsnorm(h, g_ref[...])


def _combine(d0, d1, h1, gates, g_final, y_buf, batch, seq):
    assert seq % COMBINE_TM == 0 and N_META % 8 == 0
    tile = lambda w: pltpu.VMEM((COMBINE_TM, w), F32)
    return pl.pallas_call(
        functools.partial(_combine_kernel, h1.shape[0] // batch),
        grid_spec=pltpu.PrefetchScalarGridSpec(
            num_scalar_prefetch=2,
            grid=(batch, seq // COMBINE_TM),
            in_specs=[
                pl.BlockSpec(memory_space=pl.ANY),
                pl.BlockSpec(memory_space=pl.ANY),
                pl.BlockSpec((1, D_MODEL), lambda b, j, d0, d1: (0, 0)),
                pl.BlockSpec(memory_space=pl.ANY),
            ],
            out_specs=pl.BlockSpec((None, COMBINE_TM, D_MODEL), lambda b, j, d0, d1: (b, j, 0)),
            scratch_shapes=[tile(D_MODEL), tile(LANES), tile(D_MODEL), tile(D_MODEL),
                            pltpu.SemaphoreType.DMA((4,))],
        ),
        out_shape=jax.ShapeDtypeStruct((batch, seq, D_MODEL), F32),
        compiler_params=pltpu.CompilerParams(
            dimension_semantics=("arbitrary", "arbitrary"), vmem_limit_bytes=VMEM_LIMIT),
        name="combine",
    )(d0, d1, h1, gates, g_final, y_buf)


def _block_diag(w):
    per = LRU_CB // LRU_BLOCK_DIM
    w4 = w.reshape(LRU_BLOCKS // per, per, LRU_BLOCK_DIM, LRU_BLOCK_DIM)
    eye = jnp.eye(per, dtype=w.dtype)
    bd = jnp.einsum('cpij,pq->cpiqj', w4, eye)
    return bd.reshape(LRU_BLOCKS // per, LRU_CB, LRU_CB)


def kernel(x, meta_tokens, g_mix, w_in, conv_w, conv_b, w_rgate, b_rgate, w_igate, b_igate, lru_L, lambda_q1, lambda_k1, lambda_q2, lambda_k2, subln_g, rel_bias, w_out, g_ffn, w_group, b_group, w_router, b_router, w_gate, w_up, w_down, g_final):
    batch, seq, _ = x.shape
    t_real = N_META + seq
    t_pad = _round_up(t_real, TT)
    n_pad = batch * t_pad

    meta = jnp.broadcast_to(meta_tokens[None].astype(x.dtype), (batch, N_META, D_MODEL))
    h0 = jnp.concatenate([meta, x, jnp.zeros((batch, t_pad - t_real, D_MODEL), x.dtype)], axis=1)
    h0 = h0.reshape(n_pad, D_MODEL)

    proj_lru, proj_qkv = _in_proj(h0, g_mix[0][None], w_in[0].astype(BF16))

    y_lru = _lru(proj_lru, conv_w[0], conv_b[0][None],
                 _block_diag(w_rgate[0]).astype(BF16), b_rgate[0].reshape(1, D_LRU),
                 _block_diag(w_igate[0]).astype(BF16), b_igate[0].reshape(1, D_LRU),
                 lru_L[0][None], batch, t_pad)

    lam_params = jnp.stack([lambda_q1[0], lambda_k1[0], lambda_q2[0], lambda_k2[0]])
    y_att = _attention(proj_qkv, rel_bias, lam_params, subln_g[0][None], batch, t_pad)

    w_rt = jnp.concatenate([w_group[0], w_router[0],
                            jnp.zeros((D_MODEL, LANES - N_GROUPS - N_EXPERTS), F32)], axis=1)
    b_rt = jnp.concatenate([b_group[0], b_router[0],
                            jnp.zeros((LANES - N_GROUPS - N_EXPERTS,), F32)])[None]
    h1, u2, logits = _out_proj(y_lru, y_att, h0, w_out[0].astype(BF16), g_ffn[0][None], w_rt, b_rt)

    idx, gates, counts = _route(logits)

    cnt = counts[0, N_GROUPS:N_GROUPS + N_EXPERTS].astype(jnp.int32)
    padded = (cnt + DISPATCH_BLK - 1) // DISPATCH_BLK * DISPATCH_BLK
    pends = jnp.cumsum(padded)
    pstarts = pends - padded
    dest = _dest(idx, jnp.pad(pstarts.astype(F32), (0, LANES - N_EXPERTS))[None])
    d0 = dest[:, 0]
    d1 = dest[:, 1]
    n_blocks = -(-(2 * n_pad + N_EXPERTS * (DISPATCH_BLK - 1)) // DISPATCH_BLK)
    rows = n_blocks * DISPATCH_BLK
    sched = _expert_schedule(padded, pstarts, n_blocks)

    x_buf = _dispatch(d0, d1, pends.astype(jnp.int32), u2, rows)
    y_buf = _experts(sched, x_buf, w_gate[0], w_up[0], w_down[0])
    return _combine(d0, d1, h1, gates, g_final[None], y_buf, batch, seq)
```

```python
import functools
import math

import numpy as np
import jax
import jax.numpy as jnp
from jax import lax
from jax.experimental import pallas as pl
from jax.experimental.pallas import tpu as pltpu

D_MODEL = 2048
N_META = 16
D_LRU = 1024
D_ATT = 1024
LRU_BLOCKS = 16
LRU_BLOCK_DIM = 64
CONV_WIDTH = 4
LRU_C = 8.0
N_HEADS = 8
V_HEAD_DIM = 128
QK_HEAD_DIM = 64
N_BUCKETS = 32
MAX_DISTANCE = 128
N_GROUPS = 4
EXPERTS_PER_GROUP = 8
N_EXPERTS = 32
D_EXPERT = 1024
D_IN_PROJ = 2 * D_LRU + 3 * D_ATT
EPS = 1e-6
LAMBDA_INIT = 0.8 - 0.6 * math.exp(-0.3 * 0)
LOG2_E = math.log2(math.e)

F32 = jnp.float32
BF16 = jnp.bfloat16

LANES = 128
SUBLANES = 8
MXU_DIM = 256
TT = 768
PROJ_TN = 1024
OUT_TM = TT // 2
ATT_SUB = MXU_DIM
ATT_LOOKAHEAD = 3
VT_ROWS = V_HEAD_DIM + 16
LRU_CB = MXU_DIM
DISPATCH_BLK = 256
VMEM_LIMIT = 56 * 1024 * 1024
NEG_BIG = -1e30


def _round_up(a, b):
    return -(-a // b) * b


def _rmsnorm(x, g):
    ms = jnp.mean(x * x, axis=-1, keepdims=True)
    return x * lax.rsqrt(ms + EPS) * g


def _inproj_kernel(x_ref, g_ref, w_ref, lru_ref, qkv_ref, u_scr):
    n = pl.program_id(1)

    @pl.when(n == 0)
    def _():
        u_scr[...] = _rmsnorm(x_ref[...], g_ref[...]).astype(BF16)

    y = jnp.dot(u_scr[...], w_ref[...], preferred_element_type=F32)

    @pl.when(n < 2)
    def _():
        lru_ref[...] = y

    @pl.when(n >= 2)
    def _():
        qkv_ref[...] = y.astype(BF16)


def _in_proj(h0, g_mix, w_in_bf16):
    n_pad = h0.shape[0]
    n_col = D_IN_PROJ // PROJ_TN
    return pl.pallas_call(
        _inproj_kernel,
        grid=(n_pad // TT, n_col),
        in_specs=[
            pl.BlockSpec((TT, D_MODEL), lambda m, n: (m, 0)),
            pl.BlockSpec((1, D_MODEL), lambda m, n: (0, 0)),
            pl.BlockSpec((D_MODEL, PROJ_TN), lambda m, n: (0, n)),
        ],
        out_specs=[
            pl.BlockSpec((TT, PROJ_TN), lambda m, n: (m, jnp.minimum(n, 1))),
            pl.BlockSpec((TT, PROJ_TN), lambda m, n: (m, jnp.maximum(n - 2, 0))),
        ],
        out_shape=[
            jax.ShapeDtypeStruct((n_pad, 2 * D_LRU), F32),
            jax.ShapeDtypeStruct((n_pad, 3 * D_ATT), BF16),
        ],
        scratch_shapes=[pltpu.VMEM((TT, D_MODEL), BF16)],
        compiler_params=pltpu.CompilerParams(
            dimension_semantics=("arbitrary", "arbitrary"), vmem_limit_bytes=VMEM_LIMIT),
        name="in_proj",
    )(h0, g_mix, w_in_bf16)


def _shift_rows(x, tail, s):
    r = pltpu.roll(x, s, 0)
    tail_row = lax.broadcasted_iota(jnp.int32, tail.shape, 0)
    head = jnp.where(tail_row < s, pltpu.roll(tail, s, 0), r[0:SUBLANES])
    return jnp.concatenate([head, r[SUBLANES:]], axis=0)


def _lru_kernel(x_ref, gate_ref, cw_ref, cb_ref, wr_ref, br_ref, wi_ref, bi_ref, lam_ref,
                y_ref, tail_scr, h_scr):
    t = pl.program_id(2)

    @pl.when(t == 0)
    def _():
        tail_scr[...] = jnp.zeros_like(tail_scr)
        h_scr[...] = jnp.zeros_like(h_scr)

    x = x_ref[...]
    tail = tail_scr[...]
    cw = cw_ref[...]
    xc = cb_ref[...] + x * cw[CONV_WIDTH - 1:CONV_WIDTH]
    for k in range(CONV_WIDTH - 1):
        xc = xc + _shift_rows(x, tail, CONV_WIDTH - 1 - k) * cw[k:k + 1]
    tail_scr[...] = x[TT - SUBLANES:TT]

    xb = xc.astype(BF16)
    r = jax.nn.sigmoid(jnp.dot(xb, wr_ref[...], preferred_element_type=F32) + br_ref[...])
    i = jax.nn.sigmoid(jnp.dot(xb, wi_ref[...], preferred_element_type=F32) + bi_ref[...])
    lam = lam_ref[...]
    log_sig = jnp.minimum(lam, 0.0) - jnp.log(1.0 + jnp.exp(-jnp.abs(lam)))
    log_a = (LRU_C * r) * log_sig
    a = jnp.exp(log_a)
    b = jnp.sqrt(1.0 - a * a) * (i * xc)

    row = lax.broadcasted_iota(jnp.int32, a.shape, 0) % SUBLANES
    s = 1
    while s < SUBLANES:
        keep = row >= s
        a_sh = jnp.where(keep, pltpu.roll(a, s, 0), 1.0)
        b_sh = jnp.where(keep, pltpu.roll(b, s, 0), 0.0)
        b = a * b_sh + b
        a = a * a_sh
        s *= 2
    carry = h_scr[...]
    groups = []
    for g in range(TT // SUBLANES):
        rows = slice(g * SUBLANES, (g + 1) * SUBLANES)
        hg = b[rows] + a[rows] * carry
        groups.append(hg)
        carry = jnp.broadcast_to(hg[SUBLANES - 1:SUBLANES], hg.shape)
    h_scr[...] = carry
    h = jnp.concatenate(groups, axis=0)
    y_ref[...] = (h * jax.nn.gelu(gate_ref[...])).astype(BF16)


def _lru(proj_lru, conv_w, conv_b, wr_bd, b_r, wi_bd, b_i, lru_l, batch, t_pad):
    n_pad = proj_lru.shape[0]
    n_t = t_pad // TT
    n_c = D_LRU // LRU_CB
    vec = lambda: pl.BlockSpec((1, LRU_CB), lambda b, c, t: (0, c))
    return pl.pallas_call(
        _lru_kernel,
        grid=(batch, n_c, n_t),
        in_specs=[
            pl.BlockSpec((TT, LRU_CB), lambda b, c, t: (b * n_t + t, c)),
            pl.BlockSpec((TT, LRU_CB), lambda b, c, t: (b * n_t + t, n_c + c)),
            pl.BlockSpec((CONV_WIDTH, LRU_CB), lambda b, c, t: (0, c)),
            vec(),
            pl.BlockSpec((None, LRU_CB, LRU_CB), lambda b, c, t: (c, 0, 0)),
            vec(),
            pl.BlockSpec((None, LRU_CB, LRU_CB), lambda b, c, t: (c, 0, 0)),
            vec(),
            vec(),
        ],
        out_specs=pl.BlockSpec((TT, LRU_CB), lambda b, c, t: (b * n_t + t, c)),
        out_shape=jax.ShapeDtypeStruct((n_pad, D_LRU), BF16),
        scratch_shapes=[pltpu.VMEM((SUBLANES, LRU_CB), F32), pltpu.VMEM((SUBLANES, LRU_CB), F32)],
        compiler_params=pltpu.CompilerParams(
            dimension_semantics=("arbitrary", "arbitrary", "arbitrary"),
            vmem_limit_bytes=VMEM_LIMIT),
        name="rglru",
    )(proj_lru, proj_lru, conv_w, conv_b, wr_bd, b_r, wi_bd, b_i, lru_l)


def _bucket_thresholds():
    max_exact = N_BUCKETS // 2
    n = np.arange(0, MAX_DISTANCE + 1)
    nf = np.maximum(n, 1).astype(np.float32)
    large = max_exact + (np.log(nf / np.float32(max_exact)) / np.float32(math.log(MAX_DISTANCE / max_exact))
                         * np.float32(N_BUCKETS - max_exact)).astype(np.int32)
    large = np.minimum(large, N_BUCKETS - 1)
    bucket = np.where(n < max_exact, n, large)
    return [int(np.argmax(bucket >= k)) for k in range(1, N_BUCKETS)]


_BUCKET_THR = _bucket_thresholds()


def _attn_kernel(rb_ref, q_ref, k_ref, v_ref, lam_ref, g_ref, o_ref, bias_scr, vt_scr, m_scr, acc_scr):
    h = pl.program_id(0)
    b = pl.program_id(1)
    qi = pl.program_id(2)
    tk = TT
    n_kv = vt_scr.shape[0]

    @pl.when(jnp.logical_and(b == 0, qi == 0))
    def _():
        k_pos = lax.broadcasted_iota(jnp.int32, (tk, tk), 0)
        q_pos = lax.broadcasted_iota(jnp.int32, (tk, tk), 1)
        far = rb_ref[N_BUCKETS - 1, h]
        for d in range(2):
            rel = q_pos - k_pos + d * tk
            val = jnp.full((tk, tk), rb_ref[0, h] - far, F32)
            for kk, thr in enumerate(_BUCKET_THR):
                val = jnp.where(rel >= thr, rb_ref[kk + 1, h] - far, val)
            val = val * LOG2_E
            if d == 0:
                val = jnp.where(rel >= 0, val, -jnp.inf)
            bias_scr[d] = val

    @pl.when(qi == 0)
    def _():
        for t in range(n_kv):
            vt_scr[t, 0:V_HEAD_DIM, :] = v_ref[t * tk:(t + 1) * tk, :].astype(F32).T.astype(BF16)
            vt_scr[t, V_HEAD_DIM:VT_ROWS, :] = jnp.ones((VT_ROWS - V_HEAD_DIM, tk), BF16)

    q = q_ref[...]
    lane = lax.broadcasted_iota(jnp.int32, q.shape, 1)
    scale = (QK_HEAD_DIM ** -0.5) * LOG2_E
    qf = q.astype(F32) * scale
    qs = jnp.concatenate([jnp.where(lane < QK_HEAD_DIM, qf, 0.0),
                          jnp.where(lane >= QK_HEAD_DIM, qf, 0.0)], axis=0).astype(BF16)

    m_scr[...] = jnp.full(m_scr.shape, NEG_BIG, F32)
    acc_scr[...] = jnp.zeros_like(acc_scr)

    def tile_group(tiles):
        n_sub = 2 * tk // ATT_SUB
        stages = [(t, c) for t in range(len(tiles)) for c in range(n_sub)]

        def scores(t, c):
            j, bias_idx = tiles[t]
            kt = k_ref[pl.ds(pl.multiple_of(j * tk, tk), tk), :]
            cols = slice(c * ATT_SUB, (c + 1) * ATT_SUB)
            s = lax.dot_general(kt, qs[cols], (((1,), (1,)), ((), ())), preferred_element_type=F32)
            if bias_idx is not None:
                q0 = (c * ATT_SUB) % tk
                s = s + bias_scr[bias_idx, :, q0:q0 + ATT_SUB]
            return s

        pending = [scores(*st) for st in stages[:ATT_LOOKAHEAD]]
        for n, (t, c) in enumerate(stages):
            cols = slice(c * ATT_SUB, (c + 1) * ATT_SUB)
            s = pending.pop(0)
            if n + ATT_LOOKAHEAD < len(stages):
                pending.append(scores(*stages[n + ATT_LOOKAHEAD]))
            m_prev = m_scr[:, cols]
            m_new = jnp.maximum(m_prev, jnp.max(s, axis=0, keepdims=True))
            alpha = jnp.exp2(m_prev - m_new)
            p = jnp.exp2(s - m_new).astype(BF16)
            vt = vt_scr[tiles[t][0]]
            acc_scr[:, cols] = alpha * acc_scr[:, cols] + jnp.dot(vt, p, preferred_element_type=F32)
            m_scr[:, cols] = m_new

    n_far = jnp.maximum(qi - 1, 0)

    def far_pair(i, carry):
        tile_group([(2 * i, None), (2 * i + 1, None)])
        return carry

    lax.fori_loop(0, n_far // 2, far_pair, 0)

    @pl.when(n_far % 2 == 1)
    def _():
        tile_group([(n_far - 1, None)])

    @pl.when(qi >= 1)
    def _():
        tile_group([(qi - 1, 1), (qi, 0)])

    @pl.when(qi == 0)
    def _():
        tile_group([(qi, 0)])

    lam_p = lam_ref[...]
    lam = (jnp.exp(jnp.sum(lam_p[0:1] * lam_p[1:2], axis=-1, keepdims=True))
           - jnp.exp(jnp.sum(lam_p[2:3] * lam_p[3:4], axis=-1, keepdims=True)) + LAMBDA_INIT)
    acc = acc_scr[...]
    num = acc[0:V_HEAD_DIM]
    den = acc[V_HEAD_DIM:V_HEAD_DIM + 1]
    o_t = num[:, :tk] / den[:, :tk] - lam * (num[:, tk:] / den[:, tk:])
    o_ref[...] = (_rmsnorm(o_t.T, g_ref[...]) * (1.0 - LAMBDA_INIT)).astype(BF16)


def _attention(qkv, rel_bias, lam_params, subln_g, batch, t_pad):
    n_q = t_pad // TT
    qkv3 = qkv.reshape(batch, t_pad, 3 * D_ATT)
    out = pl.pallas_call(
        _attn_kernel,
        grid_spec=pltpu.PrefetchScalarGridSpec(
            num_scalar_prefetch=0,
            grid=(N_HEADS, batch, n_q),
            in_specs=[
                pl.BlockSpec(memory_space=pltpu.SMEM),
                pl.BlockSpec((None, TT, V_HEAD_DIM), lambda h, b, q: (b, q, h)),
                pl.BlockSpec((None, t_pad, V_HEAD_DIM), lambda h, b, q: (b, 0, N_HEADS + h)),
                pl.BlockSpec((None, t_pad, V_HEAD_DIM), lambda h, b, q: (b, 0, 2 * N_HEADS + h)),
                pl.BlockSpec((4, QK_HEAD_DIM), lambda h, b, q: (0, 0)),
                pl.BlockSpec((1, V_HEAD_DIM), lambda h, b, q: (0, 0)),
            ],
            out_specs=pl.BlockSpec((None, TT, V_HEAD_DIM), lambda h, b, q: (b, q, h)),
            scratch_shapes=[
                pltpu.VMEM((2, TT, TT), F32),
                pltpu.VMEM((n_q, VT_ROWS, TT), BF16),
                pltpu.VMEM((1, 2 * TT), F32),
                pltpu.VMEM((VT_ROWS, 2 * TT), F32),
            ],
        ),
        out_shape=jax.ShapeDtypeStruct((batch, t_pad, D_ATT), BF16),
        compiler_params=pltpu.CompilerParams(
            dimension_semantics=("arbitrary", "arbitrary", "arbitrary"),
            vmem_limit_bytes=VMEM_LIMIT),
        name="diff_attention",
    )(rel_bias, qkv3, qkv3, qkv3, lam_params, subln_g)
    return out.reshape(batch * t_pad, D_ATT)


def _outproj_kernel(yl_ref, ya_ref, h0_ref, w_ref, g_ref, wrt_ref, brt_ref, h1_ref, u_ref, lg_ref):
    acc = jnp.dot(yl_ref[...], w_ref[0:D_LRU, :], preferred_element_type=F32)
    acc = acc + jnp.dot(ya_ref[...], w_ref[D_LRU:, :], preferred_element_type=F32)
    h1 = h0_ref[...] + acc
    h1_ref[...] = h1
    u = _rmsnorm(h1, g_ref[...])
    u_ref[...] = u
    u_hi = u.astype(BF16)
    u_lo = (u - u_hi.astype(F32)).astype(BF16)
    w = wrt_ref[...]
    w_hi = w.astype(BF16)
    w_lo = (w - w_hi.astype(F32)).astype(BF16)
    lg = jnp.dot(u_hi, w_hi, preferred_element_type=F32)
    lg = lg + jnp.dot(u_lo, w_hi, preferred_element_type=F32)
    lg = lg + jnp.dot(u_hi, w_lo, preferred_element_type=F32)
    lg_ref[...] = lg + brt_ref[...]


def _out_proj(y_lru, y_att, h0, w_out_bf16, g_ffn, w_rt, b_rt):
    n_pad = h0.shape[0]
    row = lambda w: pl.BlockSpec((OUT_TM, w), lambda m: (m, 0))
    full = lambda a, b: pl.BlockSpec((a, b), lambda m: (0, 0))
    return pl.pallas_call(
        _outproj_kernel,
        grid=(n_pad // OUT_TM,),
        in_specs=[row(D_LRU), row(D_ATT), row(D_MODEL), full(D_MODEL, D_MODEL), full(1, D_MODEL),
                  full(D_MODEL, LANES), full(1, LANES)],
        out_specs=[row(D_MODEL), row(D_MODEL), row(LANES)],
        out_shape=[
            jax.ShapeDtypeStruct((n_pad, D_MODEL), F32),
            jax.ShapeDtypeStruct((n_pad, D_MODEL), F32),
            jax.ShapeDtypeStruct((n_pad, LANES), F32),
        ],
        compiler_params=pltpu.CompilerParams(
            dimension_semantics=("arbitrary",), vmem_limit_bytes=VMEM_LIMIT),
        name="out_proj",
    )(y_lru, y_att, h0, w_out_bf16, g_ffn, w_rt, b_rt)


def _route_kernel(lg_ref, idx_ref, gate_ref, cnt_ref, carry_scr):
    step = pl.program_id(0)

    @pl.when(step == 0)
    def _():
        carry_scr[...] = jnp.zeros_like(carry_scr)

    lg = lg_ref[...]
    lane = lax.broadcasted_iota(jnp.int32, lg.shape, 1)
    first = lambda mask: jnp.min(jnp.where(mask, lane, LANES), axis=-1, keepdims=True)

    is_g = lane < N_GROUPS
    gl = jnp.where(is_g, lg, -jnp.inf)
    gmax = jnp.max(gl, axis=-1, keepdims=True)
    grp = first(gl == gmax)
    gsum = jnp.sum(jnp.where(is_g, jnp.exp(gl - gmax), 0.0), axis=-1, keepdims=True)
    p_grp = 1.0 / gsum

    lane_e = lane - N_GROUPS
    in_grp = (lane_e >= 0) & (lane_e < N_EXPERTS) & ((lane_e // EXPERTS_PER_GROUP) == grp)
    el = jnp.where(in_grp, lg, -jnp.inf)
    v1 = jnp.max(el, axis=-1, keepdims=True)
    i1 = first(el == v1)
    el2 = jnp.where(lane == i1, -jnp.inf, el)
    v2 = jnp.max(el2, axis=-1, keepdims=True)
    i2 = first(el2 == v2)
    e2 = jnp.exp(v2 - v1)
    den = 1.0 + e2
    g0 = p_grp * (1.0 / den)
    g1 = p_grp * (e2 / den)

    hit1 = lane == i1
    hit2 = lane == i2
    onehot = jnp.where(hit1 | hit2, 1.0, 0.0)
    rr = lax.broadcasted_iota(jnp.int32, (TT, TT), 0)
    cc = lax.broadcasted_iota(jnp.int32, (TT, TT), 1)
    tri = jnp.where(cc < rr, 1.0, 0.0).astype(BF16)
    rank = jnp.dot(tri, onehot.astype(BF16), preferred_element_type=F32) + carry_scr[...]
    r0 = jnp.sum(jnp.where(hit1, rank, 0.0), axis=-1, keepdims=True).astype(jnp.int32)
    r1 = jnp.sum(jnp.where(hit2, rank, 0.0), axis=-1, keepdims=True).astype(jnp.int32)
    carry_scr[...] = carry_scr[...] + jnp.sum(onehot, axis=0, keepdims=True)

    idx_ref[...] = jnp.where(lane == 0, i1 - N_GROUPS,
                             jnp.where(lane == 1, i2 - N_GROUPS,
                                       jnp.where(lane == 2, r0, jnp.where(lane == 3, r1, 0))))
    gate_ref[...] = jnp.where(lane == 0, g0, jnp.where(lane == 1, g1, 0.0))
    cnt_ref[...] = carry_scr[...]


def _route(logits):
    n_pad = logits.shape[0]
    row = pl.BlockSpec((TT, LANES), lambda m: (m, 0))
    return pl.pallas_call(
        _route_kernel,
        grid=(n_pad // TT,),
        in_specs=[row],
        out_specs=[row, row, pl.BlockSpec((1, LANES), lambda m: (0, 0))],
        out_shape=[
            jax.ShapeDtypeStruct((n_pad, LANES), jnp.int32),
            jax.ShapeDtypeStruct((n_pad, LANES), F32),
            jax.ShapeDtypeStruct((1, LANES), F32),
        ],
        scratch_shapes=[pltpu.VMEM((1, LANES), F32)],
        compiler_params=pltpu.CompilerParams(
            dimension_semantics=("arbitrary",), vmem_limit_bytes=VMEM_LIMIT),
        name="route",
    )(logits)


def _dest_kernel(idx_ref, ps_ref, o_ref):
    idx = idx_ref[...]
    lane = lax.broadcasted_iota(jnp.int32, idx.shape, 1)
    ps = ps_ref[...]
    pick = lambda e: jnp.sum(jnp.where(lane == e, ps, 0.0), axis=-1, keepdims=True).astype(jnp.int32)
    d0 = pick(idx[:, 0:1]) + idx[:, 2:3]
    d1 = pick(idx[:, 1:2]) + idx[:, 3:4]
    o_ref[...] = jnp.where(lane == 0, d0, jnp.where(lane == 1, d1, 0))


def _dest(idx, pstart_row):
    n_pad = idx.shape[0]
    row = pl.BlockSpec((TT, LANES), lambda m: (m, 0))
    return pl.pallas_call(
        _dest_kernel,
        grid=(n_pad // TT,),
        in_specs=[row, pl.BlockSpec((1, LANES), lambda m: (0, 0))],
        out_specs=row,
        out_shape=jax.ShapeDtypeStruct((n_pad, LANES), jnp.int32),
        compiler_params=pltpu.CompilerParams(
            dimension_semantics=("arbitrary",), vmem_limit_bytes=VMEM_LIMIT),
        name="dest",
    )(idx, pstart_row)


ROW_UNROLL = 8


def _row_copy(src, s_row, dst, d_row, sem):
    return pltpu.make_async_copy(src.at[pl.ds(s_row, 1)], dst.at[pl.ds(d_row, 1)], sem)


def _dispatch_kernel(d0_ref, d1_ref, pe_ref, u_ref, xb_ref, zero_scr, sem, zsem):
    step = pl.program_id(0)
    base = step * TT

    def zero_block(e):
        start = pl.multiple_of(pe_ref[e] - DISPATCH_BLK, DISPATCH_BLK)
        return pltpu.make_async_copy(zero_scr, xb_ref.at[pl.ds(start, DISPATCH_BLK)], zsem)

    @pl.when(step == 0)
    def _():
        zero_scr[...] = jnp.zeros_like(zero_scr)

        def nonempty(e):
            return pe_ref[e] > jnp.where(e == 0, 0, pe_ref[jnp.maximum(e - 1, 0)])

        def zstart(e, carry):
            @pl.when(nonempty(e))
            def _():
                zero_block(e).start()
            return carry

        def zwait(e, carry):
            @pl.when(nonempty(e))
            def _():
                zero_block(e).wait()
            return carry

        lax.fori_loop(0, N_EXPERTS, zstart, 0)
        lax.fori_loop(0, N_EXPERTS, zwait, 0)

        def tail_block(i):
            start = pl.multiple_of(i * DISPATCH_BLK, DISPATCH_BLK)
            return pltpu.make_async_copy(zero_scr, xb_ref.at[pl.ds(start, DISPATCH_BLK)], zsem)

        def tstart(i, carry):
            tail_block(i).start()
            return carry

        def twait(i, carry):
            tail_block(i).wait()
            return carry

        n_used = pe_ref[N_EXPERTS - 1] // DISPATCH_BLK
        n_blocks = xb_ref.shape[0] // DISPATCH_BLK
        lax.fori_loop(n_used, n_blocks, tstart, 0)
        lax.fori_loop(n_used, n_blocks, twait, 0)

    def issue(r, carry):
        _row_copy(u_ref, r, xb_ref, d0_ref[base + r], sem.at[0]).start()
        _row_copy(u_ref, r, xb_ref, d1_ref[base + r], sem.at[1]).start()
        return carry

    lax.fori_loop(0, TT, issue, 0, unroll=ROW_UNROLL)
    for k in range(2):
        pltpu.make_async_copy(u_ref, xb_ref.at[pl.ds(0, TT)], sem.at[k]).wait()


def _dispatch(d0, d1, pends, u2, rows):
    n_pad = u2.shape[0]
    return pl.pallas_call(
        _dispatch_kernel,
        grid_spec=pltpu.PrefetchScalarGridSpec(
            num_scalar_prefetch=3,
            grid=(n_pad // TT,),
            in_specs=[pl.BlockSpec((TT, D_MODEL), lambda m, d0, d1, pe: (m, 0))],
            out_specs=pl.BlockSpec(memory_space=pl.ANY),
            scratch_shapes=[
                pltpu.VMEM((DISPATCH_BLK, D_MODEL), F32),
                pltpu.SemaphoreType.DMA((2,)),
                pltpu.SemaphoreType.DMA(()),
            ],
        ),
        out_shape=jax.ShapeDtypeStruct((rows, D_MODEL), F32),
        compiler_params=pltpu.CompilerParams(
            dimension_semantics=("arbitrary",), vmem_limit_bytes=VMEM_LIMIT),
        name="dispatch",
    )(d0, d1, pends, u2)


WEIGHT_UNITS = 8
WEIGHT_DEPTH = 4


def _stream_weights(i, tgt_ref, eseq_ref, ntot_ref, w_hbm, stage, wbf, sem, done_scr):
    unit_rows = stage.shape[2]

    def unit_copies(u):
        e = eseq_ref[u // WEIGHT_UNITS]
        r = pl.multiple_of((u % WEIGHT_UNITS) * unit_rows, unit_rows)
        b = u % WEIGHT_DEPTH
        return [pltpu.make_async_copy(w.at[e, pl.ds(r, unit_rows), :], stage.at[b, k], sem.at[b, k])
                for k, w in enumerate(w_hbm)]

    @pl.when(i == 0)
    def _():
        done_scr[0] = 0
        for u in range(WEIGHT_DEPTH):
            for c in unit_copies(u):
                c.start()

    def body(u, carry):
        for c in unit_copies(u):
            c.wait()
        slot = (u // WEIGHT_UNITS) % 2
        r = pl.multiple_of((u % WEIGHT_UNITS) * unit_rows, unit_rows)
        b = u % WEIGHT_DEPTH
        for k in range(len(w_hbm)):
            wbf[slot, k, pl.ds(r, unit_rows), :] = stage[b, k].astype(BF16)

        @pl.when(u + WEIGHT_DEPTH < ntot_ref[0])
        def _():
            for c in unit_copies(u + WEIGHT_DEPTH):
                c.start()

        return carry

    lax.fori_loop(done_scr[0], tgt_ref[i], body, 0)
    done_scr[0] = tgt_ref[i]


def _expert_up_kernel(tgt_ref, eseq_ref, ntot_ref, slot_ref, nu_ref, x_ref, wg_hbm, wu_hbm, a_ref,
                      stage, wbf, sem, done_scr):
    i = pl.program_id(0)
    _stream_weights(i, tgt_ref, eseq_ref, ntot_ref, (wg_hbm, wu_hbm), stage, wbf, sem, done_scr)
    used = i < nu_ref[0]

    @pl.when(used)
    def _():
        slot = slot_ref[i]
        x = x_ref[...].astype(BF16)
        g = jnp.dot(x, wbf[slot, 0], preferred_element_type=F32)
        u = jnp.dot(x, wbf[slot, 1], preferred_element_type=F32)
        a_ref[...] = (jax.nn.silu(g) * u).astype(BF16)

    @pl.when(jnp.logical_not(used))
    def _():
        a_ref[...] = jnp.zeros_like(a_ref)


def _expert_down_kernel(tgt_ref, eseq_ref, ntot_ref, slot_ref, nu_ref, a_ref, wd_hbm, y_ref,
                        stage, wbf, sem, done_scr):
    i = pl.program_id(0)
    _stream_weights(i, tgt_ref, eseq_ref, ntot_ref, (wd_hbm,), stage, wbf, sem, done_scr)
    used = i < nu_ref[0]

    @pl.when(used)
    def _():
        y_ref[...] = jnp.dot(a_ref[...], wbf[slot_ref[i], 0], preferred_element_type=F32)

    @pl.when(jnp.logical_not(used))
    def _():
        y_ref[...] = jnp.zeros_like(y_ref)


def _experts(sched, x_buf, w_gate, w_up, w_down):
    rows = x_buf.shape[0]
    n_blocks = rows // DISPATCH_BLK
    blk_map = lambda i, tg, es, nt, sl, nu: (jnp.minimum(i, jnp.maximum(nu[0] - 1, 0)), 0)
    out_map = lambda i, tg, es, nt, sl, nu: (i, 0)
    params = pltpu.CompilerParams(dimension_semantics=("arbitrary",), vmem_limit_bytes=VMEM_LIMIT)
    hbm = pl.BlockSpec(memory_space=pl.ANY)

    def scratch(n_mats, k_dim, n_dim):
        return [
            pltpu.VMEM((WEIGHT_DEPTH, n_mats, k_dim // WEIGHT_UNITS, n_dim), F32),
            pltpu.VMEM((2, n_mats, k_dim, n_dim), BF16),
            pltpu.SemaphoreType.DMA((WEIGHT_DEPTH, n_mats)),
            pltpu.SMEM((1,), jnp.int32),
        ]

    act = pl.pallas_call(
        _expert_up_kernel,
        grid_spec=pltpu.PrefetchScalarGridSpec(
            num_scalar_prefetch=5,
            grid=(n_blocks,),
            in_specs=[pl.BlockSpec((DISPATCH_BLK, D_MODEL), blk_map), hbm, hbm],
            out_specs=pl.BlockSpec((DISPATCH_BLK, D_EXPERT), out_map),
            scratch_shapes=scratch(2, D_MODEL, D_EXPERT),
        ),
        out_shape=jax.ShapeDtypeStruct((rows, D_EXPERT), BF16),
        compiler_params=params,
        name="experts_up",
    )(*sched, x_buf, w_gate, w_up)
    return pl.pallas_call(
        _expert_down_kernel,
        grid_spec=pltpu.PrefetchScalarGridSpec(
            num_scalar_prefetch=5,
            grid=(n_blocks,),
            in_specs=[pl.BlockSpec((DISPATCH_BLK, D_EXPERT), blk_map), hbm],
            out_specs=pl.BlockSpec((DISPATCH_BLK, D_MODEL), out_map),
            scratch_shapes=scratch(1, D_EXPERT, D_MODEL),
        ),
        out_shape=jax.ShapeDtypeStruct((rows, D_MODEL), F32),
        compiler_params=params,
        name="experts_down",
    )(*sched, act, w_down)


def _expert_schedule(padded, pstarts, n_blocks):
    nonempty = padded > 0
    seq_of_expert = jnp.cumsum(nonempty.astype(jnp.int32)) - 1
    n_seq = jnp.sum(nonempty.astype(jnp.int32))
    experts = jnp.arange(N_EXPERTS, dtype=jnp.int32)
    is_kth = jnp.logical_and(nonempty[None, :], seq_of_expert[None, :] == experts[:, None])
    eseq = jnp.sum(jnp.where(is_kth, experts[None, :], 0), axis=1)
    pends = pstarts + padded
    n_used = pends[-1] // DISPATCH_BLK
    blk = jnp.arange(n_blocks, dtype=jnp.int32)
    blk_c = jnp.minimum(blk, jnp.maximum(n_used - 1, 0))
    blk_expert = jnp.minimum(
        jnp.sum((pends[None, :] <= (blk_c * DISPATCH_BLK)[:, None]).astype(jnp.int32), axis=1), N_EXPERTS - 1)
    onehot = (blk_expert[:, None] == experts[None, :]).astype(jnp.int32)
    pick = lambda v: jnp.sum(onehot * v[None, :], axis=1)
    q = pick(seq_of_expert)
    j = blk_c - pick(pstarts) // DISPATCH_BLK
    n = jnp.maximum(pick(padded) // DISPATCH_BLK, 1)
    n_total = n_seq * WEIGHT_UNITS
    tgt = jnp.minimum(WEIGHT_UNITS * (q + 1) + (WEIGHT_UNITS * j) // n, n_total)
    tgt = jnp.where(blk < n_used, tgt, n_total)
    i32 = lambda a: a.astype(jnp.int32)
    return i32(tgt), eseq, i32(n_total)[None], i32(q % 2), i32(n_used)[None]


COMBINE_TM = 512


def _combine_kernel(t_pad, d0_ref, d1_ref, h_hbm, gate_hbm, g_ref, yb_ref, o_ref,
                    h_scr, gate_scr, y0_scr, y1_scr, sem):
    b = pl.program_id(0)
    j = pl.program_id(1)
    base = pl.multiple_of(b * t_pad + N_META + j * COMBINE_TM, 8)

    tile_copies = [
        pltpu.make_async_copy(h_hbm.at[pl.ds(base, COMBINE_TM)], h_scr, sem.at[2]),
        pltpu.make_async_copy(gate_hbm.at[pl.ds(base, COMBINE_TM)], gate_scr, sem.at[3]),
    ]
    for c in tile_copies:
        c.start()

    def issue(r, carry):
        _row_copy(yb_ref, d0_ref[base + r], y0_scr, r, sem.at[0]).start()
        _row_copy(yb_ref, d1_ref[base + r], y1_scr, r, sem.at[1]).start()
        return carry

    lax.fori_loop(0, COMBINE_TM, issue, 0, unroll=ROW_UNROLL)
    pltpu.make_async_copy(yb_ref.at[pl.ds(0, COMBINE_TM)], y0_scr, sem.at[0]).wait()
    pltpu.make_async_copy(yb_ref.at[pl.ds(0, COMBINE_TM)], y1_scr, sem.at[1]).wait()
    for c in tile_copies:
        c.wait()

    gate = gate_scr[...]
    h = h_scr[...] + gate[:, 0:1] * y0_scr[...] + gate[:, 1:2] * y1_scr[...]
    o_ref[...] = _rmsnorm(h, g_ref[...])


def _combine(d0, d1, h1, gates, g_final, y_buf, batch, seq):
    assert seq % COMBINE_TM == 0 and N_META % 8 == 0
    tile = lambda w: pltpu.VMEM((COMBINE_TM, w), F32)
    return pl.pallas_call(
        functools.partial(_combine_kernel, h1.shape[0] // batch),
        grid_spec=pltpu.PrefetchScalarGridSpec(
            num_scalar_prefetch=2,
            grid=(batch, seq // COMBINE_TM),
            in_specs=[
                pl.BlockSpec(memory_space=pl.ANY),
                pl.BlockSpec(memory_space=pl.ANY),
                pl.BlockSpec((1, D_MODEL), lambda b, j, d0, d1: (0, 0)),
                pl.BlockSpec(memory_space=pl.ANY),
            ],
            out_specs=pl.BlockSpec((None, COMBINE_TM, D_MODEL), lambda b, j, d0, d1: (b, j, 0)),
            scratch_shapes=[tile(D_MODEL), tile(LANES), tile(D_MODEL), tile(D_MODEL),
                            pltpu.SemaphoreType.DMA((4,))],
        ),
        out_shape=jax.ShapeDtypeStruct((batch, seq, D_MODEL), F32),
        compiler_params=pltpu.CompilerParams(
            dimension_semantics=("arbitrary", "arbitrary"), vmem_limit_bytes=VMEM_LIMIT),
        name="combine",
    )(d0, d1, h1, gates, g_final, y_buf)


def _block_diag(w):
    per = LRU_CB // LRU_BLOCK_DIM
    w4 = w.reshape(LRU_BLOCKS // per, per, LRU_BLOCK_DIM, LRU_BLOCK_DIM)
    eye = jnp.eye(per, dtype=w.dtype)
    bd = jnp.einsum('cpij,pq->cpiqj', w4, eye)
    return bd.reshape(LRU_BLOCKS // per, LRU_CB, LRU_CB)


def kernel(x, meta_tokens, g_mix, w_in, conv_w, conv_b, w_rgate, b_rgate, w_igate, b_igate, lru_L, lambda_q1, lambda_k1, lambda_q2, lambda_k2, subln_g, rel_bias, w_out, g_ffn, w_group, b_group, w_router, b_router, w_gate, w_up, w_down, g_final):
    batch, seq, _ = x.shape
    t_real = N_META + seq
    t_pad = _round_up(t_real, TT)
    n_pad = batch * t_pad

    meta = jnp.broadcast_to(meta_tokens[None].astype(x.dtype), (batch, N_META, D_MODEL))
    h0 = jnp.concatenate([meta, x, jnp.zeros((batch, t_pad - t_real, D_MODEL), x.dtype)], axis=1)
    h0 = h0.reshape(n_pad, D_MODEL)

    proj_lru, proj_qkv = _in_proj(h0, g_mix[0][None], w_in[0].astype(BF16))

    y_lru = _lru(proj_lru, conv_w[0], conv_b[0][None],
                 _block_diag(w_rgate[0]).astype(BF16), b_rgate[0].reshape(1, D_LRU),
                 _block_diag(w_igate[0]).astype(BF16), b_igate[0].reshape(1, D_LRU),
                 lru_L[0][None], batch, t_pad)

    lam_params = jnp.stack([lambda_q1[0], lambda_k1[0], lambda_q2[0], lambda_k2[0]])
    y_att = _attention(proj_qkv, rel_bias, lam_params, subln_g[0][None], batch, t_pad)

    w_rt = jnp.concatenate([w_group[0], w_router[0],
                            jnp.zeros((D_MODEL, LANES - N_GROUPS - N_EXPERTS), F32)], axis=1)
    b_rt = jnp.concatenate([b_group[0], b_router[0],
                            jnp.zeros((LANES - N_GROUPS - N_EXPERTS,), F32)])[None]
    h1, u2, logits = _out_proj(y_lru, y_att, h0, w_out[0].astype(BF16), g_ffn[0][None], w_rt, b_rt)

    idx, gates, counts = _route(logits)

    cnt = counts[0, N_GROUPS:N_GROUPS + N_EXPERTS].astype(jnp.int32)
    padded = (cnt + DISPATCH_BLK - 1) // DISPATCH_BLK * DISPATCH_BLK
    pends = jnp.cumsum(padded)
    pstarts = pends - padded
    dest = _dest(idx, jnp.pad(pstarts.astype(F32), (0, LANES - N_EXPERTS))[None])
    d0 = dest[:, 0]
    d1 = dest[:, 1]
    n_blocks = -(-(2 * n_pad + N_EXPERTS * (DISPATCH_BLK - 1)) // DISPATCH_BLK)
    rows = n_blocks * DISPATCH_BLK
    sched = _expert_schedule(padded, pstarts, n_blocks)

    x_buf = _dispatch(d0, d1, pends.astype(jnp.int32), u2, rows)
    y_buf = _experts(sched, x_buf, w_gate[0], w_up[0], w_down[0])
    return _combine(d0, d1, h1, gates, g_final[None], y_buf, batch, seq)
```

```python
import functools
import math

import numpy as np
import jax
import jax.numpy as jnp
from jax import lax
from jax.experimental import pallas as pl
from jax.experimental.pallas import tpu as pltpu

D_MODEL = 2048
N_META = 16
D_LRU = 1024
D_ATT = 1024
LRU_BLOCKS = 16
LRU_BLOCK_DIM = 64
CONV_WIDTH = 4
LRU_C = 8.0
N_HEADS = 8
V_HEAD_DIM = 128
QK_HEAD_DIM = 64
N_BUCKETS = 32
MAX_DISTANCE = 128
N_GROUPS = 4
EXPERTS_PER_GROUP = 8
N_EXPERTS = 32
D_EXPERT = 1024
D_IN_PROJ = 2 * D_LRU + 3 * D_ATT
EPS = 1e-6
LAMBDA_INIT = 0.8 - 0.6 * math.exp(-0.3 * 0)
LOG2_E = math.log2(math.e)

F32 = jnp.float32
BF16 = jnp.bfloat16

LANES = 128
SUBLANES = 8
MXU_DIM = 256
TT = 768
PROJ_TN = 1024
OUT_TM = TT // 2
ATT_SUB = MXU_DIM
ATT_LOOKAHEAD = 4
ATT_GROUP = 4
VT_ROWS = V_HEAD_DIM + 16
LRU_CB = MXU_DIM
DISPATCH_BLK = 256
VMEM_LIMIT = 56 * 1024 * 1024
NEG_BIG = -1e30


def _round_up(a, b):
    return -(-a // b) * b


def _rmsnorm(x, g):
    ms = jnp.mean(x * x, axis=-1, keepdims=True)
    return x * lax.rsqrt(ms + EPS) * g


def _inproj_kernel(x_ref, g_ref, w_ref, lru_ref, qkv_ref, u_scr):
    n = pl.program_id(1)

    @pl.when(n == 0)
    def _():
        u_scr[...] = _rmsnorm(x_ref[...], g_ref[...]).astype(BF16)

    y = jnp.dot(u_scr[...], w_ref[...], preferred_element_type=F32)

    @pl.when(n < 2)
    def _():
        lru_ref[...] = y

    @pl.when(n >= 2)
    def _():
        qkv_ref[...] = y.astype(BF16)


def _in_proj(h0, g_mix, w_in_bf16):
    n_pad = h0.shape[0]
    n_col = D_IN_PROJ // PROJ_TN
    return pl.pallas_call(
        _inproj_kernel,
        grid=(n_pad // TT, n_col),
        in_specs=[
            pl.BlockSpec((TT, D_MODEL), lambda m, n: (m, 0)),
            pl.BlockSpec((1, D_MODEL), lambda m, n: (0, 0)),
            pl.BlockSpec((D_MODEL, PROJ_TN), lambda m, n: (0, n)),
        ],
        out_specs=[
            pl.BlockSpec((TT, PROJ_TN), lambda m, n: (m, jnp.minimum(n, 1))),
            pl.BlockSpec((TT, PROJ_TN), lambda m, n: (m, jnp.maximum(n - 2, 0))),
        ],
        out_shape=[
            jax.ShapeDtypeStruct((n_pad, 2 * D_LRU), F32),
            jax.ShapeDtypeStruct((n_pad, 3 * D_ATT), BF16),
        ],
        scratch_shapes=[pltpu.VMEM((TT, D_MODEL), BF16)],
        compiler_params=pltpu.CompilerParams(
            dimension_semantics=("arbitrary", "arbitrary"), vmem_limit_bytes=VMEM_LIMIT),
        name="in_proj",
    )(h0, g_mix, w_in_bf16)


def _shift_rows(x, tail, s):
    r = pltpu.roll(x, s, 0)
    tail_row = lax.broadcasted_iota(jnp.int32, tail.shape, 0)
    head = jnp.where(tail_row < s, pltpu.roll(tail, s, 0), r[0:SUBLANES])
    return jnp.concatenate([head, r[SUBLANES:]], axis=0)


def _lru_kernel(x_ref, gate_ref, cw_ref, cb_ref, wr_ref, br_ref, wi_ref, bi_ref, lam_ref,
                y_ref, tail_scr, h_scr):
    t = pl.program_id(2)

    @pl.when(t == 0)
    def _():
        tail_scr[...] = jnp.zeros_like(tail_scr)
        h_scr[...] = jnp.zeros_like(h_scr)

    x = x_ref[...]
    tail = tail_scr[...]
    cw = cw_ref[...]
    xc = cb_ref[...] + x * cw[CONV_WIDTH - 1:CONV_WIDTH]
    for k in range(CONV_WIDTH - 1):
        xc = xc + _shift_rows(x, tail, CONV_WIDTH - 1 - k) * cw[k:k + 1]
    tail_scr[...] = x[TT - SUBLANES:TT]

    xb = xc.astype(BF16)
    r = jax.nn.sigmoid(jnp.dot(xb, wr_ref[...], preferred_element_type=F32) + br_ref[...])
    i = jax.nn.sigmoid(jnp.dot(xb, wi_ref[...], preferred_element_type=F32) + bi_ref[...])
    lam = lam_ref[...]
    log_sig = jnp.minimum(lam, 0.0) - jnp.log(1.0 + jnp.exp(-jnp.abs(lam)))
    log_a = (LRU_C * r) * log_sig
    a = jnp.exp(log_a)
    b = jnp.sqrt(1.0 - a * a) * (i * xc)

    row = lax.broadcasted_iota(jnp.int32, a.shape, 0) % SUBLANES
    s = 1
    while s < SUBLANES:
        keep = row >= s
        a_sh = jnp.where(keep, pltpu.roll(a, s, 0), 1.0)
        b_sh = jnp.where(keep, pltpu.roll(b, s, 0), 0.0)
        b = a * b_sh + b
        a = a * a_sh
        s *= 2
    carry = h_scr[...]
    groups = []
    for g in range(TT // SUBLANES):
        rows = slice(g * SUBLANES, (g + 1) * SUBLANES)
        hg = b[rows] + a[rows] * carry
        groups.append(hg)
        carry = jnp.broadcast_to(hg[SUBLANES - 1:SUBLANES], hg.shape)
    h_scr[...] = carry
    h = jnp.concatenate(groups, axis=0)
    y_ref[...] = (h * jax.nn.gelu(gate_ref[...])).astype(BF16)


def _lru(proj_lru, conv_w, conv_b, wr_bd, b_r, wi_bd, b_i, lru_l, batch, t_pad):
    n_pad = proj_lru.shape[0]
    n_t = t_pad // TT
    n_c = D_LRU // LRU_CB
    vec = lambda: pl.BlockSpec((1, LRU_CB), lambda b, c, t: (0, c))
    return pl.pallas_call(
        _lru_kernel,
        grid=(batch, n_c, n_t),
        in_specs=[
            pl.BlockSpec((TT, LRU_CB), lambda b, c, t: (b * n_t + t, c)),
            pl.BlockSpec((TT, LRU_CB), lambda b, c, t: (b * n_t + t, n_c + c)),
            pl.BlockSpec((CONV_WIDTH, LRU_CB), lambda b, c, t: (0, c)),
            vec(),
            pl.BlockSpec((None, LRU_CB, LRU_CB), lambda b, c, t: (c, 0, 0)),
            vec(),
            pl.BlockSpec((None, LRU_CB, LRU_CB), lambda b, c, t: (c, 0, 0)),
            vec(),
            vec(),
        ],
        out_specs=pl.BlockSpec((TT, LRU_CB), lambda b, c, t: (b * n_t + t, c)),
        out_shape=jax.ShapeDtypeStruct((n_pad, D_LRU), BF16),
        scratch_shapes=[pltpu.VMEM((SUBLANES, LRU_CB), F32), pltpu.VMEM((SUBLANES, LRU_CB), F32)],
        compiler_params=pltpu.CompilerParams(
            dimension_semantics=("arbitrary", "arbitrary", "arbitrary"),
            vmem_limit_bytes=VMEM_LIMIT),
        name="rglru",
    )(proj_lru, proj_lru, conv_w, conv_b, wr_bd, b_r, wi_bd, b_i, lru_l)


def _bucket_thresholds():
    max_exact = N_BUCKETS // 2
    n = np.arange(0, MAX_DISTANCE + 1)
    nf = np.maximum(n, 1).astype(np.float32)
    large = max_exact + (np.log(nf / np.float32(max_exact)) / np.float32(math.log(MAX_DISTANCE / max_exact))
                         * np.float32(N_BUCKETS - max_exact)).astype(np.int32)
    large = np.minimum(large, N_BUCKETS - 1)
    bucket = np.where(n < max_exact, n, large)
    return [int(np.argmax(bucket >= k)) for k in range(1, N_BUCKETS)]


_BUCKET_THR = _bucket_thresholds()


def _attn_kernel(rb_ref, q_ref, k_ref, v_ref, lam_ref, g_ref, o_ref, bias_scr, vt_scr, m_scr, acc_scr):
    h = pl.program_id(0)
    b = pl.program_id(1)
    qi = pl.program_id(2)
    tk = TT
    n_kv = vt_scr.shape[0]

    @pl.when(jnp.logical_and(b == 0, qi == 0))
    def _():
        k_pos = lax.broadcasted_iota(jnp.int32, (tk, tk), 0)
        q_pos = lax.broadcasted_iota(jnp.int32, (tk, tk), 1)
        far = rb_ref[N_BUCKETS - 1, h]
        for d in range(2):
            rel = q_pos - k_pos + d * tk
            val = jnp.full((tk, tk), rb_ref[0, h] - far, F32)
            for kk, thr in enumerate(_BUCKET_THR):
                val = jnp.where(rel >= thr, rb_ref[kk + 1, h] - far, val)
            val = val * LOG2_E
            if d == 0:
                val = jnp.where(rel >= 0, val, -jnp.inf)
            bias_scr[d] = val

    @pl.when(qi == 0)
    def _():
        for t in range(n_kv):
            vt_scr[t, 0:V_HEAD_DIM, :] = v_ref[t * tk:(t + 1) * tk, :].astype(F32).T.astype(BF16)
            vt_scr[t, V_HEAD_DIM:VT_ROWS, :] = jnp.ones((VT_ROWS - V_HEAD_DIM, tk), BF16)

    q = q_ref[...]
    lane = lax.broadcasted_iota(jnp.int32, q.shape, 1)
    scale = (QK_HEAD_DIM ** -0.5) * LOG2_E
    qf = q.astype(F32) * scale
    qs = jnp.concatenate([jnp.where(lane < QK_HEAD_DIM, qf, 0.0),
                          jnp.where(lane >= QK_HEAD_DIM, qf, 0.0)], axis=0).astype(BF16)

    m_scr[...] = jnp.full(m_scr.shape, NEG_BIG, F32)
    acc_scr[...] = jnp.zeros_like(acc_scr)

    def tile_group(tiles):
        n_sub = 2 * tk // ATT_SUB
        stages = [(t, c) for t in range(len(tiles)) for c in range(n_sub)]

        def n_keys(t, c):
            return (c * ATT_SUB) % tk + ATT_SUB if tiles[t][1] == 0 else tk

        def scores(t, c):
            j, bias_idx = tiles[t]
            nk = n_keys(t, c)
            kt = k_ref[pl.ds(pl.multiple_of(j * tk, tk), nk), :]
            cols = slice(c * ATT_SUB, (c + 1) * ATT_SUB)
            s = lax.dot_general(kt, qs[cols], (((1,), (1,)), ((), ())), preferred_element_type=F32)
            if bias_idx is not None:
                q0 = (c * ATT_SUB) % tk
                s = s + bias_scr[bias_idx, 0:nk, q0:q0 + ATT_SUB]
            return s

        pending = [scores(*st) for st in stages[:ATT_LOOKAHEAD]]
        for n, (t, c) in enumerate(stages):
            cols = slice(c * ATT_SUB, (c + 1) * ATT_SUB)
            s = pending.pop(0)
            if n + ATT_LOOKAHEAD < len(stages):
                pending.append(scores(*stages[n + ATT_LOOKAHEAD]))
            m_prev = m_scr[:, cols]
            m_new = jnp.maximum(m_prev, jnp.max(s, axis=0, keepdims=True))
            alpha = jnp.exp2(m_prev - m_new)
            p = jnp.exp2(s - m_new).astype(BF16)
            vt = vt_scr[tiles[t][0], :, 0:n_keys(t, c)]
            acc_scr[:, cols] = alpha * acc_scr[:, cols] + jnp.dot(vt, p, preferred_element_type=F32)
            m_scr[:, cols] = m_new

    n_far = jnp.maximum(qi - 1, 0)

    def far_group(i, carry):
        tile_group([(ATT_GROUP * i + t, None) for t in range(ATT_GROUP)])
        return carry

    lax.fori_loop(0, n_far // ATT_GROUP, far_group, 0)
    done = n_far - n_far % ATT_GROUP
    piece = ATT_GROUP // 2
    while piece >= 1:
        @pl.when((n_far % (2 * piece)) >= piece)
        def _(done=done, piece=piece):
            tile_group([(done + t, None) for t in range(piece)])
        done = done + jnp.where((n_far % (2 * piece)) >= piece, piece, 0)
        piece //= 2

    @pl.when(qi >= 1)
    def _():
        tile_group([(qi - 1, 1), (qi, 0)])

    @pl.when(qi == 0)
    def _():
        tile_group([(qi, 0)])

    lam_p = lam_ref[...]
    lam = (jnp.exp(jnp.sum(lam_p[0:1] * lam_p[1:2], axis=-1, keepdims=True))
           - jnp.exp(jnp.sum(lam_p[2:3] * lam_p[3:4], axis=-1, keepdims=True)) + LAMBDA_INIT)
    acc = acc_scr[...]
    num = acc[0:V_HEAD_DIM]
    den = acc[V_HEAD_DIM:V_HEAD_DIM + 1]
    o_t = num[:, :tk] / den[:, :tk] - lam * (num[:, tk:] / den[:, tk:])
    o_ref[...] = (_rmsnorm(o_t.T, g_ref[...]) * (1.0 - LAMBDA_INIT)).astype(BF16)


def _attention(qkv, rel_bias, lam_params, subln_g, batch, t_pad):
    n_q = t_pad // TT
    qkv3 = qkv.reshape(batch, t_pad, 3 * D_ATT)
    out = pl.pallas_call(
        _attn_kernel,
        grid_spec=pltpu.PrefetchScalarGridSpec(
            num_scalar_prefetch=0,
            grid=(N_HEADS, batch, n_q),
            in_specs=[
                pl.BlockSpec(memory_space=pltpu.SMEM),
                pl.BlockSpec((None, TT, V_HEAD_DIM), lambda h, b, q: (b, q, h)),
                pl.BlockSpec((None, t_pad, V_HEAD_DIM), lambda h, b, q: (b, 0, N_HEADS + h)),
                pl.BlockSpec((None, t_pad, V_HEAD_DIM), lambda h, b, q: (b, 0, 2 * N_HEADS + h)),
                pl.BlockSpec((4, QK_HEAD_DIM), lambda h, b, q: (0, 0)),
                pl.BlockSpec((1, V_HEAD_DIM), lambda h, b, q: (0, 0)),
            ],
            out_specs=pl.BlockSpec((None, TT, V_HEAD_DIM), lambda h, b, q: (b, q, h)),
            scratch_shapes=[
                pltpu.VMEM((2, TT, TT), F32),
                pltpu.VMEM((n_q, VT_ROWS, TT), BF16),
                pltpu.VMEM((1, 2 * TT), F32),
                pltpu.VMEM((VT_ROWS, 2 * TT), F32),
            ],
        ),
        out_shape=jax.ShapeDtypeStruct((batch, t_pad, D_ATT), BF16),
        compiler_params=pltpu.CompilerParams(
            dimension_semantics=("arbitrary", "arbitrary", "arbitrary"),
            vmem_limit_bytes=VMEM_LIMIT),
        name="diff_attention",
    )(rel_bias, qkv3, qkv3, qkv3, lam_params, subln_g)
    return out.reshape(batch * t_pad, D_ATT)


def _outproj_kernel(yl_ref, ya_ref, h0_ref, w_ref, g_ref, wrt_ref, brt_ref, h1_ref, u_ref, lg_ref):
    acc = jnp.dot(yl_ref[...], w_ref[0:D_LRU, :], preferred_element_type=F32)
    acc = acc + jnp.dot(ya_ref[...], w_ref[D_LRU:, :], preferred_element_type=F32)
    h1 = h0_ref[...] + acc
    h1_ref[...] = h1
    u = _rmsnorm(h1, g_ref[...])
    u_ref[...] = u
    u_hi = u.astype(BF16)
    u_lo = (u - u_hi.astype(F32)).astype(BF16)
    w = wrt_ref[...]
    w_hi = w.astype(BF16)
    w_lo = (w - w_hi.astype(F32)).astype(BF16)
    lg = jnp.dot(u_hi, w_hi, preferred_element_type=F32)
    lg = lg + jnp.dot(u_lo, w_hi, preferred_element_type=F32)
    lg = lg + jnp.dot(u_hi, w_lo, preferred_element_type=F32)
    lg_ref[...] = lg + brt_ref[...]


def _out_proj(y_lru, y_att, h0, w_out_bf16, g_ffn, w_rt, b_rt):
    n_pad = h0.shape[0]
    row = lambda w: pl.BlockSpec((OUT_TM, w), lambda m: (m, 0))
    full = lambda a, b: pl.BlockSpec((a, b), lambda m: (0, 0))
    return pl.pallas_call(
        _outproj_kernel,
        grid=(n_pad // OUT_TM,),
        in_specs=[row(D_LRU), row(D_ATT), row(D_MODEL), full(D_MODEL, D_MODEL), full(1, D_MODEL),
                  full(D_MODEL, LANES), full(1, LANES)],
        out_specs=[row(D_MODEL), row(D_MODEL), row(LANES)],
        out_shape=[
            jax.ShapeDtypeStruct((n_pad, D_MODEL), F32),
            jax.ShapeDtypeStruct((n_pad, D_MODEL), F32),
            jax.ShapeDtypeStruct((n_pad, LANES), F32),
        ],
        compiler_params=pltpu.CompilerParams(
            dimension_semantics=("arbitrary",), vmem_limit_bytes=VMEM_LIMIT),
        name="out_proj",
    )(y_lru, y_att, h0, w_out_bf16, g_ffn, w_rt, b_rt)


def _route_kernel(lg_ref, idx_ref, gate_ref, cnt_ref, carry_scr):
    step = pl.program_id(0)

    @pl.when(step == 0)
    def _():
        carry_scr[...] = jnp.zeros_like(carry_scr)

    lg = lg_ref[...]
    lane = lax.broadcasted_iota(jnp.int32, lg.shape, 1)
    first = lambda mask: jnp.min(jnp.where(mask, lane, LANES), axis=-1, keepdims=True)

    is_g = lane < N_GROUPS
    gl = jnp.where(is_g, lg, -jnp.inf)
    gmax = jnp.max(gl, axis=-1, keepdims=True)
    grp = first(gl == gmax)
    gsum = jnp.sum(jnp.where(is_g, jnp.exp(gl - gmax), 0.0), axis=-1, keepdims=True)
    p_grp = 1.0 / gsum

    lane_e = lane - N_GROUPS
    in_grp = (lane_e >= 0) & (lane_e < N_EXPERTS) & ((lane_e // EXPERTS_PER_GROUP) == grp)
    el = jnp.where(in_grp, lg, -jnp.inf)
    v1 = jnp.max(el, axis=-1, keepdims=True)
    i1 = first(el == v1)
    el2 = jnp.where(lane == i1, -jnp.inf, el)
    v2 = jnp.max(el2, axis=-1, keepdims=True)
    i2 = first(el2 == v2)
    e2 = jnp.exp(v2 - v1)
    den = 1.0 + e2
    g0 = p_grp * (1.0 / den)
    g1 = p_grp * (e2 / den)

    hit1 = lane == i1
    hit2 = lane == i2
    onehot = jnp.where(hit1 | hit2, 1.0, 0.0)
    rr = lax.broadcasted_iota(jnp.int32, (TT, TT), 0)
    cc = lax.broadcasted_iota(jnp.int32, (TT, TT), 1)
    tri = jnp.where(cc < rr, 1.0, 0.0).astype(BF16)
    rank = jnp.dot(tri, onehot.astype(BF16), preferred_element_type=F32) + carry_scr[...]
    r0 = jnp.sum(jnp.where(hit1, rank, 0.0), axis=-1, keepdims=True).astype(jnp.int32)
    r1 = jnp.sum(jnp.where(hit2, rank, 0.0), axis=-1, keepdims=True).astype(jnp.int32)
    carry_scr[...] = carry_scr[...] + jnp.sum(onehot, axis=0, keepdims=True)

    idx_ref[...] = jnp.where(lane == 0, i1 - N_GROUPS,
                             jnp.where(lane == 1, i2 - N_GROUPS,
                                       jnp.where(lane == 2, r0, jnp.where(lane == 3, r1, 0))))
    gate_ref[...] = jnp.where(lane == 0, g0, jnp.where(lane == 1, g1, 0.0))
    cnt_ref[...] = carry_scr[...]


def _route(logits):
    n_pad = logits.shape[0]
    row = pl.BlockSpec((TT, LANES), lambda m: (m, 0))
    return pl.pallas_call(
        _route_kernel,
        grid=(n_pad // TT,),
        in_specs=[row],
        out_specs=[row, row, pl.BlockSpec((1, LANES), lambda m: (0, 0))],
        out_shape=[
            jax.ShapeDtypeStruct((n_pad, LANES), jnp.int32),
            jax.ShapeDtypeStruct((n_pad, LANES), F32),
            jax.ShapeDtypeStruct((1, LANES), F32),
        ],
        scratch_shapes=[pltpu.VMEM((1, LANES), F32)],
        compiler_params=pltpu.CompilerParams(
            dimension_semantics=("arbitrary",), vmem_limit_bytes=VMEM_LIMIT),
        name="route",
    )(logits)


def _dest_kernel(idx_ref, ps_ref, o_ref):
    idx = idx_ref[...]
    lane = lax.broadcasted_iota(jnp.int32, idx.shape, 1)
    ps = ps_ref[...]
    pick = lambda e: jnp.sum(jnp.where(lane == e, ps, 0.0), axis=-1, keepdims=True).astype(jnp.int32)
    d0 = pick(idx[:, 0:1]) + idx[:, 2:3]
    d1 = pick(idx[:, 1:2]) + idx[:, 3:4]
    o_ref[...] = jnp.where(lane == 0, d0, jnp.where(lane == 1, d1, 0))


def _dest(idx, pstart_row):
    n_pad = idx.shape[0]
    row = pl.BlockSpec((TT, LANES), lambda m: (m, 0))
    return pl.pallas_call(
        _dest_kernel,
        grid=(n_pad // TT,),
        in_specs=[row, pl.BlockSpec((1, LANES), lambda m: (0, 0))],
        out_specs=row,
        out_shape=jax.ShapeDtypeStruct((n_pad, LANES), jnp.int32),
        compiler_params=pltpu.CompilerParams(
            dimension_semantics=("arbitrary",), vmem_limit_bytes=VMEM_LIMIT),
        name="dest",
    )(idx, pstart_row)


ROW_UNROLL = 8


def _row_copy(src, s_row, dst, d_row, sem):
    return pltpu.make_async_copy(src.at[pl.ds(s_row, 1)], dst.at[pl.ds(d_row, 1)], sem)


def _dispatch_kernel(d0_ref, d1_ref, pe_ref, u_ref, xb_ref, zero_scr, sem, zsem):
    step = pl.program_id(0)
    base = step * TT

    def zero_block(e):
        start = pl.multiple_of(pe_ref[e] - DISPATCH_BLK, DISPATCH_BLK)
        return pltpu.make_async_copy(zero_scr, xb_ref.at[pl.ds(start, DISPATCH_BLK)], zsem)

    @pl.when(step == 0)
    def _():
        zero_scr[...] = jnp.zeros_like(zero_scr)

        def nonempty(e):
            return pe_ref[e] > jnp.where(e == 0, 0, pe_ref[jnp.maximum(e - 1, 0)])

        def zstart(e, carry):
            @pl.when(nonempty(e))
            def _():
                zero_block(e).start()
            return carry

        def zwait(e, carry):
            @pl.when(nonempty(e))
            def _():
                zero_block(e).wait()
            return carry

        lax.fori_loop(0, N_EXPERTS, zstart, 0)
        lax.fori_loop(0, N_EXPERTS, zwait, 0)

        def tail_block(i):
            start = pl.multiple_of(i * DISPATCH_BLK, DISPATCH_BLK)
            return pltpu.make_async_copy(zero_scr, xb_ref.at[pl.ds(start, DISPATCH_BLK)], zsem)

        def tstart(i, carry):
            tail_block(i).start()
            return carry

        def twait(i, carry):
            tail_block(i).wait()
            return carry

        n_used = pe_ref[N_EXPERTS - 1] // DISPATCH_BLK
        n_blocks = xb_ref.shape[0] // DISPATCH_BLK
        lax.fori_loop(n_used, n_blocks, tstart, 0)
        lax.fori_loop(n_used, n_blocks, twait, 0)

    def issue(r, carry):
        _row_copy(u_ref, r, xb_ref, d0_ref[base + r], sem.at[0]).start()
        _row_copy(u_ref, r, xb_ref, d1_ref[base + r], sem.at[1]).start()
        return carry

    lax.fori_loop(0, TT, issue, 0, unroll=ROW_UNROLL)
    for k in range(2):
        pltpu.make_async_copy(u_ref, xb_ref.at[pl.ds(0, TT)], sem.at[k]).wait()


def _dispatch(d0, d1, pends, u2, rows):
    n_pad = u2.shape[0]
    return pl.pallas_call(
        _dispatch_kernel,
        grid_spec=pltpu.PrefetchScalarGridSpec(
            num_scalar_prefetch=3,
            grid=(n_pad // TT,),
            in_specs=[pl.BlockSpec((TT, D_MODEL), lambda m, d0, d1, pe: (m, 0))],
            out_specs=pl.BlockSpec(memory_space=pl.ANY),
            scratch_shapes=[
                pltpu.VMEM((DISPATCH_BLK, D_MODEL), F32),
                pltpu.SemaphoreType.DMA((2,)),
                pltpu.SemaphoreType.DMA(()),
            ],
        ),
        out_shape=jax.ShapeDtypeStruct((rows, D_MODEL), F32),
        compiler_params=pltpu.CompilerParams(
            dimension_semantics=("arbitrary",), vmem_limit_bytes=VMEM_LIMIT),
        name="dispatch",
    )(d0, d1, pends, u2)


WEIGHT_UNITS = 8
WEIGHT_DEPTH = 4


def _stream_weights(i, tgt_ref, eseq_ref, ntot_ref, w_hbm, stage, wbf, sem, done_scr):
    unit_rows = stage.shape[2]

    def unit_copies(u):
        e = eseq_ref[u // WEIGHT_UNITS]
        r = pl.multiple_of((u % WEIGHT_UNITS) * unit_rows, unit_rows)
        b = u % WEIGHT_DEPTH
        return [pltpu.make_async_copy(w.at[e, pl.ds(r, unit_rows), :], stage.at[b, k], sem.at[b, k])
                for k, w in enumerate(w_hbm)]

    @pl.when(i == 0)
    def _():
        done_scr[0] = 0
        for u in range(WEIGHT_DEPTH):
            for c in unit_copies(u):
                c.start()

    def body(u, carry):
        for c in unit_copies(u):
            c.wait()
        slot = (u // WEIGHT_UNITS) % 2
        r = pl.multiple_of((u % WEIGHT_UNITS) * unit_rows, unit_rows)
        b = u % WEIGHT_DEPTH
        for k in range(len(w_hbm)):
            wbf[slot, k, pl.ds(r, unit_rows), :] = stage[b, k].astype(BF16)

        @pl.when(u + WEIGHT_DEPTH < ntot_ref[0])
        def _():
            for c in unit_copies(u + WEIGHT_DEPTH):
                c.start()

        return carry

    lax.fori_loop(done_scr[0], tgt_ref[i], body, 0)
    done_scr[0] = tgt_ref[i]


def _expert_up_kernel(tgt_ref, eseq_ref, ntot_ref, slot_ref, nu_ref, x_ref, wg_hbm, wu_hbm, a_ref,
                      stage, wbf, sem, done_scr):
    i = pl.program_id(0)
    _stream_weights(i, tgt_ref, eseq_ref, ntot_ref, (wg_hbm, wu_hbm), stage, wbf, sem, done_scr)
    used = i < nu_ref[0]

    @pl.when(used)
    def _():
        slot = slot_ref[i]
        x = x_ref[...].astype(BF16)
        g = jnp.dot(x, wbf[slot, 0], preferred_element_type=F32)
        u = jnp.dot(x, wbf[slot, 1], preferred_element_type=F32)
        a_ref[...] = (jax.nn.silu(g) * u).astype(BF16)

    @pl.when(jnp.logical_not(used))
    def _():
        a_ref[...] = jnp.zeros_like(a_ref)


def _expert_down_kernel(tgt_ref, eseq_ref, ntot_ref, slot_ref, nu_ref, a_ref, wd_hbm, y_ref,
                        stage, wbf, sem, done_scr):
    i = pl.program_id(0)
    _stream_weights(i, tgt_ref, eseq_ref, ntot_ref, (wd_hbm,), stage, wbf, sem, done_scr)
    used = i < nu_ref[0]

    @pl.when(used)
    def _():
        y_ref[...] = jnp.dot(a_ref[...], wbf[slot_ref[i], 0], preferred_element_type=F32)

    @pl.when(jnp.logical_not(used))
    def _():
        y_ref[...] = jnp.zeros_like(y_ref)


def _experts(sched, x_buf, w_gate, w_up, w_down):
    rows = x_buf.shape[0]
    n_blocks = rows // DISPATCH_BLK
    blk_map = lambda i, tg, es, nt, sl, nu: (jnp.minimum(i, jnp.maximum(nu[0] - 1, 0)), 0)
    out_map = lambda i, tg, es, nt, sl, nu: (i, 0)
    params = pltpu.CompilerParams(dimension_semantics=("arbitrary",), vmem_limit_bytes=VMEM_LIMIT)
    hbm = pl.BlockSpec(memory_space=pl.ANY)

    def scratch(n_mats, k_dim, n_dim):
        return [
            pltpu.VMEM((WEIGHT_DEPTH, n_mats, k_dim // WEIGHT_UNITS, n_dim), F32),
            pltpu.VMEM((2, n_mats, k_dim, n_dim), BF16),
            pltpu.SemaphoreType.DMA((WEIGHT_DEPTH, n_mats)),
            pltpu.SMEM((1,), jnp.int32),
        ]

    act = pl.pallas_call(
        _expert_up_kernel,
        grid_spec=pltpu.PrefetchScalarGridSpec(
            num_scalar_prefetch=5,
            grid=(n_blocks,),
            in_specs=[pl.BlockSpec((DISPATCH_BLK, D_MODEL), blk_map), hbm, hbm],
            out_specs=pl.BlockSpec((DISPATCH_BLK, D_EXPERT), out_map),
            scratch_shapes=scratch(2, D_MODEL, D_EXPERT),
        ),
        out_shape=jax.ShapeDtypeStruct((rows, D_EXPERT), BF16),
        compiler_params=params,
        name="experts_up",
    )(*sched, x_buf, w_gate, w_up)
    return pl.pallas_call(
        _expert_down_kernel,
        grid_spec=pltpu.PrefetchScalarGridSpec(
            num_scalar_prefetch=5,
            grid=(n_blocks,),
            in_specs=[pl.BlockSpec((DISPATCH_BLK, D_EXPERT), blk_map), hbm],
            out_specs=pl.BlockSpec((DISPATCH_BLK, D_MODEL), out_map),
            scratch_shapes=scratch(1, D_EXPERT, D_MODEL),
        ),
        out_shape=jax.ShapeDtypeStruct((rows, D_MODEL), F32),
        compiler_params=params,
        name="experts_down",
    )(*sched, act, w_down)


def _expert_schedule(padded, pstarts, n_blocks):
    nonempty = padded > 0
    seq_of_expert = jnp.cumsum(nonempty.astype(jnp.int32)) - 1
    n_seq = jnp.sum(nonempty.astype(jnp.int32))
    experts = jnp.arange(N_EXPERTS, dtype=jnp.int32)
    is_kth = jnp.logical_and(nonempty[None, :], seq_of_expert[None, :] == experts[:, None])
    eseq = jnp.sum(jnp.where(is_kth, experts[None, :], 0), axis=1)
    pends = pstarts + padded
    n_used = pends[-1] // DISPATCH_BLK
    blk = jnp.arange(n_blocks, dtype=jnp.int32)
    blk_c = jnp.minimum(blk, jnp.maximum(n_used - 1, 0))
    blk_expert = jnp.minimum(
        jnp.sum((pends[None, :] <= (blk_c * DISPATCH_BLK)[:, None]).astype(jnp.int32), axis=1), N_EXPERTS - 1)
    onehot = (blk_expert[:, None] == experts[None, :]).astype(jnp.int32)
    pick = lambda v: jnp.sum(onehot * v[None, :], axis=1)
    q = pick(seq_of_expert)
    j = blk_c - pick(pstarts) // DISPATCH_BLK
    n = jnp.maximum(pick(padded) // DISPATCH_BLK, 1)
    n_total = n_seq * WEIGHT_UNITS
    tgt = jnp.minimum(WEIGHT_UNITS * (q + 1) + (WEIGHT_UNITS * j) // n, n_total)
    tgt = jnp.where(blk < n_used, tgt, n_total)
    i32 = lambda a: a.astype(jnp.int32)
    return i32(tgt), eseq, i32(n_total)[None], i32(q % 2), i32(n_used)[None]


COMBINE_TM = 512


def _combine_kernel(t_pad, d0_ref, d1_ref, h_hbm, gate_hbm, g_ref, yb_ref, o_ref,
                    h_scr, gate_scr, y0_scr, y1_scr, sem):
    b = pl.program_id(0)
    j = pl.program_id(1)
    base = pl.multiple_of(b * t_pad + N_META + j * COMBINE_TM, 8)

    tile_copies = [
        pltpu.make_async_copy(h_hbm.at[pl.ds(base, COMBINE_TM)], h_scr, sem.at[2]),
        pltpu.make_async_copy(gate_hbm.at[pl.ds(base, COMBINE_TM)], gate_scr, sem.at[3]),
    ]
    for c in tile_copies:
        c.start()

    def issue(r, carry):
        _row_copy(yb_ref, d0_ref[base + r], y0_scr, r, sem.at[0]).start()
        _row_copy(yb_ref, d1_ref[base + r], y1_scr, r, sem.at[1]).start()
        return carry

    lax.fori_loop(0, COMBINE_TM, issue, 0, unroll=ROW_UNROLL)
    pltpu.make_async_copy(yb_ref.at[pl.ds(0, COMBINE_TM)], y0_scr, sem.at[0]).wait()
    pltpu.make_async_copy(yb_ref.at[pl.ds(0, COMBINE_TM)], y1_scr, sem.at[1]).wait()
    for c in tile_copies:
        c.wait()

    gate = gate_scr[...]
    h = h_scr[...] + gate[:, 0:1] * y0_scr[...] + gate[:, 1:2] * y1_scr[...]
    o_ref[...] = _rmsnorm(h, g_ref[...])


def _combine(d0, d1, h1, gates, g_final, y_buf, batch, seq):
    assert seq % COMBINE_TM == 0 and N_META % 8 == 0
    tile = lambda w: pltpu.VMEM((COMBINE_TM, w), F32)
    return pl.pallas_call(
        functools.partial(_combine_kernel, h1.shape[0] // batch),
        grid_spec=pltpu.PrefetchScalarGridSpec(
            num_scalar_prefetch=2,
            grid=(batch, seq // COMBINE_TM),
            in_specs=[
                pl.BlockSpec(memory_space=pl.ANY),
                pl.BlockSpec(memory_space=pl.ANY),
                pl.BlockSpec((1, D_MODEL), lambda b, j, d0, d1: (0, 0)),
                pl.BlockSpec(memory_space=pl.ANY),
            ],
            out_specs=pl.BlockSpec((None, COMBINE_TM, D_MODEL), lambda b, j, d0, d1: (b, j, 0)),
            scratch_shapes=[tile(D_MODEL), tile(LANES), tile(D_MODEL), tile(D_MODEL),
                            pltpu.SemaphoreType.DMA((4,))],
        ),
        out_shape=jax.ShapeDtypeStruct((batch, seq, D_MODEL), F32),
        compiler_params=pltpu.CompilerParams(
            dimension_semantics=("arbitrary", "arbitrary"), vmem_limit_bytes=VMEM_LIMIT),
        name="combine",
    )(d0, d1, h1, gates, g_final, y_buf)


def _block_diag(w):
    per = LRU_CB // LRU_BLOCK_DIM
    w4 = w.reshape(LRU_BLOCKS // per, per, LRU_BLOCK_DIM, LRU_BLOCK_DIM)
    eye = jnp.eye(per, dtype=w.dtype)
    bd = jnp.einsum('cpij,pq->cpiqj', w4, eye)
    return bd.reshape(LRU_BLOCKS // per, LRU_CB, LRU_CB)


def kernel(x, meta_tokens, g_mix, w_in, conv_w, conv_b, w_rgate, b_rgate, w_igate, b_igate, lru_L, lambda_q1, lambda_k1, lambda_q2, lambda_k2, subln_g, rel_bias, w_out, g_ffn, w_group, b_group, w_router, b_router, w_gate, w_up, w_down, g_final):
    batch, seq, _ = x.shape
    t_real = N_META + seq
    t_pad = _round_up(t_real, TT)
    n_pad = batch * t_pad

    meta = jnp.broadcast_to(meta_tokens[None].astype(x.dtype), (batch, N_META, D_MODEL))
    h0 = jnp.concatenate([meta, x, jnp.zeros((batch, t_pad - t_real, D_MODEL), x.dtype)], axis=1)
    h0 = h0.reshape(n_pad, D_MODEL)

    proj_lru, proj_qkv = _in_proj(h0, g_mix[0][None], w_in[0].astype(BF16))

    y_lru = _lru(proj_lru, conv_w[0], conv_b[0][None],
                 _block_diag(w_rgate[0]).astype(BF16), b_rgate[0].reshape(1, D_LRU),
                 _block_diag(w_igate[0]).astype(BF16), b_igate[0].reshape(1, D_LRU),
                 lru_L[0][None], batch, t_pad)

    lam_params = jnp.stack([lambda_q1[0], lambda_k1[0], lambda_q2[0], lambda_k2[0]])
    y_att = _attention(proj_qkv, rel_bias, lam_params, subln_g[0][None], batch, t_pad)

    w_rt = jnp.concatenate([w_group[0], w_router[0],
                            jnp.zeros((D_MODEL, LANES - N_GROUPS - N_EXPERTS), F32)], axis=1)
    b_rt = jnp.concatenate([b_group[0], b_router[0],
                            jnp.zeros((LANES - N_GROUPS - N_EXPERTS,), F32)])[None]
    h1, u2, logits = _out_proj(y_lru, y_att, h0, w_out[0].astype(BF16), g_ffn[0][None], w_rt, b_rt)

    idx, gates, counts = _route(logits)

    cnt = counts[0, N_GROUPS:N_GROUPS + N_EXPERTS].astype(jnp.int32)
    padded = (cnt + DISPATCH_BLK - 1) // DISPATCH_BLK * DISPATCH_BLK
    pends = jnp.cumsum(padded)
    pstarts = pends - padded
    dest = _dest(idx, jnp.pad(pstarts.astype(F32), (0, LANES - N_EXPERTS))[None])
    d0 = dest[:, 0]
    d1 = dest[:, 1]
    n_blocks = -(-(2 * n_pad + N_EXPERTS * (DISPATCH_BLK - 1)) // DISPATCH_BLK)
    rows = n_blocks * DISPATCH_BLK
    sched = _expert_schedule(padded, pstarts, n_blocks)

    x_buf = _dispatch(d0, d1, pends.astype(jnp.int32), u2, rows)
    y_buf = _experts(sched, x_buf, w_gate[0], w_up[0], w_down[0])
    return _combine(d0, d1, h1, gates, g_final[None], y_buf, batch, seq)
```

```python
import functools
import math

import numpy as np
import jax
import jax.numpy as jnp
from jax import lax
from jax.experimental import pallas as pl
from jax.experimental.pallas import tpu as pltpu

D_MODEL = 2048
N_META = 16
D_LRU = 1024
D_ATT = 1024
LRU_BLOCKS = 16
LRU_BLOCK_DIM = 64
CONV_WIDTH = 4
LRU_C = 8.0
N_HEADS = 8
V_HEAD_DIM = 128
QK_HEAD_DIM = 64
N_BUCKETS = 32
MAX_DISTANCE = 128
N_GROUPS = 4
EXPERTS_PER_GROUP = 8
N_EXPERTS = 32
D_EXPERT = 1024
D_IN_PROJ = 2 * D_LRU + 3 * D_ATT
EPS = 1e-6
LAMBDA_INIT = 0.8 - 0.6 * math.exp(-0.3 * 0)
LOG2_E = math.log2(math.e)

F32 = jnp.float32
BF16 = jnp.bfloat16

LANES = 128
SUBLANES = 8
MXU_DIM = 256
TT = 768
PROJ_TN = 1024
OUT_TM = TT // 2
ATT_SUB = MXU_DIM
ATT_LOOKAHEAD = 4
ATT_GROUP = 4
VT_ROWS = V_HEAD_DIM + 16
LRU_CB = MXU_DIM
DISPATCH_BLK = 256
VMEM_LIMIT = 56 * 1024 * 1024
NEG_BIG = -1e30


def _round_up(a, b):
    return -(-a // b) * b


def _rmsnorm(x, g):
    ms = jnp.mean(x * x, axis=-1, keepdims=True)
    return x * lax.rsqrt(ms + EPS) * g


def _inproj_kernel(x_ref, g_ref, w_ref, lru_ref, qkv_ref, u_scr):
    n = pl.program_id(1)

    @pl.when(n == 0)
    def _():
        u_scr[...] = _rmsnorm(x_ref[...], g_ref[...]).astype(BF16)

    y = jnp.dot(u_scr[...], w_ref[...], preferred_element_type=F32)

    @pl.when(n < 2)
    def _():
        lru_ref[...] = y

    @pl.when(n >= 2)
    def _():
        qkv_ref[...] = y.astype(BF16)


def _in_proj(h0, g_mix, w_in_bf16):
    n_pad = h0.shape[0]
    n_col = D_IN_PROJ // PROJ_TN
    return pl.pallas_call(
        _inproj_kernel,
        grid=(n_pad // TT, n_col),
        in_specs=[
            pl.BlockSpec((TT, D_MODEL), lambda m, n: (m, 0)),
            pl.BlockSpec((1, D_MODEL), lambda m, n: (0, 0)),
            pl.BlockSpec((D_MODEL, PROJ_TN), lambda m, n: (0, n)),
        ],
        out_specs=[
            pl.BlockSpec((TT, PROJ_TN), lambda m, n: (m, jnp.minimum(n, 1))),
            pl.BlockSpec((TT, PROJ_TN), lambda m, n: (m, jnp.maximum(n - 2, 0))),
        ],
        out_shape=[
            jax.ShapeDtypeStruct((n_pad, 2 * D_LRU), F32),
            jax.ShapeDtypeStruct((n_pad, 3 * D_ATT), BF16),
        ],
        scratch_shapes=[pltpu.VMEM((TT, D_MODEL), BF16)],
        compiler_params=pltpu.CompilerParams(
            dimension_semantics=("arbitrary", "arbitrary"), vmem_limit_bytes=VMEM_LIMIT),
        name="in_proj",
    )(h0, g_mix, w_in_bf16)


def _shift_rows(x, tail, s):
    r = pltpu.roll(x, s, 0)
    tail_row = lax.broadcasted_iota(jnp.int32, tail.shape, 0)
    head = jnp.where(tail_row < s, pltpu.roll(tail, s, 0), r[0:SUBLANES])
    return jnp.concatenate([head, r[SUBLANES:]], axis=0)


def _lru_kernel(x_ref, gate_ref, cw_ref, cb_ref, wr_ref, br_ref, wi_ref, bi_ref, lam_ref,
                y_ref, tail_scr, h_scr):
    t = pl.program_id(2)

    @pl.when(t == 0)
    def _():
        tail_scr[...] = jnp.zeros_like(tail_scr)
        h_scr[...] = jnp.zeros_like(h_scr)

    x = x_ref[...]
    tail = tail_scr[...]
    cw = cw_ref[...]
    xc = cb_ref[...] + x * cw[CONV_WIDTH - 1:CONV_WIDTH]
    for k in range(CONV_WIDTH - 1):
        xc = xc + _shift_rows(x, tail, CONV_WIDTH - 1 - k) * cw[k:k + 1]
    tail_scr[...] = x[TT - SUBLANES:TT]

    xb = xc.astype(BF16)
    r = jax.nn.sigmoid(jnp.dot(xb, wr_ref[...], preferred_element_type=F32) + br_ref[...])
    i = jax.nn.sigmoid(jnp.dot(xb, wi_ref[...], preferred_element_type=F32) + bi_ref[...])
    lam = lam_ref[...]
    log_sig = jnp.minimum(lam, 0.0) - jnp.log(1.0 + jnp.exp(-jnp.abs(lam)))
    log_a = (LRU_C * r) * log_sig
    a = jnp.exp(log_a)
    b = jnp.sqrt(1.0 - a * a) * (i * xc)

    row = lax.broadcasted_iota(jnp.int32, a.shape, 0) % SUBLANES
    s = 1
    while s < SUBLANES:
        keep = row >= s
        a_sh = jnp.where(keep, pltpu.roll(a, s, 0), 1.0)
        b_sh = jnp.where(keep, pltpu.roll(b, s, 0), 0.0)
        b = a * b_sh + b
        a = a * a_sh
        s *= 2
    carry = h_scr[...]
    groups = []
    for g in range(TT // SUBLANES):
        rows = slice(g * SUBLANES, (g + 1) * SUBLANES)
        hg = b[rows] + a[rows] * carry
        groups.append(hg)
        carry = jnp.broadcast_to(hg[SUBLANES - 1:SUBLANES], hg.shape)
    h_scr[...] = carry
    h = jnp.concatenate(groups, axis=0)
    y_ref[...] = (h * jax.nn.gelu(gate_ref[...])).astype(BF16)


def _lru(proj_lru, conv_w, conv_b, wr_bd, b_r, wi_bd, b_i, lru_l, batch, t_pad):
    n_pad = proj_lru.shape[0]
    n_t = t_pad // TT
    n_c = D_LRU // LRU_CB
    vec = lambda: pl.BlockSpec((1, LRU_CB), lambda b, c, t: (0, c))
    return pl.pallas_call(
        _lru_kernel,
        grid=(batch, n_c, n_t),
        in_specs=[
            pl.BlockSpec((TT, LRU_CB), lambda b, c, t: (b * n_t + t, c)),
            pl.BlockSpec((TT, LRU_CB), lambda b, c, t: (b * n_t + t, n_c + c)),
            pl.BlockSpec((CONV_WIDTH, LRU_CB), lambda b, c, t: (0, c)),
            vec(),
            pl.BlockSpec((None, LRU_CB, LRU_CB), lambda b, c, t: (c, 0, 0)),
            vec(),
            pl.BlockSpec((None, LRU_CB, LRU_CB), lambda b, c, t: (c, 0, 0)),
            vec(),
            vec(),
        ],
        out_specs=pl.BlockSpec((TT, LRU_CB), lambda b, c, t: (b * n_t + t, c)),
        out_shape=jax.ShapeDtypeStruct((n_pad, D_LRU), BF16),
        scratch_shapes=[pltpu.VMEM((SUBLANES, LRU_CB), F32), pltpu.VMEM((SUBLANES, LRU_CB), F32)],
        compiler_params=pltpu.CompilerParams(
            dimension_semantics=("arbitrary", "arbitrary", "arbitrary"),
            vmem_limit_bytes=VMEM_LIMIT),
        name="rglru",
    )(proj_lru, proj_lru, conv_w, conv_b, wr_bd, b_r, wi_bd, b_i, lru_l)


def _bucket_thresholds():
    max_exact = N_BUCKETS // 2
    n = np.arange(0, MAX_DISTANCE + 1)
    nf = np.maximum(n, 1).astype(np.float32)
    large = max_exact + (np.log(nf / np.float32(max_exact)) / np.float32(math.log(MAX_DISTANCE / max_exact))
                         * np.float32(N_BUCKETS - max_exact)).astype(np.int32)
    large = np.minimum(large, N_BUCKETS - 1)
    bucket = np.where(n < max_exact, n, large)
    return [int(np.argmax(bucket >= k)) for k in range(1, N_BUCKETS)]


_BUCKET_THR = _bucket_thresholds()


def _attn_kernel(rb_ref, q_ref, k_ref, v_ref, lam_ref, g_ref, o_ref, bias_scr, vt_scr, m_scr, acc_scr):
    h = pl.program_id(0)
    b = pl.program_id(1)
    qi = pl.program_id(2)
    tk = TT
    n_kv = vt_scr.shape[0]

    @pl.when(jnp.logical_and(b == 0, qi == 0))
    def _():
        k_pos = lax.broadcasted_iota(jnp.int32, (tk, tk), 0)
        q_pos = lax.broadcasted_iota(jnp.int32, (tk, tk), 1)
        far = rb_ref[N_BUCKETS - 1, h]
        for d in range(2):
            rel = q_pos - k_pos + d * tk
            val = jnp.full((tk, tk), rb_ref[0, h] - far, F32)
            for kk, thr in enumerate(_BUCKET_THR):
                val = jnp.where(rel >= thr, rb_ref[kk + 1, h] - far, val)
            val = val * LOG2_E
            if d == 0:
                val = jnp.where(rel >= 0, val, -jnp.inf)
            bias_scr[d] = val

    @pl.when(qi == 0)
    def _():
        for t in range(n_kv):
            vt_scr[t, 0:V_HEAD_DIM, :] = v_ref[t * tk:(t + 1) * tk, :].astype(F32).T.astype(BF16)
            vt_scr[t, V_HEAD_DIM:VT_ROWS, :] = jnp.ones((VT_ROWS - V_HEAD_DIM, tk), BF16)

    q = q_ref[...]
    lane = lax.broadcasted_iota(jnp.int32, q.shape, 1)
    scale = (QK_HEAD_DIM ** -0.5) * LOG2_E
    qf = q.astype(F32) * scale
    qs = jnp.concatenate([jnp.where(lane < QK_HEAD_DIM, qf, 0.0),
                          jnp.where(lane >= QK_HEAD_DIM, qf, 0.0)], axis=0).astype(BF16)

    m_scr[...] = jnp.full(m_scr.shape, NEG_BIG, F32)
    acc_scr[...] = jnp.zeros_like(acc_scr)

    def tile_group(tiles):
        n_sub = 2 * tk // ATT_SUB
        stages = [(t, c) for t in range(len(tiles)) for c in range(n_sub)]

        def n_keys(t, c):
            return (c * ATT_SUB) % tk + ATT_SUB if tiles[t][1] == 0 else tk

        def scores(t, c):
            j, bias_idx = tiles[t]
            nk = n_keys(t, c)
            kt = k_ref[pl.ds(pl.multiple_of(j * tk, tk), nk), :]
            cols = slice(c * ATT_SUB, (c + 1) * ATT_SUB)
            s = lax.dot_general(kt, qs[cols], (((1,), (1,)), ((), ())), preferred_element_type=F32)
            if bias_idx is not None:
                q0 = (c * ATT_SUB) % tk
                s = s + bias_scr[bias_idx, 0:nk, q0:q0 + ATT_SUB]
            return s

        pending = [scores(*st) for st in stages[:ATT_LOOKAHEAD]]
        for n, (t, c) in enumerate(stages):
            cols = slice(c * ATT_SUB, (c + 1) * ATT_SUB)
            s = pending.pop(0)
            if n + ATT_LOOKAHEAD < len(stages):
                pending.append(scores(*stages[n + ATT_LOOKAHEAD]))
            m_prev = m_scr[:, cols]
            m_new = jnp.maximum(m_prev, jnp.max(s, axis=0, keepdims=True))
            alpha = jnp.exp2(m_prev - m_new)
            p = jnp.exp2(s - m_new).astype(BF16)
            vt = vt_scr[tiles[t][0], :, 0:n_keys(t, c)]
            acc_scr[:, cols] = alpha * acc_scr[:, cols] + jnp.dot(vt, p, preferred_element_type=F32)
            m_scr[:, cols] = m_new

    n_far = jnp.maximum(qi - 1, 0)

    def far_group(i, carry):
        tile_group([(ATT_GROUP * i + t, None) for t in range(ATT_GROUP)])
        return carry

    lax.fori_loop(0, n_far // ATT_GROUP, far_group, 0)
    done = n_far - n_far % ATT_GROUP
    piece = ATT_GROUP // 2
    while piece >= 1:
        @pl.when((n_far % (2 * piece)) >= piece)
        def _(done=done, piece=piece):
            tile_group([(done + t, None) for t in range(piece)])
        done = done + jnp.where((n_far % (2 * piece)) >= piece, piece, 0)
        piece //= 2

    @pl.when(qi >= 1)
    def _():
        tile_group([(qi - 1, 1), (qi, 0)])

    @pl.when(qi == 0)
    def _():
        tile_group([(qi, 0)])

    lam_p = lam_ref[...]
    lam = (jnp.exp(jnp.sum(lam_p[0:1] * lam_p[1:2], axis=-1, keepdims=True))
           - jnp.exp(jnp.sum(lam_p[2:3] * lam_p[3:4], axis=-1, keepdims=True)) + LAMBDA_INIT)
    acc = acc_scr[...]
    num = acc[0:V_HEAD_DIM]
    den = acc[V_HEAD_DIM:V_HEAD_DIM + 1]
    o_t = num[:, :tk] / den[:, :tk] - lam * (num[:, tk:] / den[:, tk:])
    o_ref[...] = (_rmsnorm(o_t.T, g_ref[...]) * (1.0 - LAMBDA_INIT)).astype(BF16)


def _attention(qkv, rel_bias, lam_params, subln_g, batch, t_pad):
    n_q = t_pad // TT
    qkv3 = qkv.reshape(batch, t_pad, 3 * D_ATT)
    out = pl.pallas_call(
        _attn_kernel,
        grid_spec=pltpu.PrefetchScalarGridSpec(
            num_scalar_prefetch=0,
            grid=(N_HEADS, batch, n_q),
            in_specs=[
                pl.BlockSpec(memory_space=pltpu.SMEM),
                pl.BlockSpec((None, TT, V_HEAD_DIM), lambda h, b, q: (b, q, h)),
                pl.BlockSpec((None, t_pad, V_HEAD_DIM), lambda h, b, q: (b, 0, N_HEADS + h)),
                pl.BlockSpec((None, t_pad, V_HEAD_DIM), lambda h, b, q: (b, 0, 2 * N_HEADS + h)),
                pl.BlockSpec((4, QK_HEAD_DIM), lambda h, b, q: (0, 0)),
                pl.BlockSpec((1, V_HEAD_DIM), lambda h, b, q: (0, 0)),
            ],
            out_specs=pl.BlockSpec((None, TT, V_HEAD_DIM), lambda h, b, q: (b, q, h)),
            scratch_shapes=[
                pltpu.VMEM((2, TT, TT), F32),
                pltpu.VMEM((n_q, VT_ROWS, TT), BF16),
                pltpu.VMEM((1, 2 * TT), F32),
                pltpu.VMEM((VT_ROWS, 2 * TT), F32),
            ],
        ),
        out_shape=jax.ShapeDtypeStruct((batch, t_pad, D_ATT), BF16),
        compiler_params=pltpu.CompilerParams(
            dimension_semantics=("arbitrary", "arbitrary", "arbitrary"),
            vmem_limit_bytes=VMEM_LIMIT),
        name="diff_attention",
    )(rel_bias, qkv3, qkv3, qkv3, lam_params, subln_g)
    return out.reshape(batch * t_pad, D_ATT)


def _outproj_kernel(yl_ref, ya_ref, h0_ref, w_ref, g_ref, wrt_ref, brt_ref, h1_ref, u_ref, lg_ref):
    acc = jnp.dot(yl_ref[...], w_ref[0:D_LRU, :], preferred_element_type=F32)
    acc = acc + jnp.dot(ya_ref[...], w_ref[D_LRU:, :], preferred_element_type=F32)
    h1 = h0_ref[...] + acc
    h1_ref[...] = h1
    u = _rmsnorm(h1, g_ref[...])
    u_ref[...] = u
    u_hi = u.astype(BF16)
    u_lo = (u - u_hi.astype(F32)).astype(BF16)
    w = wrt_ref[...]
    w_hi = w.astype(BF16)
    w_lo = (w - w_hi.astype(F32)).astype(BF16)
    lg = jnp.dot(u_hi, w_hi, preferred_element_type=F32)
    lg = lg + jnp.dot(u_lo, w_hi, preferred_element_type=F32)
    lg = lg + jnp.dot(u_hi, w_lo, preferred_element_type=F32)
    lg_ref[...] = lg + brt_ref[...]


def _out_proj(y_lru, y_att, h0, w_out_bf16, g_ffn, w_rt, b_rt):
    n_pad = h0.shape[0]
    row = lambda w: pl.BlockSpec((OUT_TM, w), lambda m: (m, 0))
    full = lambda a, b: pl.BlockSpec((a, b), lambda m: (0, 0))
    return pl.pallas_call(
        _outproj_kernel,
        grid=(n_pad // OUT_TM,),
        in_specs=[row(D_LRU), row(D_ATT), row(D_MODEL), full(D_MODEL, D_MODEL), full(1, D_MODEL),
                  full(D_MODEL, LANES), full(1, LANES)],
        out_specs=[row(D_MODEL), row(D_MODEL), row(LANES)],
        out_shape=[
            jax.ShapeDtypeStruct((n_pad, D_MODEL), F32),
            jax.ShapeDtypeStruct((n_pad, D_MODEL), F32),
            jax.ShapeDtypeStruct((n_pad, LANES), F32),
        ],
        compiler_params=pltpu.CompilerParams(
            dimension_semantics=("arbitrary",), vmem_limit_bytes=VMEM_LIMIT),
        name="out_proj",
    )(y_lru, y_att, h0, w_out_bf16, g_ffn, w_rt, b_rt)


def _route_kernel(lg_ref, idx_ref, gate_ref, cnt_ref, carry_scr):
    step = pl.program_id(0)

    @pl.when(step == 0)
    def _():
        carry_scr[...] = jnp.zeros_like(carry_scr)

    lg = lg_ref[...]
    lane = lax.broadcasted_iota(jnp.int32, lg.shape, 1)
    first = lambda mask: jnp.min(jnp.where(mask, lane, LANES), axis=-1, keepdims=True)

    is_g = lane < N_GROUPS
    gl = jnp.where(is_g, lg, -jnp.inf)
    gmax = jnp.max(gl, axis=-1, keepdims=True)
    grp = first(gl == gmax)
    gsum = jnp.sum(jnp.where(is_g, jnp.exp(gl - gmax), 0.0), axis=-1, keepdims=True)
    p_grp = 1.0 / gsum

    lane_e = lane - N_GROUPS
    in_grp = (lane_e >= 0) & (lane_e < N_EXPERTS) & ((lane_e // EXPERTS_PER_GROUP) == grp)
    el = jnp.where(in_grp, lg, -jnp.inf)
    v1 = jnp.max(el, axis=-1, keepdims=True)
    i1 = first(el == v1)
    el2 = jnp.where(lane == i1, -jnp.inf, el)
    v2 = jnp.max(el2, axis=-1, keepdims=True)
    i2 = first(el2 == v2)
    e2 = jnp.exp(v2 - v1)
    den = 1.0 + e2
    g0 = p_grp * (1.0 / den)
    g1 = p_grp * (e2 / den)

    hit1 = lane == i1
    hit2 = lane == i2
    onehot = jnp.where(hit1 | hit2, 1.0, 0.0)
    rr = lax.broadcasted_iota(jnp.int32, (TT, TT), 0)
    cc = lax.broadcasted_iota(jnp.int32, (TT, TT), 1)
    tri = jnp.where(cc < rr, 1.0, 0.0).astype(BF16)
    rank = jnp.dot(tri, onehot.astype(BF16), preferred_element_type=F32) + carry_scr[...]
    r0 = jnp.sum(jnp.where(hit1, rank, 0.0), axis=-1, keepdims=True).astype(jnp.int32)
    r1 = jnp.sum(jnp.where(hit2, rank, 0.0), axis=-1, keepdims=True).astype(jnp.int32)
    carry_scr[...] = carry_scr[...] + jnp.sum(onehot, axis=0, keepdims=True)

    idx_ref[...] = jnp.where(lane == 0, i1 - N_GROUPS,
                             jnp.where(lane == 1, i2 - N_GROUPS,
                                       jnp.where(lane == 2, r0, jnp.where(lane == 3, r1, 0))))
    gate_ref[...] = jnp.where(lane == 0, g0, jnp.where(lane == 1, g1, 0.0))
    cnt_ref[...] = carry_scr[...]


def _route(logits):
    n_pad = logits.shape[0]
    row = pl.BlockSpec((TT, LANES), lambda m: (m, 0))
    return pl.pallas_call(
        _route_kernel,
        grid=(n_pad // TT,),
        in_specs=[row],
        out_specs=[row, row, pl.BlockSpec((1, LANES), lambda m: (0, 0))],
        out_shape=[
            jax.ShapeDtypeStruct((n_pad, LANES), jnp.int32),
            jax.ShapeDtypeStruct((n_pad, LANES), F32),
            jax.ShapeDtypeStruct((1, LANES), F32),
        ],
        scratch_shapes=[pltpu.VMEM((1, LANES), F32)],
        compiler_params=pltpu.CompilerParams(
            dimension_semantics=("arbitrary",), vmem_limit_bytes=VMEM_LIMIT),
        name="route",
    )(logits)


def _dest_kernel(idx_ref, ps_ref, o_ref):
    idx = idx_ref[...]
    lane = lax.broadcasted_iota(jnp.int32, idx.shape, 1)
    ps = ps_ref[...]
    pick = lambda e: jnp.sum(jnp.where(lane == e, ps, 0.0), axis=-1, keepdims=True).astype(jnp.int32)
    d0 = pick(idx[:, 0:1]) + idx[:, 2:3]
    d1 = pick(idx[:, 1:2]) + idx[:, 3:4]
    o_ref[...] = jnp.where(lane == 0, d0, jnp.where(lane == 1, d1, 0))


def _dest(idx, pstart_row):
    n_pad = idx.shape[0]
    row = pl.BlockSpec((TT, LANES), lambda m: (m, 0))
    return pl.pallas_call(
        _dest_kernel,
        grid=(n_pad // TT,),
        in_specs=[row, pl.BlockSpec((1, LANES), lambda m: (0, 0))],
        out_specs=row,
        out_shape=jax.ShapeDtypeStruct((n_pad, LANES), jnp.int32),
        compiler_params=pltpu.CompilerParams(
            dimension_semantics=("arbitrary",), vmem_limit_bytes=VMEM_LIMIT),
        name="dest",
    )(idx, pstart_row)


ROW_UNROLL = 8


def _row_copy(src, s_row, dst, d_row, sem):
    return pltpu.make_async_copy(src.at[pl.ds(s_row, 1)], dst.at[pl.ds(d_row, 1)], sem)


def _dispatch_kernel(d0_ref, d1_ref, pe_ref, u_ref, xb_ref, zero_scr, sem, zsem):
    step = pl.program_id(0)
    base = step * TT

    def zero_block(e):
        start = pl.multiple_of(pe_ref[e] - DISPATCH_BLK, DISPATCH_BLK)
        return pltpu.make_async_copy(zero_scr, xb_ref.at[pl.ds(start, DISPATCH_BLK)], zsem)

    @pl.when(step == 0)
    def _():
        zero_scr[...] = jnp.zeros_like(zero_scr)

        def nonempty(e):
            return pe_ref[e] > jnp.where(e == 0, 0, pe_ref[jnp.maximum(e - 1, 0)])

        def zstart(e, carry):
            @pl.when(nonempty(e))
            def _():
                zero_block(e).start()
            return carry

        def zwait(e, carry):
            @pl.when(nonempty(e))
            def _():
                zero_block(e).wait()
            return carry

        lax.fori_loop(0, N_EXPERTS, zstart, 0)
        lax.fori_loop(0, N_EXPERTS, zwait, 0)

        def tail_block(i):
            start = pl.multiple_of(i * DISPATCH_BLK, DISPATCH_BLK)
            return pltpu.make_async_copy(zero_scr, xb_ref.at[pl.ds(start, DISPATCH_BLK)], zsem)

        def tstart(i, carry):
            tail_block(i).start()
            return carry

        def twait(i, carry):
            tail_block(i).wait()
            return carry

        n_used = pe_ref[N_EXPERTS - 1] // DISPATCH_BLK
        n_blocks = xb_ref.shape[0] // DISPATCH_BLK
        lax.fori_loop(n_used, n_blocks, tstart, 0)
        lax.fori_loop(n_used, n_blocks, twait, 0)

    def issue(r, carry):
        _row_copy(u_ref, r, xb_ref, d0_ref[base + r], sem.at[0]).start()
        _row_copy(u_ref, r, xb_ref, d1_ref[base + r], sem.at[1]).start()
        return carry

    lax.fori_loop(0, TT, issue, 0, unroll=ROW_UNROLL)
    for k in range(2):
        pltpu.make_async_copy(u_ref, xb_ref.at[pl.ds(0, TT)], sem.at[k]).wait()


def _dispatch(d0, d1, pends, u2, rows):
    n_pad = u2.shape[0]
    return pl.pallas_call(
        _dispatch_kernel,
        grid_spec=pltpu.PrefetchScalarGridSpec(
            num_scalar_prefetch=3,
            grid=(n_pad // TT,),
            in_specs=[pl.BlockSpec((TT, D_MODEL), lambda m, d0, d1, pe: (m, 0))],
            out_specs=pl.BlockSpec(memory_space=pl.ANY),
            scratch_shapes=[
                pltpu.VMEM((DISPATCH_BLK, D_MODEL), F32),
                pltpu.SemaphoreType.DMA((2,)),
                pltpu.SemaphoreType.DMA(()),
            ],
        ),
        out_shape=jax.ShapeDtypeStruct((rows, D_MODEL), F32),
        compiler_params=pltpu.CompilerParams(
            dimension_semantics=("arbitrary",), vmem_limit_bytes=VMEM_LIMIT),
        name="dispatch",
    )(d0, d1, pends, u2)


WEIGHT_UNITS = 8
WEIGHT_DEPTH = 4


def _stream_weights(i, tgt_ref, eseq_ref, ntot_ref, w_hbm, stages, wbfs, sem, done_scr):
    def slab(u, k):
        rows = stages[k].shape[1]
        return pl.ds(pl.multiple_of((u % WEIGHT_UNITS) * rows, rows), rows)

    def unit_copies(u):
        e = eseq_ref[u // WEIGHT_UNITS]
        b = u % WEIGHT_DEPTH
        return [pltpu.make_async_copy(w.at[e, slab(u, k), :], stages[k].at[b], sem.at[b, k])
                for k, w in enumerate(w_hbm)]

    @pl.when(i == 0)
    def _():
        done_scr[0] = 0
        for u in range(WEIGHT_DEPTH):
            for c in unit_copies(u):
                c.start()

    def body(u, carry):
        for c in unit_copies(u):
            c.wait()
        slot = (u // WEIGHT_UNITS) % 2
        b = u % WEIGHT_DEPTH
        for k in range(len(w_hbm)):
            wbfs[k][slot, slab(u, k), :] = stages[k][b].astype(BF16)

        @pl.when(u + WEIGHT_DEPTH < ntot_ref[0])
        def _():
            for c in unit_copies(u + WEIGHT_DEPTH):
                c.start()

        return carry

    lax.fori_loop(done_scr[0], tgt_ref[i], body, 0)
    done_scr[0] = tgt_ref[i]


def _expert_kernel(tgt_ref, eseq_ref, ntot_ref, slot_ref, nu_ref, x_ref, wg_hbm, wu_hbm, wd_hbm, y_ref,
                   sg, su, sd, bg, bu, bd, sem, done_scr):
    i = pl.program_id(0)
    _stream_weights(i, tgt_ref, eseq_ref, ntot_ref, (wg_hbm, wu_hbm, wd_hbm), (sg, su, sd), (bg, bu, bd),
                    sem, done_scr)
    used = i < nu_ref[0]

    @pl.when(used)
    def _():
        slot = slot_ref[i]
        x = x_ref[...].astype(BF16)
        g = jnp.dot(x, bg[slot], preferred_element_type=F32)
        u = jnp.dot(x, bu[slot], preferred_element_type=F32)
        a = (jax.nn.silu(g) * u).astype(BF16)
        y_ref[...] = jnp.dot(a, bd[slot], preferred_element_type=F32)

    @pl.when(jnp.logical_not(used))
    def _():
        y_ref[...] = jnp.zeros_like(y_ref)


def _experts(sched, x_buf, w_gate, w_up, w_down):
    rows = x_buf.shape[0]
    n_blocks = rows // DISPATCH_BLK
    blk_map = lambda i, tg, es, nt, sl, nu: (jnp.minimum(i, jnp.maximum(nu[0] - 1, 0)), 0)
    out_map = lambda i, tg, es, nt, sl, nu: (i, 0)
    hbm = pl.BlockSpec(memory_space=pl.ANY)
    stage = lambda k_dim, n_dim: pltpu.VMEM((WEIGHT_DEPTH, k_dim // WEIGHT_UNITS, n_dim), F32)
    resident = lambda k_dim, n_dim: pltpu.VMEM((2, k_dim, n_dim), BF16)
    return pl.pallas_call(
        _expert_kernel,
        grid_spec=pltpu.PrefetchScalarGridSpec(
            num_scalar_prefetch=5,
            grid=(n_blocks,),
            in_specs=[pl.BlockSpec((DISPATCH_BLK, D_MODEL), blk_map), hbm, hbm, hbm],
            out_specs=pl.BlockSpec((DISPATCH_BLK, D_MODEL), out_map),
            scratch_shapes=[
                stage(D_MODEL, D_EXPERT), stage(D_MODEL, D_EXPERT), stage(D_EXPERT, D_MODEL),
                resident(D_MODEL, D_EXPERT), resident(D_MODEL, D_EXPERT), resident(D_EXPERT, D_MODEL),
                pltpu.SemaphoreType.DMA((WEIGHT_DEPTH, 3)),
                pltpu.SMEM((1,), jnp.int32),
            ],
        ),
        out_shape=jax.ShapeDtypeStruct((rows, D_MODEL), F32),
        compiler_params=pltpu.CompilerParams(
            dimension_semantics=("arbitrary",), vmem_limit_bytes=VMEM_LIMIT),
        name="experts",
    )(*sched, x_buf, w_gate, w_up, w_down)


def _expert_schedule(padded, pstarts, n_blocks):
    nonempty = padded > 0
    seq_of_expert = jnp.cumsum(nonempty.astype(jnp.int32)) - 1
    n_seq = jnp.sum(nonempty.astype(jnp.int32))
    experts = jnp.arange(N_EXPERTS, dtype=jnp.int32)
    is_kth = jnp.logical_and(nonempty[None, :], seq_of_expert[None, :] == experts[:, None])
    eseq = jnp.sum(jnp.where(is_kth, experts[None, :], 0), axis=1)
    pends = pstarts + padded
    n_used = pends[-1] // DISPATCH_BLK
    blk = jnp.arange(n_blocks, dtype=jnp.int32)
    blk_c = jnp.minimum(blk, jnp.maximum(n_used - 1, 0))
    blk_expert = jnp.minimum(
        jnp.sum((pends[None, :] <= (blk_c * DISPATCH_BLK)[:, None]).astype(jnp.int32), axis=1), N_EXPERTS - 1)
    onehot = (blk_expert[:, None] == experts[None, :]).astype(jnp.int32)
    pick = lambda v: jnp.sum(onehot * v[None, :], axis=1)
    q = pick(seq_of_expert)
    j = blk_c - pick(pstarts) // DISPATCH_BLK
    n = jnp.maximum(pick(padded) // DISPATCH_BLK, 1)
    n_total = n_seq * WEIGHT_UNITS
    tgt = jnp.minimum(WEIGHT_UNITS * (q + 1) + (WEIGHT_UNITS * j) // n, n_total)
    tgt = jnp.where(blk < n_used, tgt, n_total)
    i32 = lambda a: a.astype(jnp.int32)
    return i32(tgt), eseq, i32(n_total)[None], i32(q % 2), i32(n_used)[None]


COMBINE_TM = 512


def _combine_kernel(t_pad, d0_ref, d1_ref, h_hbm, gate_hbm, g_ref, yb_ref, o_ref,
                    h_scr, gate_scr, y0_scr, y1_scr, sem):
    b = pl.program_id(0)
    j = pl.program_id(1)
    base = pl.multiple_of(b * t_pad + N_META + j * COMBINE_TM, 8)

    tile_copies = [
        pltpu.make_async_copy(h_hbm.at[pl.ds(base, COMBINE_TM)], h_scr, sem.at[2]),
        pltpu.make_async_copy(gate_hbm.at[pl.ds(base, COMBINE_TM)], gate_scr, sem.at[3]),
    ]
    for c in tile_copies:
        c.start()

    def issue(r, carry):
        _row_copy(yb_ref, d0_ref[base + r], y0_scr, r, sem.at[0]).start()
        _row_copy(yb_ref, d1_ref[base + r], y1_scr, r, sem.at[1]).start()
        return carry

    lax.fori_loop(0, COMBINE_TM, issue, 0, unroll=ROW_UNROLL)
    pltpu.make_async_copy(yb_ref.at[pl.ds(0, COMBINE_TM)], y0_scr, sem.at[0]).wait()
    pltpu.make_async_copy(yb_ref.at[pl.ds(0, COMBINE_TM)], y1_scr, sem.at[1]).wait()
    for c in tile_copies:
        c.wait()

    gate = gate_scr[...]
    h = h_scr[...] + gate[:, 0:1] * y0_scr[...] + gate[:, 1:2] * y1_scr[...]
    o_ref[...] = _rmsnorm(h, g_ref[...])


def _combine(d0, d1, h1, gates, g_final, y_buf, batch, seq):
    assert seq % COMBINE_TM == 0 and N_META % 8 == 0
    tile = lambda w: pltpu.VMEM((COMBINE_TM, w), F32)
    return pl.pallas_call(
        functools.partial(_combine_kernel, h1.shape[0] // batch),
        grid_spec=pltpu.PrefetchScalarGridSpec(
            num_scalar_prefetch=2,
            grid=(batch, seq // COMBINE_TM),
            in_specs=[
                pl.BlockSpec(memory_space=pl.ANY),
                pl.BlockSpec(memory_space=pl.ANY),
                pl.BlockSpec((1, D_MODEL), lambda b, j, d0, d1: (0, 0)),
                pl.BlockSpec(memory_space=pl.ANY),
            ],
            out_specs=pl.BlockSpec((None, COMBINE_TM, D_MODEL), lambda b, j, d0, d1: (b, j, 0)),
            scratch_shapes=[tile(D_MODEL), tile(LANES), tile(D_MODEL), tile(D_MODEL),
                            pltpu.SemaphoreType.DMA((4,))],
        ),
        out_shape=jax.ShapeDtypeStruct((batch, seq, D_MODEL), F32),
        compiler_params=pltpu.CompilerParams(
            dimension_semantics=("arbitrary", "arbitrary"), vmem_limit_bytes=VMEM_LIMIT),
        name="combine",
    )(d0, d1, h1, gates, g_final, y_buf)


def _block_diag(w):
    per = LRU_CB // LRU_BLOCK_DIM
    w4 = w.reshape(LRU_BLOCKS // per, per, LRU_BLOCK_DIM, LRU_BLOCK_DIM)
    eye = jnp.eye(per, dtype=w.dtype)
    bd = jnp.einsum('cpij,pq->cpiqj', w4, eye)
    return bd.reshape(LRU_BLOCKS // per, LRU_CB, LRU_CB)


def kernel(x, meta_tokens, g_mix, w_in, conv_w, conv_b, w_rgate, b_rgate, w_igate, b_igate, lru_L, lambda_q1, lambda_k1, lambda_q2, lambda_k2, subln_g, rel_bias, w_out, g_ffn, w_group, b_group, w_router, b_router, w_gate, w_up, w_down, g_final):
    batch, seq, _ = x.shape
    t_real = N_META + seq
    t_pad = _round_up(t_real, TT)
    n_pad = batch * t_pad

    meta = jnp.broadcast_to(meta_tokens[None].astype(x.dtype), (batch, N_META, D_MODEL))
    h0 = jnp.concatenate([meta, x, jnp.zeros((batch, t_pad - t_real, D_MODEL), x.dtype)], axis=1)
    h0 = h0.reshape(n_pad, D_MODEL)

    proj_lru, proj_qkv = _in_proj(h0, g_mix[0][None], w_in[0].astype(BF16))

    y_lru = _lru(proj_lru, conv_w[0], conv_b[0][None],
                 _block_diag(w_rgate[0]).astype(BF16), b_rgate[0].reshape(1, D_LRU),
                 _block_diag(w_igate[0]).astype(BF16), b_igate[0].reshape(1, D_LRU),
                 lru_L[0][None], batch, t_pad)

    lam_params = jnp.stack([lambda_q1[0], lambda_k1[0], lambda_q2[0], lambda_k2[0]])
    y_att = _attention(proj_qkv, rel_bias, lam_params, subln_g[0][None], batch, t_pad)

    w_rt = jnp.concatenate([w_group[0], w_router[0],
                            jnp.zeros((D_MODEL, LANES - N_GROUPS - N_EXPERTS), F32)], axis=1)
    b_rt = jnp.concatenate([b_group[0], b_router[0],
                            jnp.zeros((LANES - N_GROUPS - N_EXPERTS,), F32)])[None]
    h1, u2, logits = _out_proj(y_lru, y_att, h0, w_out[0].astype(BF16), g_ffn[0][None], w_rt, b_rt)

    idx, gates, counts = _route(logits)

    cnt = counts[0, N_GROUPS:N_GROUPS + N_EXPERTS].astype(jnp.int32)
    padded = (cnt + DISPATCH_BLK - 1) // DISPATCH_BLK * DISPATCH_BLK
    pends = jnp.cumsum(padded)
    pstarts = pends - padded
    dest = _dest(idx, jnp.pad(pstarts.astype(F32), (0, LANES - N_EXPERTS))[None])
    d0 = dest[:, 0]
    d1 = dest[:, 1]
    n_blocks = -(-(2 * n_pad + N_EXPERTS * (DISPATCH_BLK - 1)) // DISPATCH_BLK)
    rows = n_blocks * DISPATCH_BLK
    sched = _expert_schedule(padded, pstarts, n_blocks)

    x_buf = _dispatch(d0, d1, pends.astype(jnp.int32), u2, rows)
    y_buf = _experts(sched, x_buf, w_gate[0], w_up[0], w_down[0])
    return _combine(d0, d1, h1, gates, g_final[None], y_buf, batch, seq)
```

```python
import functools
import math

import numpy as np
import jax
import jax.numpy as jnp
from jax import lax
from jax.experimental import pallas as pl
from jax.experimental.pallas import tpu as pltpu

D_MODEL = 2048
N_META = 16
D_LRU = 1024
D_ATT = 1024
LRU_BLOCKS = 16
LRU_BLOCK_DIM = 64
CONV_WIDTH = 4
LRU_C = 8.0
N_HEADS = 8
V_HEAD_DIM = 128
QK_HEAD_DIM = 64
N_BUCKETS = 32
MAX_DISTANCE = 128
N_GROUPS = 4
EXPERTS_PER_GROUP = 8
N_EXPERTS = 32
D_EXPERT = 1024
D_IN_PROJ = 2 * D_LRU + 3 * D_ATT
EPS = 1e-6
LAMBDA_INIT = 0.8 - 0.6 * math.exp(-0.3 * 0)
LOG2_E = math.log2(math.e)

F32 = jnp.float32
BF16 = jnp.bfloat16

LANES = 128
SUBLANES = 8
MXU_DIM = 256
TT = 768
PROJ_TN = 1024
OUT_TM = TT // 2
ATT_SUB = MXU_DIM
ATT_LOOKAHEAD = 4
ATT_GROUP = 4
VT_ROWS = V_HEAD_DIM + 16
LRU_CB = MXU_DIM
DISPATCH_BLK = 256
VMEM_LIMIT = 56 * 1024 * 1024
NEG_BIG = -1e30


def _round_up(a, b):
    return -(-a // b) * b


def _rmsnorm(x, g):
    ms = jnp.mean(x * x, axis=-1, keepdims=True)
    return x * lax.rsqrt(ms + EPS) * g


def _inproj_kernel(x_ref, g_ref, w_ref, lru_ref, qkv_ref, u_scr):
    n = pl.program_id(1)

    @pl.when(n == 0)
    def _():
        u_scr[...] = _rmsnorm(x_ref[...], g_ref[...]).astype(BF16)

    y = jnp.dot(u_scr[...], w_ref[...], preferred_element_type=F32)

    @pl.when(n < 2)
    def _():
        lru_ref[...] = y

    @pl.when(n >= 2)
    def _():
        qkv_ref[...] = y.astype(BF16)


def _in_proj(h0, g_mix, w_in_bf16):
    n_pad = h0.shape[0]
    n_col = D_IN_PROJ // PROJ_TN
    return pl.pallas_call(
        _inproj_kernel,
        grid=(n_pad // TT, n_col),
        in_specs=[
            pl.BlockSpec((TT, D_MODEL), lambda m, n: (m, 0)),
            pl.BlockSpec((1, D_MODEL), lambda m, n: (0, 0)),
            pl.BlockSpec((D_MODEL, PROJ_TN), lambda m, n: (0, n)),
        ],
        out_specs=[
            pl.BlockSpec((TT, PROJ_TN), lambda m, n: (m, jnp.minimum(n, 1))),
            pl.BlockSpec((TT, PROJ_TN), lambda m, n: (m, jnp.maximum(n - 2, 0))),
        ],
        out_shape=[
            jax.ShapeDtypeStruct((n_pad, 2 * D_LRU), F32),
            jax.ShapeDtypeStruct((n_pad, 3 * D_ATT), BF16),
        ],
        scratch_shapes=[pltpu.VMEM((TT, D_MODEL), BF16)],
        compiler_params=pltpu.CompilerParams(
            dimension_semantics=("arbitrary", "arbitrary"), vmem_limit_bytes=VMEM_LIMIT),
        name="in_proj",
    )(h0, g_mix, w_in_bf16)


def _shift_rows(x, tail, s):
    r = pltpu.roll(x, s, 0)
    tail_row = lax.broadcasted_iota(jnp.int32, tail.shape, 0)
    head = jnp.where(tail_row < s, pltpu.roll(tail, s, 0), r[0:SUBLANES])
    return jnp.concatenate([head, r[SUBLANES:]], axis=0)


def _lru_kernel(x_ref, gate_ref, cw_ref, cb_ref, wr_ref, br_ref, wi_ref, bi_ref, lam_ref,
                y_ref, tail_scr, h_scr):
    t = pl.program_id(2)

    @pl.when(t == 0)
    def _():
        tail_scr[...] = jnp.zeros_like(tail_scr)
        h_scr[...] = jnp.zeros_like(h_scr)

    x = x_ref[...]
    tail = tail_scr[...]
    cw = cw_ref[...]
    xc = cb_ref[...] + x * cw[CONV_WIDTH - 1:CONV_WIDTH]
    for k in range(CONV_WIDTH - 1):
        xc = xc + _shift_rows(x, tail, CONV_WIDTH - 1 - k) * cw[k:k + 1]
    tail_scr[...] = x[TT - SUBLANES:TT]

    xb = xc.astype(BF16)
    r = jax.nn.sigmoid(jnp.dot(xb, wr_ref[...], preferred_element_type=F32) + br_ref[...])
    i = jax.nn.sigmoid(jnp.dot(xb, wi_ref[...], preferred_element_type=F32) + bi_ref[...])
    lam = lam_ref[...]
    log_sig = jnp.minimum(lam, 0.0) - jnp.log(1.0 + jnp.exp(-jnp.abs(lam)))
    log_a = (LRU_C * r) * log_sig
    a = jnp.exp(log_a)
    b = jnp.sqrt(1.0 - a * a) * (i * xc)

    row = lax.broadcasted_iota(jnp.int32, a.shape, 0) % SUBLANES
    s = 1
    while s < SUBLANES:
        keep = row >= s
        a_sh = jnp.where(keep, pltpu.roll(a, s, 0), 1.0)
        b_sh = jnp.where(keep, pltpu.roll(b, s, 0), 0.0)
        b = a * b_sh + b
        a = a * a_sh
        s *= 2
    carry = h_scr[...]
    groups = []
    for g in range(TT // SUBLANES):
        rows = slice(g * SUBLANES, (g + 1) * SUBLANES)
        hg = b[rows] + a[rows] * carry
        groups.append(hg)
        carry = jnp.broadcast_to(hg[SUBLANES - 1:SUBLANES], hg.shape)
    h_scr[...] = carry
    h = jnp.concatenate(groups, axis=0)
    y_ref[...] = (h * jax.nn.gelu(gate_ref[...])).astype(BF16)


def _lru(proj_lru, conv_w, conv_b, wr_bd, b_r, wi_bd, b_i, lru_l, batch, t_pad):
    n_pad = proj_lru.shape[0]
    n_t = t_pad // TT
    n_c = D_LRU // LRU_CB
    vec = lambda: pl.BlockSpec((1, LRU_CB), lambda b, c, t: (0, c))
    return pl.pallas_call(
        _lru_kernel,
        grid=(batch, n_c, n_t),
        in_specs=[
            pl.BlockSpec((TT, LRU_CB), lambda b, c, t: (b * n_t + t, c)),
            pl.BlockSpec((TT, LRU_CB), lambda b, c, t: (b * n_t + t, n_c + c)),
            pl.BlockSpec((CONV_WIDTH, LRU_CB), lambda b, c, t: (0, c)),
            vec(),
            pl.BlockSpec((None, LRU_CB, LRU_CB), lambda b, c, t: (c, 0, 0)),
            vec(),
            pl.BlockSpec((None, LRU_CB, LRU_CB), lambda b, c, t: (c, 0, 0)),
            vec(),
            vec(),
        ],
        out_specs=pl.BlockSpec((TT, LRU_CB), lambda b, c, t: (b * n_t + t, c)),
        out_shape=jax.ShapeDtypeStruct((n_pad, D_LRU), BF16),
        scratch_shapes=[pltpu.VMEM((SUBLANES, LRU_CB), F32), pltpu.VMEM((SUBLANES, LRU_CB), F32)],
        compiler_params=pltpu.CompilerParams(
            dimension_semantics=("arbitrary", "arbitrary", "arbitrary"),
            vmem_limit_bytes=VMEM_LIMIT),
        name="rglru",
    )(proj_lru, proj_lru, conv_w, conv_b, wr_bd, b_r, wi_bd, b_i, lru_l)


def _bucket_thresholds():
    max_exact = N_BUCKETS // 2
    n = np.arange(0, MAX_DISTANCE + 1)
    nf = np.maximum(n, 1).astype(np.float32)
    large = max_exact + (np.log(nf / np.float32(max_exact)) / np.float32(math.log(MAX_DISTANCE / max_exact))
                         * np.float32(N_BUCKETS - max_exact)).astype(np.int32)
    large = np.minimum(large, N_BUCKETS - 1)
    bucket = np.where(n < max_exact, n, large)
    return [int(np.argmax(bucket >= k)) for k in range(1, N_BUCKETS)]


_BUCKET_THR = _bucket_thresholds()


def _attn_kernel(rb_ref, q_ref, k_ref, v_ref, lam_ref, g_ref, o_ref, bias_scr, vt_scr, m_scr, acc_scr):
    h = pl.program_id(0)
    b = pl.program_id(1)
    qi = pl.program_id(2)
    tk = TT
    n_kv = vt_scr.shape[0]

    @pl.when(jnp.logical_and(b == 0, qi == 0))
    def _():
        k_pos = lax.broadcasted_iota(jnp.int32, (tk, tk), 0)
        q_pos = lax.broadcasted_iota(jnp.int32, (tk, tk), 1)
        far = rb_ref[N_BUCKETS - 1, h]
        for d in range(2):
            rel = q_pos - k_pos + d * tk
            val = jnp.full((tk, tk), rb_ref[0, h] - far, F32)
            for kk, thr in enumerate(_BUCKET_THR):
                val = jnp.where(rel >= thr, rb_ref[kk + 1, h] - far, val)
            val = val * LOG2_E
            if d == 0:
                val = jnp.where(rel >= 0, val, -jnp.inf)
            bias_scr[d] = val

    @pl.when(qi == 0)
    def _():
        for t in range(n_kv):
            vt_scr[t, 0:V_HEAD_DIM, :] = v_ref[t * tk:(t + 1) * tk, :].astype(F32).T.astype(BF16)
            vt_scr[t, V_HEAD_DIM:VT_ROWS, :] = jnp.ones((VT_ROWS - V_HEAD_DIM, tk), BF16)

    q = q_ref[...]
    lane = lax.broadcasted_iota(jnp.int32, q.shape, 1)
    scale = (QK_HEAD_DIM ** -0.5) * LOG2_E
    qf = q.astype(F32) * scale
    qs = jnp.concatenate([jnp.where(lane < QK_HEAD_DIM, qf, 0.0),
                          jnp.where(lane >= QK_HEAD_DIM, qf, 0.0)], axis=0).astype(BF16)

    m_scr[...] = jnp.full(m_scr.shape, NEG_BIG, F32)
    acc_scr[...] = jnp.zeros_like(acc_scr)

    def tile_group(tiles):
        n_sub = 2 * tk // ATT_SUB
        stages = [(t, c) for t in range(len(tiles)) for c in range(n_sub)]

        def n_keys(t, c):
            return (c * ATT_SUB) % tk + ATT_SUB if tiles[t][1] == 0 else tk

        def scores(t, c):
            j, bias_idx = tiles[t]
            nk = n_keys(t, c)
            kt = k_ref[pl.ds(pl.multiple_of(j * tk, tk), nk), :]
            cols = slice(c * ATT_SUB, (c + 1) * ATT_SUB)
            s = lax.dot_general(kt, qs[cols], (((1,), (1,)), ((), ())), preferred_element_type=F32)
            if bias_idx is not None:
                q0 = (c * ATT_SUB) % tk
                s = s + bias_scr[bias_idx, 0:nk, q0:q0 + ATT_SUB]
            return s

        pending = [scores(*st) for st in stages[:ATT_LOOKAHEAD]]
        for n, (t, c) in enumerate(stages):
            cols = slice(c * ATT_SUB, (c + 1) * ATT_SUB)
            s = pending.pop(0)
            if n + ATT_LOOKAHEAD < len(stages):
                pending.append(scores(*stages[n + ATT_LOOKAHEAD]))
            m_prev = m_scr[:, cols]
            m_new = jnp.maximum(m_prev, jnp.max(s, axis=0, keepdims=True))
            alpha = jnp.exp2(m_prev - m_new)
            p = jnp.exp2(s - m_new).astype(BF16)
            vt = vt_scr[tiles[t][0], :, 0:n_keys(t, c)]
            acc_scr[:, cols] = alpha * acc_scr[:, cols] + jnp.dot(vt, p, preferred_element_type=F32)
            m_scr[:, cols] = m_new

    n_far = jnp.maximum(qi - 1, 0)

    def far_group(i, carry):
        tile_group([(ATT_GROUP * i + t, None) for t in range(ATT_GROUP)])
        return carry

    lax.fori_loop(0, n_far // ATT_GROUP, far_group, 0)
    done = n_far - n_far % ATT_GROUP
    piece = ATT_GROUP // 2
    while piece >= 1:
        @pl.when((n_far % (2 * piece)) >= piece)
        def _(done=done, piece=piece):
            tile_group([(done + t, None) for t in range(piece)])
        done = done + jnp.where((n_far % (2 * piece)) >= piece, piece, 0)
        piece //= 2

    @pl.when(qi >= 1)
    def _():
        tile_group([(qi - 1, 1), (qi, 0)])

    @pl.when(qi == 0)
    def _():
        tile_group([(qi, 0)])

    lam_p = lam_ref[...]
    lam = (jnp.exp(jnp.sum(lam_p[0:1] * lam_p[1:2], axis=-1, keepdims=True))
           - jnp.exp(jnp.sum(lam_p[2:3] * lam_p[3:4], axis=-1, keepdims=True)) + LAMBDA_INIT)
    acc = acc_scr[...]
    num = acc[0:V_HEAD_DIM]
    den = acc[V_HEAD_DIM:V_HEAD_DIM + 1]
    o_t = num[:, :tk] / den[:, :tk] - lam * (num[:, tk:] / den[:, tk:])
    o_ref[...] = (_rmsnorm(o_t.T, g_ref[...]) * (1.0 - LAMBDA_INIT)).astype(BF16)


def _attention(qkv, rel_bias, lam_params, subln_g, batch, t_pad):
    n_q = t_pad // TT
    qkv3 = qkv.reshape(batch, t_pad, 3 * D_ATT)
    out = pl.pallas_call(
        _attn_kernel,
        grid_spec=pltpu.PrefetchScalarGridSpec(
            num_scalar_prefetch=0,
            grid=(N_HEADS, batch, n_q),
            in_specs=[
                pl.BlockSpec(memory_space=pltpu.SMEM),
                pl.BlockSpec((None, TT, V_HEAD_DIM), lambda h, b, q: (b, q, h)),
                pl.BlockSpec((None, t_pad, V_HEAD_DIM), lambda h, b, q: (b, 0, N_HEADS + h)),
                pl.BlockSpec((None, t_pad, V_HEAD_DIM), lambda h, b, q: (b, 0, 2 * N_HEADS + h)),
                pl.BlockSpec((4, QK_HEAD_DIM), lambda h, b, q: (0, 0)),
                pl.BlockSpec((1, V_HEAD_DIM), lambda h, b, q: (0, 0)),
            ],
            out_specs=pl.BlockSpec((None, TT, V_HEAD_DIM), lambda h, b, q: (b, q, h)),
            scratch_shapes=[
                pltpu.VMEM((2, TT, TT), F32),
                pltpu.VMEM((n_q, VT_ROWS, TT), BF16),
                pltpu.VMEM((1, 2 * TT), F32),
                pltpu.VMEM((VT_ROWS, 2 * TT), F32),
            ],
        ),
        out_shape=jax.ShapeDtypeStruct((batch, t_pad, D_ATT), BF16),
        compiler_params=pltpu.CompilerParams(
            dimension_semantics=("arbitrary", "arbitrary", "arbitrary"),
            vmem_limit_bytes=VMEM_LIMIT),
        name="diff_attention",
    )(rel_bias, qkv3, qkv3, qkv3, lam_params, subln_g)
    return out.reshape(batch * t_pad, D_ATT)


def _outproj_kernel(yl_ref, ya_ref, h0_ref, w_ref, g_ref, wrt_ref, brt_ref, h1_ref, u_ref, lg_ref):
    acc = jnp.dot(yl_ref[...], w_ref[0:D_LRU, :], preferred_element_type=F32)
    acc = acc + jnp.dot(ya_ref[...], w_ref[D_LRU:, :], preferred_element_type=F32)
    h1 = h0_ref[...] + acc
    h1_ref[...] = h1
    u = _rmsnorm(h1, g_ref[...])
    u_ref[...] = u
    u_hi = u.astype(BF16)
    u_lo = (u - u_hi.astype(F32)).astype(BF16)
    w = wrt_ref[...]
    w_hi = w.astype(BF16)
    w_lo = (w - w_hi.astype(F32)).astype(BF16)
    parts = jnp.dot(jnp.concatenate([u_hi, u_lo], axis=0), jnp.concatenate([w_hi, w_lo], axis=1),
                    preferred_element_type=F32)
    lg = (parts[:OUT_TM, :LANES] + parts[:OUT_TM, LANES:]) + (parts[OUT_TM:, :LANES] + parts[OUT_TM:, LANES:])
    lg_ref[...] = lg + brt_ref[...]


def _out_proj(y_lru, y_att, h0, w_out_bf16, g_ffn, w_rt, b_rt):
    n_pad = h0.shape[0]
    row = lambda w: pl.BlockSpec((OUT_TM, w), lambda m: (m, 0))
    full = lambda a, b: pl.BlockSpec((a, b), lambda m: (0, 0))
    return pl.pallas_call(
        _outproj_kernel,
        grid=(n_pad // OUT_TM,),
        in_specs=[row(D_LRU), row(D_ATT), row(D_MODEL), full(D_MODEL, D_MODEL), full(1, D_MODEL),
                  full(D_MODEL, LANES), full(1, LANES)],
        out_specs=[row(D_MODEL), row(D_MODEL), row(LANES)],
        out_shape=[
            jax.ShapeDtypeStruct((n_pad, D_MODEL), F32),
            jax.ShapeDtypeStruct((n_pad, D_MODEL), F32),
            jax.ShapeDtypeStruct((n_pad, LANES), F32),
        ],
        compiler_params=pltpu.CompilerParams(
            dimension_semantics=("arbitrary",), vmem_limit_bytes=VMEM_LIMIT),
        name="out_proj",
    )(y_lru, y_att, h0, w_out_bf16, g_ffn, w_rt, b_rt)


def _route_kernel(lg_ref, idx_ref, gate_ref, cnt_ref, carry_scr):
    step = pl.program_id(0)

    @pl.when(step == 0)
    def _():
        carry_scr[...] = jnp.zeros_like(carry_scr)

    lg = lg_ref[...]
    lane = lax.broadcasted_iota(jnp.int32, lg.shape, 1)
    first = lambda mask: jnp.min(jnp.where(mask, lane, LANES), axis=-1, keepdims=True)

    is_g = lane < N_GROUPS
    gl = jnp.where(is_g, lg, -jnp.inf)
    gmax = jnp.max(gl, axis=-1, keepdims=True)
    grp = first(gl == gmax)
    gsum = jnp.sum(jnp.where(is_g, jnp.exp(gl - gmax), 0.0), axis=-1, keepdims=True)
    p_grp = 1.0 / gsum

    lane_e = lane - N_GROUPS
    in_grp = (lane_e >= 0) & (lane_e < N_EXPERTS) & ((lane_e // EXPERTS_PER_GROUP) == grp)
    el = jnp.where(in_grp, lg, -jnp.inf)
    v1 = jnp.max(el, axis=-1, keepdims=True)
    i1 = first(el == v1)
    el2 = jnp.where(lane == i1, -jnp.inf, el)
    v2 = jnp.max(el2, axis=-1, keepdims=True)
    i2 = first(el2 == v2)
    e2 = jnp.exp(v2 - v1)
    den = 1.0 + e2
    g0 = p_grp * (1.0 / den)
    g1 = p_grp * (e2 / den)

    hit1 = lane == i1
    hit2 = lane == i2
    onehot = jnp.where(hit1 | hit2, 1.0, 0.0)
    rr = lax.broadcasted_iota(jnp.int32, (TT, TT), 0)
    cc = lax.broadcasted_iota(jnp.int32, (TT, TT), 1)
    tri = jnp.where(cc < rr, 1.0, 0.0).astype(BF16)
    rank = jnp.dot(tri, onehot.astype(BF16), preferred_element_type=F32) + carry_scr[...]
    r0 = jnp.sum(jnp.where(hit1, rank, 0.0), axis=-1, keepdims=True).astype(jnp.int32)
    r1 = jnp.sum(jnp.where(hit2, rank, 0.0), axis=-1, keepdims=True).astype(jnp.int32)
    carry_scr[...] = carry_scr[...] + jnp.sum(onehot, axis=0, keepdims=True)

    idx_ref[...] = jnp.where(lane == 0, i1 - N_GROUPS,
                             jnp.where(lane == 1, i2 - N_GROUPS,
                                       jnp.where(lane == 2, r0, jnp.where(lane == 3, r1, 0))))
    gate_ref[...] = jnp.where(lane == 0, g0, jnp.where(lane == 1, g1, 0.0))
    cnt_ref[...] = carry_scr[...]


def _route(logits):
    n_pad = logits.shape[0]
    row = pl.BlockSpec((TT, LANES), lambda m: (m, 0))
    return pl.pallas_call(
        _route_kernel,
        grid=(n_pad // TT,),
        in_specs=[row],
        out_specs=[row, row, pl.BlockSpec((1, LANES), lambda m: (0, 0))],
        out_shape=[
            jax.ShapeDtypeStruct((n_pad, LANES), jnp.int32),
            jax.ShapeDtypeStruct((n_pad, LANES), F32),
            jax.ShapeDtypeStruct((1, LANES), F32),
        ],
        scratch_shapes=[pltpu.VMEM((1, LANES), F32)],
        compiler_params=pltpu.CompilerParams(
            dimension_semantics=("arbitrary",), vmem_limit_bytes=VMEM_LIMIT),
        name="route",
    )(logits)


def _dest_kernel(idx_ref, ps_ref, o_ref):
    idx = idx_ref[...]
    lane = lax.broadcasted_iota(jnp.int32, idx.shape, 1)
    ps = ps_ref[...]
    pick = lambda e: jnp.sum(jnp.where(lane == e, ps, 0.0), axis=-1, keepdims=True).astype(jnp.int32)
    d0 = pick(idx[:, 0:1]) + idx[:, 2:3]
    d1 = pick(idx[:, 1:2]) + idx[:, 3:4]
    o_ref[...] = jnp.where(lane == 0, d0, jnp.where(lane == 1, d1, 0))


def _dest(idx, pstart_row):
    n_pad = idx.shape[0]
    row = pl.BlockSpec((TT, LANES), lambda m: (m, 0))
    return pl.pallas_call(
        _dest_kernel,
        grid=(n_pad // TT,),
        in_specs=[row, pl.BlockSpec((1, LANES), lambda m: (0, 0))],
        out_specs=row,
        out_shape=jax.ShapeDtypeStruct((n_pad, LANES), jnp.int32),
        compiler_params=pltpu.CompilerParams(
            dimension_semantics=("arbitrary",), vmem_limit_bytes=VMEM_LIMIT),
        name="dest",
    )(idx, pstart_row)


ROW_UNROLL = 8


def _row_copy(src, s_row, dst, d_row, sem):
    return pltpu.make_async_copy(src.at[pl.ds(s_row, 1)], dst.at[pl.ds(d_row, 1)], sem)


def _dispatch_kernel(d0_ref, d1_ref, pe_ref, u_ref, xb_ref, zero_scr, sem, zsem):
    step = pl.program_id(0)
    base = step * TT

    def zero_block(e):
        start = pl.multiple_of(pe_ref[e] - DISPATCH_BLK, DISPATCH_BLK)
        return pltpu.make_async_copy(zero_scr, xb_ref.at[pl.ds(start, DISPATCH_BLK)], zsem)

    @pl.when(step == 0)
    def _():
        zero_scr[...] = jnp.zeros_like(zero_scr)

        def nonempty(e):
            return pe_ref[e] > jnp.where(e == 0, 0, pe_ref[jnp.maximum(e - 1, 0)])

        def zstart(e, carry):
            @pl.when(nonempty(e))
            def _():
                zero_block(e).start()
            return carry

        def zwait(e, carry):
            @pl.when(nonempty(e))
            def _():
                zero_block(e).wait()
            return carry

        lax.fori_loop(0, N_EXPERTS, zstart, 0)
        lax.fori_loop(0, N_EXPERTS, zwait, 0)

        def tail_block(i):
            start = pl.multiple_of(i * DISPATCH_BLK, DISPATCH_BLK)
            return pltpu.make_async_copy(zero_scr, xb_ref.at[pl.ds(start, DISPATCH_BLK)], zsem)

        def tstart(i, carry):
            tail_block(i).start()
            return carry

        def twait(i, carry):
            tail_block(i).wait()
            return carry

        n_used = pe_ref[N_EXPERTS - 1] // DISPATCH_BLK
        n_blocks = xb_ref.shape[0] // DISPATCH_BLK
        lax.fori_loop(n_used, n_blocks, tstart, 0)
        lax.fori_loop(n_used, n_blocks, twait, 0)

    def issue(r, carry):
        _row_copy(u_ref, r, xb_ref, d0_ref[base + r], sem.at[0]).start()
        _row_copy(u_ref, r, xb_ref, d1_ref[base + r], sem.at[1]).start()
        return carry

    lax.fori_loop(0, TT, issue, 0, unroll=ROW_UNROLL)
    for k in range(2):
        pltpu.make_async_copy(u_ref, xb_ref.at[pl.ds(0, TT)], sem.at[k]).wait()


def _dispatch(d0, d1, pends, u2, rows):
    n_pad = u2.shape[0]
    return pl.pallas_call(
        _dispatch_kernel,
        grid_spec=pltpu.PrefetchScalarGridSpec(
            num_scalar_prefetch=3,
            grid=(n_pad // TT,),
            in_specs=[pl.BlockSpec((TT, D_MODEL), lambda m, d0, d1, pe: (m, 0))],
            out_specs=pl.BlockSpec(memory_space=pl.ANY),
            scratch_shapes=[
                pltpu.VMEM((DISPATCH_BLK, D_MODEL), F32),
                pltpu.SemaphoreType.DMA((2,)),
                pltpu.SemaphoreType.DMA(()),
            ],
        ),
        out_shape=jax.ShapeDtypeStruct((rows, D_MODEL), F32),
        compiler_params=pltpu.CompilerParams(
            dimension_semantics=("arbitrary",), vmem_limit_bytes=VMEM_LIMIT),
        name="dispatch",
    )(d0, d1, pends, u2)


WEIGHT_UNITS = 8
WEIGHT_DEPTH = 4


def _stream_weights(i, tgt_ref, eseq_ref, ntot_ref, w_hbm, stages, wbfs, sem, done_scr):
    def slab(u, k):
        rows = stages[k].shape[1]
        return pl.ds(pl.multiple_of((u % WEIGHT_UNITS) * rows, rows), rows)

    def unit_copies(u):
        e = eseq_ref[u // WEIGHT_UNITS]
        b = u % WEIGHT_DEPTH
        return [pltpu.make_async_copy(w.at[e, slab(u, k), :], stages[k].at[b], sem.at[b, k])
                for k, w in enumerate(w_hbm)]

    @pl.when(i == 0)
    def _():
        done_scr[0] = 0
        for u in range(WEIGHT_DEPTH):
            for c in unit_copies(u):
                c.start()

    def body(u, carry):
        for c in unit_copies(u):
            c.wait()
        slot = (u // WEIGHT_UNITS) % 2
        b = u % WEIGHT_DEPTH
        for k in range(len(w_hbm)):
            wbfs[k][slot, slab(u, k), :] = stages[k][b].astype(BF16)

        @pl.when(u + WEIGHT_DEPTH < ntot_ref[0])
        def _():
            for c in unit_copies(u + WEIGHT_DEPTH):
                c.start()

        return carry

    lax.fori_loop(done_scr[0], tgt_ref[i], body, 0)
    done_scr[0] = tgt_ref[i]


def _expert_kernel(tgt_ref, eseq_ref, ntot_ref, slot_ref, nu_ref, x_ref, wg_hbm, wu_hbm, wd_hbm, y_ref,
                   sg, su, sd, bg, bu, bd, sem, done_scr):
    i = pl.program_id(0)
    _stream_weights(i, tgt_ref, eseq_ref, ntot_ref, (wg_hbm, wu_hbm, wd_hbm), (sg, su, sd), (bg, bu, bd),
                    sem, done_scr)
    used = i < nu_ref[0]

    @pl.when(used)
    def _():
        slot = slot_ref[i]
        x = x_ref[...].astype(BF16)
        g = jnp.dot(x, bg[slot], preferred_element_type=F32)
        u = jnp.dot(x, bu[slot], preferred_element_type=F32)
        a = (jax.nn.silu(g) * u).astype(BF16)
        y_ref[...] = jnp.dot(a, bd[slot], preferred_element_type=F32)

    @pl.when(jnp.logical_not(used))
    def _():
        y_ref[...] = jnp.zeros_like(y_ref)


def _experts(sched, x_buf, w_gate, w_up, w_down):
    rows = x_buf.shape[0]
    n_blocks = rows // DISPATCH_BLK
    blk_map = lambda i, tg, es, nt, sl, nu: (jnp.minimum(i, jnp.maximum(nu[0] - 1, 0)), 0)
    out_map = lambda i, tg, es, nt, sl, nu: (i, 0)
    hbm = pl.BlockSpec(memory_space=pl.ANY)
    stage = lambda k_dim, n_dim: pltpu.VMEM((WEIGHT_DEPTH, k_dim // WEIGHT_UNITS, n_dim), F32)
    resident = lambda k_dim, n_dim: pltpu.VMEM((2, k_dim, n_dim), BF16)
    return pl.pallas_call(
        _expert_kernel,
        grid_spec=pltpu.PrefetchScalarGridSpec(
            num_scalar_prefetch=5,
            grid=(n_blocks,),
            in_specs=[pl.BlockSpec((DISPATCH_BLK, D_MODEL), blk_map), hbm, hbm, hbm],
            out_specs=pl.BlockSpec((DISPATCH_BLK, D_MODEL), out_map),
            scratch_shapes=[
                stage(D_MODEL, D_EXPERT), stage(D_MODEL, D_EXPERT), stage(D_EXPERT, D_MODEL),
                resident(D_MODEL, D_EXPERT), resident(D_MODEL, D_EXPERT), resident(D_EXPERT, D_MODEL),
                pltpu.SemaphoreType.DMA((WEIGHT_DEPTH, 3)),
                pltpu.SMEM((1,), jnp.int32),
            ],
        ),
        out_shape=jax.ShapeDtypeStruct((rows, D_MODEL), F32),
        compiler_params=pltpu.CompilerParams(
            dimension_semantics=("arbitrary",), vmem_limit_bytes=VMEM_LIMIT),
        name="experts",
    )(*sched, x_buf, w_gate, w_up, w_down)


def _expert_schedule(padded, pstarts, n_blocks):
    nonempty = padded > 0
    seq_of_expert = jnp.cumsum(nonempty.astype(jnp.int32)) - 1
    n_seq = jnp.sum(nonempty.astype(jnp.int32))
    experts = jnp.arange(N_EXPERTS, dtype=jnp.int32)
    is_kth = jnp.logical_and(nonempty[None, :], seq_of_expert[None, :] == experts[:, None])
    eseq = jnp.sum(jnp.where(is_kth, experts[None, :], 0), axis=1)
    pends = pstarts + padded
    n_used = pends[-1] // DISPATCH_BLK
    blk = jnp.arange(n_blocks, dtype=jnp.int32)
    blk_c = jnp.minimum(blk, jnp.maximum(n_used - 1, 0))
    blk_expert = jnp.minimum(
        jnp.sum((pends[None, :] <= (blk_c * DISPATCH_BLK)[:, None]).astype(jnp.int32), axis=1), N_EXPERTS - 1)
    onehot = (blk_expert[:, None] == experts[None, :]).astype(jnp.int32)
    pick = lambda v: jnp.sum(onehot * v[None, :], axis=1)
    q = pick(seq_of_expert)
    j = blk_c - pick(pstarts) // DISPATCH_BLK
    n = jnp.maximum(pick(padded) // DISPATCH_BLK, 1)
    n_total = n_seq * WEIGHT_UNITS
    tgt = jnp.minimum(WEIGHT_UNITS * (q + 1) + (WEIGHT_UNITS * j) // n, n_total)
    tgt = jnp.where(blk < n_used, tgt, n_total)
    i32 = lambda a: a.astype(jnp.int32)
    return i32(tgt), eseq, i32(n_total)[None], i32(q % 2), i32(n_used)[None]


COMBINE_TM = 512


def _combine_kernel(t_pad, d0_ref, d1_ref, h_hbm, gate_hbm, g_ref, yb_ref, o_ref,
                    h_scr, gate_scr, y0_scr, y1_scr, sem):
    n_j = pl.num_programs(1)
    step = pl.program_id(0) * n_j + pl.program_id(1)
    n_steps = pl.num_programs(0) * n_j

    def base_row(s):
        return pl.multiple_of((s // n_j) * t_pad + N_META + (s % n_j) * COMBINE_TM, 8)

    def tile_copies(s, slot):
        base = base_row(s)
        return [
            pltpu.make_async_copy(h_hbm.at[pl.ds(base, COMBINE_TM)], h_scr.at[slot], sem.at[slot, 2]),
            pltpu.make_async_copy(gate_hbm.at[pl.ds(base, COMBINE_TM)], gate_scr.at[slot], sem.at[slot, 3]),
        ]

    def fetch(s, slot):
        base = base_row(s)
        for c in tile_copies(s, slot):
            c.start()

        def issue(r, carry):
            _row_copy(yb_ref, d0_ref[base + r], y0_scr.at[slot], r, sem.at[slot, 0]).start()
            _row_copy(yb_ref, d1_ref[base + r], y1_scr.at[slot], r, sem.at[slot, 1]).start()
            return carry

        lax.fori_loop(0, COMBINE_TM, issue, 0, unroll=ROW_UNROLL)

    slot = step % 2

    @pl.when(step == 0)
    def _():
        fetch(step, slot)

    @pl.when(step + 1 < n_steps)
    def _():
        fetch(step + 1, 1 - slot)

    pltpu.make_async_copy(yb_ref.at[pl.ds(0, COMBINE_TM)], y0_scr.at[slot], sem.at[slot, 0]).wait()
    pltpu.make_async_copy(yb_ref.at[pl.ds(0, COMBINE_TM)], y1_scr.at[slot], sem.at[slot, 1]).wait()
    for c in tile_copies(step, slot):
        c.wait()

    gate = gate_scr[slot]
    h = h_scr[slot] + gate[:, 0:1] * y0_scr[slot] + gate[:, 1:2] * y1_scr[slot]
    o_ref[...] = _rmsnorm(h, g_ref[...])


def _combine(d0, d1, h1, gates, g_final, y_buf, batch, seq):
    assert seq % COMBINE_TM == 0 and N_META % 8 == 0
    tile = lambda w: pltpu.VMEM((2, COMBINE_TM, w), F32)
    return pl.pallas_call(
        functools.partial(_combine_kernel, h1.shape[0] // batch),
        grid_spec=pltpu.PrefetchScalarGridSpec(
            num_scalar_prefetch=2,
            grid=(batch, seq // COMBINE_TM),
            in_specs=[
                pl.BlockSpec(memory_space=pl.ANY),
                pl.BlockSpec(memory_space=pl.ANY),
                pl.BlockSpec((1, D_MODEL), lambda b, j, d0, d1: (0, 0)),
                pl.BlockSpec(memory_space=pl.ANY),
            ],
            out_specs=pl.BlockSpec((None, COMBINE_TM, D_MODEL), lambda b, j, d0, d1: (b, j, 0)),
            scratch_shapes=[tile(D_MODEL), tile(LANES), tile(D_MODEL), tile(D_MODEL),
                            pltpu.SemaphoreType.DMA((2, 4))],
        ),
        out_shape=jax.ShapeDtypeStruct((batch, seq, D_MODEL), F32),
        compiler_params=pltpu.CompilerParams(
            dimension_semantics=("arbitrary", "arbitrary"), vmem_limit_bytes=VMEM_LIMIT),
        name="combine",
    )(d0, d1, h1, gates, g_final, y_buf)


def _block_diag(w):
    per = LRU_CB // LRU_BLOCK_DIM
    w4 = w.reshape(LRU_BLOCKS // per, per, LRU_BLOCK_DIM, LRU_BLOCK_DIM)
    eye = jnp.eye(per, dtype=w.dtype)
    bd = jnp.einsum('cpij,pq->cpiqj', w4, eye)
    return bd.reshape(LRU_BLOCKS // per, LRU_CB, LRU_CB)


def kernel(x, meta_tokens, g_mix, w_in, conv_w, conv_b, w_rgate, b_rgate, w_igate, b_igate, lru_L, lambda_q1, lambda_k1, lambda_q2, lambda_k2, subln_g, rel_bias, w_out, g_ffn, w_group, b_group, w_router, b_router, w_gate, w_up, w_down, g_final):
    batch, seq, _ = x.shape
    t_real = N_META + seq
    t_pad = _round_up(t_real, TT)
    n_pad = batch * t_pad

    meta = jnp.broadcast_to(meta_tokens[None].astype(x.dtype), (batch, N_META, D_MODEL))
    h0 = jnp.concatenate([meta, x, jnp.zeros((batch, t_pad - t_real, D_MODEL), x.dtype)], axis=1)
    h0 = h0.reshape(n_pad, D_MODEL)

    proj_lru, proj_qkv = _in_proj(h0, g_mix[0][None], w_in[0].astype(BF16))

    y_lru = _lru(proj_lru, conv_w[0], conv_b[0][None],
                 _block_diag(w_rgate[0]).astype(BF16), b_rgate[0].reshape(1, D_LRU),
                 _block_diag(w_igate[0]).astype(BF16), b_igate[0].reshape(1, D_LRU),
                 lru_L[0][None], batch, t_pad)

    lam_params = jnp.stack([lambda_q1[0], lambda_k1[0], lambda_q2[0], lambda_k2[0]])
    y_att = _attention(proj_qkv, rel_bias, lam_params, subln_g[0][None], batch, t_pad)

    w_rt = jnp.concatenate([w_group[0], w_router[0],
                            jnp.zeros((D_MODEL, LANES - N_GROUPS - N_EXPERTS), F32)], axis=1)
    b_rt = jnp.concatenate([b_group[0], b_router[0],
                            jnp.zeros((LANES - N_GROUPS - N_EXPERTS,), F32)])[None]
    h1, u2, logits = _out_proj(y_lru, y_att, h0, w_out[0].astype(BF16), g_ffn[0][None], w_rt, b_rt)

    idx, gates, counts = _route(logits)

    cnt = counts[0, N_GROUPS:N_GROUPS + N_EXPERTS].astype(jnp.int32)
    padded = (cnt + DISPATCH_BLK - 1) // DISPATCH_BLK * DISPATCH_BLK
    pends = jnp.cumsum(padded)
    pstarts = pends - padded
    dest = _dest(idx, jnp.pad(pstarts.astype(F32), (0, LANES - N_EXPERTS))[None])
    d0 = dest[:, 0]
    d1 = dest[:, 1]
    n_blocks = -(-(2 * n_pad + N_EXPERTS * (DISPATCH_BLK - 1)) // DISPATCH_BLK)
    rows = n_blocks * DISPATCH_BLK
    sched = _expert_schedule(padded, pstarts, n_blocks)

    x_buf = _dispatch(d0, d1, pends.astype(jnp.int32), u2, rows)
    y_buf = _experts(sched, x_buf, w_gate[0], w_up[0], w_down[0])
    return _combine(d0, d1, h1, gates, g_final[None], y_buf, batch, seq)
```

```python
import functools
import math

import numpy as np
import jax
import jax.numpy as jnp
from jax import lax
from jax.experimental import pallas as pl
from jax.experimental.pallas import tpu as pltpu

D_MODEL = 2048
N_META = 16
D_LRU = 1024
D_ATT = 1024
LRU_BLOCKS = 16
LRU_BLOCK_DIM = 64
CONV_WIDTH = 4
LRU_C = 8.0
N_HEADS = 8
V_HEAD_DIM = 128
QK_HEAD_DIM = 64
N_BUCKETS = 32
MAX_DISTANCE = 128
N_GROUPS = 4
EXPERTS_PER_GROUP = 8
N_EXPERTS = 32
D_EXPERT = 1024
D_IN_PROJ = 2 * D_LRU + 3 * D_ATT
EPS = 1e-6
LAMBDA_INIT = 0.8 - 0.6 * math.exp(-0.3 * 0)
LOG2_E = math.log2(math.e)

F32 = jnp.float32
BF16 = jnp.bfloat16

LANES = 128
SUBLANES = 8
MXU_DIM = 256
TT = 768
PROJ_TN = 1024
OUT_TM = TT // 2
ATT_SUB = MXU_DIM
ATT_LOOKAHEAD = 4
ATT_GROUP = 4
VT_ROWS = V_HEAD_DIM + 16
LRU_CB = MXU_DIM
DISPATCH_BLK = 256
VMEM_LIMIT = 56 * 1024 * 1024
NEG_BIG = -1e30


def _round_up(a, b):
    return -(-a // b) * b


def _rmsnorm(x, g):
    ms = jnp.mean(x * x, axis=-1, keepdims=True)
    return x * lax.rsqrt(ms + EPS) * g


def _inproj_kernel(n_t, seq, x_hbm, meta_ref, g_ref, w_ref, h0_ref, lru_ref, qkv_ref, xbuf, u_scr, sem):
    m = pl.program_id(0)
    n = pl.program_id(1)
    n_m = pl.num_programs(0)
    last_rows = seq + N_META - (n_t - 1) * TT

    def tile_copy(mm, slot):
        b = mm // n_t
        t = mm % n_t
        row0 = pl.multiple_of(t * TT - N_META, SUBLANES)
        return [
            (t == 0, pltpu.make_async_copy(x_hbm.at[b, pl.ds(0, TT - N_META)],
                                           xbuf.at[slot, pl.ds(N_META, TT - N_META)], sem.at[slot])),
            (jnp.logical_and(t > 0, t < n_t - 1),
             pltpu.make_async_copy(x_hbm.at[b, pl.ds(row0, TT)], xbuf.at[slot], sem.at[slot])),
            (t == n_t - 1, pltpu.make_async_copy(x_hbm.at[b, pl.ds(row0, last_rows)],
                                                 xbuf.at[slot, pl.ds(0, last_rows)], sem.at[slot])),
        ]

    def start(mm, slot):
        for cond, copy in tile_copy(mm, slot):
            @pl.when(cond)
            def _(copy=copy):
                copy.start()

    def wait(mm, slot):
        for cond, copy in tile_copy(mm, slot):
            @pl.when(cond)
            def _(copy=copy):
                copy.wait()

    slot = m % 2

    @pl.when(jnp.logical_and(m == 0, n == 0))
    def _():
        start(m, slot)

    @pl.when(jnp.logical_and(n == 1, m + 1 < n_m))
    def _():
        start(m + 1, 1 - slot)

    @pl.when(n == 0)
    def _():
        wait(m, slot)
        t = m % n_t

        @pl.when(t == 0)
        def _():
            xbuf[slot, 0:N_META, :] = meta_ref[...]

        @pl.when(t == n_t - 1)
        def _():
            xbuf[slot, last_rows:TT, :] = jnp.zeros((TT - last_rows, D_MODEL), F32)

        h0 = xbuf[slot]
        h0_ref[...] = h0
        u_scr[...] = _rmsnorm(h0, g_ref[...]).astype(BF16)

    y = jnp.dot(u_scr[...], w_ref[...], preferred_element_type=F32)

    @pl.when(n < 2)
    def _():
        lru_ref[...] = y

    @pl.when(n >= 2)
    def _():
        qkv_ref[...] = y.astype(BF16)


def _in_proj(x, meta_tokens, g_mix, w_in_bf16, t_pad):
    batch, seq, _ = x.shape
    n_t = t_pad // TT
    n_pad = batch * t_pad
    n_col = D_IN_PROJ // PROJ_TN
    assert N_META % SUBLANES == 0 and 0 < seq + N_META - (n_t - 1) * TT <= TT
    return pl.pallas_call(
        functools.partial(_inproj_kernel, n_t, seq),
        grid=(n_pad // TT, n_col),
        in_specs=[
            pl.BlockSpec(memory_space=pl.ANY),
            pl.BlockSpec((N_META, D_MODEL), lambda m, n: (0, 0)),
            pl.BlockSpec((1, D_MODEL), lambda m, n: (0, 0)),
            pl.BlockSpec((D_MODEL, PROJ_TN), lambda m, n: (0, n)),
        ],
        out_specs=[
            pl.BlockSpec((TT, D_MODEL), lambda m, n: (m, 0)),
            pl.BlockSpec((TT, PROJ_TN), lambda m, n: (m, jnp.minimum(n, 1))),
            pl.BlockSpec((TT, PROJ_TN), lambda m, n: (m, jnp.maximum(n - 2, 0))),
        ],
        out_shape=[
            jax.ShapeDtypeStruct((n_pad, D_MODEL), F32),
            jax.ShapeDtypeStruct((n_pad, 2 * D_LRU), F32),
            jax.ShapeDtypeStruct((n_pad, 3 * D_ATT), BF16),
        ],
        scratch_shapes=[
            pltpu.VMEM((2, TT, D_MODEL), F32),
            pltpu.VMEM((TT, D_MODEL), BF16),
            pltpu.SemaphoreType.DMA((2,)),
        ],
        compiler_params=pltpu.CompilerParams(
            dimension_semantics=("arbitrary", "arbitrary"), vmem_limit_bytes=VMEM_LIMIT),
        name="in_proj",
    )(x, meta_tokens, g_mix, w_in_bf16)


def _shift_rows(x, tail, s):
    r = pltpu.roll(x, s, 0)
    tail_row = lax.broadcasted_iota(jnp.int32, tail.shape, 0)
    head = jnp.where(tail_row < s, pltpu.roll(tail, s, 0), r[0:SUBLANES])
    return jnp.concatenate([head, r[SUBLANES:]], axis=0)


def _lru_kernel(x_ref, gate_ref, cw_ref, cb_ref, wr_ref, br_ref, wi_ref, bi_ref, lam_ref,
                y_ref, tail_scr, h_scr):
    t = pl.program_id(2)

    @pl.when(t == 0)
    def _():
        tail_scr[...] = jnp.zeros_like(tail_scr)
        h_scr[...] = jnp.zeros_like(h_scr)

    x = x_ref[...]
    tail = tail_scr[...]
    cw = cw_ref[...]
    xc = cb_ref[...] + x * cw[CONV_WIDTH - 1:CONV_WIDTH]
    for k in range(CONV_WIDTH - 1):
        xc = xc + _shift_rows(x, tail, CONV_WIDTH - 1 - k) * cw[k:k + 1]
    tail_scr[...] = x[TT - SUBLANES:TT]

    xb = xc.astype(BF16)
    r = jax.nn.sigmoid(jnp.dot(xb, wr_ref[...], preferred_element_type=F32) + br_ref[...])
    i = jax.nn.sigmoid(jnp.dot(xb, wi_ref[...], preferred_element_type=F32) + bi_ref[...])
    lam = lam_ref[...]
    log_sig = jnp.minimum(lam, 0.0) - jnp.log(1.0 + jnp.exp(-jnp.abs(lam)))
    log_a = (LRU_C * r) * log_sig
    a = jnp.exp(log_a)
    b = jnp.sqrt(1.0 - a * a) * (i * xc)

    row = lax.broadcasted_iota(jnp.int32, a.shape, 0) % SUBLANES
    s = 1
    while s < SUBLANES:
        keep = row >= s
        a_sh = jnp.where(keep, pltpu.roll(a, s, 0), 1.0)
        b_sh = jnp.where(keep, pltpu.roll(b, s, 0), 0.0)
        b = a * b_sh + b
        a = a * a_sh
        s *= 2
    carry = h_scr[...]
    groups = []
    for g in range(TT // SUBLANES):
        rows = slice(g * SUBLANES, (g + 1) * SUBLANES)
        hg = b[rows] + a[rows] * carry
        groups.append(hg)
        carry = jnp.broadcast_to(hg[SUBLANES - 1:SUBLANES], hg.shape)
    h_scr[...] = carry
    h = jnp.concatenate(groups, axis=0)
    y_ref[...] = (h * jax.nn.gelu(gate_ref[...])).astype(BF16)


def _lru(proj_lru, conv_w, conv_b, wr_bd, b_r, wi_bd, b_i, lru_l, batch, t_pad):
    n_pad = proj_lru.shape[0]
    n_t = t_pad // TT
    n_c = D_LRU // LRU_CB
    vec = lambda: pl.BlockSpec((1, LRU_CB), lambda b, c, t: (0, c))
    return pl.pallas_call(
        _lru_kernel,
        grid=(batch, n_c, n_t),
        in_specs=[
            pl.BlockSpec((TT, LRU_CB), lambda b, c, t: (b * n_t + t, c)),
            pl.BlockSpec((TT, LRU_CB), lambda b, c, t: (b * n_t + t, n_c + c)),
            pl.BlockSpec((CONV_WIDTH, LRU_CB), lambda b, c, t: (0, c)),
            vec(),
            pl.BlockSpec((None, LRU_CB, LRU_CB), lambda b, c, t: (c, 0, 0)),
            vec(),
            pl.BlockSpec((None, LRU_CB, LRU_CB), lambda b, c, t: (c, 0, 0)),
            vec(),
            vec(),
        ],
        out_specs=pl.BlockSpec((TT, LRU_CB), lambda b, c, t: (b * n_t + t, c)),
        out_shape=jax.ShapeDtypeStruct((n_pad, D_LRU), BF16),
        scratch_shapes=[pltpu.VMEM((SUBLANES, LRU_CB), F32), pltpu.VMEM((SUBLANES, LRU_CB), F32)],
        compiler_params=pltpu.CompilerParams(
            dimension_semantics=("arbitrary", "arbitrary", "arbitrary"),
            vmem_limit_bytes=VMEM_LIMIT),
        name="rglru",
    )(proj_lru, proj_lru, conv_w, conv_b, wr_bd, b_r, wi_bd, b_i, lru_l)


def _bucket_thresholds():
    max_exact = N_BUCKETS // 2
    n = np.arange(0, MAX_DISTANCE + 1)
    nf = np.maximum(n, 1).astype(np.float32)
    large = max_exact + (np.log(nf / np.float32(max_exact)) / np.float32(math.log(MAX_DISTANCE / max_exact))
                         * np.float32(N_BUCKETS - max_exact)).astype(np.int32)
    large = np.minimum(large, N_BUCKETS - 1)
    bucket = np.where(n < max_exact, n, large)
    return [int(np.argmax(bucket >= k)) for k in range(1, N_BUCKETS)]


_BUCKET_THR = _bucket_thresholds()


def _attn_kernel(rb_ref, q_ref, k_ref, v_ref, lam_ref, g_ref, o_ref, bias_scr, vt_scr, m_scr, acc_scr):
    h = pl.program_id(0)
    b = pl.program_id(1)
    qi = pl.program_id(2)
    tk = TT
    n_kv = vt_scr.shape[0]

    @pl.when(jnp.logical_and(b == 0, qi == 0))
    def _():
        k_pos = lax.broadcasted_iota(jnp.int32, (tk, tk), 0)
        q_pos = lax.broadcasted_iota(jnp.int32, (tk, tk), 1)
        far = rb_ref[N_BUCKETS - 1, h]
        for d in range(2):
            rel = q_pos - k_pos + d * tk
            val = jnp.full((tk, tk), rb_ref[0, h] - far, F32)
            for kk, thr in enumerate(_BUCKET_THR):
                val = jnp.where(rel >= thr, rb_ref[kk + 1, h] - far, val)
            val = val * LOG2_E
            if d == 0:
                val = jnp.where(rel >= 0, val, -jnp.inf)
            bias_scr[d] = val

    @pl.when(qi == 0)
    def _():
        for t in range(n_kv):
            vt_scr[t, 0:V_HEAD_DIM, :] = v_ref[t * tk:(t + 1) * tk, :].astype(F32).T.astype(BF16)
            vt_scr[t, V_HEAD_DIM:VT_ROWS, :] = jnp.ones((VT_ROWS - V_HEAD_DIM, tk), BF16)

    q = q_ref[...]
    lane = lax.broadcasted_iota(jnp.int32, q.shape, 1)
    scale = (QK_HEAD_DIM ** -0.5) * LOG2_E
    qf = q.astype(F32) * scale
    qs = jnp.concatenate([jnp.where(lane < QK_HEAD_DIM, qf, 0.0),
                          jnp.where(lane >= QK_HEAD_DIM, qf, 0.0)], axis=0).astype(BF16)

    m_scr[...] = jnp.full(m_scr.shape, NEG_BIG, F32)
    acc_scr[...] = jnp.zeros_like(acc_scr)

    def tile_group(tiles):
        n_sub = 2 * tk // ATT_SUB
        stages = [(t, c) for t in range(len(tiles)) for c in range(n_sub)]

        def n_keys(t, c):
            return (c * ATT_SUB) % tk + ATT_SUB if tiles[t][1] == 0 else tk

        def scores(t, c):
            j, bias_idx = tiles[t]
            nk = n_keys(t, c)
            kt = k_ref[pl.ds(pl.multiple_of(j * tk, tk), nk), :]
            cols = slice(c * ATT_SUB, (c + 1) * ATT_SUB)
            s = lax.dot_general(kt, qs[cols], (((1,), (1,)), ((), ())), preferred_element_type=F32)
            if bias_idx is not None:
                q0 = (c * ATT_SUB) % tk
                s = s + bias_scr[bias_idx, 0:nk, q0:q0 + ATT_SUB]
            return s

        pending = [scores(*st) for st in stages[:ATT_LOOKAHEAD]]
        for n, (t, c) in enumerate(stages):
            cols = slice(c * ATT_SUB, (c + 1) * ATT_SUB)
            s = pending.pop(0)
            if n + ATT_LOOKAHEAD < len(stages):
                pending.append(scores(*stages[n + ATT_LOOKAHEAD]))
            m_prev = m_scr[:, cols]
            m_new = jnp.maximum(m_prev, jnp.max(s, axis=0, keepdims=True))
            alpha = jnp.exp2(m_prev - m_new)
            p = jnp.exp2(s - m_new).astype(BF16)
            vt = vt_scr[tiles[t][0], :, 0:n_keys(t, c)]
            acc_scr[:, cols] = alpha * acc_scr[:, cols] + jnp.dot(vt, p, preferred_element_type=F32)
            m_scr[:, cols] = m_new

    n_far = jnp.maximum(qi - 1, 0)

    def far_group(i, carry):
        tile_group([(ATT_GROUP * i + t, None) for t in range(ATT_GROUP)])
        return carry

    lax.fori_loop(0, n_far // ATT_GROUP, far_group, 0)
    done = n_far - n_far % ATT_GROUP
    piece = ATT_GROUP // 2
    while piece >= 1:
        @pl.when((n_far % (2 * piece)) >= piece)
        def _(done=done, piece=piece):
            tile_group([(done + t, None) for t in range(piece)])
        done = done + jnp.where((n_far % (2 * piece)) >= piece, piece, 0)
        piece //= 2

    @pl.when(qi >= 1)
    def _():
        tile_group([(qi - 1, 1), (qi, 0)])

    @pl.when(qi == 0)
    def _():
        tile_group([(qi, 0)])

    lam_p = lam_ref[...]
    lam = (jnp.exp(jnp.sum(lam_p[0:1] * lam_p[1:2], axis=-1, keepdims=True))
           - jnp.exp(jnp.sum(lam_p[2:3] * lam_p[3:4], axis=-1, keepdims=True)) + LAMBDA_INIT)
    acc = acc_scr[...]
    num = acc[0:V_HEAD_DIM]
    den = acc[V_HEAD_DIM:V_HEAD_DIM + 1]
    o_t = num[:, :tk] / den[:, :tk] - lam * (num[:, tk:] / den[:, tk:])
    o_ref[...] = (_rmsnorm(o_t.T, g_ref[...]) * (1.0 - LAMBDA_INIT)).astype(BF16)


def _attention(qkv, rel_bias, lam_params, subln_g, batch, t_pad):
    n_q = t_pad // TT
    qkv3 = qkv.reshape(batch, t_pad, 3 * D_ATT)
    out = pl.pallas_call(
        _attn_kernel,
        grid_spec=pltpu.PrefetchScalarGridSpec(
            num_scalar_prefetch=0,
            grid=(N_HEADS, batch, n_q),
            in_specs=[
                pl.BlockSpec(memory_space=pltpu.SMEM),
                pl.BlockSpec((None, TT, V_HEAD_DIM), lambda h, b, q: (b, q, h)),
                pl.BlockSpec((None, t_pad, V_HEAD_DIM), lambda h, b, q: (b, 0, N_HEADS + h)),
                pl.BlockSpec((None, t_pad, V_HEAD_DIM), lambda h, b, q: (b, 0, 2 * N_HEADS + h)),
                pl.BlockSpec((4, QK_HEAD_DIM), lambda h, b, q: (0, 0)),
                pl.BlockSpec((1, V_HEAD_DIM), lambda h, b, q: (0, 0)),
            ],
            out_specs=pl.BlockSpec((None, TT, V_HEAD_DIM), lambda h, b, q: (b, q, h)),
            scratch_shapes=[
                pltpu.VMEM((2, TT, TT), F32),
                pltpu.VMEM((n_q, VT_ROWS, TT), BF16),
                pltpu.VMEM((1, 2 * TT), F32),
                pltpu.VMEM((VT_ROWS, 2 * TT), F32),
            ],
        ),
        out_shape=jax.ShapeDtypeStruct((batch, t_pad, D_ATT), BF16),
        compiler_params=pltpu.CompilerParams(
            dimension_semantics=("arbitrary", "arbitrary", "arbitrary"),
            vmem_limit_bytes=VMEM_LIMIT),
        name="diff_attention",
    )(rel_bias, qkv3, qkv3, qkv3, lam_params, subln_g)
    return out.reshape(batch * t_pad, D_ATT)


def _outproj_kernel(yl_ref, ya_ref, h0_ref, w_ref, g_ref, wrt_ref, brt_ref, h1_ref, u_ref, lg_ref):
    acc = jnp.dot(yl_ref[...], w_ref[0:D_LRU, :], preferred_element_type=F32)
    acc = acc + jnp.dot(ya_ref[...], w_ref[D_LRU:, :], preferred_element_type=F32)
    h1 = h0_ref[...] + acc
    h1_ref[...] = h1
    u = _rmsnorm(h1, g_ref[...])
    u_ref[...] = u
    u_hi = u.astype(BF16)
    u_lo = (u - u_hi.astype(F32)).astype(BF16)
    w = wrt_ref[...]
    w_hi = w.astype(BF16)
    w_lo = (w - w_hi.astype(F32)).astype(BF16)
    parts = jnp.dot(jnp.concatenate([u_hi, u_lo], axis=0), jnp.concatenate([w_hi, w_lo], axis=1),
                    preferred_element_type=F32)
    lg = (parts[:OUT_TM, :LANES] + parts[:OUT_TM, LANES:]) + (parts[OUT_TM:, :LANES] + parts[OUT_TM:, LANES:])
    lg_ref[...] = lg + brt_ref[...]


def _out_proj(y_lru, y_att, h0, w_out_bf16, g_ffn, w_rt, b_rt):
    n_pad = h0.shape[0]
    row = lambda w: pl.BlockSpec((OUT_TM, w), lambda m: (m, 0))
    full = lambda a, b: pl.BlockSpec((a, b), lambda m: (0, 0))
    return pl.pallas_call(
        _outproj_kernel,
        grid=(n_pad // OUT_TM,),
        in_specs=[row(D_LRU), row(D_ATT), row(D_MODEL), full(D_MODEL, D_MODEL), full(1, D_MODEL),
                  full(D_MODEL, LANES), full(1, LANES)],
        out_specs=[row(D_MODEL), row(D_MODEL), row(LANES)],
        out_shape=[
            jax.ShapeDtypeStruct((n_pad, D_MODEL), F32),
            jax.ShapeDtypeStruct((n_pad, D_MODEL), F32),
            jax.ShapeDtypeStruct((n_pad, LANES), F32),
        ],
        compiler_params=pltpu.CompilerParams(
            dimension_semantics=("arbitrary",), vmem_limit_bytes=VMEM_LIMIT),
        name="out_proj",
    )(y_lru, y_att, h0, w_out_bf16, g_ffn, w_rt, b_rt)


def _route_kernel(lg_ref, idx_ref, gate_ref, cnt_ref, carry_scr):
    step = pl.program_id(0)

    @pl.when(step == 0)
    def _():
        carry_scr[...] = jnp.zeros_like(carry_scr)

    lg = lg_ref[...]
    lane = lax.broadcasted_iota(jnp.int32, lg.shape, 1)
    first = lambda mask: jnp.min(jnp.where(mask, lane, LANES), axis=-1, keepdims=True)

    is_g = lane < N_GROUPS
    gl = jnp.where(is_g, lg, -jnp.inf)
    gmax = jnp.max(gl, axis=-1, keepdims=True)
    grp = first(gl == gmax)
    gsum = jnp.sum(jnp.where(is_g, jnp.exp(gl - gmax), 0.0), axis=-1, keepdims=True)
    p_grp = 1.0 / gsum

    lane_e = lane - N_GROUPS
    in_grp = (lane_e >= 0) & (lane_e < N_EXPERTS) & ((lane_e // EXPERTS_PER_GROUP) == grp)
    el = jnp.where(in_grp, lg, -jnp.inf)
    v1 = jnp.max(el, axis=-1, keepdims=True)
    i1 = first(el == v1)
    el2 = jnp.where(lane == i1, -jnp.inf, el)
    v2 = jnp.max(el2, axis=-1, keepdims=True)
    i2 = first(el2 == v2)
    e2 = jnp.exp(v2 - v1)
    den = 1.0 + e2
    g0 = p_grp * (1.0 / den)
    g1 = p_grp * (e2 / den)

    hit1 = lane == i1
    hit2 = lane == i2
    onehot = jnp.where(hit1 | hit2, 1.0, 0.0)
    rr = lax.broadcasted_iota(jnp.int32, (TT, TT), 0)
    cc = lax.broadcasted_iota(jnp.int32, (TT, TT), 1)
    tri = jnp.where(cc < rr, 1.0, 0.0).astype(BF16)
    rank = jnp.dot(tri, onehot.astype(BF16), preferred_element_type=F32) + carry_scr[...]
    r0 = jnp.sum(jnp.where(hit1, rank, 0.0), axis=-1, keepdims=True).astype(jnp.int32)
    r1 = jnp.sum(jnp.where(hit2, rank, 0.0), axis=-1, keepdims=True).astype(jnp.int32)
    carry_scr[...] = carry_scr[...] + jnp.sum(onehot, axis=0, keepdims=True)

    idx_ref[...] = jnp.where(lane == 0, i1 - N_GROUPS,
                             jnp.where(lane == 1, i2 - N_GROUPS,
                                       jnp.where(lane == 2, r0, jnp.where(lane == 3, r1, 0))))
    gate_ref[...] = jnp.where(lane == 0, g0, jnp.where(lane == 1, g1, 0.0))
    cnt_ref[...] = carry_scr[...]


def _route(logits):
    n_pad = logits.shape[0]
    row = pl.BlockSpec((TT, LANES), lambda m: (m, 0))
    return pl.pallas_call(
        _route_kernel,
        grid=(n_pad // TT,),
        in_specs=[row],
        out_specs=[row, row, pl.BlockSpec((1, LANES), lambda m: (0, 0))],
        out_shape=[
            jax.ShapeDtypeStruct((n_pad, LANES), jnp.int32),
            jax.ShapeDtypeStruct((n_pad, LANES), F32),
            jax.ShapeDtypeStruct((1, LANES), F32),
        ],
        scratch_shapes=[pltpu.VMEM((1, LANES), F32)],
        compiler_params=pltpu.CompilerParams(
            dimension_semantics=("arbitrary",), vmem_limit_bytes=VMEM_LIMIT),
        name="route",
    )(logits)


def _dest_kernel(idx_ref, ps_ref, o_ref):
    idx = idx_ref[...]
    lane = lax.broadcasted_iota(jnp.int32, idx.shape, 1)
    ps = ps_ref[...]
    pick = lambda e: jnp.sum(jnp.where(lane == e, ps, 0.0), axis=-1, keepdims=True).astype(jnp.int32)
    d0 = pick(idx[:, 0:1]) + idx[:, 2:3]
    d1 = pick(idx[:, 1:2]) + idx[:, 3:4]
    o_ref[...] = jnp.where(lane == 0, d0, jnp.where(lane == 1, d1, 0))


def _dest(idx, pstart_row):
    n_pad = idx.shape[0]
    row = pl.BlockSpec((TT, LANES), lambda m: (m, 0))
    return pl.pallas_call(
        _dest_kernel,
        grid=(n_pad // TT,),
        in_specs=[row, pl.BlockSpec((1, LANES), lambda m: (0, 0))],
        out_specs=row,
        out_shape=jax.ShapeDtypeStruct((n_pad, LANES), jnp.int32),
        compiler_params=pltpu.CompilerParams(
            dimension_semantics=("arbitrary",), vmem_limit_bytes=VMEM_LIMIT),
        name="dest",
    )(idx, pstart_row)


ROW_UNROLL = 8


def _row_copy(src, s_row, dst, d_row, sem):
    return pltpu.make_async_copy(src.at[pl.ds(s_row, 1)], dst.at[pl.ds(d_row, 1)], sem)


def _dispatch_kernel(d0_ref, d1_ref, pe_ref, u_ref, xb_ref, zero_scr, sem, zsem):
    step = pl.program_id(0)
    base = step * TT

    def zero_block(e):
        start = pl.multiple_of(pe_ref[e] - DISPATCH_BLK, DISPATCH_BLK)
        return pltpu.make_async_copy(zero_scr, xb_ref.at[pl.ds(start, DISPATCH_BLK)], zsem)

    @pl.when(step == 0)
    def _():
        zero_scr[...] = jnp.zeros_like(zero_scr)

        def nonempty(e):
            return pe_ref[e] > jnp.where(e == 0, 0, pe_ref[jnp.maximum(e - 1, 0)])

        def zstart(e, carry):
            @pl.when(nonempty(e))
            def _():
                zero_block(e).start()
            return carry

        def zwait(e, carry):
            @pl.when(nonempty(e))
            def _():
                zero_block(e).wait()
            return carry

        lax.fori_loop(0, N_EXPERTS, zstart, 0)
        lax.fori_loop(0, N_EXPERTS, zwait, 0)

        def tail_block(i):
            start = pl.multiple_of(i * DISPATCH_BLK, DISPATCH_BLK)
            return pltpu.make_async_copy(zero_scr, xb_ref.at[pl.ds(start, DISPATCH_BLK)], zsem)

        def tstart(i, carry):
            tail_block(i).start()
            return carry

        def twait(i, carry):
            tail_block(i).wait()
            return carry

        n_used = pe_ref[N_EXPERTS - 1] // DISPATCH_BLK
        n_blocks = xb_ref.shape[0] // DISPATCH_BLK
        lax.fori_loop(n_used, n_blocks, tstart, 0)
        lax.fori_loop(n_used, n_blocks, twait, 0)

    def issue(r, carry):
        _row_copy(u_ref, r, xb_ref, d0_ref[base + r], sem.at[0]).start()
        _row_copy(u_ref, r, xb_ref, d1_ref[base + r], sem.at[1]).start()
        return carry

    lax.fori_loop(0, TT, issue, 0, unroll=ROW_UNROLL)
    for k in range(2):
        pltpu.make_async_copy(u_ref, xb_ref.at[pl.ds(0, TT)], sem.at[k]).wait()


def _dispatch(d0, d1, pends, u2, rows):
    n_pad = u2.shape[0]
    return pl.pallas_call(
        _dispatch_kernel,
        grid_spec=pltpu.PrefetchScalarGridSpec(
            num_scalar_prefetch=3,
            grid=(n_pad // TT,),
            in_specs=[pl.BlockSpec((TT, D_MODEL), lambda m, d0, d1, pe: (m, 0))],
            out_specs=pl.BlockSpec(memory_space=pl.ANY),
            scratch_shapes=[
                pltpu.VMEM((DISPATCH_BLK, D_MODEL), F32),
                pltpu.SemaphoreType.DMA((2,)),
                pltpu.SemaphoreType.DMA(()),
            ],
        ),
        out_shape=jax.ShapeDtypeStruct((rows, D_MODEL), F32),
        compiler_params=pltpu.CompilerParams(
            dimension_semantics=("arbitrary",), vmem_limit_bytes=VMEM_LIMIT),
        name="dispatch",
    )(d0, d1, pends, u2)


WEIGHT_UNITS = 8
WEIGHT_DEPTH = 4


def _stream_weights(i, tgt_ref, eseq_ref, ntot_ref, w_hbm, stages, wbfs, sem, done_scr):
    def slab(u, k):
        rows = stages[k].shape[1]
        return pl.ds(pl.multiple_of((u % WEIGHT_UNITS) * rows, rows), rows)

    def unit_copies(u):
        e = eseq_ref[u // WEIGHT_UNITS]
        b = u % WEIGHT_DEPTH
        return [pltpu.make_async_copy(w.at[e, slab(u, k), :], stages[k].at[b], sem.at[b, k])
                for k, w in enumerate(w_hbm)]

    @pl.when(i == 0)
    def _():
        done_scr[0] = 0
        for u in range(WEIGHT_DEPTH):
            for c in unit_copies(u):
                c.start()

    def body(u, carry):
        for c in unit_copies(u):
            c.wait()
        slot = (u // WEIGHT_UNITS) % 2
        b = u % WEIGHT_DEPTH
        for k in range(len(w_hbm)):
            wbfs[k][slot, slab(u, k), :] = stages[k][b].astype(BF16)

        @pl.when(u + WEIGHT_DEPTH < ntot_ref[0])
        def _():
            for c in unit_copies(u + WEIGHT_DEPTH):
                c.start()

        return carry

    lax.fori_loop(done_scr[0], tgt_ref[i], body, 0)
    done_scr[0] = tgt_ref[i]


def _expert_kernel(tgt_ref, eseq_ref, ntot_ref, slot_ref, nu_ref, x_ref, wg_hbm, wu_hbm, wd_hbm, y_ref,
                   sg, su, sd, bg, bu, bd, sem, done_scr):
    i = pl.program_id(0)
    _stream_weights(i, tgt_ref, eseq_ref, ntot_ref, (wg_hbm, wu_hbm, wd_hbm), (sg, su, sd), (bg, bu, bd),
                    sem, done_scr)
    used = i < nu_ref[0]

    @pl.when(used)
    def _():
        slot = slot_ref[i]
        x = x_ref[...].astype(BF16)
        g = jnp.dot(x, bg[slot], preferred_element_type=F32)
        u = jnp.dot(x, bu[slot], preferred_element_type=F32)
        a = (jax.nn.silu(g) * u).astype(BF16)
        y_ref[...] = jnp.dot(a, bd[slot], preferred_element_type=F32)

    @pl.when(jnp.logical_not(used))
    def _():
        y_ref[...] = jnp.zeros_like(y_ref)


def _experts(sched, x_buf, w_gate, w_up, w_down):
    rows = x_buf.shape[0]
    n_blocks = rows // DISPATCH_BLK
    blk_map = lambda i, tg, es, nt, sl, nu: (jnp.minimum(i, jnp.maximum(nu[0] - 1, 0)), 0)
    out_map = lambda i, tg, es, nt, sl, nu: (i, 0)
    hbm = pl.BlockSpec(memory_space=pl.ANY)
    stage = lambda k_dim, n_dim: pltpu.VMEM((WEIGHT_DEPTH, k_dim // WEIGHT_UNITS, n_dim), F32)
    resident = lambda k_dim, n_dim: pltpu.VMEM((2, k_dim, n_dim), BF16)
    return pl.pallas_call(
        _expert_kernel,
        grid_spec=pltpu.PrefetchScalarGridSpec(
            num_scalar_prefetch=5,
            grid=(n_blocks,),
            in_specs=[pl.BlockSpec((DISPATCH_BLK, D_MODEL), blk_map), hbm, hbm, hbm],
            out_specs=pl.BlockSpec((DISPATCH_BLK, D_MODEL), out_map),
            scratch_shapes=[
                stage(D_MODEL, D_EXPERT), stage(D_MODEL, D_EXPERT), stage(D_EXPERT, D_MODEL),
                resident(D_MODEL, D_EXPERT), resident(D_MODEL, D_EXPERT), resident(D_EXPERT, D_MODEL),
                pltpu.SemaphoreType.DMA((WEIGHT_DEPTH, 3)),
                pltpu.SMEM((1,), jnp.int32),
            ],
        ),
        out_shape=jax.ShapeDtypeStruct((rows, D_MODEL), F32),
        compiler_params=pltpu.CompilerParams(
            dimension_semantics=("arbitrary",), vmem_limit_bytes=VMEM_LIMIT),
        name="experts",
    )(*sched, x_buf, w_gate, w_up, w_down)


def _expert_schedule(padded, pstarts, n_blocks):
    nonempty = padded > 0
    seq_of_expert = jnp.cumsum(nonempty.astype(jnp.int32)) - 1
    n_seq = jnp.sum(nonempty.astype(jnp.int32))
    experts = jnp.arange(N_EXPERTS, dtype=jnp.int32)
    is_kth = jnp.logical_and(nonempty[None, :], seq_of_expert[None, :] == experts[:, None])
    eseq = jnp.sum(jnp.where(is_kth, experts[None, :], 0), axis=1)
    pends = pstarts + padded
    n_used = pends[-1] // DISPATCH_BLK
    blk = jnp.arange(n_blocks, dtype=jnp.int32)
    blk_c = jnp.minimum(blk, jnp.maximum(n_used - 1, 0))
    blk_expert = jnp.minimum(
        jnp.sum((pends[None, :] <= (blk_c * DISPATCH_BLK)[:, None]).astype(jnp.int32), axis=1), N_EXPERTS - 1)
    onehot = (blk_expert[:, None] == experts[None, :]).astype(jnp.int32)
    pick = lambda v: jnp.sum(onehot * v[None, :], axis=1)
    q = pick(seq_of_expert)
    j = blk_c - pick(pstarts) // DISPATCH_BLK
    n = jnp.maximum(pick(padded) // DISPATCH_BLK, 1)
    n_total = n_seq * WEIGHT_UNITS
    tgt = jnp.minimum(WEIGHT_UNITS * (q + 1) + (WEIGHT_UNITS * j) // n, n_total)
    tgt = jnp.where(blk < n_used, tgt, n_total)
    i32 = lambda a: a.astype(jnp.int32)
    return i32(tgt), eseq, i32(n_total)[None], i32(q % 2), i32(n_used)[None]


COMBINE_TM = 512


def _combine_kernel(t_pad, d0_ref, d1_ref, h_hbm, gate_hbm, g_ref, yb_ref, o_ref,
                    h_scr, gate_scr, y0_scr, y1_scr, sem):
    n_j = pl.num_programs(1)
    step = pl.program_id(0) * n_j + pl.program_id(1)
    n_steps = pl.num_programs(0) * n_j

    def base_row(s):
        return pl.multiple_of((s // n_j) * t_pad + N_META + (s % n_j) * COMBINE_TM, 8)

    def tile_copies(s, slot):
        base = base_row(s)
        return [
            pltpu.make_async_copy(h_hbm.at[pl.ds(base, COMBINE_TM)], h_scr.at[slot], sem.at[slot, 2]),
            pltpu.make_async_copy(gate_hbm.at[pl.ds(base, COMBINE_TM)], gate_scr.at[slot], sem.at[slot, 3]),
        ]

    def fetch(s, slot):
        base = base_row(s)
        for c in tile_copies(s, slot):
            c.start()

        def issue(r, carry):
            _row_copy(yb_ref, d0_ref[base + r], y0_scr.at[slot], r, sem.at[slot, 0]).start()
            _row_copy(yb_ref, d1_ref[base + r], y1_scr.at[slot], r, sem.at[slot, 1]).start()
            return carry

        lax.fori_loop(0, COMBINE_TM, issue, 0, unroll=ROW_UNROLL)

    slot = step % 2

    @pl.when(step == 0)
    def _():
        fetch(step, slot)

    @pl.when(step + 1 < n_steps)
    def _():
        fetch(step + 1, 1 - slot)

    pltpu.make_async_copy(yb_ref.at[pl.ds(0, COMBINE_TM)], y0_scr.at[slot], sem.at[slot, 0]).wait()
    pltpu.make_async_copy(yb_ref.at[pl.ds(0, COMBINE_TM)], y1_scr.at[slot], sem.at[slot, 1]).wait()
    for c in tile_copies(step, slot):
        c.wait()

    gate = gate_scr[slot]
    h = h_scr[slot] + gate[:, 0:1] * y0_scr[slot] + gate[:, 1:2] * y1_scr[slot]
    o_ref[...] = _rmsnorm(h, g_ref[...])


def _combine(d0, d1, h1, gates, g_final, y_buf, batch, seq):
    assert seq % COMBINE_TM == 0 and N_META % 8 == 0
    tile = lambda w: pltpu.VMEM((2, COMBINE_TM, w), F32)
    return pl.pallas_call(
        functools.partial(_combine_kernel, h1.shape[0] // batch),
        grid_spec=pltpu.PrefetchScalarGridSpec(
            num_scalar_prefetch=2,
            grid=(batch, seq // COMBINE_TM),
            in_specs=[
                pl.BlockSpec(memory_space=pl.ANY),
                pl.BlockSpec(memory_space=pl.ANY),
                pl.BlockSpec((1, D_MODEL), lambda b, j, d0, d1: (0, 0)),
                pl.BlockSpec(memory_space=pl.ANY),
            ],
            out_specs=pl.BlockSpec((None, COMBINE_TM, D_MODEL), lambda b, j, d0, d1: (b, j, 0)),
            scratch_shapes=[tile(D_MODEL), tile(LANES), tile(D_MODEL), tile(D_MODEL),
                            pltpu.SemaphoreType.DMA((2, 4))],
        ),
        out_shape=jax.ShapeDtypeStruct((batch, seq, D_MODEL), F32),
        compiler_params=pltpu.CompilerParams(
            dimension_semantics=("arbitrary", "arbitrary"), vmem_limit_bytes=VMEM_LIMIT),
        name="combine",
    )(d0, d1, h1, gates, g_final, y_buf)


def _block_diag(w):
    per = LRU_CB // LRU_BLOCK_DIM
    w4 = w.reshape(LRU_BLOCKS // per, per, LRU_BLOCK_DIM, LRU_BLOCK_DIM)
    eye = jnp.eye(per, dtype=w.dtype)
    bd = jnp.einsum('cpij,pq->cpiqj', w4, eye)
    return bd.reshape(LRU_BLOCKS // per, LRU_CB, LRU_CB)


def kernel(x, meta_tokens, g_mix, w_in, conv_w, conv_b, w_rgate, b_rgate, w_igate, b_igate, lru_L, lambda_q1, lambda_k1, lambda_q2, lambda_k2, subln_g, rel_bias, w_out, g_ffn, w_group, b_group, w_router, b_router, w_gate, w_up, w_down, g_final):
    batch, seq, _ = x.shape
    t_real = N_META + seq
    t_pad = _round_up(t_real, TT)
    n_pad = batch * t_pad

    h0, proj_lru, proj_qkv = _in_proj(x, meta_tokens.astype(x.dtype), g_mix[0][None],
                                      w_in[0].astype(BF16), t_pad)

    y_lru = _lru(proj_lru, conv_w[0], conv_b[0][None],
                 _block_diag(w_rgate[0]).astype(BF16), b_rgate[0].reshape(1, D_LRU),
                 _block_diag(w_igate[0]).astype(BF16), b_igate[0].reshape(1, D_LRU),
                 lru_L[0][None], batch, t_pad)

    lam_params = jnp.stack([lambda_q1[0], lambda_k1[0], lambda_q2[0], lambda_k2[0]])
    y_att = _attention(proj_qkv, rel_bias, lam_params, subln_g[0][None], batch, t_pad)

    w_rt = jnp.concatenate([w_group[0], w_router[0],
                            jnp.zeros((D_MODEL, LANES - N_GROUPS - N_EXPERTS), F32)], axis=1)
    b_rt = jnp.concatenate([b_group[0], b_router[0],
                            jnp.zeros((LANES - N_GROUPS - N_EXPERTS,), F32)])[None]
    h1, u2, logits = _out_proj(y_lru, y_att, h0, w_out[0].astype(BF16), g_ffn[0][None], w_rt, b_rt)

    idx, gates, counts = _route(logits)

    cnt = counts[0, N_GROUPS:N_GROUPS + N_EXPERTS].astype(jnp.int32)
    padded = (cnt + DISPATCH_BLK - 1) // DISPATCH_BLK * DISPATCH_BLK
    pends = jnp.cumsum(padded)
    pstarts = pends - padded
    dest = _dest(idx, jnp.pad(pstarts.astype(F32), (0, LANES - N_EXPERTS))[None])
    d0 = dest[:, 0]
    d1 = dest[:, 1]
    n_blocks = -(-(2 * n_pad + N_EXPERTS * (DISPATCH_BLK - 1)) // DISPATCH_BLK)
    rows = n_blocks * DISPATCH_BLK
    sched = _expert_schedule(padded, pstarts, n_blocks)

    x_buf = _dispatch(d0, d1, pends.astype(jnp.int32), u2, rows)
    y_buf = _experts(sched, x_buf, w_gate[0], w_up[0], w_down[0])
    return _combine(d0, d1, h1, gates, g_final[None], y_buf, batch, seq)
```

```python
import functools
import math

import numpy as np
import jax
import jax.numpy as jnp
from jax import lax
from jax.experimental import pallas as pl
from jax.experimental.pallas import tpu as pltpu

D_MODEL = 2048
N_META = 16
D_LRU = 1024
D_ATT = 1024
LRU_BLOCKS = 16
LRU_BLOCK_DIM = 64
CONV_WIDTH = 4
LRU_C = 8.0
N_HEADS = 8
V_HEAD_DIM = 128
QK_HEAD_DIM = 64
N_BUCKETS = 32
MAX_DISTANCE = 128
N_GROUPS = 4
EXPERTS_PER_GROUP = 8
N_EXPERTS = 32
D_EXPERT = 1024
D_IN_PROJ = 2 * D_LRU + 3 * D_ATT
EPS = 1e-6
LAMBDA_INIT = 0.8 - 0.6 * math.exp(-0.3 * 0)
LOG2_E = math.log2(math.e)

F32 = jnp.float32
BF16 = jnp.bfloat16

LANES = 128
SUBLANES = 8
MXU_DIM = 256
TT = 768
PROJ_TN = 1024
OUT_TM = TT // 2
ATT_SUB = MXU_DIM
ATT_LOOKAHEAD = 4
ATT_GROUP = 4
VT_ROWS = V_HEAD_DIM + 16
LRU_CB = MXU_DIM
DISPATCH_BLK = 256
VMEM_LIMIT = 56 * 1024 * 1024
NEG_BIG = -1e30


def _round_up(a, b):
    return -(-a // b) * b


def _rmsnorm(x, g):
    ms = jnp.mean(x * x, axis=-1, keepdims=True)
    return x * lax.rsqrt(ms + EPS) * g


def _inproj_kernel(n_t, seq, x_hbm, meta_ref, g_ref, w_ref, h0_ref, lru_ref, qkv_ref, xbuf, u_scr, sem):
    m = pl.program_id(0)
    n = pl.program_id(1)
    n_m = pl.num_programs(0)
    last_rows = seq + N_META - (n_t - 1) * TT

    def tile_copy(mm, slot):
        b = mm // n_t
        t = mm % n_t
        row0 = pl.multiple_of(t * TT - N_META, SUBLANES)
        return [
            (t == 0, pltpu.make_async_copy(x_hbm.at[b, pl.ds(0, TT - N_META)],
                                           xbuf.at[slot, pl.ds(N_META, TT - N_META)], sem.at[slot])),
            (jnp.logical_and(t > 0, t < n_t - 1),
             pltpu.make_async_copy(x_hbm.at[b, pl.ds(row0, TT)], xbuf.at[slot], sem.at[slot])),
            (t == n_t - 1, pltpu.make_async_copy(x_hbm.at[b, pl.ds(row0, last_rows)],
                                                 xbuf.at[slot, pl.ds(0, last_rows)], sem.at[slot])),
        ]

    def start(mm, slot):
        for cond, copy in tile_copy(mm, slot):
            @pl.when(cond)
            def _(copy=copy):
                copy.start()

    def wait(mm, slot):
        for cond, copy in tile_copy(mm, slot):
            @pl.when(cond)
            def _(copy=copy):
                copy.wait()

    slot = m % 2

    @pl.when(jnp.logical_and(m == 0, n == 0))
    def _():
        start(m, slot)

    @pl.when(jnp.logical_and(n == 1, m + 1 < n_m))
    def _():
        start(m + 1, 1 - slot)

    @pl.when(n == 0)
    def _():
        wait(m, slot)
        t = m % n_t

        @pl.when(t == 0)
        def _():
            xbuf[slot, 0:N_META, :] = meta_ref[...]

        @pl.when(t == n_t - 1)
        def _():
            xbuf[slot, last_rows:TT, :] = jnp.zeros((TT - last_rows, D_MODEL), F32)

        h0 = xbuf[slot]
        h0_ref[...] = h0
        u_scr[...] = _rmsnorm(h0, g_ref[...]).astype(BF16)

    y = jnp.dot(u_scr[...], w_ref[...], preferred_element_type=F32)

    @pl.when(n < 2)
    def _():
        lru_ref[...] = y

    @pl.when(n >= 2)
    def _():
        qkv_ref[...] = y.astype(BF16)


def _in_proj(x, meta_tokens, g_mix, w_in_bf16, t_pad):
    batch, seq, _ = x.shape
    n_t = t_pad // TT
    n_pad = batch * t_pad
    n_col = D_IN_PROJ // PROJ_TN
    assert N_META % SUBLANES == 0 and 0 < seq + N_META - (n_t - 1) * TT <= TT
    return pl.pallas_call(
        functools.partial(_inproj_kernel, n_t, seq),
        grid=(n_pad // TT, n_col),
        in_specs=[
            pl.BlockSpec(memory_space=pl.ANY),
            pl.BlockSpec((N_META, D_MODEL), lambda m, n: (0, 0)),
            pl.BlockSpec((1, D_MODEL), lambda m, n: (0, 0)),
            pl.BlockSpec((D_MODEL, PROJ_TN), lambda m, n: (0, n)),
        ],
        out_specs=[
            pl.BlockSpec((TT, D_MODEL), lambda m, n: (m, 0)),
            pl.BlockSpec((TT, PROJ_TN), lambda m, n: (m, jnp.minimum(n, 1))),
            pl.BlockSpec((TT, PROJ_TN), lambda m, n: (m, jnp.maximum(n - 2, 0))),
        ],
        out_shape=[
            jax.ShapeDtypeStruct((n_pad, D_MODEL), F32),
            jax.ShapeDtypeStruct((n_pad, 2 * D_LRU), F32),
            jax.ShapeDtypeStruct((n_pad, 3 * D_ATT), BF16),
        ],
        scratch_shapes=[
            pltpu.VMEM((2, TT, D_MODEL), F32),
            pltpu.VMEM((TT, D_MODEL), BF16),
            pltpu.SemaphoreType.DMA((2,)),
        ],
        compiler_params=pltpu.CompilerParams(
            dimension_semantics=("arbitrary", "arbitrary"), vmem_limit_bytes=VMEM_LIMIT),
        name="in_proj",
    )(x, meta_tokens, g_mix, w_in_bf16)


def _shift_rows(x, tail, s):
    r = pltpu.roll(x, s, 0)
    tail_row = lax.broadcasted_iota(jnp.int32, tail.shape, 0)
    head = jnp.where(tail_row < s, pltpu.roll(tail, s, 0), r[0:SUBLANES])
    return jnp.concatenate([head, r[SUBLANES:]], axis=0)


def _lru_kernel(x_ref, gate_ref, cw_ref, cb_ref, wr_ref, br_ref, wi_ref, bi_ref, lam_ref,
                y_ref, tail_scr, h_scr):
    t = pl.program_id(2)

    @pl.when(t == 0)
    def _():
        tail_scr[...] = jnp.zeros_like(tail_scr)
        h_scr[...] = jnp.zeros_like(h_scr)

    x = x_ref[...]
    tail = tail_scr[...]
    cw = cw_ref[...]
    xc = cb_ref[...] + x * cw[CONV_WIDTH - 1:CONV_WIDTH]
    for k in range(CONV_WIDTH - 1):
        xc = xc + _shift_rows(x, tail, CONV_WIDTH - 1 - k) * cw[k:k + 1]
    tail_scr[...] = x[TT - SUBLANES:TT]

    xb = xc.astype(BF16)
    r = jax.nn.sigmoid(jnp.dot(xb, wr_ref[...], preferred_element_type=F32) + br_ref[...])
    i = jax.nn.sigmoid(jnp.dot(xb, wi_ref[...], preferred_element_type=F32) + bi_ref[...])
    lam = lam_ref[...]
    log_sig = jnp.minimum(lam, 0.0) - jnp.log(1.0 + jnp.exp(-jnp.abs(lam)))
    log_a = (LRU_C * r) * log_sig
    a = jnp.exp(log_a)
    b = jnp.sqrt(1.0 - a * a) * (i * xc)

    row = lax.broadcasted_iota(jnp.int32, a.shape, 0) % SUBLANES
    s = 1
    while s < SUBLANES:
        keep = row >= s
        a_sh = jnp.where(keep, pltpu.roll(a, s, 0), 1.0)
        b_sh = jnp.where(keep, pltpu.roll(b, s, 0), 0.0)
        b = a * b_sh + b
        a = a * a_sh
        s *= 2
    carry = h_scr[...]
    groups = []
    for g in range(TT // SUBLANES):
        rows = slice(g * SUBLANES, (g + 1) * SUBLANES)
        hg = b[rows] + a[rows] * carry
        groups.append(hg)
        carry = jnp.broadcast_to(hg[SUBLANES - 1:SUBLANES], hg.shape)
    h_scr[...] = carry
    h = jnp.concatenate(groups, axis=0)
    y_ref[...] = (h * jax.nn.gelu(gate_ref[...])).astype(BF16)


def _lru(proj_lru, conv_w, conv_b, wr_bd, b_r, wi_bd, b_i, lru_l, batch, t_pad):
    n_pad = proj_lru.shape[0]
    n_t = t_pad // TT
    n_c = D_LRU // LRU_CB
    vec = lambda: pl.BlockSpec((1, LRU_CB), lambda b, c, t: (0, c))
    return pl.pallas_call(
        _lru_kernel,
        grid=(batch, n_c, n_t),
        in_specs=[
            pl.BlockSpec((TT, LRU_CB), lambda b, c, t: (b * n_t + t, c)),
            pl.BlockSpec((TT, LRU_CB), lambda b, c, t: (b * n_t + t, n_c + c)),
            pl.BlockSpec((CONV_WIDTH, LRU_CB), lambda b, c, t: (0, c)),
            vec(),
            pl.BlockSpec((None, LRU_CB, LRU_CB), lambda b, c, t: (c, 0, 0)),
            vec(),
            pl.BlockSpec((None, LRU_CB, LRU_CB), lambda b, c, t: (c, 0, 0)),
            vec(),
            vec(),
        ],
        out_specs=pl.BlockSpec((TT, LRU_CB), lambda b, c, t: (b * n_t + t, c)),
        out_shape=jax.ShapeDtypeStruct((n_pad, D_LRU), BF16),
        scratch_shapes=[pltpu.VMEM((SUBLANES, LRU_CB), F32), pltpu.VMEM((SUBLANES, LRU_CB), F32)],
        compiler_params=pltpu.CompilerParams(
            dimension_semantics=("arbitrary", "arbitrary", "arbitrary"),
            vmem_limit_bytes=VMEM_LIMIT),
        name="rglru",
    )(proj_lru, proj_lru, conv_w, conv_b, wr_bd, b_r, wi_bd, b_i, lru_l)


def _bucket_thresholds():
    max_exact = N_BUCKETS // 2
    n = np.arange(0, MAX_DISTANCE + 1)
    nf = np.maximum(n, 1).astype(np.float32)
    large = max_exact + (np.log(nf / np.float32(max_exact)) / np.float32(math.log(MAX_DISTANCE / max_exact))
                         * np.float32(N_BUCKETS - max_exact)).astype(np.int32)
    large = np.minimum(large, N_BUCKETS - 1)
    bucket = np.where(n < max_exact, n, large)
    return [int(np.argmax(bucket >= k)) for k in range(1, N_BUCKETS)]


_BUCKET_THR = _bucket_thresholds()


def _attn_kernel(rb_ref, q_ref, k_ref, v_ref, lam_ref, g_ref, o_ref, bias_scr, vt_scr, m_scr, acc_scr):
    h = pl.program_id(0)
    b = pl.program_id(1)
    qi = pl.program_id(2)
    tk = TT
    n_kv = vt_scr.shape[0]

    @pl.when(jnp.logical_and(b == 0, qi == 0))
    def _():
        k_pos = lax.broadcasted_iota(jnp.int32, (tk, tk), 0)
        q_pos = lax.broadcasted_iota(jnp.int32, (tk, tk), 1)
        far = rb_ref[N_BUCKETS - 1, h]
        for d in range(2):
            rel = q_pos - k_pos + d * tk
            val = jnp.full((tk, tk), rb_ref[0, h] - far, F32)
            for kk, thr in enumerate(_BUCKET_THR):
                val = jnp.where(rel >= thr, rb_ref[kk + 1, h] - far, val)
            val = val * LOG2_E
            if d == 0:
                val = jnp.where(rel >= 0, val, -jnp.inf)
            bias_scr[d] = val

    @pl.when(qi == 0)
    def _():
        for t in range(n_kv):
            vt_scr[t, 0:V_HEAD_DIM, :] = v_ref[t * tk:(t + 1) * tk, :].astype(F32).T.astype(BF16)
            vt_scr[t, V_HEAD_DIM:VT_ROWS, :] = jnp.ones((VT_ROWS - V_HEAD_DIM, tk), BF16)

    q = q_ref[...]
    lane = lax.broadcasted_iota(jnp.int32, q.shape, 1)
    scale = (QK_HEAD_DIM ** -0.5) * LOG2_E
    qf = q.astype(F32) * scale
    qs = jnp.concatenate([jnp.where(lane < QK_HEAD_DIM, qf, 0.0),
                          jnp.where(lane >= QK_HEAD_DIM, qf, 0.0)], axis=0).astype(BF16)

    m_scr[...] = jnp.full(m_scr.shape, NEG_BIG, F32)
    acc_scr[...] = jnp.zeros_like(acc_scr)

    def tile_group(tiles):
        n_sub = 2 * tk // ATT_SUB
        stages = [(t, c) for t in range(len(tiles)) for c in range(n_sub)]

        def n_keys(t, c):
            return (c * ATT_SUB) % tk + ATT_SUB if tiles[t][1] == 0 else tk

        def scores(t, c):
            j, bias_idx = tiles[t]
            nk = n_keys(t, c)
            kt = k_ref[pl.ds(pl.multiple_of(j * tk, tk), nk), :]
            cols = slice(c * ATT_SUB, (c + 1) * ATT_SUB)
            s = lax.dot_general(kt, qs[cols], (((1,), (1,)), ((), ())), preferred_element_type=F32)
            if bias_idx is not None:
                q0 = (c * ATT_SUB) % tk
                s = s + bias_scr[bias_idx, 0:nk, q0:q0 + ATT_SUB]
            return s

        compact = all(bias_idx is None for _, bias_idx in tiles)

        def produce(t, c):
            s = scores(t, c)
            if not compact:
                return s
            cols = slice(c * ATT_SUB, (c + 1) * ATT_SUB)
            m_prev = m_scr[:, cols]
            m_new = jnp.maximum(m_prev, jnp.max(s, axis=0, keepdims=True))
            m_scr[:, cols] = m_new
            return (s - m_prev).astype(BF16), m_prev, m_new

        pending = [produce(*st) for st in stages[:ATT_LOOKAHEAD]]
        for n, (t, c) in enumerate(stages):
            cols = slice(c * ATT_SUB, (c + 1) * ATT_SUB)
            item = pending.pop(0)
            if n + ATT_LOOKAHEAD < len(stages):
                pending.append(produce(*stages[n + ATT_LOOKAHEAD]))
            if compact:
                d, m_prev, m_new = item
                p = jnp.exp2(d - (m_new - m_prev).astype(BF16))
            else:
                m_prev = m_scr[:, cols]
                m_new = jnp.maximum(m_prev, jnp.max(item, axis=0, keepdims=True))
                m_scr[:, cols] = m_new
                p = jnp.exp2(item - m_new).astype(BF16)
            alpha = jnp.exp2(m_prev - m_new)
            vt = vt_scr[tiles[t][0], :, 0:n_keys(t, c)]
            acc_scr[:, cols] = alpha * acc_scr[:, cols] + jnp.dot(vt, p, preferred_element_type=F32)

    @pl.when(qi >= 1)
    def _():
        tile_group([(qi, 0), (qi - 1, 1)])

    @pl.when(qi == 0)
    def _():
        tile_group([(qi, 0)])

    n_far = jnp.maximum(qi - 1, 0)

    def far_group(i, carry):
        tile_group([(ATT_GROUP * i + t, None) for t in range(ATT_GROUP)])
        return carry

    lax.fori_loop(0, n_far // ATT_GROUP, far_group, 0)
    done = n_far - n_far % ATT_GROUP
    piece = ATT_GROUP // 2
    while piece >= 1:
        @pl.when((n_far % (2 * piece)) >= piece)
        def _(done=done, piece=piece):
            tile_group([(done + t, None) for t in range(piece)])
        done = done + jnp.where((n_far % (2 * piece)) >= piece, piece, 0)
        piece //= 2

    lam_p = lam_ref[...]
    lam = (jnp.exp(jnp.sum(lam_p[0:1] * lam_p[1:2], axis=-1, keepdims=True))
           - jnp.exp(jnp.sum(lam_p[2:3] * lam_p[3:4], axis=-1, keepdims=True)) + LAMBDA_INIT)
    acc = acc_scr[...]
    num = acc[0:V_HEAD_DIM]
    den = acc[V_HEAD_DIM:V_HEAD_DIM + 1]
    o_t = num[:, :tk] / den[:, :tk] - lam * (num[:, tk:] / den[:, tk:])
    o_ref[...] = (_rmsnorm(o_t.T, g_ref[...]) * (1.0 - LAMBDA_INIT)).astype(BF16)


def _attention(qkv, rel_bias, lam_params, subln_g, batch, t_pad):
    n_q = t_pad // TT
    qkv3 = qkv.reshape(batch, t_pad, 3 * D_ATT)
    out = pl.pallas_call(
        _attn_kernel,
        grid_spec=pltpu.PrefetchScalarGridSpec(
            num_scalar_prefetch=0,
            grid=(N_HEADS, batch, n_q),
            in_specs=[
                pl.BlockSpec(memory_space=pltpu.SMEM),
                pl.BlockSpec((None, TT, V_HEAD_DIM), lambda h, b, q: (b, q, h)),
                pl.BlockSpec((None, t_pad, V_HEAD_DIM), lambda h, b, q: (b, 0, N_HEADS + h)),
                pl.BlockSpec((None, t_pad, V_HEAD_DIM), lambda h, b, q: (b, 0, 2 * N_HEADS + h)),
                pl.BlockSpec((4, QK_HEAD_DIM), lambda h, b, q: (0, 0)),
                pl.BlockSpec((1, V_HEAD_DIM), lambda h, b, q: (0, 0)),
            ],
            out_specs=pl.BlockSpec((None, TT, V_HEAD_DIM), lambda h, b, q: (b, q, h)),
            scratch_shapes=[
                pltpu.VMEM((2, TT, TT), F32),
                pltpu.VMEM((n_q, VT_ROWS, TT), BF16),
                pltpu.VMEM((1, 2 * TT), F32),
                pltpu.VMEM((VT_ROWS, 2 * TT), F32),
            ],
        ),
        out_shape=jax.ShapeDtypeStruct((batch, t_pad, D_ATT), BF16),
        compiler_params=pltpu.CompilerParams(
            dimension_semantics=("arbitrary", "arbitrary", "arbitrary"),
            vmem_limit_bytes=VMEM_LIMIT),
        name="diff_attention",
    )(rel_bias, qkv3, qkv3, qkv3, lam_params, subln_g)
    return out.reshape(batch * t_pad, D_ATT)


def _outproj_kernel(yl_ref, ya_ref, h0_ref, w_ref, g_ref, wrt_ref, brt_ref, h1_ref, u_ref, lg_ref):
    acc = jnp.dot(yl_ref[...], w_ref[0:D_LRU, :], preferred_element_type=F32)
    acc = acc + jnp.dot(ya_ref[...], w_ref[D_LRU:, :], preferred_element_type=F32)
    h1 = h0_ref[...] + acc
    h1_ref[...] = h1
    u = _rmsnorm(h1, g_ref[...])
    u_ref[...] = u
    u_hi = u.astype(BF16)
    u_lo = (u - u_hi.astype(F32)).astype(BF16)
    w = wrt_ref[...]
    w_hi = w.astype(BF16)
    w_lo = (w - w_hi.astype(F32)).astype(BF16)
    parts = jnp.dot(jnp.concatenate([u_hi, u_lo], axis=0), jnp.concatenate([w_hi, w_lo], axis=1),
                    preferred_element_type=F32)
    lg = (parts[:OUT_TM, :LANES] + parts[:OUT_TM, LANES:]) + (parts[OUT_TM:, :LANES] + parts[OUT_TM:, LANES:])
    lg_ref[...] = lg + brt_ref[...]


def _out_proj(y_lru, y_att, h0, w_out_bf16, g_ffn, w_rt, b_rt):
    n_pad = h0.shape[0]
    row = lambda w: pl.BlockSpec((OUT_TM, w), lambda m: (m, 0))
    full = lambda a, b: pl.BlockSpec((a, b), lambda m: (0, 0))
    return pl.pallas_call(
        _outproj_kernel,
        grid=(n_pad // OUT_TM,),
        in_specs=[row(D_LRU), row(D_ATT), row(D_MODEL), full(D_MODEL, D_MODEL), full(1, D_MODEL),
                  full(D_MODEL, LANES), full(1, LANES)],
        out_specs=[row(D_MODEL), row(D_MODEL), row(LANES)],
        out_shape=[
            jax.ShapeDtypeStruct((n_pad, D_MODEL), F32),
            jax.ShapeDtypeStruct((n_pad, D_MODEL), F32),
            jax.ShapeDtypeStruct((n_pad, LANES), F32),
        ],
        compiler_params=pltpu.CompilerParams(
            dimension_semantics=("arbitrary",), vmem_limit_bytes=VMEM_LIMIT),
        name="out_proj",
    )(y_lru, y_att, h0, w_out_bf16, g_ffn, w_rt, b_rt)


def _route_kernel(lg_ref, idx_ref, gate_ref, cnt_ref, carry_scr):
    step = pl.program_id(0)

    @pl.when(step == 0)
    def _():
        carry_scr[...] = jnp.zeros_like(carry_scr)

    lg = lg_ref[...]
    lane = lax.broadcasted_iota(jnp.int32, lg.shape, 1)
    first = lambda mask: jnp.min(jnp.where(mask, lane, LANES), axis=-1, keepdims=True)

    is_g = lane < N_GROUPS
    gl = jnp.where(is_g, lg, -jnp.inf)
    gmax = jnp.max(gl, axis=-1, keepdims=True)
    grp = first(gl == gmax)
    gsum = jnp.sum(jnp.where(is_g, jnp.exp(gl - gmax), 0.0), axis=-1, keepdims=True)
    p_grp = 1.0 / gsum

    lane_e = lane - N_GROUPS
    in_grp = (lane_e >= 0) & (lane_e < N_EXPERTS) & ((lane_e // EXPERTS_PER_GROUP) == grp)
    el = jnp.where(in_grp, lg, -jnp.inf)
    v1 = jnp.max(el, axis=-1, keepdims=True)
    i1 = first(el == v1)
    el2 = jnp.where(lane == i1, -jnp.inf, el)
    v2 = jnp.max(el2, axis=-1, keepdims=True)
    i2 = first(el2 == v2)
    e2 = jnp.exp(v2 - v1)
    den = 1.0 + e2
    g0 = p_grp * (1.0 / den)
    g1 = p_grp * (e2 / den)

    hit1 = lane == i1
    hit2 = lane == i2
    onehot = jnp.where(hit1 | hit2, 1.0, 0.0)
    rr = lax.broadcasted_iota(jnp.int32, (TT, TT), 0)
    cc = lax.broadcasted_iota(jnp.int32, (TT, TT), 1)
    tri = jnp.where(cc < rr, 1.0, 0.0).astype(BF16)
    rank = jnp.dot(tri, onehot.astype(BF16), preferred_element_type=F32) + carry_scr[...]
    r0 = jnp.sum(jnp.where(hit1, rank, 0.0), axis=-1, keepdims=True).astype(jnp.int32)
    r1 = jnp.sum(jnp.where(hit2, rank, 0.0), axis=-1, keepdims=True).astype(jnp.int32)
    carry_scr[...] = carry_scr[...] + jnp.sum(onehot, axis=0, keepdims=True)

    idx_ref[...] = jnp.where(lane == 0, i1 - N_GROUPS,
                             jnp.where(lane == 1, i2 - N_GROUPS,
                                       jnp.where(lane == 2, r0, jnp.where(lane == 3, r1, 0))))
    gate_ref[...] = jnp.where(lane == 0, g0, jnp.where(lane == 1, g1, 0.0))
    cnt_ref[...] = carry_scr[...]


def _route(logits):
    n_pad = logits.shape[0]
    row = pl.BlockSpec((TT, LANES), lambda m: (m, 0))
    return pl.pallas_call(
        _route_kernel,
        grid=(n_pad // TT,),
        in_specs=[row],
        out_specs=[row, row, pl.BlockSpec((1, LANES), lambda m: (0, 0))],
        out_shape=[
            jax.ShapeDtypeStruct((n_pad, LANES), jnp.int32),
            jax.ShapeDtypeStruct((n_pad, LANES), F32),
            jax.ShapeDtypeStruct((1, LANES), F32),
        ],
        scratch_shapes=[pltpu.VMEM((1, LANES), F32)],
        compiler_params=pltpu.CompilerParams(
            dimension_semantics=("arbitrary",), vmem_limit_bytes=VMEM_LIMIT),
        name="route",
    )(logits)


def _dest_kernel(idx_ref, ps_ref, o_ref):
    idx = idx_ref[...]
    lane = lax.broadcasted_iota(jnp.int32, idx.shape, 1)
    ps = ps_ref[...]
    pick = lambda e: jnp.sum(jnp.where(lane == e, ps, 0.0), axis=-1, keepdims=True).astype(jnp.int32)
    d0 = pick(idx[:, 0:1]) + idx[:, 2:3]
    d1 = pick(idx[:, 1:2]) + idx[:, 3:4]
    o_ref[...] = jnp.where(lane == 0, d0, jnp.where(lane == 1, d1, 0))


def _dest(idx, pstart_row):
    n_pad = idx.shape[0]
    row = pl.BlockSpec((TT, LANES), lambda m: (m, 0))
    return pl.pallas_call(
        _dest_kernel,
        grid=(n_pad // TT,),
        in_specs=[row, pl.BlockSpec((1, LANES), lambda m: (0, 0))],
        out_specs=row,
        out_shape=jax.ShapeDtypeStruct((n_pad, LANES), jnp.int32),
        compiler_params=pltpu.CompilerParams(
            dimension_semantics=("arbitrary",), vmem_limit_bytes=VMEM_LIMIT),
        name="dest",
    )(idx, pstart_row)


ROW_UNROLL = 8


def _row_copy(src, s_row, dst, d_row, sem):
    return pltpu.make_async_copy(src.at[pl.ds(s_row, 1)], dst.at[pl.ds(d_row, 1)], sem)


def _dispatch_kernel(d0_ref, d1_ref, pe_ref, u_ref, xb_ref, zero_scr, sem, zsem):
    step = pl.program_id(0)
    base = step * TT

    def zero_block(e):
        start = pl.multiple_of(pe_ref[e] - DISPATCH_BLK, DISPATCH_BLK)
        return pltpu.make_async_copy(zero_scr, xb_ref.at[pl.ds(start, DISPATCH_BLK)], zsem)

    @pl.when(step == 0)
    def _():
        zero_scr[...] = jnp.zeros_like(zero_scr)

        def nonempty(e):
            return pe_ref[e] > jnp.where(e == 0, 0, pe_ref[jnp.maximum(e - 1, 0)])

        def zstart(e, carry):
            @pl.when(nonempty(e))
            def _():
                zero_block(e).start()
            return carry

        def zwait(e, carry):
            @pl.when(nonempty(e))
            def _():
                zero_block(e).wait()
            return carry

        lax.fori_loop(0, N_EXPERTS, zstart, 0)
        lax.fori_loop(0, N_EXPERTS, zwait, 0)

        def tail_block(i):
            start = pl.multiple_of(i * DISPATCH_BLK, DISPATCH_BLK)
            return pltpu.make_async_copy(zero_scr, xb_ref.at[pl.ds(start, DISPATCH_BLK)], zsem)

        def tstart(i, carry):
            tail_block(i).start()
            return carry

        def twait(i, carry):
            tail_block(i).wait()
            return carry

        n_used = pe_ref[N_EXPERTS - 1] // DISPATCH_BLK
        n_blocks = xb_ref.shape[0] // DISPATCH_BLK
        lax.fori_loop(n_used, n_blocks, tstart, 0)
        lax.fori_loop(n_used, n_blocks, twait, 0)

    def issue(r, carry):
        _row_copy(u_ref, r, xb_ref, d0_ref[base + r], sem.at[0]).start()
        _row_copy(u_ref, r, xb_ref, d1_ref[base + r], sem.at[1]).start()
        return carry

    lax.fori_loop(0, TT, issue, 0, unroll=ROW_UNROLL)
    for k in range(2):
        pltpu.make_async_copy(u_ref, xb_ref.at[pl.ds(0, TT)], sem.at[k]).wait()


def _dispatch(d0, d1, pends, u2, rows):
    n_pad = u2.shape[0]
    return pl.pallas_call(
        _dispatch_kernel,
        grid_spec=pltpu.PrefetchScalarGridSpec(
            num_scalar_prefetch=3,
            grid=(n_pad // TT,),
            in_specs=[pl.BlockSpec((TT, D_MODEL), lambda m, d0, d1, pe: (m, 0))],
            out_specs=pl.BlockSpec(memory_space=pl.ANY),
            scratch_shapes=[
                pltpu.VMEM((DISPATCH_BLK, D_MODEL), F32),
                pltpu.SemaphoreType.DMA((2,)),
                pltpu.SemaphoreType.DMA(()),
            ],
        ),
        out_shape=jax.ShapeDtypeStruct((rows, D_MODEL), F32),
        compiler_params=pltpu.CompilerParams(
            dimension_semantics=("arbitrary",), vmem_limit_bytes=VMEM_LIMIT),
        name="dispatch",
    )(d0, d1, pends, u2)


WEIGHT_UNITS = 8
WEIGHT_DEPTH = 4


def _stream_weights(i, tgt_ref, eseq_ref, ntot_ref, w_hbm, stages, wbfs, sem, done_scr):
    def slab(u, k):
        rows = stages[k].shape[1]
        return pl.ds(pl.multiple_of((u % WEIGHT_UNITS) * rows, rows), rows)

    def unit_copies(u):
        e = eseq_ref[u // WEIGHT_UNITS]
        b = u % WEIGHT_DEPTH
        return [pltpu.make_async_copy(w.at[e, slab(u, k), :], stages[k].at[b], sem.at[b, k])
                for k, w in enumerate(w_hbm)]

    @pl.when(i == 0)
    def _():
        done_scr[0] = 0
        for u in range(WEIGHT_DEPTH):
            for c in unit_copies(u):
                c.start()

    def body(u, carry):
        for c in unit_copies(u):
            c.wait()
        slot = (u // WEIGHT_UNITS) % 2
        b = u % WEIGHT_DEPTH
        for k in range(len(w_hbm)):
            wbfs[k][slot, slab(u, k), :] = stages[k][b].astype(BF16)

        @pl.when(u + WEIGHT_DEPTH < ntot_ref[0])
        def _():
            for c in unit_copies(u + WEIGHT_DEPTH):
                c.start()

        return carry

    lax.fori_loop(done_scr[0], tgt_ref[i], body, 0)
    done_scr[0] = tgt_ref[i]


def _expert_kernel(tgt_ref, eseq_ref, ntot_ref, slot_ref, nu_ref, x_ref, wg_hbm, wu_hbm, wd_hbm, y_ref,
                   sg, su, sd, bg, bu, bd, sem, done_scr):
    i = pl.program_id(0)
    _stream_weights(i, tgt_ref, eseq_ref, ntot_ref, (wg_hbm, wu_hbm, wd_hbm), (sg, su, sd), (bg, bu, bd),
                    sem, done_scr)
    used = i < nu_ref[0]

    @pl.when(used)
    def _():
        slot = slot_ref[i]
        x = x_ref[...].astype(BF16)
        g = jnp.dot(x, bg[slot], preferred_element_type=F32)
        u = jnp.dot(x, bu[slot], preferred_element_type=F32)
        a = (jax.nn.silu(g) * u).astype(BF16)
        y_ref[...] = jnp.dot(a, bd[slot], preferred_element_type=F32)

    @pl.when(jnp.logical_not(used))
    def _():
        y_ref[...] = jnp.zeros_like(y_ref)


def _experts(sched, x_buf, w_gate, w_up, w_down):
    rows = x_buf.shape[0]
    n_blocks = rows // DISPATCH_BLK
    blk_map = lambda i, tg, es, nt, sl, nu: (jnp.minimum(i, jnp.maximum(nu[0] - 1, 0)), 0)
    out_map = lambda i, tg, es, nt, sl, nu: (i, 0)
    hbm = pl.BlockSpec(memory_space=pl.ANY)
    stage = lambda k_dim, n_dim: pltpu.VMEM((WEIGHT_DEPTH, k_dim // WEIGHT_UNITS, n_dim), F32)
    resident = lambda k_dim, n_dim: pltpu.VMEM((2, k_dim, n_dim), BF16)
    return pl.pallas_call(
        _expert_kernel,
        grid_spec=pltpu.PrefetchScalarGridSpec(
            num_scalar_prefetch=5,
            grid=(n_blocks,),
            in_specs=[pl.BlockSpec((DISPATCH_BLK, D_MODEL), blk_map), hbm, hbm, hbm],
            out_specs=pl.BlockSpec((DISPATCH_BLK, D_MODEL), out_map),
            scratch_shapes=[
                stage(D_MODEL, D_EXPERT), stage(D_MODEL, D_EXPERT), stage(D_EXPERT, D_MODEL),
                resident(D_MODEL, D_EXPERT), resident(D_MODEL, D_EXPERT), resident(D_EXPERT, D_MODEL),
                pltpu.SemaphoreType.DMA((WEIGHT_DEPTH, 3)),
                pltpu.SMEM((1,), jnp.int32),
            ],
        ),
        out_shape=jax.ShapeDtypeStruct((rows, D_MODEL), F32),
        compiler_params=pltpu.CompilerParams(
            dimension_semantics=("arbitrary",), vmem_limit_bytes=VMEM_LIMIT),
        name="experts",
    )(*sched, x_buf, w_gate, w_up, w_down)


def _expert_schedule(padded, pstarts, n_blocks):
    nonempty = padded > 0
    seq_of_expert = jnp.cumsum(nonempty.astype(jnp.int32)) - 1
    n_seq = jnp.sum(nonempty.astype(jnp.int32))
    experts = jnp.arange(N_EXPERTS, dtype=jnp.int32)
    is_kth = jnp.logical_and(nonempty[None, :], seq_of_expert[None, :] == experts[:, None])
    eseq = jnp.sum(jnp.where(is_kth, experts[None, :], 0), axis=1)
    pends = pstarts + padded
    n_used = pends[-1] // DISPATCH_BLK
    blk = jnp.arange(n_blocks, dtype=jnp.int32)
    blk_c = jnp.minimum(blk, jnp.maximum(n_used - 1, 0))
    blk_expert = jnp.minimum(
        jnp.sum((pends[None, :] <= (blk_c * DISPATCH_BLK)[:, None]).astype(jnp.int32), axis=1), N_EXPERTS - 1)
    onehot = (blk_expert[:, None] == experts[None, :]).astype(jnp.int32)
    pick = lambda v: jnp.sum(onehot * v[None, :], axis=1)
    q = pick(seq_of_expert)
    j = blk_c - pick(pstarts) // DISPATCH_BLK
    n = jnp.maximum(pick(padded) // DISPATCH_BLK, 1)
    n_total = n_seq * WEIGHT_UNITS
    tgt = jnp.minimum(WEIGHT_UNITS * (q + 1) + (WEIGHT_UNITS * j) // n, n_total)
    tgt = jnp.where(blk < n_used, tgt, n_total)
    i32 = lambda a: a.astype(jnp.int32)
    return i32(tgt), eseq, i32(n_total)[None], i32(q % 2), i32(n_used)[None]


COMBINE_TM = 512


def _combine_kernel(t_pad, d0_ref, d1_ref, h_hbm, gate_hbm, g_ref, yb_ref, o_ref,
                    h_scr, gate_scr, y0_scr, y1_scr, sem):
    n_j = pl.num_programs(1)
    step = pl.program_id(0) * n_j + pl.program_id(1)
    n_steps = pl.num_programs(0) * n_j

    def base_row(s):
        return pl.multiple_of((s // n_j) * t_pad + N_META + (s % n_j) * COMBINE_TM, 8)

    def tile_copies(s, slot):
        base = base_row(s)
        return [
            pltpu.make_async_copy(h_hbm.at[pl.ds(base, COMBINE_TM)], h_scr.at[slot], sem.at[slot, 2]),
            pltpu.make_async_copy(gate_hbm.at[pl.ds(base, COMBINE_TM)], gate_scr.at[slot], sem.at[slot, 3]),
        ]

    def fetch(s, slot):
        base = base_row(s)
        for c in tile_copies(s, slot):
            c.start()

        def issue(r, carry):
            _row_copy(yb_ref, d0_ref[base + r], y0_scr.at[slot], r, sem.at[slot, 0]).start()
            _row_copy(yb_ref, d1_ref[base + r], y1_scr.at[slot], r, sem.at[slot, 1]).start()
            return carry

        lax.fori_loop(0, COMBINE_TM, issue, 0, unroll=ROW_UNROLL)

    slot = step % 2

    @pl.when(step == 0)
    def _():
        fetch(step, slot)

    @pl.when(step + 1 < n_steps)
    def _():
        fetch(step + 1, 1 - slot)

    pltpu.make_async_copy(yb_ref.at[pl.ds(0, COMBINE_TM)], y0_scr.at[slot], sem.at[slot, 0]).wait()
    pltpu.make_async_copy(yb_ref.at[pl.ds(0, COMBINE_TM)], y1_scr.at[slot], sem.at[slot, 1]).wait()
    for c in tile_copies(step, slot):
        c.wait()

    gate = gate_scr[slot]
    h = h_scr[slot] + gate[:, 0:1] * y0_scr[slot] + gate[:, 1:2] * y1_scr[slot]
    o_ref[...] = _rmsnorm(h, g_ref[...])


def _combine(d0, d1, h1, gates, g_final, y_buf, batch, seq):
    assert seq % COMBINE_TM == 0 and N_META % 8 == 0
    tile = lambda w: pltpu.VMEM((2, COMBINE_TM, w), F32)
    return pl.pallas_call(
        functools.partial(_combine_kernel, h1.shape[0] // batch),
        grid_spec=pltpu.PrefetchScalarGridSpec(
            num_scalar_prefetch=2,
            grid=(batch, seq // COMBINE_TM),
            in_specs=[
                pl.BlockSpec(memory_space=pl.ANY),
                pl.BlockSpec(memory_space=pl.ANY),
                pl.BlockSpec((1, D_MODEL), lambda b, j, d0, d1: (0, 0)),
                pl.BlockSpec(memory_space=pl.ANY),
            ],
            out_specs=pl.BlockSpec((None, COMBINE_TM, D_MODEL), lambda b, j, d0, d1: (b, j, 0)),
            scratch_shapes=[tile(D_MODEL), tile(LANES), tile(D_MODEL), tile(D_MODEL),
                            pltpu.SemaphoreType.DMA((2, 4))],
        ),
        out_shape=jax.ShapeDtypeStruct((batch, seq, D_MODEL), F32),
        compiler_params=pltpu.CompilerParams(
            dimension_semantics=("arbitrary", "arbitrary"), vmem_limit_bytes=VMEM_LIMIT),
        name="combine",
    )(d0, d1, h1, gates, g_final, y_buf)


def _block_diag(w):
    per = LRU_CB // LRU_BLOCK_DIM
    w4 = w.reshape(LRU_BLOCKS // per, per, LRU_BLOCK_DIM, LRU_BLOCK_DIM)
    eye = jnp.eye(per, dtype=w.dtype)
    bd = jnp.einsum('cpij,pq->cpiqj', w4, eye)
    return bd.reshape(LRU_BLOCKS // per, LRU_CB, LRU_CB)


def kernel(x, meta_tokens, g_mix, w_in, conv_w, conv_b, w_rgate, b_rgate, w_igate, b_igate, lru_L, lambda_q1, lambda_k1, lambda_q2, lambda_k2, subln_g, rel_bias, w_out, g_ffn, w_group, b_group, w_router, b_router, w_gate, w_up, w_down, g_final):
    batch, seq, _ = x.shape
    t_real = N_META + seq
    t_pad = _round_up(t_real, TT)
    n_pad = batch * t_pad

    h0, proj_lru, proj_qkv = _in_proj(x, meta_tokens.astype(x.dtype), g_mix[0][None],
                                      w_in[0].astype(BF16), t_pad)

    y_lru = _lru(proj_lru, conv_w[0], conv_b[0][None],
                 _block_diag(w_rgate[0]).astype(BF16), b_rgate[0].reshape(1, D_LRU),
                 _block_diag(w_igate[0]).astype(BF16), b_igate[0].reshape(1, D_LRU),
                 lru_L[0][None], batch, t_pad)

    lam_params = jnp.stack([lambda_q1[0], lambda_k1[0], lambda_q2[0], lambda_k2[0]])
    y_att = _attention(proj_qkv, rel_bias, lam_params, subln_g[0][None], batch, t_pad)

    w_rt = jnp.concatenate([w_group[0], w_router[0],
                            jnp.zeros((D_MODEL, LANES - N_GROUPS - N_EXPERTS), F32)], axis=1)
    b_rt = jnp.concatenate([b_group[0], b_router[0],
                            jnp.zeros((LANES - N_GROUPS - N_EXPERTS,), F32)])[None]
    h1, u2, logits = _out_proj(y_lru, y_att, h0, w_out[0].astype(BF16), g_ffn[0][None], w_rt, b_rt)

    idx, gates, counts = _route(logits)

    cnt = counts[0, N_GROUPS:N_GROUPS + N_EXPERTS].astype(jnp.int32)
    padded = (cnt + DISPATCH_BLK - 1) // DISPATCH_BLK * DISPATCH_BLK
    pends = jnp.cumsum(padded)
    pstarts = pends - padded
    dest = _dest(idx, jnp.pad(pstarts.astype(F32), (0, LANES - N_EXPERTS))[None])
    d0 = dest[:, 0]
    d1 = dest[:, 1]
    n_blocks = -(-(2 * n_pad + N_EXPERTS * (DISPATCH_BLK - 1)) // DISPATCH_BLK)
    rows = n_blocks * DISPATCH_BLK
    sched = _expert_schedule(padded, pstarts, n_blocks)

    x_buf = _dispatch(d0, d1, pends.astype(jnp.int32), u2, rows)
    y_buf = _experts(sched, x_buf, w_gate[0], w_up[0], w_down[0])
    return _combine(d0, d1, h1, gates, g_final[None], y_buf, batch, seq)
```

```python
import functools
import math

import numpy as np
import jax
import jax.numpy as jnp
from jax import lax
from jax.experimental import pallas as pl
from jax.experimental.pallas import tpu as pltpu

D_MODEL = 2048
N_META = 16
D_LRU = 1024
D_ATT = 1024
LRU_BLOCKS = 16
LRU_BLOCK_DIM = 64
CONV_WIDTH = 4
LRU_C = 8.0
N_HEADS = 8
V_HEAD_DIM = 128
QK_HEAD_DIM = 64
N_BUCKETS = 32
MAX_DISTANCE = 128
N_GROUPS = 4
EXPERTS_PER_GROUP = 8
N_EXPERTS = 32
D_EXPERT = 1024
D_IN_PROJ = 2 * D_LRU + 3 * D_ATT
EPS = 1e-6
LAMBDA_INIT = 0.8 - 0.6 * math.exp(-0.3 * 0)
LOG2_E = math.log2(math.e)

F32 = jnp.float32
BF16 = jnp.bfloat16

LANES = 128
SUBLANES = 8
MXU_DIM = 256
TT = 768
PROJ_TN = 1024
OUT_TM = TT // 2
ATT_SUB = MXU_DIM
ATT_LOOKAHEAD = 4
ATT_GROUP = 4
VT_ROWS = V_HEAD_DIM + 16
LRU_CB = MXU_DIM
DISPATCH_BLK = 256
VMEM_LIMIT = 56 * 1024 * 1024
NEG_BIG = -1e30


def _round_up(a, b):
    return -(-a // b) * b


def _rmsnorm(x, g):
    ms = jnp.mean(x * x, axis=-1, keepdims=True)
    return x * lax.rsqrt(ms + EPS) * g


def _inproj_kernel(n_t, seq, x_hbm, meta_ref, g_ref, w_ref, h0_ref, lru_ref, qkv_ref, xbuf, u_scr, sem):
    m = pl.program_id(0)
    n = pl.program_id(1)
    n_m = pl.num_programs(0)
    last_rows = seq + N_META - (n_t - 1) * TT

    def tile_copy(mm, slot):
        b = mm // n_t
        t = mm % n_t
        row0 = pl.multiple_of(t * TT - N_META, SUBLANES)
        return [
            (t == 0, pltpu.make_async_copy(x_hbm.at[b, pl.ds(0, TT - N_META)],
                                           xbuf.at[slot, pl.ds(N_META, TT - N_META)], sem.at[slot])),
            (jnp.logical_and(t > 0, t < n_t - 1),
             pltpu.make_async_copy(x_hbm.at[b, pl.ds(row0, TT)], xbuf.at[slot], sem.at[slot])),
            (t == n_t - 1, pltpu.make_async_copy(x_hbm.at[b, pl.ds(row0, last_rows)],
                                                 xbuf.at[slot, pl.ds(0, last_rows)], sem.at[slot])),
        ]

    def start(mm, slot):
        for cond, copy in tile_copy(mm, slot):
            @pl.when(cond)
            def _(copy=copy):
                copy.start()

    def wait(mm, slot):
        for cond, copy in tile_copy(mm, slot):
            @pl.when(cond)
            def _(copy=copy):
                copy.wait()

    slot = m % 2

    @pl.when(jnp.logical_and(m == 0, n == 0))
    def _():
        start(m, slot)

    @pl.when(jnp.logical_and(n == 1, m + 1 < n_m))
    def _():
        start(m + 1, 1 - slot)

    @pl.when(n == 0)
    def _():
        wait(m, slot)
        t = m % n_t

        @pl.when(t == 0)
        def _():
            xbuf[slot, 0:N_META, :] = meta_ref[...]

        @pl.when(t == n_t - 1)
        def _():
            xbuf[slot, last_rows:TT, :] = jnp.zeros((TT - last_rows, D_MODEL), F32)

        h0 = xbuf[slot]
        h0_ref[...] = h0
        u_scr[...] = _rmsnorm(h0, g_ref[...]).astype(BF16)

    y = jnp.dot(u_scr[...], w_ref[...], preferred_element_type=F32)

    @pl.when(n < 2)
    def _():
        lru_ref[...] = y

    @pl.when(n >= 2)
    def _():
        qkv_ref[...] = y.astype(BF16)


def _in_proj(x, meta_tokens, g_mix, w_in_bf16, t_pad):
    batch, seq, _ = x.shape
    n_t = t_pad // TT
    n_pad = batch * t_pad
    n_col = D_IN_PROJ // PROJ_TN
    assert N_META % SUBLANES == 0 and 0 < seq + N_META - (n_t - 1) * TT <= TT
    return pl.pallas_call(
        functools.partial(_inproj_kernel, n_t, seq),
        grid=(n_pad // TT, n_col),
        in_specs=[
            pl.BlockSpec(memory_space=pl.ANY),
            pl.BlockSpec((N_META, D_MODEL), lambda m, n: (0, 0)),
            pl.BlockSpec((1, D_MODEL), lambda m, n: (0, 0)),
            pl.BlockSpec((D_MODEL, PROJ_TN), lambda m, n: (0, n)),
        ],
        out_specs=[
            pl.BlockSpec((TT, D_MODEL), lambda m, n: (m, 0)),
            pl.BlockSpec((TT, PROJ_TN), lambda m, n: (m, jnp.minimum(n, 1))),
            pl.BlockSpec((TT, PROJ_TN), lambda m, n: (m, jnp.maximum(n - 2, 0))),
        ],
        out_shape=[
            jax.ShapeDtypeStruct((n_pad, D_MODEL), F32),
            jax.ShapeDtypeStruct((n_pad, 2 * D_LRU), F32),
            jax.ShapeDtypeStruct((n_pad, 3 * D_ATT), BF16),
        ],
        scratch_shapes=[
            pltpu.VMEM((2, TT, D_MODEL), F32),
            pltpu.VMEM((TT, D_MODEL), BF16),
            pltpu.SemaphoreType.DMA((2,)),
        ],
        compiler_params=pltpu.CompilerParams(
            dimension_semantics=("arbitrary", "arbitrary"), vmem_limit_bytes=VMEM_LIMIT),
        name="in_proj",
    )(x, meta_tokens, g_mix, w_in_bf16)


def _shift_rows(x, tail, s):
    r = pltpu.roll(x, s, 0)
    tail_row = lax.broadcasted_iota(jnp.int32, tail.shape, 0)
    head = jnp.where(tail_row < s, pltpu.roll(tail, s, 0), r[0:SUBLANES])
    return jnp.concatenate([head, r[SUBLANES:]], axis=0)


def _lru_kernel(x_ref, gate_ref, cw_ref, cb_ref, wr_ref, br_ref, wi_ref, bi_ref, lam_ref,
                y_ref, tail_scr, h_scr):
    t = pl.program_id(2)

    @pl.when(t == 0)
    def _():
        tail_scr[...] = jnp.zeros_like(tail_scr)
        h_scr[...] = jnp.zeros_like(h_scr)

    x = x_ref[...]
    tail = tail_scr[...]
    cw = cw_ref[...]
    xc = cb_ref[...] + x * cw[CONV_WIDTH - 1:CONV_WIDTH]
    for k in range(CONV_WIDTH - 1):
        xc = xc + _shift_rows(x, tail, CONV_WIDTH - 1 - k) * cw[k:k + 1]
    tail_scr[...] = x[TT - SUBLANES:TT]

    xb = xc.astype(BF16)
    r = jax.nn.sigmoid(jnp.dot(xb, wr_ref[...], preferred_element_type=F32) + br_ref[...])
    i = jax.nn.sigmoid(jnp.dot(xb, wi_ref[...], preferred_element_type=F32) + bi_ref[...])
    lam = lam_ref[...]
    log_sig = jnp.minimum(lam, 0.0) - jnp.log(1.0 + jnp.exp(-jnp.abs(lam)))
    log_a = (LRU_C * r) * log_sig
    a = jnp.exp(log_a)
    b = jnp.sqrt(1.0 - a * a) * (i * xc)

    row = lax.broadcasted_iota(jnp.int32, a.shape, 0) % SUBLANES
    s = 1
    while s < SUBLANES:
        keep = row >= s
        a_sh = jnp.where(keep, pltpu.roll(a, s, 0), 1.0)
        b_sh = jnp.where(keep, pltpu.roll(b, s, 0), 0.0)
        b = a * b_sh + b
        a = a * a_sh
        s *= 2
    carry = h_scr[...]
    groups = []
    for g in range(TT // SUBLANES):
        rows = slice(g * SUBLANES, (g + 1) * SUBLANES)
        hg = b[rows] + a[rows] * carry
        groups.append(hg)
        carry = jnp.broadcast_to(hg[SUBLANES - 1:SUBLANES], hg.shape)
    h_scr[...] = carry
    h = jnp.concatenate(groups, axis=0)
    y_ref[...] = (h * jax.nn.gelu(gate_ref[...])).astype(BF16)


def _lru(proj_lru, conv_w, conv_b, wr_bd, b_r, wi_bd, b_i, lru_l, batch, t_pad):
    n_pad = proj_lru.shape[0]
    n_t = t_pad // TT
    n_c = D_LRU // LRU_CB
    vec = lambda: pl.BlockSpec((1, LRU_CB), lambda b, c, t: (0, c))
    return pl.pallas_call(
        _lru_kernel,
        grid=(batch, n_c, n_t),
        in_specs=[
            pl.BlockSpec((TT, LRU_CB), lambda b, c, t: (b * n_t + t, c)),
            pl.BlockSpec((TT, LRU_CB), lambda b, c, t: (b * n_t + t, n_c + c)),
            pl.BlockSpec((CONV_WIDTH, LRU_CB), lambda b, c, t: (0, c)),
            vec(),
            pl.BlockSpec((None, LRU_CB, LRU_CB), lambda b, c, t: (c, 0, 0)),
            vec(),
            pl.BlockSpec((None, LRU_CB, LRU_CB), lambda b, c, t: (c, 0, 0)),
            vec(),
            vec(),
        ],
        out_specs=pl.BlockSpec((TT, LRU_CB), lambda b, c, t: (b * n_t + t, c)),
        out_shape=jax.ShapeDtypeStruct((n_pad, D_LRU), BF16),
        scratch_shapes=[pltpu.VMEM((SUBLANES, LRU_CB), F32), pltpu.VMEM((SUBLANES, LRU_CB), F32)],
        compiler_params=pltpu.CompilerParams(
            dimension_semantics=("arbitrary", "arbitrary", "arbitrary"),
            vmem_limit_bytes=VMEM_LIMIT),
        name="rglru",
    )(proj_lru, proj_lru, conv_w, conv_b, wr_bd, b_r, wi_bd, b_i, lru_l)


def _bucket_thresholds():
    max_exact = N_BUCKETS // 2
    n = np.arange(0, MAX_DISTANCE + 1)
    nf = np.maximum(n, 1).astype(np.float32)
    large = max_exact + (np.log(nf / np.float32(max_exact)) / np.float32(math.log(MAX_DISTANCE / max_exact))
                         * np.float32(N_BUCKETS - max_exact)).astype(np.int32)
    large = np.minimum(large, N_BUCKETS - 1)
    bucket = np.where(n < max_exact, n, large)
    return [int(np.argmax(bucket >= k)) for k in range(1, N_BUCKETS)]


_BUCKET_THR = _bucket_thresholds()


def _attn_kernel(rb_ref, q_ref, k_ref, v_ref, lam_ref, g_ref, o_ref, bias_scr, vt_scr, m_scr, acc_scr):
    h = pl.program_id(0)
    b = pl.program_id(1)
    qi = pl.program_id(2)
    tk = TT
    n_kv = vt_scr.shape[0]

    @pl.when(jnp.logical_and(b == 0, qi == 0))
    def _():
        k_pos = lax.broadcasted_iota(jnp.int32, (tk, tk), 0)
        q_pos = lax.broadcasted_iota(jnp.int32, (tk, tk), 1)
        far = rb_ref[N_BUCKETS - 1, h]
        for d in range(2):
            rel = q_pos - k_pos + d * tk
            val = jnp.full((tk, tk), rb_ref[0, h] - far, F32)
            for kk, thr in enumerate(_BUCKET_THR):
                val = jnp.where(rel >= thr, rb_ref[kk + 1, h] - far, val)
            val = val * LOG2_E
            if d == 0:
                val = jnp.where(rel >= 0, val, -jnp.inf)
            bias_scr[d] = val

    @pl.when(qi == 0)
    def _():
        for t in range(n_kv):
            vt_scr[t, 0:V_HEAD_DIM, :] = v_ref[t * tk:(t + 1) * tk, :].astype(F32).T.astype(BF16)
            vt_scr[t, V_HEAD_DIM:VT_ROWS, :] = jnp.ones((VT_ROWS - V_HEAD_DIM, tk), BF16)

    q = q_ref[...]
    lane = lax.broadcasted_iota(jnp.int32, q.shape, 1)
    scale = (QK_HEAD_DIM ** -0.5) * LOG2_E
    qf = q.astype(F32) * scale
    qs = jnp.concatenate([jnp.where(lane < QK_HEAD_DIM, qf, 0.0),
                          jnp.where(lane >= QK_HEAD_DIM, qf, 0.0)], axis=0).astype(BF16)

    m_scr[...] = jnp.full(m_scr.shape, NEG_BIG, F32)
    acc_scr[...] = jnp.zeros_like(acc_scr)

    def tile_group(tiles):
        n_sub = 2 * tk // ATT_SUB
        stages = [(t, c) for t in range(len(tiles)) for c in range(n_sub)]

        def n_keys(t, c):
            return (c * ATT_SUB) % tk + ATT_SUB if tiles[t][1] == 0 else tk

        def scores(t, c):
            j, bias_idx = tiles[t]
            nk = n_keys(t, c)
            kt = k_ref[pl.ds(pl.multiple_of(j * tk, tk), nk), :]
            cols = slice(c * ATT_SUB, (c + 1) * ATT_SUB)
            s = lax.dot_general(kt, qs[cols], (((1,), (1,)), ((), ())), preferred_element_type=F32)
            if bias_idx is not None:
                q0 = (c * ATT_SUB) % tk
                s = s + bias_scr[bias_idx, 0:nk, q0:q0 + ATT_SUB]
            return s

        def compact(t):
            return tiles[t][1] != 0

        def produce(t, c):
            s = scores(t, c)
            if not compact(t):
                return s
            cols = slice(c * ATT_SUB, (c + 1) * ATT_SUB)
            m_prev = m_scr[:, cols]
            m_new = jnp.maximum(m_prev, jnp.max(s, axis=0, keepdims=True))
            m_scr[:, cols] = m_new
            return (s - m_prev).astype(BF16), m_prev, m_new

        pending = [produce(*st) for st in stages[:ATT_LOOKAHEAD]]
        for n, (t, c) in enumerate(stages):
            cols = slice(c * ATT_SUB, (c + 1) * ATT_SUB)
            item = pending.pop(0)
            if n + ATT_LOOKAHEAD < len(stages):
                pending.append(produce(*stages[n + ATT_LOOKAHEAD]))
            if compact(t):
                d, m_prev, m_new = item
                p = jnp.exp2(d - (m_new - m_prev).astype(BF16))
            else:
                m_prev = m_scr[:, cols]
                m_new = jnp.maximum(m_prev, jnp.max(item, axis=0, keepdims=True))
                m_scr[:, cols] = m_new
                p = jnp.exp2(item - m_new).astype(BF16)
            alpha = jnp.exp2(m_prev - m_new)
            vt = vt_scr[tiles[t][0], :, 0:n_keys(t, c)]
            acc_scr[:, cols] = alpha * acc_scr[:, cols] + jnp.dot(vt, p, preferred_element_type=F32)

    @pl.when(qi >= 1)
    def _():
        tile_group([(qi, 0), (qi - 1, 1)])

    @pl.when(qi == 0)
    def _():
        tile_group([(qi, 0)])

    n_far = jnp.maximum(qi - 1, 0)

    def far_group(i, carry):
        tile_group([(ATT_GROUP * i + t, None) for t in range(ATT_GROUP)])
        return carry

    lax.fori_loop(0, n_far // ATT_GROUP, far_group, 0)
    done = n_far - n_far % ATT_GROUP
    piece = ATT_GROUP // 2
    while piece >= 1:
        @pl.when((n_far % (2 * piece)) >= piece)
        def _(done=done, piece=piece):
            tile_group([(done + t, None) for t in range(piece)])
        done = done + jnp.where((n_far % (2 * piece)) >= piece, piece, 0)
        piece //= 2

    lam_p = lam_ref[...]
    lam = (jnp.exp(jnp.sum(lam_p[0:1] * lam_p[1:2], axis=-1, keepdims=True))
           - jnp.exp(jnp.sum(lam_p[2:3] * lam_p[3:4], axis=-1, keepdims=True)) + LAMBDA_INIT)
    acc = acc_scr[...]
    num = acc[0:V_HEAD_DIM]
    den = acc[V_HEAD_DIM:V_HEAD_DIM + 1]
    o_t = num[:, :tk] / den[:, :tk] - lam * (num[:, tk:] / den[:, tk:])
    o_ref[...] = (_rmsnorm(o_t.T, g_ref[...]) * (1.0 - LAMBDA_INIT)).astype(BF16)


def _attention(qkv, rel_bias, lam_params, subln_g, batch, t_pad):
    n_q = t_pad // TT
    qkv3 = qkv.reshape(batch, t_pad, 3 * D_ATT)
    out = pl.pallas_call(
        _attn_kernel,
        grid_spec=pltpu.PrefetchScalarGridSpec(
            num_scalar_prefetch=0,
            grid=(N_HEADS, batch, n_q),
            in_specs=[
                pl.BlockSpec(memory_space=pltpu.SMEM),
                pl.BlockSpec((None, TT, V_HEAD_DIM), lambda h, b, q: (b, q, h)),
                pl.BlockSpec((None, t_pad, V_HEAD_DIM), lambda h, b, q: (b, 0, N_HEADS + h)),
                pl.BlockSpec((None, t_pad, V_HEAD_DIM), lambda h, b, q: (b, 0, 2 * N_HEADS + h)),
                pl.BlockSpec((4, QK_HEAD_DIM), lambda h, b, q: (0, 0)),
                pl.BlockSpec((1, V_HEAD_DIM), lambda h, b, q: (0, 0)),
            ],
            out_specs=pl.BlockSpec((None, TT, V_HEAD_DIM), lambda h, b, q: (b, q, h)),
            scratch_shapes=[
                pltpu.VMEM((2, TT, TT), F32),
                pltpu.VMEM((n_q, VT_ROWS, TT), BF16),
                pltpu.VMEM((1, 2 * TT), F32),
                pltpu.VMEM((VT_ROWS, 2 * TT), F32),
            ],
        ),
        out_shape=jax.ShapeDtypeStruct((batch, t_pad, D_ATT), BF16),
        compiler_params=pltpu.CompilerParams(
            dimension_semantics=("arbitrary", "arbitrary", "arbitrary"),
            vmem_limit_bytes=VMEM_LIMIT),
        name="diff_attention",
    )(rel_bias, qkv3, qkv3, qkv3, lam_params, subln_g)
    return out.reshape(batch * t_pad, D_ATT)


def _outproj_kernel(yl_ref, ya_ref, h0_ref, w_ref, g_ref, wrt_ref, brt_ref, h1_ref, u_ref, lg_ref):
    acc = jnp.dot(yl_ref[...], w_ref[0:D_LRU, :], preferred_element_type=F32)
    acc = acc + jnp.dot(ya_ref[...], w_ref[D_LRU:, :], preferred_element_type=F32)
    h1 = h0_ref[...] + acc
    h1_ref[...] = h1
    u = _rmsnorm(h1, g_ref[...])
    u_ref[...] = u
    u_hi = u.astype(BF16)
    u_lo = (u - u_hi.astype(F32)).astype(BF16)
    w = wrt_ref[...]
    w_hi = w.astype(BF16)
    w_lo = (w - w_hi.astype(F32)).astype(BF16)
    parts = jnp.dot(jnp.concatenate([u_hi, u_lo], axis=0), jnp.concatenate([w_hi, w_lo], axis=1),
                    preferred_element_type=F32)
    lg = (parts[:OUT_TM, :LANES] + parts[:OUT_TM, LANES:]) + (parts[OUT_TM:, :LANES] + parts[OUT_TM:, LANES:])
    lg_ref[...] = lg + brt_ref[...]


def _out_proj(y_lru, y_att, h0, w_out_bf16, g_ffn, w_rt, b_rt):
    n_pad = h0.shape[0]
    row = lambda w: pl.BlockSpec((OUT_TM, w), lambda m: (m, 0))
    full = lambda a, b: pl.BlockSpec((a, b), lambda m: (0, 0))
    return pl.pallas_call(
        _outproj_kernel,
        grid=(n_pad // OUT_TM,),
        in_specs=[row(D_LRU), row(D_ATT), row(D_MODEL), full(D_MODEL, D_MODEL), full(1, D_MODEL),
                  full(D_MODEL, LANES), full(1, LANES)],
        out_specs=[row(D_MODEL), row(D_MODEL), row(LANES)],
        out_shape=[
            jax.ShapeDtypeStruct((n_pad, D_MODEL), F32),
            jax.ShapeDtypeStruct((n_pad, D_MODEL), F32),
            jax.ShapeDtypeStruct((n_pad, LANES), F32),
        ],
        compiler_params=pltpu.CompilerParams(
            dimension_semantics=("arbitrary",), vmem_limit_bytes=VMEM_LIMIT),
        name="out_proj",
    )(y_lru, y_att, h0, w_out_bf16, g_ffn, w_rt, b_rt)


def _route_kernel(lg_ref, idx_ref, gate_ref, cnt_ref, carry_scr):
    step = pl.program_id(0)

    @pl.when(step == 0)
    def _():
        carry_scr[...] = jnp.zeros_like(carry_scr)

    lg = lg_ref[...]
    lane = lax.broadcasted_iota(jnp.int32, lg.shape, 1)
    first = lambda mask: jnp.min(jnp.where(mask, lane, LANES), axis=-1, keepdims=True)

    is_g = lane < N_GROUPS
    gl = jnp.where(is_g, lg, -jnp.inf)
    gmax = jnp.max(gl, axis=-1, keepdims=True)
    grp = first(gl == gmax)
    gsum = jnp.sum(jnp.where(is_g, jnp.exp(gl - gmax), 0.0), axis=-1, keepdims=True)
    p_grp = 1.0 / gsum

    lane_e = lane - N_GROUPS
    in_grp = (lane_e >= 0) & (lane_e < N_EXPERTS) & ((lane_e // EXPERTS_PER_GROUP) == grp)
    el = jnp.where(in_grp, lg, -jnp.inf)
    v1 = jnp.max(el, axis=-1, keepdims=True)
    i1 = first(el == v1)
    el2 = jnp.where(lane == i1, -jnp.inf, el)
    v2 = jnp.max(el2, axis=-1, keepdims=True)
    i2 = first(el2 == v2)
    e2 = jnp.exp(v2 - v1)
    den = 1.0 + e2
    g0 = p_grp * (1.0 / den)
    g1 = p_grp * (e2 / den)

    hit1 = lane == i1
    hit2 = lane == i2
    onehot = jnp.where(hit1 | hit2, 1.0, 0.0)
    rr = lax.broadcasted_iota(jnp.int32, (TT, TT), 0)
    cc = lax.broadcasted_iota(jnp.int32, (TT, TT), 1)
    tri = jnp.where(cc < rr, 1.0, 0.0).astype(BF16)
    rank = jnp.dot(tri, onehot.astype(BF16), preferred_element_type=F32) + carry_scr[...]
    r0 = jnp.sum(jnp.where(hit1, rank, 0.0), axis=-1, keepdims=True).astype(jnp.int32)
    r1 = jnp.sum(jnp.where(hit2, rank, 0.0), axis=-1, keepdims=True).astype(jnp.int32)
    carry_scr[...] = carry_scr[...] + jnp.sum(onehot, axis=0, keepdims=True)

    idx_ref[...] = jnp.where(lane == 0, i1 - N_GROUPS,
                             jnp.where(lane == 1, i2 - N_GROUPS,
                                       jnp.where(lane == 2, r0, jnp.where(lane == 3, r1, 0))))
    gate_ref[...] = jnp.where(lane == 0, g0, jnp.where(lane == 1, g1, 0.0))
    cnt_ref[...] = carry_scr[...]


def _route(logits):
    n_pad = logits.shape[0]
    row = pl.BlockSpec((TT, LANES), lambda m: (m, 0))
    return pl.pallas_call(
        _route_kernel,
        grid=(n_pad // TT,),
        in_specs=[row],
        out_specs=[row, row, pl.BlockSpec((1, LANES), lambda m: (0, 0))],
        out_shape=[
            jax.ShapeDtypeStruct((n_pad, LANES), jnp.int32),
            jax.ShapeDtypeStruct((n_pad, LANES), F32),
            jax.ShapeDtypeStruct((1, LANES), F32),
        ],
        scratch_shapes=[pltpu.VMEM((1, LANES), F32)],
        compiler_params=pltpu.CompilerParams(
            dimension_semantics=("arbitrary",), vmem_limit_bytes=VMEM_LIMIT),
        name="route",
    )(logits)


def _dest_kernel(idx_ref, ps_ref, o_ref):
    idx = idx_ref[...]
    lane = lax.broadcasted_iota(jnp.int32, idx.shape, 1)
    ps = ps_ref[...]
    pick = lambda e: jnp.sum(jnp.where(lane == e, ps, 0.0), axis=-1, keepdims=True).astype(jnp.int32)
    d0 = pick(idx[:, 0:1]) + idx[:, 2:3]
    d1 = pick(idx[:, 1:2]) + idx[:, 3:4]
    o_ref[...] = jnp.where(lane == 0, d0, jnp.where(lane == 1, d1, 0))


def _dest(idx, pstart_row):
    n_pad = idx.shape[0]
    row = pl.BlockSpec((TT, LANES), lambda m: (m, 0))
    return pl.pallas_call(
        _dest_kernel,
        grid=(n_pad // TT,),
        in_specs=[row, pl.BlockSpec((1, LANES), lambda m: (0, 0))],
        out_specs=row,
        out_shape=jax.ShapeDtypeStruct((n_pad, LANES), jnp.int32),
        compiler_params=pltpu.CompilerParams(
            dimension_semantics=("arbitrary",), vmem_limit_bytes=VMEM_LIMIT),
        name="dest",
    )(idx, pstart_row)


ROW_UNROLL = 8


def _row_copy(src, s_row, dst, d_row, sem):
    return pltpu.make_async_copy(src.at[pl.ds(s_row, 1)], dst.at[pl.ds(d_row, 1)], sem)


def _dispatch_kernel(d0_ref, d1_ref, pe_ref, u_ref, xb_ref, zero_scr, sem, zsem):
    step = pl.program_id(0)
    base = step * TT

    def zero_block(e):
        start = pl.multiple_of(pe_ref[e] - DISPATCH_BLK, DISPATCH_BLK)
        return pltpu.make_async_copy(zero_scr, xb_ref.at[pl.ds(start, DISPATCH_BLK)], zsem)

    @pl.when(step == 0)
    def _():
        zero_scr[...] = jnp.zeros_like(zero_scr)

        def nonempty(e):
            return pe_ref[e] > jnp.where(e == 0, 0, pe_ref[jnp.maximum(e - 1, 0)])

        def zstart(e, carry):
            @pl.when(nonempty(e))
            def _():
                zero_block(e).start()
            return carry

        def zwait(e, carry):
            @pl.when(nonempty(e))
            def _():
                zero_block(e).wait()
            return carry

        lax.fori_loop(0, N_EXPERTS, zstart, 0)
        lax.fori_loop(0, N_EXPERTS, zwait, 0)

        def tail_block(i):
            start = pl.multiple_of(i * DISPATCH_BLK, DISPATCH_BLK)
            return pltpu.make_async_copy(zero_scr, xb_ref.at[pl.ds(start, DISPATCH_BLK)], zsem)

        def tstart(i, carry):
            tail_block(i).start()
            return carry

        def twait(i, carry):
            tail_block(i).wait()
            return carry

        n_used = pe_ref[N_EXPERTS - 1] // DISPATCH_BLK
        n_blocks = xb_ref.shape[0] // DISPATCH_BLK
        lax.fori_loop(n_used, n_blocks, tstart, 0)
        lax.fori_loop(n_used, n_blocks, twait, 0)

    def issue(r, carry):
        _row_copy(u_ref, r, xb_ref, d0_ref[base + r], sem.at[0]).start()
        _row_copy(u_ref, r, xb_ref, d1_ref[base + r], sem.at[1]).start()
        return carry

    lax.fori_loop(0, TT, issue, 0, unroll=ROW_UNROLL)
    for k in range(2):
        pltpu.make_async_copy(u_ref, xb_ref.at[pl.ds(0, TT)], sem.at[k]).wait()


def _dispatch(d0, d1, pends, u2, rows):
    n_pad = u2.shape[0]
    return pl.pallas_call(
        _dispatch_kernel,
        grid_spec=pltpu.PrefetchScalarGridSpec(
            num_scalar_prefetch=3,
            grid=(n_pad // TT,),
            in_specs=[pl.BlockSpec((TT, D_MODEL), lambda m, d0, d1, pe: (m, 0))],
            out_specs=pl.BlockSpec(memory_space=pl.ANY),
            scratch_shapes=[
                pltpu.VMEM((DISPATCH_BLK, D_MODEL), F32),
                pltpu.SemaphoreType.DMA((2,)),
                pltpu.SemaphoreType.DMA(()),
            ],
        ),
        out_shape=jax.ShapeDtypeStruct((rows, D_MODEL), F32),
        compiler_params=pltpu.CompilerParams(
            dimension_semantics=("arbitrary",), vmem_limit_bytes=VMEM_LIMIT),
        name="dispatch",
    )(d0, d1, pends, u2)


WEIGHT_UNITS = 8
WEIGHT_DEPTH = 4


def _stream_weights(i, tgt_ref, eseq_ref, ntot_ref, w_hbm, stages, wbfs, sem, done_scr):
    def slab(u, k):
        rows = stages[k].shape[1]
        return pl.ds(pl.multiple_of((u % WEIGHT_UNITS) * rows, rows), rows)

    def unit_copies(u):
        e = eseq_ref[u // WEIGHT_UNITS]
        b = u % WEIGHT_DEPTH
        return [pltpu.make_async_copy(w.at[e, slab(u, k), :], stages[k].at[b], sem.at[b, k])
                for k, w in enumerate(w_hbm)]

    @pl.when(i == 0)
    def _():
        done_scr[0] = 0
        for u in range(WEIGHT_DEPTH):
            for c in unit_copies(u):
                c.start()

    def body(u, carry):
        for c in unit_copies(u):
            c.wait()
        slot = (u // WEIGHT_UNITS) % 2
        b = u % WEIGHT_DEPTH
        for k in range(len(w_hbm)):
            wbfs[k][slot, slab(u, k), :] = stages[k][b].astype(BF16)

        @pl.when(u + WEIGHT_DEPTH < ntot_ref[0])
        def _():
            for c in unit_copies(u + WEIGHT_DEPTH):
                c.start()

        return carry

    lax.fori_loop(done_scr[0], tgt_ref[i], body, 0)
    done_scr[0] = tgt_ref[i]


def _expert_kernel(tgt_ref, eseq_ref, ntot_ref, slot_ref, nu_ref, x_ref, wg_hbm, wu_hbm, wd_hbm, y_ref,
                   sg, su, sd, bg, bu, bd, sem, done_scr):
    i = pl.program_id(0)
    _stream_weights(i, tgt_ref, eseq_ref, ntot_ref, (wg_hbm, wu_hbm, wd_hbm), (sg, su, sd), (bg, bu, bd),
                    sem, done_scr)
    used = i < nu_ref[0]

    @pl.when(used)
    def _():
        slot = slot_ref[i]
        x = x_ref[...].astype(BF16)
        g = jnp.dot(x, bg[slot], preferred_element_type=F32)
        u = jnp.dot(x, bu[slot], preferred_element_type=F32)
        a = (jax.nn.silu(g) * u).astype(BF16)
        y_ref[...] = jnp.dot(a, bd[slot], preferred_element_type=F32)

    @pl.when(jnp.logical_not(used))
    def _():
        y_ref[...] = jnp.zeros_like(y_ref)


def _experts(sched, x_buf, w_gate, w_up, w_down):
    rows = x_buf.shape[0]
    n_blocks = rows // DISPATCH_BLK
    blk_map = lambda i, tg, es, nt, sl, nu: (jnp.minimum(i, jnp.maximum(nu[0] - 1, 0)), 0)
    out_map = lambda i, tg, es, nt, sl, nu: (i, 0)
    hbm = pl.BlockSpec(memory_space=pl.ANY)
    stage = lambda k_dim, n_dim: pltpu.VMEM((WEIGHT_DEPTH, k_dim // WEIGHT_UNITS, n_dim), F32)
    resident = lambda k_dim, n_dim: pltpu.VMEM((2, k_dim, n_dim), BF16)
    return pl.pallas_call(
        _expert_kernel,
        grid_spec=pltpu.PrefetchScalarGridSpec(
            num_scalar_prefetch=5,
            grid=(n_blocks,),
            in_specs=[pl.BlockSpec((DISPATCH_BLK, D_MODEL), blk_map), hbm, hbm, hbm],
            out_specs=pl.BlockSpec((DISPATCH_BLK, D_MODEL), out_map),
            scratch_shapes=[
                stage(D_MODEL, D_EXPERT), stage(D_MODEL, D_EXPERT), stage(D_EXPERT, D_MODEL),
                resident(D_MODEL, D_EXPERT), resident(D_MODEL, D_EXPERT), resident(D_EXPERT, D_MODEL),
                pltpu.SemaphoreType.DMA((WEIGHT_DEPTH, 3)),
                pltpu.SMEM((1,), jnp.int32),
            ],
        ),
        out_shape=jax.ShapeDtypeStruct((rows, D_MODEL), F32),
        compiler_params=pltpu.CompilerParams(
            dimension_semantics=("arbitrary",), vmem_limit_bytes=VMEM_LIMIT),
        name="experts",
    )(*sched, x_buf, w_gate, w_up, w_down)


def _expert_schedule(padded, pstarts, n_blocks):
    nonempty = padded > 0
    seq_of_expert = jnp.cumsum(nonempty.astype(jnp.int32)) - 1
    n_seq = jnp.sum(nonempty.astype(jnp.int32))
    experts = jnp.arange(N_EXPERTS, dtype=jnp.int32)
    is_kth = jnp.logical_and(nonempty[None, :], seq_of_expert[None, :] == experts[:, None])
    eseq = jnp.sum(jnp.where(is_kth, experts[None, :], 0), axis=1)
    pends = pstarts + padded
    n_used = pends[-1] // DISPATCH_BLK
    blk = jnp.arange(n_blocks, dtype=jnp.int32)
    blk_c = jnp.minimum(blk, jnp.maximum(n_used - 1, 0))
    blk_expert = jnp.minimum(
        jnp.sum((pends[None, :] <= (blk_c * DISPATCH_BLK)[:, None]).astype(jnp.int32), axis=1), N_EXPERTS - 1)
    onehot = (blk_expert[:, None] == experts[None, :]).astype(jnp.int32)
    pick = lambda v: jnp.sum(onehot * v[None, :], axis=1)
    q = pick(seq_of_expert)
    j = blk_c - pick(pstarts) // DISPATCH_BLK
    n = jnp.maximum(pick(padded) // DISPATCH_BLK, 1)
    n_total = n_seq * WEIGHT_UNITS
    tgt = jnp.minimum(WEIGHT_UNITS * (q + 1) + (WEIGHT_UNITS * j) // n, n_total)
    tgt = jnp.where(blk < n_used, tgt, n_total)
    i32 = lambda a: a.astype(jnp.int32)
    return i32(tgt), eseq, i32(n_total)[None], i32(q % 2), i32(n_used)[None]


COMBINE_TM = 512


def _combine_kernel(t_pad, d0_ref, d1_ref, h_hbm, gate_hbm, g_ref, yb_ref, o_ref,
                    h_scr, gate_scr, y0_scr, y1_scr, sem):
    n_j = pl.num_programs(1)
    step = pl.program_id(0) * n_j + pl.program_id(1)
    n_steps = pl.num_programs(0) * n_j

    def base_row(s):
        return pl.multiple_of((s // n_j) * t_pad + N_META + (s % n_j) * COMBINE_TM, 8)

    def tile_copies(s, slot):
        base = base_row(s)
        return [
            pltpu.make_async_copy(h_hbm.at[pl.ds(base, COMBINE_TM)], h_scr.at[slot], sem.at[slot, 2]),
            pltpu.make_async_copy(gate_hbm.at[pl.ds(base, COMBINE_TM)], gate_scr.at[slot], sem.at[slot, 3]),
        ]

    def fetch(s, slot):
        base = base_row(s)
        for c in tile_copies(s, slot):
            c.start()

        def issue(r, carry):
            _row_copy(yb_ref, d0_ref[base + r], y0_scr.at[slot], r, sem.at[slot, 0]).start()
            _row_copy(yb_ref, d1_ref[base + r], y1_scr.at[slot], r, sem.at[slot, 1]).start()
            return carry

        lax.fori_loop(0, COMBINE_TM, issue, 0, unroll=ROW_UNROLL)

    slot = step % 2

    @pl.when(step == 0)
    def _():
        fetch(step, slot)

    @pl.when(step + 1 < n_steps)
    def _():
        fetch(step + 1, 1 - slot)

    pltpu.make_async_copy(yb_ref.at[pl.ds(0, COMBINE_TM)], y0_scr.at[slot], sem.at[slot, 0]).wait()
    pltpu.make_async_copy(yb_ref.at[pl.ds(0, COMBINE_TM)], y1_scr.at[slot], sem.at[slot, 1]).wait()
    for c in tile_copies(step, slot):
        c.wait()

    gate = gate_scr[slot]
    h = h_scr[slot] + gate[:, 0:1] * y0_scr[slot] + gate[:, 1:2] * y1_scr[slot]
    o_ref[...] = _rmsnorm(h, g_ref[...])


def _combine(d0, d1, h1, gates, g_final, y_buf, batch, seq):
    assert seq % COMBINE_TM == 0 and N_META % 8 == 0
    tile = lambda w: pltpu.VMEM((2, COMBINE_TM, w), F32)
    return pl.pallas_call(
        functools.partial(_combine_kernel, h1.shape[0] // batch),
        grid_spec=pltpu.PrefetchScalarGridSpec(
            num_scalar_prefetch=2,
            grid=(batch, seq // COMBINE_TM),
            in_specs=[
                pl.BlockSpec(memory_space=pl.ANY),
                pl.BlockSpec(memory_space=pl.ANY),
                pl.BlockSpec((1, D_MODEL), lambda b, j, d0, d1: (0, 0)),
                pl.BlockSpec(memory_space=pl.ANY),
            ],
            out_specs=pl.BlockSpec((None, COMBINE_TM, D_MODEL), lambda b, j, d0, d1: (b, j, 0)),
            scratch_shapes=[tile(D_MODEL), tile(LANES), tile(D_MODEL), tile(D_MODEL),
                            pltpu.SemaphoreType.DMA((2, 4))],
        ),
        out_shape=jax.ShapeDtypeStruct((batch, seq, D_MODEL), F32),
        compiler_params=pltpu.CompilerParams(
            dimension_semantics=("arbitrary", "arbitrary"), vmem_limit_bytes=VMEM_LIMIT),
        name="combine",
    )(d0, d1, h1, gates, g_final, y_buf)


def _block_diag(w):
    per = LRU_CB // LRU_BLOCK_DIM
    w4 = w.reshape(LRU_BLOCKS // per, per, LRU_BLOCK_DIM, LRU_BLOCK_DIM)
    eye = jnp.eye(per, dtype=w.dtype)
    bd = jnp.einsum('cpij,pq->cpiqj', w4, eye)
    return bd.reshape(LRU_BLOCKS // per, LRU_CB, LRU_CB)


def kernel(x, meta_tokens, g_mix, w_in, conv_w, conv_b, w_rgate, b_rgate, w_igate, b_igate, lru_L, lambda_q1, lambda_k1, lambda_q2, lambda_k2, subln_g, rel_bias, w_out, g_ffn, w_group, b_group, w_router, b_router, w_gate, w_up, w_down, g_final):
    batch, seq, _ = x.shape
    t_real = N_META + seq
    t_pad = _round_up(t_real, TT)
    n_pad = batch * t_pad

    h0, proj_lru, proj_qkv = _in_proj(x, meta_tokens.astype(x.dtype), g_mix[0][None],
                                      w_in[0].astype(BF16), t_pad)

    y_lru = _lru(proj_lru, conv_w[0], conv_b[0][None],
                 _block_diag(w_rgate[0]).astype(BF16), b_rgate[0].reshape(1, D_LRU),
                 _block_diag(w_igate[0]).astype(BF16), b_igate[0].reshape(1, D_LRU),
                 lru_L[0][None], batch, t_pad)

    lam_params = jnp.stack([lambda_q1[0], lambda_k1[0], lambda_q2[0], lambda_k2[0]])
    y_att = _attention(proj_qkv, rel_bias, lam_params, subln_g[0][None], batch, t_pad)

    w_rt = jnp.concatenate([w_group[0], w_router[0],
                            jnp.zeros((D_MODEL, LANES - N_GROUPS - N_EXPERTS), F32)], axis=1)
    b_rt = jnp.concatenate([b_group[0], b_router[0],
                            jnp.zeros((LANES - N_GROUPS - N_EXPERTS,), F32)])[None]
    h1, u2, logits = _out_proj(y_lru, y_att, h0, w_out[0].astype(BF16), g_ffn[0][None], w_rt, b_rt)

    idx, gates, counts = _route(logits)

    cnt = counts[0, N_GROUPS:N_GROUPS + N_EXPERTS].astype(jnp.int32)
    padded = (cnt + DISPATCH_BLK - 1) // DISPATCH_BLK * DISPATCH_BLK
    pends = jnp.cumsum(padded)
    pstarts = pends - padded
    dest = _dest(idx, jnp.pad(pstarts.astype(F32), (0, LANES - N_EXPERTS))[None])
    d0 = dest[:, 0]
    d1 = dest[:, 1]
    n_blocks = -(-(2 * n_pad + N_EXPERTS * (DISPATCH_BLK - 1)) // DISPATCH_BLK)
    rows = n_blocks * DISPATCH_BLK
    sched = _expert_schedule(padded, pstarts, n_blocks)

    x_buf = _dispatch(d0, d1, pends.astype(jnp.int32), u2, rows)
    y_buf = _experts(sched, x_buf, w_gate[0], w_up[0], w_down[0])
    return _combine(d0, d1, h1, gates, g_final[None], y_buf, batch, seq)
```

```python
import functools
import math

import numpy as np
import jax
import jax.numpy as jnp
from jax import lax
from jax.experimental import pallas as pl
from jax.experimental.pallas import tpu as pltpu

D_MODEL = 2048
N_META = 16
D_LRU = 1024
D_ATT = 1024
LRU_BLOCKS = 16
LRU_BLOCK_DIM = 64
CONV_WIDTH = 4
LRU_C = 8.0
N_HEADS = 8
V_HEAD_DIM = 128
QK_HEAD_DIM = 64
N_BUCKETS = 32
MAX_DISTANCE = 128
N_GROUPS = 4
EXPERTS_PER_GROUP = 8
N_EXPERTS = 32
D_EXPERT = 1024
D_IN_PROJ = 2 * D_LRU + 3 * D_ATT
EPS = 1e-6
LAMBDA_INIT = 0.8 - 0.6 * math.exp(-0.3 * 0)
LOG2_E = math.log2(math.e)

F32 = jnp.float32
BF16 = jnp.bfloat16

LANES = 128
SUBLANES = 8
MXU_DIM = 256
TT = 768
PROJ_TN = 1024
OUT_TM = TT // 2
ATT_SUB = MXU_DIM
ATT_LOOKAHEAD = 4
ATT_GROUP = 4
VT_ROWS = V_HEAD_DIM + 16
LRU_CB = MXU_DIM
DISPATCH_BLK = 256
VMEM_LIMIT = 56 * 1024 * 1024
NEG_BIG = -1e30


def _round_up(a, b):
    return -(-a // b) * b


def _rmsnorm(x, g):
    ms = jnp.mean(x * x, axis=-1, keepdims=True)
    return x * lax.rsqrt(ms + EPS) * g


def _inproj_kernel(n_t, seq, x_hbm, meta_ref, g_ref, w_ref, h0_ref, lru_ref, qkv_ref, xbuf, u_scr, sem):
    m = pl.program_id(0)
    n = pl.program_id(1)
    n_m = pl.num_programs(0)
    last_rows = seq + N_META - (n_t - 1) * TT

    def tile_copy(mm, slot):
        b = mm // n_t
        t = mm % n_t
        row0 = pl.multiple_of(t * TT - N_META, SUBLANES)
        return [
            (t == 0, pltpu.make_async_copy(x_hbm.at[b, pl.ds(0, TT - N_META)],
                                           xbuf.at[slot, pl.ds(N_META, TT - N_META)], sem.at[slot])),
            (jnp.logical_and(t > 0, t < n_t - 1),
             pltpu.make_async_copy(x_hbm.at[b, pl.ds(row0, TT)], xbuf.at[slot], sem.at[slot])),
            (t == n_t - 1, pltpu.make_async_copy(x_hbm.at[b, pl.ds(row0, last_rows)],
                                                 xbuf.at[slot, pl.ds(0, last_rows)], sem.at[slot])),
        ]

    def start(mm, slot):
        for cond, copy in tile_copy(mm, slot):
            @pl.when(cond)
            def _(copy=copy):
                copy.start()

    def wait(mm, slot):
        for cond, copy in tile_copy(mm, slot):
            @pl.when(cond)
            def _(copy=copy):
                copy.wait()

    slot = m % 2

    @pl.when(jnp.logical_and(m == 0, n == 0))
    def _():
        start(m, slot)

    @pl.when(jnp.logical_and(n == 1, m + 1 < n_m))
    def _():
        start(m + 1, 1 - slot)

    @pl.when(n == 0)
    def _():
        wait(m, slot)
        t = m % n_t

        @pl.when(t == 0)
        def _():
            xbuf[slot, 0:N_META, :] = meta_ref[...]

        @pl.when(t == n_t - 1)
        def _():
            xbuf[slot, last_rows:TT, :] = jnp.zeros((TT - last_rows, D_MODEL), F32)

        h0 = xbuf[slot]
        h0_ref[...] = h0
        u_scr[...] = _rmsnorm(h0, g_ref[...]).astype(BF16)

    y = jnp.dot(u_scr[...], w_ref[...], preferred_element_type=F32)

    @pl.when(n < 2)
    def _():
        lru_ref[...] = y

    @pl.when(n >= 2)
    def _():
        qkv_ref[...] = y.astype(BF16)


def _in_proj(x, meta_tokens, g_mix, w_in_bf16, t_pad):
    batch, seq, _ = x.shape
    n_t = t_pad // TT
    n_pad = batch * t_pad
    n_col = D_IN_PROJ // PROJ_TN
    assert N_META % SUBLANES == 0 and 0 < seq + N_META - (n_t - 1) * TT <= TT
    return pl.pallas_call(
        functools.partial(_inproj_kernel, n_t, seq),
        grid=(n_pad // TT, n_col),
        in_specs=[
            pl.BlockSpec(memory_space=pl.ANY),
            pl.BlockSpec((N_META, D_MODEL), lambda m, n: (0, 0)),
            pl.BlockSpec((1, D_MODEL), lambda m, n: (0, 0)),
            pl.BlockSpec((D_MODEL, PROJ_TN), lambda m, n: (0, n)),
        ],
        out_specs=[
            pl.BlockSpec((TT, D_MODEL), lambda m, n: (m, 0)),
            pl.BlockSpec((TT, PROJ_TN), lambda m, n: (m, jnp.minimum(n, 1))),
            pl.BlockSpec((TT, PROJ_TN), lambda m, n: (m, jnp.maximum(n - 2, 0))),
        ],
        out_shape=[
            jax.ShapeDtypeStruct((n_pad, D_MODEL), F32),
            jax.ShapeDtypeStruct((n_pad, 2 * D_LRU), F32),
            jax.ShapeDtypeStruct((n_pad, 3 * D_ATT), BF16),
        ],
        scratch_shapes=[
            pltpu.VMEM((2, TT, D_MODEL), F32),
            pltpu.VMEM((TT, D_MODEL), BF16),
            pltpu.SemaphoreType.DMA((2,)),
        ],
        compiler_params=pltpu.CompilerParams(
            dimension_semantics=("arbitrary", "arbitrary"), vmem_limit_bytes=VMEM_LIMIT),
        name="in_proj",
    )(x, meta_tokens, g_mix, w_in_bf16)


def _lru_kernel(x_ref, gate_ref, cw_ref, cb_ref, wr_ref, br_ref, wi_ref, bi_ref, lam_ref,
                y_ref, xpad_scr, h_scr):
    t = pl.program_id(2)

    @pl.when(t == 0)
    def _():
        xpad_scr[0:SUBLANES, :] = jnp.zeros((SUBLANES, LRU_CB), F32)
        h_scr[...] = jnp.zeros_like(h_scr)

    x = x_ref[...]
    xpad_scr[SUBLANES:, :] = x
    cw = cw_ref[...]
    xc = cb_ref[...] + x * cw[CONV_WIDTH - 1:CONV_WIDTH]
    for k in range(CONV_WIDTH - 1):
        shift = CONV_WIDTH - 1 - k
        xc = xc + xpad_scr[pl.ds(SUBLANES - shift, TT), :] * cw[k:k + 1]
    xpad_scr[0:SUBLANES, :] = x[TT - SUBLANES:TT]

    xb = xc.astype(BF16)
    r = jax.nn.sigmoid(jnp.dot(xb, wr_ref[...], preferred_element_type=F32) + br_ref[...])
    i = jax.nn.sigmoid(jnp.dot(xb, wi_ref[...], preferred_element_type=F32) + bi_ref[...])
    lam = lam_ref[...]
    log_sig = jnp.minimum(lam, 0.0) - jnp.log(1.0 + jnp.exp(-jnp.abs(lam)))
    log_a = (LRU_C * r) * log_sig
    a = jnp.exp(log_a)
    one_m_a2 = 1.0 - a * a
    root = jnp.where(one_m_a2 > 0.0, one_m_a2 * lax.rsqrt(one_m_a2), 0.0)
    b = root * (i * xc)

    row = lax.broadcasted_iota(jnp.int32, (SUBLANES, LRU_CB), 0)
    carry = h_scr[...]
    groups = []
    for g in range(TT // SUBLANES):
        rows = slice(g * SUBLANES, (g + 1) * SUBLANES)
        ag, bg = a[rows], b[rows]
        s = 1
        while s < SUBLANES:
            keep = row >= s
            a_sh = jnp.where(keep, pltpu.roll(ag, s, 0), 1.0)
            b_sh = jnp.where(keep, pltpu.roll(bg, s, 0), 0.0)
            bg = ag * b_sh + bg
            ag = ag * a_sh
            s *= 2
        hg = bg + ag * carry
        groups.append(hg)
        carry = jnp.broadcast_to(hg[SUBLANES - 1:SUBLANES], hg.shape)
    h_scr[...] = carry
    h = jnp.concatenate(groups, axis=0)
    y_ref[...] = (h * jax.nn.gelu(gate_ref[...])).astype(BF16)


def _lru(proj_lru, conv_w, conv_b, wr_bd, b_r, wi_bd, b_i, lru_l, batch, t_pad):
    n_pad = proj_lru.shape[0]
    n_t = t_pad // TT
    n_c = D_LRU // LRU_CB
    vec = lambda: pl.BlockSpec((1, LRU_CB), lambda b, c, t: (0, c))
    return pl.pallas_call(
        _lru_kernel,
        grid=(batch, n_c, n_t),
        in_specs=[
            pl.BlockSpec((TT, LRU_CB), lambda b, c, t: (b * n_t + t, c)),
            pl.BlockSpec((TT, LRU_CB), lambda b, c, t: (b * n_t + t, n_c + c)),
            pl.BlockSpec((CONV_WIDTH, LRU_CB), lambda b, c, t: (0, c)),
            vec(),
            pl.BlockSpec((None, LRU_CB, LRU_CB), lambda b, c, t: (c, 0, 0)),
            vec(),
            pl.BlockSpec((None, LRU_CB, LRU_CB), lambda b, c, t: (c, 0, 0)),
            vec(),
            vec(),
        ],
        out_specs=pl.BlockSpec((TT, LRU_CB), lambda b, c, t: (b * n_t + t, c)),
        out_shape=jax.ShapeDtypeStruct((n_pad, D_LRU), BF16),
        scratch_shapes=[pltpu.VMEM((SUBLANES + TT, LRU_CB), F32), pltpu.VMEM((SUBLANES, LRU_CB), F32)],
        compiler_params=pltpu.CompilerParams(
            dimension_semantics=("arbitrary", "arbitrary", "arbitrary"),
            vmem_limit_bytes=VMEM_LIMIT),
        name="rglru",
    )(proj_lru, proj_lru, conv_w, conv_b, wr_bd, b_r, wi_bd, b_i, lru_l)


def _bucket_thresholds():
    max_exact = N_BUCKETS // 2
    n = np.arange(0, MAX_DISTANCE + 1)
    nf = np.maximum(n, 1).astype(np.float32)
    large = max_exact + (np.log(nf / np.float32(max_exact)) / np.float32(math.log(MAX_DISTANCE / max_exact))
                         * np.float32(N_BUCKETS - max_exact)).astype(np.int32)
    large = np.minimum(large, N_BUCKETS - 1)
    bucket = np.where(n < max_exact, n, large)
    return [int(np.argmax(bucket >= k)) for k in range(1, N_BUCKETS)]


_BUCKET_THR = _bucket_thresholds()


def _attn_kernel(rb_ref, q_ref, k_ref, v_ref, lam_ref, g_ref, o_ref, bias_scr, vt_scr, m_scr, acc_scr):
    h = pl.program_id(0)
    b = pl.program_id(1)
    qi = pl.program_id(2)
    tk = TT
    n_kv = vt_scr.shape[0]

    @pl.when(jnp.logical_and(b == 0, qi == 0))
    def _():
        k_pos = lax.broadcasted_iota(jnp.int32, (tk, tk), 0)
        q_pos = lax.broadcasted_iota(jnp.int32, (tk, tk), 1)
        far = rb_ref[N_BUCKETS - 1, h]
        for d in range(2):
            rel = q_pos - k_pos + d * tk
            val = jnp.full((tk, tk), rb_ref[0, h] - far, F32)
            for kk, thr in enumerate(_BUCKET_THR):
                val = jnp.where(rel >= thr, rb_ref[kk + 1, h] - far, val)
            val = val * LOG2_E
            if d == 0:
                val = jnp.where(rel >= 0, val, -jnp.inf)
            bias_scr[d] = val

    @pl.when(qi == 0)
    def _():
        for t in range(n_kv):
            vt_scr[t, 0:V_HEAD_DIM, :] = v_ref[t * tk:(t + 1) * tk, :].astype(F32).T.astype(BF16)
            vt_scr[t, V_HEAD_DIM:VT_ROWS, :] = jnp.ones((VT_ROWS - V_HEAD_DIM, tk), BF16)

    q = q_ref[...]
    lane = lax.broadcasted_iota(jnp.int32, q.shape, 1)
    scale = (QK_HEAD_DIM ** -0.5) * LOG2_E
    qf = q.astype(F32) * scale
    qs = jnp.concatenate([jnp.where(lane < QK_HEAD_DIM, qf, 0.0),
                          jnp.where(lane >= QK_HEAD_DIM, qf, 0.0)], axis=0).astype(BF16)

    m_scr[...] = jnp.full(m_scr.shape, NEG_BIG, F32)
    acc_scr[...] = jnp.zeros_like(acc_scr)

    def tile_group(tiles):
        n_sub = 2 * tk // ATT_SUB
        stages = [(t, c) for t in range(len(tiles)) for c in range(n_sub)]

        def n_keys(t, c):
            return (c * ATT_SUB) % tk + ATT_SUB if tiles[t][1] == 0 else tk

        def scores(t, c):
            j, bias_idx = tiles[t]
            nk = n_keys(t, c)
            kt = k_ref[pl.ds(pl.multiple_of(j * tk, tk), nk), :]
            cols = slice(c * ATT_SUB, (c + 1) * ATT_SUB)
            s = lax.dot_general(kt, qs[cols], (((1,), (1,)), ((), ())), preferred_element_type=F32)
            if bias_idx is not None:
                q0 = (c * ATT_SUB) % tk
                s = s + bias_scr[bias_idx, 0:nk, q0:q0 + ATT_SUB]
            return s

        def compact(t):
            return tiles[t][1] != 0

        def produce(t, c):
            s = scores(t, c)
            if not compact(t):
                return s
            cols = slice(c * ATT_SUB, (c + 1) * ATT_SUB)
            m_prev = m_scr[:, cols]
            m_new = jnp.maximum(m_prev, jnp.max(s, axis=0, keepdims=True))
            m_scr[:, cols] = m_new
            return (s - m_prev).astype(BF16), m_prev, m_new

        pending = [produce(*st) for st in stages[:ATT_LOOKAHEAD]]
        for n, (t, c) in enumerate(stages):
            cols = slice(c * ATT_SUB, (c + 1) * ATT_SUB)
            item = pending.pop(0)
            if n + ATT_LOOKAHEAD < len(stages):
                pending.append(produce(*stages[n + ATT_LOOKAHEAD]))
            if compact(t):
                d, m_prev, m_new = item
                p = jnp.exp2(d - (m_new - m_prev).astype(BF16))
            else:
                m_prev = m_scr[:, cols]
                m_new = jnp.maximum(m_prev, jnp.max(item, axis=0, keepdims=True))
                m_scr[:, cols] = m_new
                p = jnp.exp2(item - m_new).astype(BF16)
            alpha = jnp.exp2(m_prev - m_new)
            vt = vt_scr[tiles[t][0], :, 0:n_keys(t, c)]
            acc_scr[:, cols] = alpha * acc_scr[:, cols] + jnp.dot(vt, p, preferred_element_type=F32)

    @pl.when(qi >= 1)
    def _():
        tile_group([(qi, 0), (qi - 1, 1)])

    @pl.when(qi == 0)
    def _():
        tile_group([(qi, 0)])

    n_far = jnp.maximum(qi - 1, 0)

    def far_group(i, carry):
        tile_group([(ATT_GROUP * i + t, None) for t in range(ATT_GROUP)])
        return carry

    lax.fori_loop(0, n_far // ATT_GROUP, far_group, 0)
    done = n_far - n_far % ATT_GROUP
    piece = ATT_GROUP // 2
    while piece >= 1:
        @pl.when((n_far % (2 * piece)) >= piece)
        def _(done=done, piece=piece):
            tile_group([(done + t, None) for t in range(piece)])
        done = done + jnp.where((n_far % (2 * piece)) >= piece, piece, 0)
        piece //= 2

    lam_p = lam_ref[...]
    lam = (jnp.exp(jnp.sum(lam_p[0:1] * lam_p[1:2], axis=-1, keepdims=True))
           - jnp.exp(jnp.sum(lam_p[2:3] * lam_p[3:4], axis=-1, keepdims=True)) + LAMBDA_INIT)
    inv = 1.0 / acc_scr[V_HEAD_DIM:V_HEAD_DIM + 1, :]
    o_t = (acc_scr[0:V_HEAD_DIM, 0:tk] * inv[:, :tk]
           - acc_scr[0:V_HEAD_DIM, tk:] * (lam * inv[:, tk:]))
    ms = jnp.mean(o_t * o_t, axis=0, keepdims=True)
    o_n = (o_t * lax.rsqrt(ms + EPS)).T
    o_ref[...] = (o_n * g_ref[...] * (1.0 - LAMBDA_INIT)).astype(BF16)


def _attention(qkv, rel_bias, lam_params, subln_g, batch, t_pad):
    n_q = t_pad // TT
    qkv3 = qkv.reshape(batch, t_pad, 3 * D_ATT)
    out = pl.pallas_call(
        _attn_kernel,
        grid_spec=pltpu.PrefetchScalarGridSpec(
            num_scalar_prefetch=0,
            grid=(N_HEADS, batch, n_q),
            in_specs=[
                pl.BlockSpec(memory_space=pltpu.SMEM),
                pl.BlockSpec((None, TT, V_HEAD_DIM), lambda h, b, q: (b, q, h)),
                pl.BlockSpec((None, t_pad, V_HEAD_DIM), lambda h, b, q: (b, 0, N_HEADS + h)),
                pl.BlockSpec((None, t_pad, V_HEAD_DIM), lambda h, b, q: (b, 0, 2 * N_HEADS + h)),
                pl.BlockSpec((4, QK_HEAD_DIM), lambda h, b, q: (0, 0)),
                pl.BlockSpec((1, V_HEAD_DIM), lambda h, b, q: (0, 0)),
            ],
            out_specs=pl.BlockSpec((None, TT, V_HEAD_DIM), lambda h, b, q: (b, q, h)),
            scratch_shapes=[
                pltpu.VMEM((2, TT, TT), F32),
                pltpu.VMEM((n_q, VT_ROWS, TT), BF16),
                pltpu.VMEM((1, 2 * TT), F32),
                pltpu.VMEM((VT_ROWS, 2 * TT), F32),
            ],
        ),
        out_shape=jax.ShapeDtypeStruct((batch, t_pad, D_ATT), BF16),
        compiler_params=pltpu.CompilerParams(
            dimension_semantics=("arbitrary", "arbitrary", "arbitrary"),
            vmem_limit_bytes=VMEM_LIMIT),
        name="diff_attention",
    )(rel_bias, qkv3, qkv3, qkv3, lam_params, subln_g)
    return out.reshape(batch * t_pad, D_ATT)


def _route_tile(lg, carry):
    rows = lg.shape[0]
    lane = lax.broadcasted_iota(jnp.int32, lg.shape, 1)
    first = lambda mask: jnp.min(jnp.where(mask, lane, LANES), axis=-1, keepdims=True)

    is_g = lane < N_GROUPS
    gl = jnp.where(is_g, lg, -jnp.inf)
    gmax = jnp.max(gl, axis=-1, keepdims=True)
    grp = first(gl == gmax)
    gsum = jnp.sum(jnp.where(is_g, jnp.exp(gl - gmax), 0.0), axis=-1, keepdims=True)
    p_grp = 1.0 / gsum

    lane_e = lane - N_GROUPS
    in_grp = (lane_e >= 0) & (lane_e < N_EXPERTS) & ((lane_e // EXPERTS_PER_GROUP) == grp)
    el = jnp.where(in_grp, lg, -jnp.inf)
    v1 = jnp.max(el, axis=-1, keepdims=True)
    i1 = first(el == v1)
    el2 = jnp.where(lane == i1, -jnp.inf, el)
    v2 = jnp.max(el2, axis=-1, keepdims=True)
    i2 = first(el2 == v2)
    e2 = jnp.exp(v2 - v1)
    den = 1.0 + e2
    g0 = p_grp * (1.0 / den)
    g1 = p_grp * (e2 / den)

    hit1 = lane == i1
    hit2 = lane == i2
    onehot = jnp.where(hit1 | hit2, 1.0, 0.0)
    rr = lax.broadcasted_iota(jnp.int32, (rows, rows), 0)
    cc = lax.broadcasted_iota(jnp.int32, (rows, rows), 1)
    tri = jnp.where(cc < rr, 1.0, 0.0).astype(BF16)
    rank = jnp.dot(tri, onehot.astype(BF16), preferred_element_type=F32) + carry
    r0 = jnp.sum(jnp.where(hit1, rank, 0.0), axis=-1, keepdims=True).astype(jnp.int32)
    r1 = jnp.sum(jnp.where(hit2, rank, 0.0), axis=-1, keepdims=True).astype(jnp.int32)

    idx = jnp.where(lane == 0, i1 - N_GROUPS,
                    jnp.where(lane == 1, i2 - N_GROUPS,
                              jnp.where(lane == 2, r0, jnp.where(lane == 3, r1, 0))))
    gates = jnp.where(lane == 0, g0, jnp.where(lane == 1, g1, 0.0))
    return idx, gates, carry + jnp.sum(onehot, axis=0, keepdims=True)


def _outproj_kernel(yl_ref, ya_ref, h0_ref, w_ref, g_ref, wrt_ref, brt_ref, h1_ref, u_ref, lg_ref):
    acc = jnp.dot(yl_ref[...], w_ref[0:D_LRU, :], preferred_element_type=F32)
    acc = acc + jnp.dot(ya_ref[...], w_ref[D_LRU:, :], preferred_element_type=F32)
    h1 = h0_ref[...] + acc
    h1_ref[...] = h1
    u = _rmsnorm(h1, g_ref[...])
    u_ref[...] = u
    u_hi = u.astype(BF16)
    u_lo = (u - u_hi.astype(F32)).astype(BF16)
    w = wrt_ref[...]
    w_hi = w.astype(BF16)
    w_lo = (w - w_hi.astype(F32)).astype(BF16)
    parts = jnp.dot(jnp.concatenate([u_hi, u_lo], axis=0), jnp.concatenate([w_hi, w_lo], axis=1),
                    preferred_element_type=F32)
    lg = (parts[:OUT_TM, :LANES] + parts[:OUT_TM, LANES:]) + (parts[OUT_TM:, :LANES] + parts[OUT_TM:, LANES:])
    lg_ref[...] = lg + brt_ref[...]


def _out_proj(y_lru, y_att, h0, w_out_bf16, g_ffn, w_rt, b_rt):
    n_pad = h0.shape[0]
    row = lambda w: pl.BlockSpec((OUT_TM, w), lambda m: (m, 0))
    full = lambda a, b: pl.BlockSpec((a, b), lambda m: (0, 0))
    return pl.pallas_call(
        _outproj_kernel,
        grid=(n_pad // OUT_TM,),
        in_specs=[row(D_LRU), row(D_ATT), row(D_MODEL), full(D_MODEL, D_MODEL), full(1, D_MODEL),
                  full(D_MODEL, LANES), full(1, LANES)],
        out_specs=[row(D_MODEL), row(D_MODEL), row(LANES)],
        out_shape=[
            jax.ShapeDtypeStruct((n_pad, D_MODEL), F32),
            jax.ShapeDtypeStruct((n_pad, D_MODEL), F32),
            jax.ShapeDtypeStruct((n_pad, LANES), F32),
        ],
        compiler_params=pltpu.CompilerParams(
            dimension_semantics=("arbitrary",), vmem_limit_bytes=VMEM_LIMIT),
        name="out_proj",
    )(y_lru, y_att, h0, w_out_bf16, g_ffn, w_rt, b_rt)


def _route_kernel(lg_ref, idx_ref, gate_ref, cnt_ref, carry_scr):
    @pl.when(pl.program_id(0) == 0)
    def _():
        carry_scr[...] = jnp.zeros_like(carry_scr)

    idx, gates, carry = _route_tile(lg_ref[...], carry_scr[...])
    idx_ref[...] = idx
    gate_ref[...] = gates
    carry_scr[...] = carry
    cnt_ref[...] = carry


def _route(logits):
    n_pad = logits.shape[0]
    row = pl.BlockSpec((TT, LANES), lambda m: (m, 0))
    return pl.pallas_call(
        _route_kernel,
        grid=(n_pad // TT,),
        in_specs=[row],
        out_specs=[row, row, pl.BlockSpec((1, LANES), lambda m: (0, 0))],
        out_shape=[
            jax.ShapeDtypeStruct((n_pad, LANES), jnp.int32),
            jax.ShapeDtypeStruct((n_pad, LANES), F32),
            jax.ShapeDtypeStruct((1, LANES), F32),
        ],
        scratch_shapes=[pltpu.VMEM((1, LANES), F32)],
        compiler_params=pltpu.CompilerParams(
            dimension_semantics=("arbitrary",), vmem_limit_bytes=VMEM_LIMIT),
        name="route",
    )(logits)


def _dest_kernel(idx_ref, ps_ref, o_ref):
    idx = idx_ref[...]
    lane = lax.broadcasted_iota(jnp.int32, idx.shape, 1)
    ps = ps_ref[...]
    pick = lambda e: jnp.sum(jnp.where(lane == e, ps, 0.0), axis=-1, keepdims=True).astype(jnp.int32)
    d0 = pick(idx[:, 0:1]) + idx[:, 2:3]
    d1 = pick(idx[:, 1:2]) + idx[:, 3:4]
    o_ref[...] = jnp.where(lane == 0, d0, jnp.where(lane == 1, d1, 0))


def _dest(idx, pstart_row):
    n_pad = idx.shape[0]
    row = pl.BlockSpec((TT, LANES), lambda m: (m, 0))
    return pl.pallas_call(
        _dest_kernel,
        grid=(n_pad // TT,),
        in_specs=[row, pl.BlockSpec((1, LANES), lambda m: (0, 0))],
        out_specs=row,
        out_shape=jax.ShapeDtypeStruct((n_pad, LANES), jnp.int32),
        compiler_params=pltpu.CompilerParams(
            dimension_semantics=("arbitrary",), vmem_limit_bytes=VMEM_LIMIT),
        name="dest",
    )(idx, pstart_row)


ROW_UNROLL = 8


def _row_copy(src, s_row, dst, d_row, sem):
    return pltpu.make_async_copy(src.at[pl.ds(s_row, 1)], dst.at[pl.ds(d_row, 1)], sem)


def _dispatch_kernel(d0_ref, d1_ref, pe_ref, u_ref, xb_ref, zero_scr, sem, zsem):
    step = pl.program_id(0)
    base = step * TT

    def zero_block(e):
        start = pl.multiple_of(pe_ref[e] - DISPATCH_BLK, DISPATCH_BLK)
        return pltpu.make_async_copy(zero_scr, xb_ref.at[pl.ds(start, DISPATCH_BLK)], zsem)

    @pl.when(step == 0)
    def _():
        zero_scr[...] = jnp.zeros_like(zero_scr)

        def nonempty(e):
            return pe_ref[e] > jnp.where(e == 0, 0, pe_ref[jnp.maximum(e - 1, 0)])

        def zstart(e, carry):
            @pl.when(nonempty(e))
            def _():
                zero_block(e).start()
            return carry

        def zwait(e, carry):
            @pl.when(nonempty(e))
            def _():
                zero_block(e).wait()
            return carry

        lax.fori_loop(0, N_EXPERTS, zstart, 0)
        lax.fori_loop(0, N_EXPERTS, zwait, 0)

        def tail_block(i):
            start = pl.multiple_of(i * DISPATCH_BLK, DISPATCH_BLK)
            return pltpu.make_async_copy(zero_scr, xb_ref.at[pl.ds(start, DISPATCH_BLK)], zsem)

        def tstart(i, carry):
            tail_block(i).start()
            return carry

        def twait(i, carry):
            tail_block(i).wait()
            return carry

        n_used = pe_ref[N_EXPERTS - 1] // DISPATCH_BLK
        n_blocks = xb_ref.shape[0] // DISPATCH_BLK
        lax.fori_loop(n_used, n_blocks, tstart, 0)
        lax.fori_loop(n_used, n_blocks, twait, 0)

    def issue(r, carry):
        _row_copy(u_ref, r, xb_ref, d0_ref[base + r], sem.at[0]).start()
        _row_copy(u_ref, r, xb_ref, d1_ref[base + r], sem.at[1]).start()
        return carry

    lax.fori_loop(0, TT, issue, 0, unroll=ROW_UNROLL)
    for k in range(2):
        pltpu.make_async_copy(u_ref, xb_ref.at[pl.ds(0, TT)], sem.at[k]).wait()


def _dispatch(d0, d1, pends, u2, rows):
    n_pad = u2.shape[0]
    return pl.pallas_call(
        _dispatch_kernel,
        grid_spec=pltpu.PrefetchScalarGridSpec(
            num_scalar_prefetch=3,
            grid=(n_pad // TT,),
            in_specs=[pl.BlockSpec((TT, D_MODEL), lambda m, d0, d1, pe: (m, 0))],
            out_specs=pl.BlockSpec(memory_space=pl.ANY),
            scratch_shapes=[
                pltpu.VMEM((DISPATCH_BLK, D_MODEL), F32),
                pltpu.SemaphoreType.DMA((2,)),
                pltpu.SemaphoreType.DMA(()),
            ],
        ),
        out_shape=jax.ShapeDtypeStruct((rows, D_MODEL), F32),
        compiler_params=pltpu.CompilerParams(
            dimension_semantics=("arbitrary",), vmem_limit_bytes=VMEM_LIMIT),
        name="dispatch",
    )(d0, d1, pends, u2)


WEIGHT_UNITS = 8
WEIGHT_DEPTH = 4


def _stream_weights(i, tgt_ref, eseq_ref, ntot_ref, w_hbm, stages, wbfs, sem, done_scr):
    def slab(u, k):
        rows = stages[k].shape[1]
        return pl.ds(pl.multiple_of((u % WEIGHT_UNITS) * rows, rows), rows)

    def unit_copies(u):
        e = eseq_ref[u // WEIGHT_UNITS]
        b = u % WEIGHT_DEPTH
        return [pltpu.make_async_copy(w.at[e, slab(u, k), :], stages[k].at[b], sem.at[b, k])
                for k, w in enumerate(w_hbm)]

    @pl.when(i == 0)
    def _():
        done_scr[0] = 0
        for u in range(WEIGHT_DEPTH):
            for c in unit_copies(u):
                c.start()

    def body(u, carry):
        for c in unit_copies(u):
            c.wait()
        slot = (u // WEIGHT_UNITS) % 2
        b = u % WEIGHT_DEPTH
        for k in range(len(w_hbm)):
            wbfs[k][slot, slab(u, k), :] = stages[k][b].astype(BF16)

        @pl.when(u + WEIGHT_DEPTH < ntot_ref[0])
        def _():
            for c in unit_copies(u + WEIGHT_DEPTH):
                c.start()

        return carry

    lax.fori_loop(done_scr[0], tgt_ref[i], body, 0)
    done_scr[0] = tgt_ref[i]


def _expert_kernel(tgt_ref, eseq_ref, ntot_ref, slot_ref, nu_ref, x_ref, wg_hbm, wu_hbm, wd_hbm, y_ref,
                   sg, su, sd, bg, bu, bd, sem, done_scr):
    i = pl.program_id(0)
    _stream_weights(i, tgt_ref, eseq_ref, ntot_ref, (wg_hbm, wu_hbm, wd_hbm), (sg, su, sd), (bg, bu, bd),
                    sem, done_scr)
    used = i < nu_ref[0]

    @pl.when(used)
    def _():
        slot = slot_ref[i]
        x = x_ref[...].astype(BF16)
        g = jnp.dot(x, bg[slot], preferred_element_type=F32)
        u = jnp.dot(x, bu[slot], preferred_element_type=F32)
        a = (jax.nn.silu(g) * u).astype(BF16)
        y_ref[...] = jnp.dot(a, bd[slot], preferred_element_type=F32)

    @pl.when(jnp.logical_not(used))
    def _():
        y_ref[...] = jnp.zeros_like(y_ref)


def _experts(sched, x_buf, w_gate, w_up, w_down):
    rows = x_buf.shape[0]
    n_blocks = rows // DISPATCH_BLK
    blk_map = lambda i, tg, es, nt, sl, nu: (jnp.minimum(i, jnp.maximum(nu[0] - 1, 0)), 0)
    out_map = lambda i, tg, es, nt, sl, nu: (i, 0)
    hbm = pl.BlockSpec(memory_space=pl.ANY)
    stage = lambda k_dim, n_dim: pltpu.VMEM((WEIGHT_DEPTH, k_dim // WEIGHT_UNITS, n_dim), F32)
    resident = lambda k_dim, n_dim: pltpu.VMEM((2, k_dim, n_dim), BF16)
    return pl.pallas_call(
        _expert_kernel,
        grid_spec=pltpu.PrefetchScalarGridSpec(
            num_scalar_prefetch=5,
            grid=(n_blocks,),
            in_specs=[pl.BlockSpec((DISPATCH_BLK, D_MODEL), blk_map), hbm, hbm, hbm],
            out_specs=pl.BlockSpec((DISPATCH_BLK, D_MODEL), out_map),
            scratch_shapes=[
                stage(D_MODEL, D_EXPERT), stage(D_MODEL, D_EXPERT), stage(D_EXPERT, D_MODEL),
                resident(D_MODEL, D_EXPERT), resident(D_MODEL, D_EXPERT), resident(D_EXPERT, D_MODEL),
                pltpu.SemaphoreType.DMA((WEIGHT_DEPTH, 3)),
                pltpu.SMEM((1,), jnp.int32),
            ],
        ),
        out_shape=jax.ShapeDtypeStruct((rows, D_MODEL), F32),
        compiler_params=pltpu.CompilerParams(
            dimension_semantics=("arbitrary",), vmem_limit_bytes=VMEM_LIMIT),
        name="experts",
    )(*sched, x_buf, w_gate, w_up, w_down)


def _expert_schedule(padded, pstarts, n_blocks):
    nonempty = padded > 0
    seq_of_expert = jnp.cumsum(nonempty.astype(jnp.int32)) - 1
    n_seq = jnp.sum(nonempty.astype(jnp.int32))
    experts = jnp.arange(N_EXPERTS, dtype=jnp.int32)
    is_kth = jnp.logical_and(nonempty[None, :], seq_of_expert[None, :] == experts[:, None])
    eseq = jnp.sum(jnp.where(is_kth, experts[None, :], 0), axis=1)
    pends = pstarts + padded
    n_used = pends[-1] // DISPATCH_BLK
    blk = jnp.arange(n_blocks, dtype=jnp.int32)
    blk_c = jnp.minimum(blk, jnp.maximum(n_used - 1, 0))
    blk_expert = jnp.minimum(
        jnp.sum((pends[None, :] <= (blk_c * DISPATCH_BLK)[:, None]).astype(jnp.int32), axis=1), N_EXPERTS - 1)
    onehot = (blk_expert[:, None] == experts[None, :]).astype(jnp.int32)
    pick = lambda v: jnp.sum(onehot * v[None, :], axis=1)
    q = pick(seq_of_expert)
    j = blk_c - pick(pstarts) // DISPATCH_BLK
    n = jnp.maximum(pick(padded) // DISPATCH_BLK, 1)
    n_total = n_seq * WEIGHT_UNITS
    tgt = jnp.minimum(WEIGHT_UNITS * (q + 1) + (WEIGHT_UNITS * j) // n, n_total)
    tgt = jnp.where(blk < n_used, tgt, n_total)
    i32 = lambda a: a.astype(jnp.int32)
    return i32(tgt), eseq, i32(n_total)[None], i32(q % 2), i32(n_used)[None]


COMBINE_TM = 512


def _combine_kernel(t_pad, d0_ref, d1_ref, h_hbm, gate_hbm, g_ref, yb_ref, o_ref,
                    h_scr, gate_scr, y0_scr, y1_scr, sem):
    n_j = pl.num_programs(1)
    step = pl.program_id(0) * n_j + pl.program_id(1)
    n_steps = pl.num_programs(0) * n_j

    def base_row(s):
        return pl.multiple_of((s // n_j) * t_pad + N_META + (s % n_j) * COMBINE_TM, 8)

    def tile_copies(s, slot):
        base = base_row(s)
        return [
            pltpu.make_async_copy(h_hbm.at[pl.ds(base, COMBINE_TM)], h_scr.at[slot], sem.at[slot, 2]),
            pltpu.make_async_copy(gate_hbm.at[pl.ds(base, COMBINE_TM)], gate_scr.at[slot], sem.at[slot, 3]),
        ]

    def fetch(s, slot):
        base = base_row(s)
        for c in tile_copies(s, slot):
            c.start()

        def issue(r, carry):
            _row_copy(yb_ref, d0_ref[base + r], y0_scr.at[slot], r, sem.at[slot, 0]).start()
            _row_copy(yb_ref, d1_ref[base + r], y1_scr.at[slot], r, sem.at[slot, 1]).start()
            return carry

        lax.fori_loop(0, COMBINE_TM, issue, 0, unroll=ROW_UNROLL)

    slot = step % 2

    @pl.when(step == 0)
    def _():
        fetch(step, slot)

    @pl.when(step + 1 < n_steps)
    def _():
        fetch(step + 1, 1 - slot)

    pltpu.make_async_copy(yb_ref.at[pl.ds(0, COMBINE_TM)], y0_scr.at[slot], sem.at[slot, 0]).wait()
    pltpu.make_async_copy(yb_ref.at[pl.ds(0, COMBINE_TM)], y1_scr.at[slot], sem.at[slot, 1]).wait()
    for c in tile_copies(step, slot):
        c.wait()

    gate = gate_scr[slot]
    h = h_scr[slot] + gate[:, 0:1] * y0_scr[slot] + gate[:, 1:2] * y1_scr[slot]
    o_ref[...] = _rmsnorm(h, g_ref[...])


def _combine(d0, d1, h1, gates, g_final, y_buf, batch, seq):
    assert seq % COMBINE_TM == 0 and N_META % 8 == 0
    tile = lambda w: pltpu.VMEM((2, COMBINE_TM, w), F32)
    return pl.pallas_call(
        functools.partial(_combine_kernel, h1.shape[0] // batch),
        grid_spec=pltpu.PrefetchScalarGridSpec(
            num_scalar_prefetch=2,
            grid=(batch, seq // COMBINE_TM),
            in_specs=[
                pl.BlockSpec(memory_space=pl.ANY),
                pl.BlockSpec(memory_space=pl.ANY),
                pl.BlockSpec((1, D_MODEL), lambda b, j, d0, d1: (0, 0)),
                pl.BlockSpec(memory_space=pl.ANY),
            ],
            out_specs=pl.BlockSpec((None, COMBINE_TM, D_MODEL), lambda b, j, d0, d1: (b, j, 0)),
            scratch_shapes=[tile(D_MODEL), tile(LANES), tile(D_MODEL), tile(D_MODEL),
                            pltpu.SemaphoreType.DMA((2, 4))],
        ),
        out_shape=jax.ShapeDtypeStruct((batch, seq, D_MODEL), F32),
        compiler_params=pltpu.CompilerParams(
            dimension_semantics=("arbitrary", "arbitrary"), vmem_limit_bytes=VMEM_LIMIT),
        name="combine",
    )(d0, d1, h1, gates, g_final, y_buf)


def _block_diag(w):
    per = LRU_CB // LRU_BLOCK_DIM
    w4 = w.reshape(LRU_BLOCKS // per, per, LRU_BLOCK_DIM, LRU_BLOCK_DIM)
    eye = jnp.eye(per, dtype=w.dtype)
    bd = jnp.einsum('cpij,pq->cpiqj', w4, eye)
    return bd.reshape(LRU_BLOCKS // per, LRU_CB, LRU_CB)


def kernel(x, meta_tokens, g_mix, w_in, conv_w, conv_b, w_rgate, b_rgate, w_igate, b_igate, lru_L, lambda_q1, lambda_k1, lambda_q2, lambda_k2, subln_g, rel_bias, w_out, g_ffn, w_group, b_group, w_router, b_router, w_gate, w_up, w_down, g_final):
    batch, seq, _ = x.shape
    t_real = N_META + seq
    t_pad = _round_up(t_real, TT)
    n_pad = batch * t_pad

    h0, proj_lru, proj_qkv = _in_proj(x, meta_tokens.astype(x.dtype), g_mix[0][None],
                                      w_in[0].astype(BF16), t_pad)

    y_lru = _lru(proj_lru, conv_w[0], conv_b[0][None],
                 _block_diag(w_rgate[0]).astype(BF16), b_rgate[0].reshape(1, D_LRU),
                 _block_diag(w_igate[0]).astype(BF16), b_igate[0].reshape(1, D_LRU),
                 lru_L[0][None], batch, t_pad)

    lam_params = jnp.stack([lambda_q1[0], lambda_k1[0], lambda_q2[0], lambda_k2[0]])
    y_att = _attention(proj_qkv, rel_bias, lam_params, subln_g[0][None], batch, t_pad)

    w_rt = jnp.concatenate([w_group[0], w_router[0],
                            jnp.zeros((D_MODEL, LANES - N_GROUPS - N_EXPERTS), F32)], axis=1)
    b_rt = jnp.concatenate([b_group[0], b_router[0],
                            jnp.zeros((LANES - N_GROUPS - N_EXPERTS,), F32)])[None]
    h1, u2, logits = _out_proj(y_lru, y_att, h0, w_out[0].astype(BF16), g_ffn[0][None], w_rt, b_rt)

    idx, gates, counts = _route(logits)

    cnt = counts[0, N_GROUPS:N_GROUPS + N_EXPERTS].astype(jnp.int32)
    padded = (cnt + DISPATCH_BLK - 1) // DISPATCH_BLK * DISPATCH_BLK
    pends = jnp.cumsum(padded)
    pstarts = pends - padded
    dest = _dest(idx, jnp.pad(pstarts.astype(F32), (0, LANES - N_EXPERTS))[None])
    d0 = dest[:, 0]
    d1 = dest[:, 1]
    n_blocks = -(-(2 * n_pad + N_EXPERTS * (DISPATCH_BLK - 1)) // DISPATCH_BLK)
    rows = n_blocks * DISPATCH_BLK
    sched = _expert_schedule(padded, pstarts, n_blocks)

    x_buf = _dispatch(d0, d1, pends.astype(jnp.int32), u2, rows)
    y_buf = _experts(sched, x_buf, w_gate[0], w_up[0], w_down[0])
    return _combine(d0, d1, h1, gates, g_final[None], y_buf, batch, seq)
```

```python
import functools
import math

import numpy as np
import jax
import jax.numpy as jnp
from jax import lax
from jax.experimental import pallas as pl
from jax.experimental.pallas import tpu as pltpu

D_MODEL = 2048
N_META = 16
D_LRU = 1024
D_ATT = 1024
LRU_BLOCKS = 16
LRU_BLOCK_DIM = 64
CONV_WIDTH = 4
LRU_C = 8.0
N_HEADS = 8
V_HEAD_DIM = 128
QK_HEAD_DIM = 64
N_BUCKETS = 32
MAX_DISTANCE = 128
N_GROUPS = 4
EXPERTS_PER_GROUP = 8
N_EXPERTS = 32
D_EXPERT = 1024
D_IN_PROJ = 2 * D_LRU + 3 * D_ATT
EPS = 1e-6
LAMBDA_INIT = 0.8 - 0.6 * math.exp(-0.3 * 0)
LOG2_E = math.log2(math.e)

F32 = jnp.float32
BF16 = jnp.bfloat16

LANES = 128
SUBLANES = 8
MXU_DIM = 256
TT = 768
PROJ_TN = 1024
OUT_TM = TT // 2
ATT_SUB = MXU_DIM
ATT_LOOKAHEAD = 4
ATT_GROUP = 4
VT_ROWS = V_HEAD_DIM + 16
LRU_CB = MXU_DIM
DISPATCH_BLK = 256
VMEM_LIMIT = 56 * 1024 * 1024
NEG_BIG = -1e30


def _round_up(a, b):
    return -(-a // b) * b


def _rmsnorm(x, g):
    ms = jnp.mean(x * x, axis=-1, keepdims=True)
    return x * lax.rsqrt(ms + EPS) * g


def _inproj_kernel(n_t, seq, x_hbm, meta_ref, g_ref, w_ref, h0_ref, lru_ref, qkv_ref, xbuf, u_scr, sem):
    m = pl.program_id(0)
    n = pl.program_id(1)
    n_m = pl.num_programs(0)
    last_rows = seq + N_META - (n_t - 1) * TT

    def tile_copy(mm, slot):
        b = mm // n_t
        t = mm % n_t
        row0 = pl.multiple_of(t * TT - N_META, SUBLANES)
        return [
            (t == 0, pltpu.make_async_copy(x_hbm.at[b, pl.ds(0, TT - N_META)],
                                           xbuf.at[slot, pl.ds(N_META, TT - N_META)], sem.at[slot])),
            (jnp.logical_and(t > 0, t < n_t - 1),
             pltpu.make_async_copy(x_hbm.at[b, pl.ds(row0, TT)], xbuf.at[slot], sem.at[slot])),
            (t == n_t - 1, pltpu.make_async_copy(x_hbm.at[b, pl.ds(row0, last_rows)],
                                                 xbuf.at[slot, pl.ds(0, last_rows)], sem.at[slot])),
        ]

    def start(mm, slot):
        for cond, copy in tile_copy(mm, slot):
            @pl.when(cond)
            def _(copy=copy):
                copy.start()

    def wait(mm, slot):
        for cond, copy in tile_copy(mm, slot):
            @pl.when(cond)
            def _(copy=copy):
                copy.wait()

    slot = m % 2

    @pl.when(jnp.logical_and(m == 0, n == 0))
    def _():
        start(m, slot)

    @pl.when(jnp.logical_and(n == 1, m + 1 < n_m))
    def _():
        start(m + 1, 1 - slot)

    @pl.when(n == 0)
    def _():
        wait(m, slot)
        t = m % n_t

        @pl.when(t == 0)
        def _():
            xbuf[slot, 0:N_META, :] = meta_ref[...]

        @pl.when(t == n_t - 1)
        def _():
            xbuf[slot, last_rows:TT, :] = jnp.zeros((TT - last_rows, D_MODEL), F32)

        h0 = xbuf[slot]
        h0_ref[...] = h0
        u_scr[...] = _rmsnorm(h0, g_ref[...]).astype(BF16)

    y = jnp.dot(u_scr[...], w_ref[...], preferred_element_type=F32)

    @pl.when(n < 2)
    def _():
        lru_ref[...] = y

    @pl.when(n >= 2)
    def _():
        qkv_ref[...] = y.astype(BF16)


def _in_proj(x, meta_tokens, g_mix, w_in_bf16, t_pad):
    batch, seq, _ = x.shape
    n_t = t_pad // TT
    n_pad = batch * t_pad
    n_col = D_IN_PROJ // PROJ_TN
    assert N_META % SUBLANES == 0 and 0 < seq + N_META - (n_t - 1) * TT <= TT
    return pl.pallas_call(
        functools.partial(_inproj_kernel, n_t, seq),
        grid=(n_pad // TT, n_col),
        in_specs=[
            pl.BlockSpec(memory_space=pl.ANY),
            pl.BlockSpec((N_META, D_MODEL), lambda m, n: (0, 0)),
            pl.BlockSpec((1, D_MODEL), lambda m, n: (0, 0)),
            pl.BlockSpec((D_MODEL, PROJ_TN), lambda m, n: (0, n)),
        ],
        out_specs=[
            pl.BlockSpec((TT, D_MODEL), lambda m, n: (m, 0)),
            pl.BlockSpec((TT, PROJ_TN), lambda m, n: (m, jnp.minimum(n, 1))),
            pl.BlockSpec((TT, PROJ_TN), lambda m, n: (m, jnp.maximum(n - 2, 0))),
        ],
        out_shape=[
            jax.ShapeDtypeStruct((n_pad, D_MODEL), F32),
            jax.ShapeDtypeStruct((n_pad, 2 * D_LRU), F32),
            jax.ShapeDtypeStruct((n_pad, 3 * D_ATT), BF16),
        ],
        scratch_shapes=[
            pltpu.VMEM((2, TT, D_MODEL), F32),
            pltpu.VMEM((TT, D_MODEL), BF16),
            pltpu.SemaphoreType.DMA((2,)),
        ],
        compiler_params=pltpu.CompilerParams(
            dimension_semantics=("arbitrary", "arbitrary"), vmem_limit_bytes=VMEM_LIMIT),
        name="in_proj",
    )(x, meta_tokens, g_mix, w_in_bf16)


def _lru_kernel(x_ref, gate_ref, cw_ref, cb_ref, wr_ref, br_ref, wi_ref, bi_ref, lam_ref,
                y_ref, xpad_scr, h_scr):
    t = pl.program_id(2)

    @pl.when(t == 0)
    def _():
        xpad_scr[0:SUBLANES, :] = jnp.zeros((SUBLANES, LRU_CB), F32)
        h_scr[...] = jnp.zeros_like(h_scr)

    x = x_ref[...]
    xpad_scr[SUBLANES:, :] = x
    cw = cw_ref[...]
    xc = cb_ref[...] + x * cw[CONV_WIDTH - 1:CONV_WIDTH]
    for k in range(CONV_WIDTH - 1):
        shift = CONV_WIDTH - 1 - k
        xc = xc + xpad_scr[pl.ds(SUBLANES - shift, TT), :] * cw[k:k + 1]
    xpad_scr[0:SUBLANES, :] = x[TT - SUBLANES:TT]

    xb = xc.astype(BF16)
    r = jax.nn.sigmoid(jnp.dot(xb, wr_ref[...], preferred_element_type=F32) + br_ref[...])
    i = jax.nn.sigmoid(jnp.dot(xb, wi_ref[...], preferred_element_type=F32) + bi_ref[...])
    lam = lam_ref[...]
    log_sig = jnp.minimum(lam, 0.0) - jnp.log(1.0 + jnp.exp(-jnp.abs(lam)))
    log_a = (LRU_C * r) * log_sig
    a = jnp.exp(log_a)
    one_m_a2 = 1.0 - a * a
    root = jnp.where(one_m_a2 > 0.0, one_m_a2 * lax.rsqrt(one_m_a2), 0.0)
    b = root * (i * xc)

    row = lax.broadcasted_iota(jnp.int32, (SUBLANES, LRU_CB), 0)
    carry = h_scr[...]
    groups = []
    for g in range(TT // SUBLANES):
        rows = slice(g * SUBLANES, (g + 1) * SUBLANES)
        ag, bg = a[rows], b[rows]
        s = 1
        while s < SUBLANES:
            keep = row >= s
            a_sh = jnp.where(keep, pltpu.roll(ag, s, 0), 1.0)
            b_sh = jnp.where(keep, pltpu.roll(bg, s, 0), 0.0)
            bg = ag * b_sh + bg
            ag = ag * a_sh
            s *= 2
        hg = bg + ag * carry
        groups.append(hg)
        carry = jnp.broadcast_to(hg[SUBLANES - 1:SUBLANES], hg.shape)
    h_scr[...] = carry
    h = jnp.concatenate(groups, axis=0)
    y_ref[...] = (h * jax.nn.gelu(gate_ref[...])).astype(BF16)


def _lru(proj_lru, conv_w, conv_b, wr_bd, b_r, wi_bd, b_i, lru_l, batch, t_pad):
    n_pad = proj_lru.shape[0]
    n_t = t_pad // TT
    n_c = D_LRU // LRU_CB
    vec = lambda: pl.BlockSpec((1, LRU_CB), lambda b, c, t: (0, c))
    return pl.pallas_call(
        _lru_kernel,
        grid=(batch, n_c, n_t),
        in_specs=[
            pl.BlockSpec((TT, LRU_CB), lambda b, c, t: (b * n_t + t, c)),
            pl.BlockSpec((TT, LRU_CB), lambda b, c, t: (b * n_t + t, n_c + c)),
            pl.BlockSpec((CONV_WIDTH, LRU_CB), lambda b, c, t: (0, c)),
            vec(),
            pl.BlockSpec((None, LRU_CB, LRU_CB), lambda b, c, t: (c, 0, 0)),
            vec(),
            pl.BlockSpec((None, LRU_CB, LRU_CB), lambda b, c, t: (c, 0, 0)),
            vec(),
            vec(),
        ],
        out_specs=pl.BlockSpec((TT, LRU_CB), lambda b, c, t: (b * n_t + t, c)),
        out_shape=jax.ShapeDtypeStruct((n_pad, D_LRU), BF16),
        scratch_shapes=[pltpu.VMEM((SUBLANES + TT, LRU_CB), F32), pltpu.VMEM((SUBLANES, LRU_CB), F32)],
        compiler_params=pltpu.CompilerParams(
            dimension_semantics=("arbitrary", "arbitrary", "arbitrary"),
            vmem_limit_bytes=VMEM_LIMIT),
        name="rglru",
    )(proj_lru, proj_lru, conv_w, conv_b, wr_bd, b_r, wi_bd, b_i, lru_l)


def _bucket_thresholds():
    max_exact = N_BUCKETS // 2
    n = np.arange(0, MAX_DISTANCE + 1)
    nf = np.maximum(n, 1).astype(np.float32)
    large = max_exact + (np.log(nf / np.float32(max_exact)) / np.float32(math.log(MAX_DISTANCE / max_exact))
                         * np.float32(N_BUCKETS - max_exact)).astype(np.int32)
    large = np.minimum(large, N_BUCKETS - 1)
    bucket = np.where(n < max_exact, n, large)
    return [int(np.argmax(bucket >= k)) for k in range(1, N_BUCKETS)]


_BUCKET_THR = _bucket_thresholds()


def _attn_kernel(rb_ref, q_ref, k_ref, v_ref, lam_ref, g_ref, o_ref, bias_scr, vt_scr, m_scr, acc_scr):
    h = pl.program_id(0)
    b = pl.program_id(1)
    qi = pl.program_id(2)
    tk = TT
    n_kv = vt_scr.shape[0]

    @pl.when(jnp.logical_and(b == 0, qi == 0))
    def _():
        k_pos = lax.broadcasted_iota(jnp.int32, (tk, tk), 0)
        q_pos = lax.broadcasted_iota(jnp.int32, (tk, tk), 1)
        far = rb_ref[N_BUCKETS - 1, h]
        for d in range(2):
            rel = q_pos - k_pos + d * tk
            val = jnp.full((tk, tk), rb_ref[0, h] - far, F32)
            for kk, thr in enumerate(_BUCKET_THR):
                val = jnp.where(rel >= thr, rb_ref[kk + 1, h] - far, val)
            val = val * LOG2_E
            if d == 0:
                val = jnp.where(rel >= 0, val, -jnp.inf)
            bias_scr[d] = val

    @pl.when(qi == 0)
    def _():
        for t in range(n_kv):
            vt_scr[t, 0:V_HEAD_DIM, :] = v_ref[t * tk:(t + 1) * tk, :].astype(F32).T.astype(BF16)
            vt_scr[t, V_HEAD_DIM:VT_ROWS, :] = jnp.ones((VT_ROWS - V_HEAD_DIM, tk), BF16)

    q = q_ref[...]
    lane = lax.broadcasted_iota(jnp.int32, q.shape, 1)
    scale = (QK_HEAD_DIM ** -0.5) * LOG2_E
    qf = q.astype(F32) * scale
    qs = jnp.concatenate([jnp.where(lane < QK_HEAD_DIM, qf, 0.0),
                          jnp.where(lane >= QK_HEAD_DIM, qf, 0.0)], axis=0).astype(BF16)

    m_scr[...] = jnp.full(m_scr.shape, NEG_BIG, F32)
    acc_scr[...] = jnp.zeros_like(acc_scr)

    def tile_group(tiles):
        n_sub = 2 * tk // ATT_SUB
        stages = [(t, c) for t in range(len(tiles)) for c in range(n_sub)]

        def n_keys(t, c):
            return (c * ATT_SUB) % tk + ATT_SUB if tiles[t][1] == 0 else tk

        def scores(t, c):
            j, bias_idx = tiles[t]
            nk = n_keys(t, c)
            kt = k_ref[pl.ds(pl.multiple_of(j * tk, tk), nk), :]
            cols = slice(c * ATT_SUB, (c + 1) * ATT_SUB)
            s = lax.dot_general(kt, qs[cols], (((1,), (1,)), ((), ())), preferred_element_type=F32)
            if bias_idx is not None:
                q0 = (c * ATT_SUB) % tk
                s = s + bias_scr[bias_idx, 0:nk, q0:q0 + ATT_SUB]
            return s

        def compact(t):
            return tiles[t][1] != 0

        def produce(t, c):
            s = scores(t, c)
            if not compact(t):
                return s
            cols = slice(c * ATT_SUB, (c + 1) * ATT_SUB)
            m_prev = m_scr[:, cols]
            m_new = jnp.maximum(m_prev, jnp.max(s, axis=0, keepdims=True))
            m_scr[:, cols] = m_new
            return (s - m_prev).astype(BF16), m_prev, m_new

        pending = [produce(*st) for st in stages[:ATT_LOOKAHEAD]]
        for n, (t, c) in enumerate(stages):
            cols = slice(c * ATT_SUB, (c + 1) * ATT_SUB)
            item = pending.pop(0)
            if n + ATT_LOOKAHEAD < len(stages):
                pending.append(produce(*stages[n + ATT_LOOKAHEAD]))
            if compact(t):
                d, m_prev, m_new = item
                p = jnp.exp2(d - (m_new - m_prev).astype(BF16))
            else:
                m_prev = m_scr[:, cols]
                m_new = jnp.maximum(m_prev, jnp.max(item, axis=0, keepdims=True))
                m_scr[:, cols] = m_new
                p = jnp.exp2(item - m_new).astype(BF16)
            alpha = jnp.exp2(m_prev - m_new)
            vt = vt_scr[tiles[t][0], :, 0:n_keys(t, c)]
            acc_scr[:, cols] = alpha * acc_scr[:, cols] + jnp.dot(vt, p, preferred_element_type=F32)

    @pl.when(qi >= 1)
    def _():
        tile_group([(qi, 0), (qi - 1, 1)])

    @pl.when(qi == 0)
    def _():
        tile_group([(qi, 0)])

    n_far = jnp.maximum(qi - 1, 0)

    def far_group(i, carry):
        tile_group([(ATT_GROUP * i + t, None) for t in range(ATT_GROUP)])
        return carry

    lax.fori_loop(0, n_far // ATT_GROUP, far_group, 0)
    done = n_far - n_far % ATT_GROUP
    piece = ATT_GROUP // 2
    while piece >= 1:
        @pl.when((n_far % (2 * piece)) >= piece)
        def _(done=done, piece=piece):
            tile_group([(done + t, None) for t in range(piece)])
        done = done + jnp.where((n_far % (2 * piece)) >= piece, piece, 0)
        piece //= 2

    lam_p = lam_ref[...]
    lam = (jnp.exp(jnp.sum(lam_p[0:1] * lam_p[1:2], axis=-1, keepdims=True))
           - jnp.exp(jnp.sum(lam_p[2:3] * lam_p[3:4], axis=-1, keepdims=True)) + LAMBDA_INIT)
    inv = 1.0 / acc_scr[V_HEAD_DIM:V_HEAD_DIM + 1, :]
    o_t = (acc_scr[0:V_HEAD_DIM, 0:tk] * inv[:, :tk]
           - acc_scr[0:V_HEAD_DIM, tk:] * (lam * inv[:, tk:]))
    ms = jnp.mean(o_t * o_t, axis=0, keepdims=True)
    o_n = (o_t * lax.rsqrt(ms + EPS)).T
    o_ref[...] = (o_n * g_ref[...] * (1.0 - LAMBDA_INIT)).astype(BF16)


def _attention(qkv, rel_bias, lam_params, subln_g, batch, t_pad):
    n_q = t_pad // TT
    qkv3 = qkv.reshape(batch, t_pad, 3 * D_ATT)
    out = pl.pallas_call(
        _attn_kernel,
        grid_spec=pltpu.PrefetchScalarGridSpec(
            num_scalar_prefetch=0,
            grid=(N_HEADS, batch, n_q),
            in_specs=[
                pl.BlockSpec(memory_space=pltpu.SMEM),
                pl.BlockSpec((None, TT, V_HEAD_DIM), lambda h, b, q: (b, q, h)),
                pl.BlockSpec((None, t_pad, V_HEAD_DIM), lambda h, b, q: (b, 0, N_HEADS + h)),
                pl.BlockSpec((None, t_pad, V_HEAD_DIM), lambda h, b, q: (b, 0, 2 * N_HEADS + h)),
                pl.BlockSpec((4, QK_HEAD_DIM), lambda h, b, q: (0, 0)),
                pl.BlockSpec((1, V_HEAD_DIM), lambda h, b, q: (0, 0)),
            ],
            out_specs=pl.BlockSpec((None, TT, V_HEAD_DIM), lambda h, b, q: (b, q, h)),
            scratch_shapes=[
                pltpu.VMEM((2, TT, TT), F32),
                pltpu.VMEM((n_q, VT_ROWS, TT), BF16),
                pltpu.VMEM((1, 2 * TT), F32),
                pltpu.VMEM((VT_ROWS, 2 * TT), F32),
            ],
        ),
        out_shape=jax.ShapeDtypeStruct((batch, t_pad, D_ATT), BF16),
        compiler_params=pltpu.CompilerParams(
            dimension_semantics=("arbitrary", "arbitrary", "arbitrary"),
            vmem_limit_bytes=VMEM_LIMIT),
        name="diff_attention",
    )(rel_bias, qkv3, qkv3, qkv3, lam_params, subln_g)
    return out.reshape(batch * t_pad, D_ATT)


def _route_tile(lg, carry):
    rows = lg.shape[0]
    lane = lax.broadcasted_iota(jnp.int32, lg.shape, 1)
    first = lambda mask: jnp.min(jnp.where(mask, lane, LANES), axis=-1, keepdims=True)

    is_g = lane < N_GROUPS
    gl = jnp.where(is_g, lg, -jnp.inf)
    gmax = jnp.max(gl, axis=-1, keepdims=True)
    grp = first(gl == gmax)
    gsum = jnp.sum(jnp.where(is_g, jnp.exp(gl - gmax), 0.0), axis=-1, keepdims=True)
    p_grp = 1.0 / gsum

    lane_e = lane - N_GROUPS
    in_grp = (lane_e >= 0) & (lane_e < N_EXPERTS) & ((lane_e // EXPERTS_PER_GROUP) == grp)
    el = jnp.where(in_grp, lg, -jnp.inf)
    v1 = jnp.max(el, axis=-1, keepdims=True)
    i1 = first(el == v1)
    el2 = jnp.where(lane == i1, -jnp.inf, el)
    v2 = jnp.max(el2, axis=-1, keepdims=True)
    i2 = first(el2 == v2)
    e2 = jnp.exp(v2 - v1)
    den = 1.0 + e2
    g0 = p_grp * (1.0 / den)
    g1 = p_grp * (e2 / den)

    hit1 = lane == i1
    hit2 = lane == i2
    onehot = jnp.where(hit1 | hit2, 1.0, 0.0)
    rr = lax.broadcasted_iota(jnp.int32, (rows, rows), 0)
    cc = lax.broadcasted_iota(jnp.int32, (rows, rows), 1)
    tri = jnp.where(cc < rr, 1.0, 0.0).astype(BF16)
    rank = jnp.dot(tri, onehot.astype(BF16), preferred_element_type=F32) + carry
    r0 = jnp.sum(jnp.where(hit1, rank, 0.0), axis=-1, keepdims=True).astype(jnp.int32)
    r1 = jnp.sum(jnp.where(hit2, rank, 0.0), axis=-1, keepdims=True).astype(jnp.int32)

    idx = jnp.where(lane == 0, i1 - N_GROUPS,
                    jnp.where(lane == 1, i2 - N_GROUPS,
                              jnp.where(lane == 2, r0, jnp.where(lane == 3, r1, 0))))
    gates = jnp.where(lane == 0, g0, jnp.where(lane == 1, g1, 0.0))
    return idx, gates, carry + jnp.sum(onehot, axis=0, keepdims=True)


def _outproj_kernel(yl_ref, ya_ref, h0_ref, w_ref, g_ref, wrt_ref, brt_ref, h1_ref, u_ref, lg_ref):
    acc = jnp.dot(yl_ref[...], w_ref[0:D_LRU, :], preferred_element_type=F32)
    acc = acc + jnp.dot(ya_ref[...], w_ref[D_LRU:, :], preferred_element_type=F32)
    h1 = h0_ref[...] + acc
    h1_ref[...] = h1
    u = _rmsnorm(h1, g_ref[...])
    u_ref[...] = u
    u_hi = u.astype(BF16)
    u_lo = (u - u_hi.astype(F32)).astype(BF16)
    w = wrt_ref[...]
    w_hi = w.astype(BF16)
    w_lo = (w - w_hi.astype(F32)).astype(BF16)
    parts = jnp.dot(jnp.concatenate([u_hi, u_lo], axis=0), jnp.concatenate([w_hi, w_lo], axis=1),
                    preferred_element_type=F32)
    lg = (parts[:OUT_TM, :LANES] + parts[:OUT_TM, LANES:]) + (parts[OUT_TM:, :LANES] + parts[OUT_TM:, LANES:])
    lg_ref[...] = lg + brt_ref[...]


def _out_proj(y_lru, y_att, h0, w_out_bf16, g_ffn, w_rt, b_rt):
    n_pad = h0.shape[0]
    row = lambda w: pl.BlockSpec((OUT_TM, w), lambda m: (m, 0))
    full = lambda a, b: pl.BlockSpec((a, b), lambda m: (0, 0))
    return pl.pallas_call(
        _outproj_kernel,
        grid=(n_pad // OUT_TM,),
        in_specs=[row(D_LRU), row(D_ATT), row(D_MODEL), full(D_MODEL, D_MODEL), full(1, D_MODEL),
                  full(D_MODEL, LANES), full(1, LANES)],
        out_specs=[row(D_MODEL), row(D_MODEL), row(LANES)],
        out_shape=[
            jax.ShapeDtypeStruct((n_pad, D_MODEL), F32),
            jax.ShapeDtypeStruct((n_pad, D_MODEL), F32),
            jax.ShapeDtypeStruct((n_pad, LANES), F32),
        ],
        compiler_params=pltpu.CompilerParams(
            dimension_semantics=("arbitrary",), vmem_limit_bytes=VMEM_LIMIT),
        name="out_proj",
    )(y_lru, y_att, h0, w_out_bf16, g_ffn, w_rt, b_rt)


def _route_kernel(lg_ref, idx_ref, gate_ref, cnt_ref, carry_scr):
    @pl.when(pl.program_id(0) == 0)
    def _():
        carry_scr[...] = jnp.zeros_like(carry_scr)

    idx, gates, carry = _route_tile(lg_ref[...], carry_scr[...])
    idx_ref[...] = idx
    gate_ref[...] = gates
    carry_scr[...] = carry
    cnt_ref[...] = carry


def _route(logits):
    n_pad = logits.shape[0]
    row = pl.BlockSpec((TT, LANES), lambda m: (m, 0))
    return pl.pallas_call(
        _route_kernel,
        grid=(n_pad // TT,),
        in_specs=[row],
        out_specs=[row, row, pl.BlockSpec((1, LANES), lambda m: (0, 0))],
        out_shape=[
            jax.ShapeDtypeStruct((n_pad, LANES), jnp.int32),
            jax.ShapeDtypeStruct((n_pad, LANES), F32),
            jax.ShapeDtypeStruct((1, LANES), F32),
        ],
        scratch_shapes=[pltpu.VMEM((1, LANES), F32)],
        compiler_params=pltpu.CompilerParams(
            dimension_semantics=("arbitrary",), vmem_limit_bytes=VMEM_LIMIT),
        name="route",
    )(logits)


def _dest_kernel(idx_ref, ps_ref, o_ref):
    idx = idx_ref[...]
    lane = lax.broadcasted_iota(jnp.int32, idx.shape, 1)
    ps = ps_ref[...]
    pick = lambda e: jnp.sum(jnp.where(lane == e, ps, 0.0), axis=-1, keepdims=True).astype(jnp.int32)
    d0 = pick(idx[:, 0:1]) + idx[:, 2:3]
    d1 = pick(idx[:, 1:2]) + idx[:, 3:4]
    o_ref[...] = jnp.where(lane == 0, d0, jnp.where(lane == 1, d1, 0))


def _dest(idx, pstart_row):
    n_pad = idx.shape[0]
    row = pl.BlockSpec((TT, LANES), lambda m: (m, 0))
    return pl.pallas_call(
        _dest_kernel,
        grid=(n_pad // TT,),
        in_specs=[row, pl.BlockSpec((1, LANES), lambda m: (0, 0))],
        out_specs=row,
        out_shape=jax.ShapeDtypeStruct((n_pad, LANES), jnp.int32),
        compiler_params=pltpu.CompilerParams(
            dimension_semantics=("arbitrary",), vmem_limit_bytes=VMEM_LIMIT),
        name="dest",
    )(idx, pstart_row)


def _row_copy(src, s_row, dst, d_row, sem):
    return pltpu.make_async_copy(src.at[pl.ds(s_row, 1)], dst.at[pl.ds(d_row, 1)], sem)


def _dispatch_kernel(d0_ref, d1_ref, pe_ref, u_ref, xb_ref, zero_scr, sem, zsem):
    step = pl.program_id(0)
    base = step * TT

    def zero_block(e):
        start = pl.multiple_of(pe_ref[e] - DISPATCH_BLK, DISPATCH_BLK)
        return pltpu.make_async_copy(zero_scr, xb_ref.at[pl.ds(start, DISPATCH_BLK)], zsem)

    @pl.when(step == 0)
    def _():
        zero_scr[...] = jnp.zeros_like(zero_scr)

        def nonempty(e):
            return pe_ref[e] > jnp.where(e == 0, 0, pe_ref[jnp.maximum(e - 1, 0)])

        def zstart(e, carry):
            @pl.when(nonempty(e))
            def _():
                zero_block(e).start()
            return carry

        def zwait(e, carry):
            @pl.when(nonempty(e))
            def _():
                zero_block(e).wait()
            return carry

        lax.fori_loop(0, N_EXPERTS, zstart, 0)
        lax.fori_loop(0, N_EXPERTS, zwait, 0)

        def tail_block(i):
            start = pl.multiple_of(i * DISPATCH_BLK, DISPATCH_BLK)
            return pltpu.make_async_copy(zero_scr, xb_ref.at[pl.ds(start, DISPATCH_BLK)], zsem)

        def tstart(i, carry):
            tail_block(i).start()
            return carry

        def twait(i, carry):
            tail_block(i).wait()
            return carry

        n_used = pe_ref[N_EXPERTS - 1] // DISPATCH_BLK
        n_blocks = xb_ref.shape[0] // DISPATCH_BLK
        lax.fori_loop(n_used, n_blocks, tstart, 0)
        lax.fori_loop(n_used, n_blocks, twait, 0)

    def issue(r, carry):
        _row_copy(u_ref, r, xb_ref, d0_ref[base + r], sem.at[0]).start()
        _row_copy(u_ref, r, xb_ref, d1_ref[base + r], sem.at[1]).start()
        return carry

    for r in range(TT):
        issue(r, 0)
    for k in range(2):
        pltpu.make_async_copy(u_ref, xb_ref.at[pl.ds(0, TT)], sem.at[k]).wait()


def _dispatch(d0, d1, pends, u2, rows):
    n_pad = u2.shape[0]
    return pl.pallas_call(
        _dispatch_kernel,
        grid_spec=pltpu.PrefetchScalarGridSpec(
            num_scalar_prefetch=3,
            grid=(n_pad // TT,),
            in_specs=[pl.BlockSpec((TT, D_MODEL), lambda m, d0, d1, pe: (m, 0))],
            out_specs=pl.BlockSpec(memory_space=pl.ANY),
            scratch_shapes=[
                pltpu.VMEM((DISPATCH_BLK, D_MODEL), F32),
                pltpu.SemaphoreType.DMA((2,)),
                pltpu.SemaphoreType.DMA(()),
            ],
        ),
        out_shape=jax.ShapeDtypeStruct((rows, D_MODEL), F32),
        compiler_params=pltpu.CompilerParams(
            dimension_semantics=("arbitrary",), vmem_limit_bytes=VMEM_LIMIT),
        name="dispatch",
    )(d0, d1, pends, u2)


WEIGHT_UNITS = 8
WEIGHT_DEPTH = 4


def _stream_weights(i, tgt_ref, eseq_ref, ntot_ref, w_hbm, stages, wbfs, sem, done_scr):
    def slab(u, k):
        rows = stages[k].shape[1]
        return pl.ds(pl.multiple_of((u % WEIGHT_UNITS) * rows, rows), rows)

    def unit_copies(u):
        e = eseq_ref[u // WEIGHT_UNITS]
        b = u % WEIGHT_DEPTH
        return [pltpu.make_async_copy(w.at[e, slab(u, k), :], stages[k].at[b], sem.at[b, k])
                for k, w in enumerate(w_hbm)]

    @pl.when(i == 0)
    def _():
        done_scr[0] = 0
        for u in range(WEIGHT_DEPTH):
            for c in unit_copies(u):
                c.start()

    def body(u, carry):
        for c in unit_copies(u):
            c.wait()
        slot = (u // WEIGHT_UNITS) % 2
        b = u % WEIGHT_DEPTH
        for k in range(len(w_hbm)):
            wbfs[k][slot, slab(u, k), :] = stages[k][b].astype(BF16)

        @pl.when(u + WEIGHT_DEPTH < ntot_ref[0])
        def _():
            for c in unit_copies(u + WEIGHT_DEPTH):
                c.start()

        return carry

    lax.fori_loop(done_scr[0], tgt_ref[i], body, 0)
    done_scr[0] = tgt_ref[i]


def _expert_kernel(tgt_ref, eseq_ref, ntot_ref, slot_ref, nu_ref, x_ref, wg_hbm, wu_hbm, wd_hbm, y_ref,
                   sg, su, sd, bg, bu, bd, sem, done_scr):
    i = pl.program_id(0)
    _stream_weights(i, tgt_ref, eseq_ref, ntot_ref, (wg_hbm, wu_hbm, wd_hbm), (sg, su, sd), (bg, bu, bd),
                    sem, done_scr)
    used = i < nu_ref[0]

    @pl.when(used)
    def _():
        slot = slot_ref[i]
        x = x_ref[...].astype(BF16)
        g = jnp.dot(x, bg[slot], preferred_element_type=F32)
        u = jnp.dot(x, bu[slot], preferred_element_type=F32)
        a = (jax.nn.silu(g) * u).astype(BF16)
        y_ref[...] = jnp.dot(a, bd[slot], preferred_element_type=F32)

    @pl.when(jnp.logical_not(used))
    def _():
        y_ref[...] = jnp.zeros_like(y_ref)


def _experts(sched, x_buf, w_gate, w_up, w_down):
    rows = x_buf.shape[0]
    n_blocks = rows // DISPATCH_BLK
    blk_map = lambda i, tg, es, nt, sl, nu: (jnp.minimum(i, jnp.maximum(nu[0] - 1, 0)), 0)
    out_map = lambda i, tg, es, nt, sl, nu: (i, 0)
    hbm = pl.BlockSpec(memory_space=pl.ANY)
    stage = lambda k_dim, n_dim: pltpu.VMEM((WEIGHT_DEPTH, k_dim // WEIGHT_UNITS, n_dim), F32)
    resident = lambda k_dim, n_dim: pltpu.VMEM((2, k_dim, n_dim), BF16)
    return pl.pallas_call(
        _expert_kernel,
        grid_spec=pltpu.PrefetchScalarGridSpec(
            num_scalar_prefetch=5,
            grid=(n_blocks,),
            in_specs=[pl.BlockSpec((DISPATCH_BLK, D_MODEL), blk_map), hbm, hbm, hbm],
            out_specs=pl.BlockSpec((DISPATCH_BLK, D_MODEL), out_map),
            scratch_shapes=[
                stage(D_MODEL, D_EXPERT), stage(D_MODEL, D_EXPERT), stage(D_EXPERT, D_MODEL),
                resident(D_MODEL, D_EXPERT), resident(D_MODEL, D_EXPERT), resident(D_EXPERT, D_MODEL),
                pltpu.SemaphoreType.DMA((WEIGHT_DEPTH, 3)),
                pltpu.SMEM((1,), jnp.int32),
            ],
        ),
        out_shape=jax.ShapeDtypeStruct((rows, D_MODEL), F32),
        compiler_params=pltpu.CompilerParams(
            dimension_semantics=("arbitrary",), vmem_limit_bytes=VMEM_LIMIT),
        name="experts",
    )(*sched, x_buf, w_gate, w_up, w_down)


def _expert_schedule(padded, pstarts, n_blocks):
    nonempty = padded > 0
    seq_of_expert = jnp.cumsum(nonempty.astype(jnp.int32)) - 1
    n_seq = jnp.sum(nonempty.astype(jnp.int32))
    experts = jnp.arange(N_EXPERTS, dtype=jnp.int32)
    is_kth = jnp.logical_and(nonempty[None, :], seq_of_expert[None, :] == experts[:, None])
    eseq = jnp.sum(jnp.where(is_kth, experts[None, :], 0), axis=1)
    pends = pstarts + padded
    n_used = pends[-1] // DISPATCH_BLK
    blk = jnp.arange(n_blocks, dtype=jnp.int32)
    blk_c = jnp.minimum(blk, jnp.maximum(n_used - 1, 0))
    blk_expert = jnp.minimum(
        jnp.sum((pends[None, :] <= (blk_c * DISPATCH_BLK)[:, None]).astype(jnp.int32), axis=1), N_EXPERTS - 1)
    onehot = (blk_expert[:, None] == experts[None, :]).astype(jnp.int32)
    pick = lambda v: jnp.sum(onehot * v[None, :], axis=1)
    q = pick(seq_of_expert)
    j = blk_c - pick(pstarts) // DISPATCH_BLK
    n = jnp.maximum(pick(padded) // DISPATCH_BLK, 1)
    n_total = n_seq * WEIGHT_UNITS
    tgt = jnp.minimum(WEIGHT_UNITS * (q + 1) + (WEIGHT_UNITS * j) // n, n_total)
    tgt = jnp.where(blk < n_used, tgt, n_total)
    i32 = lambda a: a.astype(jnp.int32)
    return i32(tgt), eseq, i32(n_total)[None], i32(q % 2), i32(n_used)[None]


COMBINE_TM = 512


def _combine_kernel(t_pad, d0_ref, d1_ref, h_hbm, gate_hbm, g_ref, yb_ref, o_ref,
                    h_scr, gate_scr, y0_scr, y1_scr, sem):
    n_j = pl.num_programs(1)
    step = pl.program_id(0) * n_j + pl.program_id(1)
    n_steps = pl.num_programs(0) * n_j

    def base_row(s):
        return pl.multiple_of((s // n_j) * t_pad + N_META + (s % n_j) * COMBINE_TM, 8)

    def tile_copies(s, slot):
        base = base_row(s)
        return [
            pltpu.make_async_copy(h_hbm.at[pl.ds(base, COMBINE_TM)], h_scr.at[slot], sem.at[slot, 2]),
            pltpu.make_async_copy(gate_hbm.at[pl.ds(base, COMBINE_TM)], gate_scr.at[slot], sem.at[slot, 3]),
        ]

    def fetch(s, slot):
        base = base_row(s)
        for c in tile_copies(s, slot):
            c.start()

        def issue(r, carry):
            _row_copy(yb_ref, d0_ref[base + r], y0_scr.at[slot], r, sem.at[slot, 0]).start()
            _row_copy(yb_ref, d1_ref[base + r], y1_scr.at[slot], r, sem.at[slot, 1]).start()
            return carry

        for r in range(COMBINE_TM):
            issue(r, 0)

    slot = step % 2

    @pl.when(step == 0)
    def _():
        fetch(step, slot)

    @pl.when(step + 1 < n_steps)
    def _():
        fetch(step + 1, 1 - slot)

    pltpu.make_async_copy(yb_ref.at[pl.ds(0, COMBINE_TM)], y0_scr.at[slot], sem.at[slot, 0]).wait()
    pltpu.make_async_copy(yb_ref.at[pl.ds(0, COMBINE_TM)], y1_scr.at[slot], sem.at[slot, 1]).wait()
    for c in tile_copies(step, slot):
        c.wait()

    gate = gate_scr[slot]
    h = h_scr[slot] + gate[:, 0:1] * y0_scr[slot] + gate[:, 1:2] * y1_scr[slot]
    o_ref[...] = _rmsnorm(h, g_ref[...])


def _combine(d0, d1, h1, gates, g_final, y_buf, batch, seq):
    assert seq % COMBINE_TM == 0 and N_META % 8 == 0
    tile = lambda w: pltpu.VMEM((2, COMBINE_TM, w), F32)
    return pl.pallas_call(
        functools.partial(_combine_kernel, h1.shape[0] // batch),
        grid_spec=pltpu.PrefetchScalarGridSpec(
            num_scalar_prefetch=2,
            grid=(batch, seq // COMBINE_TM),
            in_specs=[
                pl.BlockSpec(memory_space=pl.ANY),
                pl.BlockSpec(memory_space=pl.ANY),
                pl.BlockSpec((1, D_MODEL), lambda b, j, d0, d1: (0, 0)),
                pl.BlockSpec(memory_space=pl.ANY),
            ],
            out_specs=pl.BlockSpec((None, COMBINE_TM, D_MODEL), lambda b, j, d0, d1: (b, j, 0)),
            scratch_shapes=[tile(D_MODEL), tile(LANES), tile(D_MODEL), tile(D_MODEL),
                            pltpu.SemaphoreType.DMA((2, 4))],
        ),
        out_shape=jax.ShapeDtypeStruct((batch, seq, D_MODEL), F32),
        compiler_params=pltpu.CompilerParams(
            dimension_semantics=("arbitrary", "arbitrary"), vmem_limit_bytes=VMEM_LIMIT),
        name="combine",
    )(d0, d1, h1, gates, g_final, y_buf)


def _block_diag(w):
    per = LRU_CB // LRU_BLOCK_DIM
    w4 = w.reshape(LRU_BLOCKS // per, per, LRU_BLOCK_DIM, LRU_BLOCK_DIM)
    eye = jnp.eye(per, dtype=w.dtype)
    bd = jnp.einsum('cpij,pq->cpiqj', w4, eye)
    return bd.reshape(LRU_BLOCKS // per, LRU_CB, LRU_CB)


def kernel(x, meta_tokens, g_mix, w_in, conv_w, conv_b, w_rgate, b_rgate, w_igate, b_igate, lru_L, lambda_q1, lambda_k1, lambda_q2, lambda_k2, subln_g, rel_bias, w_out, g_ffn, w_group, b_group, w_router, b_router, w_gate, w_up, w_down, g_final):
    batch, seq, _ = x.shape
    t_real = N_META + seq
    t_pad = _round_up(t_real, TT)
    n_pad = batch * t_pad

    h0, proj_lru, proj_qkv = _in_proj(x, meta_tokens.astype(x.dtype), g_mix[0][None],
                                      w_in[0].astype(BF16), t_pad)

    y_lru = _lru(proj_lru, conv_w[0], conv_b[0][None],
                 _block_diag(w_rgate[0]).astype(BF16), b_rgate[0].reshape(1, D_LRU),
                 _block_diag(w_igate[0]).astype(BF16), b_igate[0].reshape(1, D_LRU),
                 lru_L[0][None], batch, t_pad)

    lam_params = jnp.stack([lambda_q1[0], lambda_k1[0], lambda_q2[0], lambda_k2[0]])
    y_att = _attention(proj_qkv, rel_bias, lam_params, subln_g[0][None], batch, t_pad)

    w_rt = jnp.concatenate([w_group[0], w_router[0],
                            jnp.zeros((D_MODEL, LANES - N_GROUPS - N_EXPERTS), F32)], axis=1)
    b_rt = jnp.concatenate([b_group[0], b_router[0],
                            jnp.zeros((LANES - N_GROUPS - N_EXPERTS,), F32)])[None]
    h1, u2, logits = _out_proj(y_lru, y_att, h0, w_out[0].astype(BF16), g_ffn[0][None], w_rt, b_rt)

    idx, gates, counts = _route(logits)

    cnt = counts[0, N_GROUPS:N_GROUPS + N_EXPERTS].astype(jnp.int32)
    padded = (cnt + DISPATCH_BLK - 1) // DISPATCH_BLK * DISPATCH_BLK
    pends = jnp.cumsum(padded)
    pstarts = pends - padded
    dest = _dest(idx, jnp.pad(pstarts.astype(F32), (0, LANES - N_EXPERTS))[None])
    d0 = dest[:, 0]
    d1 = dest[:, 1]
    n_blocks = -(-(2 * n_pad + N_EXPERTS * (DISPATCH_BLK - 1)) // DISPATCH_BLK)
    rows = n_blocks * DISPATCH_BLK
    sched = _expert_schedule(padded, pstarts, n_blocks)

    x_buf = _dispatch(d0, d1, pends.astype(jnp.int32), u2, rows)
    y_buf = _experts(sched, x_buf, w_gate[0], w_up[0], w_down[0])
    return _combine(d0, d1, h1, gates, g_final[None], y_buf, batch, seq)
```

```python
import functools
import math

import numpy as np
import jax
import jax.numpy as jnp
from jax import lax
from jax.experimental import pallas as pl
from jax.experimental.pallas import tpu as pltpu

D_MODEL = 2048
N_META = 16
D_LRU = 1024
D_ATT = 1024
LRU_BLOCKS = 16
LRU_BLOCK_DIM = 64
CONV_WIDTH = 4
LRU_C = 8.0
N_HEADS = 8
V_HEAD_DIM = 128
QK_HEAD_DIM = 64
N_BUCKETS = 32
MAX_DISTANCE = 128
N_GROUPS = 4
EXPERTS_PER_GROUP = 8
N_EXPERTS = 32
D_EXPERT = 1024
D_IN_PROJ = 2 * D_LRU + 3 * D_ATT
EPS = 1e-6
LAMBDA_INIT = 0.8 - 0.6 * math.exp(-0.3 * 0)
LOG2_E = math.log2(math.e)

F32 = jnp.float32
BF16 = jnp.bfloat16

LANES = 128
SUBLANES = 8
MXU_DIM = 256
TT = 768
PROJ_TN = 1024
OUT_TM = TT // 2
ATT_SUB = MXU_DIM
ATT_LOOKAHEAD = 4
ATT_GROUP = 4
VT_ROWS = V_HEAD_DIM + 16
LRU_CB = MXU_DIM
DISPATCH_BLK = 256
VMEM_LIMIT = 56 * 1024 * 1024
NEG_BIG = -1e30


def _round_up(a, b):
    return -(-a // b) * b


def _pack_bf16_pairs(x):
    n = x.shape[1] // 2
    bits = lambda v: pltpu.bitcast(v.astype(BF16).astype(F32), jnp.uint32)
    return lax.shift_right_logical(bits(x[:, :n]), jnp.uint32(16)) | bits(x[:, n:])


def _unpack_bf16_pairs(w):
    lo = pltpu.bitcast(lax.shift_left(w, jnp.uint32(16)), F32)
    hi = pltpu.bitcast(w & jnp.uint32(0xFFFF0000), F32)
    return lo.astype(BF16), hi.astype(BF16)


def _rmsnorm(x, g):
    ms = jnp.mean(x * x, axis=-1, keepdims=True)
    return x * lax.rsqrt(ms + EPS) * g


def _inproj_kernel(n_t, seq, x_hbm, meta_ref, g_ref, w_ref, h0_ref, lru_ref, qkv_ref, xbuf, u_scr, sem):
    m = pl.program_id(0)
    n = pl.program_id(1)
    n_m = pl.num_programs(0)
    last_rows = seq + N_META - (n_t - 1) * TT

    def tile_copy(mm, slot):
        b = mm // n_t
        t = mm % n_t
        row0 = pl.multiple_of(t * TT - N_META, SUBLANES)
        return [
            (t == 0, pltpu.make_async_copy(x_hbm.at[b, pl.ds(0, TT - N_META)],
                                           xbuf.at[slot, pl.ds(N_META, TT - N_META)], sem.at[slot])),
            (jnp.logical_and(t > 0, t < n_t - 1),
             pltpu.make_async_copy(x_hbm.at[b, pl.ds(row0, TT)], xbuf.at[slot], sem.at[slot])),
            (t == n_t - 1, pltpu.make_async_copy(x_hbm.at[b, pl.ds(row0, last_rows)],
                                                 xbuf.at[slot, pl.ds(0, last_rows)], sem.at[slot])),
        ]

    def start(mm, slot):
        for cond, copy in tile_copy(mm, slot):
            @pl.when(cond)
            def _(copy=copy):
                copy.start()

    def wait(mm, slot):
        for cond, copy in tile_copy(mm, slot):
            @pl.when(cond)
            def _(copy=copy):
                copy.wait()

    slot = m % 2

    @pl.when(jnp.logical_and(m == 0, n == 0))
    def _():
        start(m, slot)

    @pl.when(jnp.logical_and(n == 1, m + 1 < n_m))
    def _():
        start(m + 1, 1 - slot)

    @pl.when(n == 0)
    def _():
        wait(m, slot)
        t = m % n_t

        @pl.when(t == 0)
        def _():
            xbuf[slot, 0:N_META, :] = meta_ref[...]

        @pl.when(t == n_t - 1)
        def _():
            xbuf[slot, last_rows:TT, :] = jnp.zeros((TT - last_rows, D_MODEL), F32)

        h0 = xbuf[slot]
        h0_ref[...] = h0
        u_scr[...] = _rmsnorm(h0, g_ref[...]).astype(BF16)

    y = jnp.dot(u_scr[...], w_ref[...], preferred_element_type=F32)

    @pl.when(n < 2)
    def _():
        lru_ref[...] = y

    @pl.when(n >= 2)
    def _():
        qkv_ref[...] = y.astype(BF16)


def _in_proj(x, meta_tokens, g_mix, w_in_bf16, t_pad):
    batch, seq, _ = x.shape
    n_t = t_pad // TT
    n_pad = batch * t_pad
    n_col = D_IN_PROJ // PROJ_TN
    assert N_META % SUBLANES == 0 and 0 < seq + N_META - (n_t - 1) * TT <= TT
    return pl.pallas_call(
        functools.partial(_inproj_kernel, n_t, seq),
        grid=(n_pad // TT, n_col),
        in_specs=[
            pl.BlockSpec(memory_space=pl.ANY),
            pl.BlockSpec((N_META, D_MODEL), lambda m, n: (0, 0)),
            pl.BlockSpec((1, D_MODEL), lambda m, n: (0, 0)),
            pl.BlockSpec((D_MODEL, PROJ_TN), lambda m, n: (0, n)),
        ],
        out_specs=[
            pl.BlockSpec((TT, D_MODEL), lambda m, n: (m, 0)),
            pl.BlockSpec((TT, PROJ_TN), lambda m, n: (m, jnp.minimum(n, 1))),
            pl.BlockSpec((TT, PROJ_TN), lambda m, n: (m, jnp.maximum(n - 2, 0))),
        ],
        out_shape=[
            jax.ShapeDtypeStruct((n_pad, D_MODEL), F32),
            jax.ShapeDtypeStruct((n_pad, 2 * D_LRU), F32),
            jax.ShapeDtypeStruct((n_pad, 3 * D_ATT), BF16),
        ],
        scratch_shapes=[
            pltpu.VMEM((2, TT, D_MODEL), F32),
            pltpu.VMEM((TT, D_MODEL), BF16),
            pltpu.SemaphoreType.DMA((2,)),
        ],
        compiler_params=pltpu.CompilerParams(
            dimension_semantics=("arbitrary", "arbitrary"), vmem_limit_bytes=VMEM_LIMIT),
        name="in_proj",
    )(x, meta_tokens, g_mix, w_in_bf16)


def _lru_kernel(x_ref, gate_ref, cw_ref, cb_ref, wr_ref, br_ref, wi_ref, bi_ref, lam_ref,
                y_ref, xpad_scr, h_scr):
    t = pl.program_id(2)

    @pl.when(t == 0)
    def _():
        xpad_scr[0:SUBLANES, :] = jnp.zeros((SUBLANES, LRU_CB), F32)
        h_scr[...] = jnp.zeros_like(h_scr)

    x = x_ref[...]
    xpad_scr[SUBLANES:, :] = x
    cw = cw_ref[...]
    xc = cb_ref[...] + x * cw[CONV_WIDTH - 1:CONV_WIDTH]
    for k in range(CONV_WIDTH - 1):
        shift = CONV_WIDTH - 1 - k
        xc = xc + xpad_scr[pl.ds(SUBLANES - shift, TT), :] * cw[k:k + 1]
    xpad_scr[0:SUBLANES, :] = x[TT - SUBLANES:TT]

    xb = xc.astype(BF16)
    r = jax.nn.sigmoid(jnp.dot(xb, wr_ref[...], preferred_element_type=F32) + br_ref[...])
    i = jax.nn.sigmoid(jnp.dot(xb, wi_ref[...], preferred_element_type=F32) + bi_ref[...])
    lam = lam_ref[...]
    log_sig = jnp.minimum(lam, 0.0) - jnp.log(1.0 + jnp.exp(-jnp.abs(lam)))
    log_a = (LRU_C * r) * log_sig
    a = jnp.exp(log_a)
    one_m_a2 = 1.0 - a * a
    root = jnp.where(one_m_a2 > 0.0, one_m_a2 * lax.rsqrt(one_m_a2), 0.0)
    b = root * (i * xc)

    row = lax.broadcasted_iota(jnp.int32, (SUBLANES, LRU_CB), 0)
    carry = h_scr[...]
    groups = []
    for g in range(TT // SUBLANES):
        rows = slice(g * SUBLANES, (g + 1) * SUBLANES)
        ag, bg = a[rows], b[rows]
        s = 1
        while s < SUBLANES:
            keep = row >= s
            a_sh = jnp.where(keep, pltpu.roll(ag, s, 0), 1.0)
            b_sh = jnp.where(keep, pltpu.roll(bg, s, 0), 0.0)
            bg = ag * b_sh + bg
            ag = ag * a_sh
            s *= 2
        hg = bg + ag * carry
        groups.append(hg)
        carry = jnp.broadcast_to(hg[SUBLANES - 1:SUBLANES], hg.shape)
    h_scr[...] = carry
    h = jnp.concatenate(groups, axis=0)
    y_ref[...] = (h * jax.nn.gelu(gate_ref[...])).astype(BF16)


def _lru(proj_lru, conv_w, conv_b, wr_bd, b_r, wi_bd, b_i, lru_l, batch, t_pad):
    n_pad = proj_lru.shape[0]
    n_t = t_pad // TT
    n_c = D_LRU // LRU_CB
    vec = lambda: pl.BlockSpec((1, LRU_CB), lambda b, c, t: (0, c))
    return pl.pallas_call(
        _lru_kernel,
        grid=(batch, n_c, n_t),
        in_specs=[
            pl.BlockSpec((TT, LRU_CB), lambda b, c, t: (b * n_t + t, c)),
            pl.BlockSpec((TT, LRU_CB), lambda b, c, t: (b * n_t + t, n_c + c)),
            pl.BlockSpec((CONV_WIDTH, LRU_CB), lambda b, c, t: (0, c)),
            vec(),
            pl.BlockSpec((None, LRU_CB, LRU_CB), lambda b, c, t: (c, 0, 0)),
            vec(),
            pl.BlockSpec((None, LRU_CB, LRU_CB), lambda b, c, t: (c, 0, 0)),
            vec(),
            vec(),
        ],
        out_specs=pl.BlockSpec((TT, LRU_CB), lambda b, c, t: (b * n_t + t, c)),
        out_shape=jax.ShapeDtypeStruct((n_pad, D_LRU), BF16),
        scratch_shapes=[pltpu.VMEM((SUBLANES + TT, LRU_CB), F32), pltpu.VMEM((SUBLANES, LRU_CB), F32)],
        compiler_params=pltpu.CompilerParams(
            dimension_semantics=("arbitrary", "arbitrary", "arbitrary"),
            vmem_limit_bytes=VMEM_LIMIT),
        name="rglru",
    )(proj_lru, proj_lru, conv_w, conv_b, wr_bd, b_r, wi_bd, b_i, lru_l)


def _bucket_thresholds():
    max_exact = N_BUCKETS // 2
    n = np.arange(0, MAX_DISTANCE + 1)
    nf = np.maximum(n, 1).astype(np.float32)
    large = max_exact + (np.log(nf / np.float32(max_exact)) / np.float32(math.log(MAX_DISTANCE / max_exact))
                         * np.float32(N_BUCKETS - max_exact)).astype(np.int32)
    large = np.minimum(large, N_BUCKETS - 1)
    bucket = np.where(n < max_exact, n, large)
    return [int(np.argmax(bucket >= k)) for k in range(1, N_BUCKETS)]


_BUCKET_THR = _bucket_thresholds()


def _attn_kernel(rb_ref, q_ref, k_ref, v_ref, lam_ref, g_ref, o_ref, bias_scr, vt_scr, m_scr, acc_scr):
    h = pl.program_id(0)
    b = pl.program_id(1)
    qi = pl.program_id(2)
    tk = TT
    n_kv = vt_scr.shape[0]

    @pl.when(jnp.logical_and(b == 0, qi == 0))
    def _():
        k_pos = lax.broadcasted_iota(jnp.int32, (tk, tk), 0)
        q_pos = lax.broadcasted_iota(jnp.int32, (tk, tk), 1)
        far = rb_ref[N_BUCKETS - 1, h]
        for d in range(2):
            rel = q_pos - k_pos + d * tk
            val = jnp.full((tk, tk), rb_ref[0, h] - far, F32)
            for kk, thr in enumerate(_BUCKET_THR):
                val = jnp.where(rel >= thr, rb_ref[kk + 1, h] - far, val)
            val = val * LOG2_E
            if d == 0:
                val = jnp.where(rel >= 0, val, -jnp.inf)
            bias_scr[d] = val

    @pl.when(qi == 0)
    def _():
        for t in range(n_kv):
            vt_scr[t, 0:V_HEAD_DIM, :] = v_ref[t * tk:(t + 1) * tk, :].astype(F32).T.astype(BF16)
            vt_scr[t, V_HEAD_DIM:VT_ROWS, :] = jnp.ones((VT_ROWS - V_HEAD_DIM, tk), BF16)

    q = q_ref[...]
    lane = lax.broadcasted_iota(jnp.int32, q.shape, 1)
    scale = (QK_HEAD_DIM ** -0.5) * LOG2_E
    qf = q.astype(F32) * scale
    qs = jnp.concatenate([jnp.where(lane < QK_HEAD_DIM, qf, 0.0),
                          jnp.where(lane >= QK_HEAD_DIM, qf, 0.0)], axis=0).astype(BF16)

    m_scr[...] = jnp.full(m_scr.shape, NEG_BIG, F32)
    acc_scr[...] = jnp.zeros_like(acc_scr)

    def tile_group(tiles):
        n_sub = 2 * tk // ATT_SUB
        stages = [(t, c) for t in range(len(tiles)) for c in range(n_sub)]

        def n_keys(t, c):
            return (c * ATT_SUB) % tk + ATT_SUB if tiles[t][1] == 0 else tk

        def scores(t, c):
            j, bias_idx = tiles[t]
            nk = n_keys(t, c)
            kt = k_ref[pl.ds(pl.multiple_of(j * tk, tk), nk), :]
            cols = slice(c * ATT_SUB, (c + 1) * ATT_SUB)
            s = lax.dot_general(kt, qs[cols], (((1,), (1,)), ((), ())), preferred_element_type=F32)
            if bias_idx is not None:
                q0 = (c * ATT_SUB) % tk
                s = s + bias_scr[bias_idx, 0:nk, q0:q0 + ATT_SUB]
            return s

        def compact(t):
            return tiles[t][1] != 0

        def produce(t, c):
            s = scores(t, c)
            if not compact(t):
                return s
            cols = slice(c * ATT_SUB, (c + 1) * ATT_SUB)
            m_prev = m_scr[:, cols]
            m_new = jnp.maximum(m_prev, jnp.max(s, axis=0, keepdims=True))
            m_scr[:, cols] = m_new
            return (s - m_prev).astype(BF16), m_prev, m_new

        pending = [produce(*st) for st in stages[:ATT_LOOKAHEAD]]
        for n, (t, c) in enumerate(stages):
            cols = slice(c * ATT_SUB, (c + 1) * ATT_SUB)
            item = pending.pop(0)
            if n + ATT_LOOKAHEAD < len(stages):
                pending.append(produce(*stages[n + ATT_LOOKAHEAD]))
            if compact(t):
                d, m_prev, m_new = item
                p = jnp.exp2(d - (m_new - m_prev).astype(BF16))
            else:
                m_prev = m_scr[:, cols]
                m_new = jnp.maximum(m_prev, jnp.max(item, axis=0, keepdims=True))
                m_scr[:, cols] = m_new
                p = jnp.exp2(item - m_new).astype(BF16)
            alpha = jnp.exp2(m_prev - m_new)
            vt = vt_scr[tiles[t][0], :, 0:n_keys(t, c)]
            acc_scr[:, cols] = alpha * acc_scr[:, cols] + jnp.dot(vt, p, preferred_element_type=F32)

    @pl.when(qi >= 1)
    def _():
        tile_group([(qi, 0), (qi - 1, 1)])

    @pl.when(qi == 0)
    def _():
        tile_group([(qi, 0)])

    n_far = jnp.maximum(qi - 1, 0)

    def far_group(i, carry):
        tile_group([(ATT_GROUP * i + t, None) for t in range(ATT_GROUP)])
        return carry

    lax.fori_loop(0, n_far // ATT_GROUP, far_group, 0)
    done = n_far - n_far % ATT_GROUP
    piece = ATT_GROUP // 2
    while piece >= 1:
        @pl.when((n_far % (2 * piece)) >= piece)
        def _(done=done, piece=piece):
            tile_group([(done + t, None) for t in range(piece)])
        done = done + jnp.where((n_far % (2 * piece)) >= piece, piece, 0)
        piece //= 2

    lam_p = lam_ref[...]
    lam = (jnp.exp(jnp.sum(lam_p[0:1] * lam_p[1:2], axis=-1, keepdims=True))
           - jnp.exp(jnp.sum(lam_p[2:3] * lam_p[3:4], axis=-1, keepdims=True)) + LAMBDA_INIT)
    inv = 1.0 / acc_scr[V_HEAD_DIM:V_HEAD_DIM + 1, :]
    o_t = (acc_scr[0:V_HEAD_DIM, 0:tk] * inv[:, :tk]
           - acc_scr[0:V_HEAD_DIM, tk:] * (lam * inv[:, tk:]))
    ms = jnp.mean(o_t * o_t, axis=0, keepdims=True)
    o_n = (o_t * lax.rsqrt(ms + EPS)).T
    o_ref[...] = (o_n * g_ref[...] * (1.0 - LAMBDA_INIT)).astype(BF16)


def _attention(qkv, rel_bias, lam_params, subln_g, batch, t_pad):
    n_q = t_pad // TT
    qkv3 = qkv.reshape(batch, t_pad, 3 * D_ATT)
    out = pl.pallas_call(
        _attn_kernel,
        grid_spec=pltpu.PrefetchScalarGridSpec(
            num_scalar_prefetch=0,
            grid=(N_HEADS, batch, n_q),
            in_specs=[
                pl.BlockSpec(memory_space=pltpu.SMEM),
                pl.BlockSpec((None, TT, V_HEAD_DIM), lambda h, b, q: (b, q, h)),
                pl.BlockSpec((None, t_pad, V_HEAD_DIM), lambda h, b, q: (b, 0, N_HEADS + h)),
                pl.BlockSpec((None, t_pad, V_HEAD_DIM), lambda h, b, q: (b, 0, 2 * N_HEADS + h)),
                pl.BlockSpec((4, QK_HEAD_DIM), lambda h, b, q: (0, 0)),
                pl.BlockSpec((1, V_HEAD_DIM), lambda h, b, q: (0, 0)),
            ],
            out_specs=pl.BlockSpec((None, TT, V_HEAD_DIM), lambda h, b, q: (b, q, h)),
            scratch_shapes=[
                pltpu.VMEM((2, TT, TT), F32),
                pltpu.VMEM((n_q, VT_ROWS, TT), BF16),
                pltpu.VMEM((1, 2 * TT), F32),
                pltpu.VMEM((VT_ROWS, 2 * TT), F32),
            ],
        ),
        out_shape=jax.ShapeDtypeStruct((batch, t_pad, D_ATT), BF16),
        compiler_params=pltpu.CompilerParams(
            dimension_semantics=("arbitrary", "arbitrary", "arbitrary"),
            vmem_limit_bytes=VMEM_LIMIT),
        name="diff_attention",
    )(rel_bias, qkv3, qkv3, qkv3, lam_params, subln_g)
    return out.reshape(batch * t_pad, D_ATT)


def _route_tile(lg, carry):
    rows = lg.shape[0]
    lane = lax.broadcasted_iota(jnp.int32, lg.shape, 1)
    first = lambda mask: jnp.min(jnp.where(mask, lane, LANES), axis=-1, keepdims=True)

    is_g = lane < N_GROUPS
    gl = jnp.where(is_g, lg, -jnp.inf)
    gmax = jnp.max(gl, axis=-1, keepdims=True)
    grp = first(gl == gmax)
    gsum = jnp.sum(jnp.where(is_g, jnp.exp(gl - gmax), 0.0), axis=-1, keepdims=True)
    p_grp = 1.0 / gsum

    lane_e = lane - N_GROUPS
    in_grp = (lane_e >= 0) & (lane_e < N_EXPERTS) & ((lane_e // EXPERTS_PER_GROUP) == grp)
    el = jnp.where(in_grp, lg, -jnp.inf)
    v1 = jnp.max(el, axis=-1, keepdims=True)
    i1 = first(el == v1)
    el2 = jnp.where(lane == i1, -jnp.inf, el)
    v2 = jnp.max(el2, axis=-1, keepdims=True)
    i2 = first(el2 == v2)
    e2 = jnp.exp(v2 - v1)
    den = 1.0 + e2
    g0 = p_grp * (1.0 / den)
    g1 = p_grp * (e2 / den)

    hit1 = lane == i1
    hit2 = lane == i2
    onehot = jnp.where(hit1 | hit2, 1.0, 0.0)
    rr = lax.broadcasted_iota(jnp.int32, (rows, rows), 0)
    cc = lax.broadcasted_iota(jnp.int32, (rows, rows), 1)
    tri = jnp.where(cc < rr, 1.0, 0.0).astype(BF16)
    rank = jnp.dot(tri, onehot.astype(BF16), preferred_element_type=F32) + carry
    r0 = jnp.sum(jnp.where(hit1, rank, 0.0), axis=-1, keepdims=True).astype(jnp.int32)
    r1 = jnp.sum(jnp.where(hit2, rank, 0.0), axis=-1, keepdims=True).astype(jnp.int32)

    idx = jnp.where(lane == 0, i1 - N_GROUPS,
                    jnp.where(lane == 1, i2 - N_GROUPS,
                              jnp.where(lane == 2, r0, jnp.where(lane == 3, r1, 0))))
    gates = jnp.where(lane == 0, g0, jnp.where(lane == 1, g1, 0.0))
    return idx, gates, carry + jnp.sum(onehot, axis=0, keepdims=True)


def _outproj_kernel(yl_ref, ya_ref, h0_ref, w_ref, g_ref, wrt_ref, brt_ref, h1_ref, u_ref, lg_ref):
    acc = jnp.dot(yl_ref[...], w_ref[0:D_LRU, :], preferred_element_type=F32)
    acc = acc + jnp.dot(ya_ref[...], w_ref[D_LRU:, :], preferred_element_type=F32)
    h1 = h0_ref[...] + acc
    h1_ref[...] = h1
    u = _rmsnorm(h1, g_ref[...])
    u_ref[...] = _pack_bf16_pairs(u)
    u_hi = u.astype(BF16)
    u_lo = (u - u_hi.astype(F32)).astype(BF16)
    w = wrt_ref[...]
    w_hi = w.astype(BF16)
    w_lo = (w - w_hi.astype(F32)).astype(BF16)
    parts = jnp.dot(jnp.concatenate([u_hi, u_lo], axis=0), jnp.concatenate([w_hi, w_lo], axis=1),
                    preferred_element_type=F32)
    lg = (parts[:OUT_TM, :LANES] + parts[:OUT_TM, LANES:]) + (parts[OUT_TM:, :LANES] + parts[OUT_TM:, LANES:])
    lg_ref[...] = lg + brt_ref[...]


def _out_proj(y_lru, y_att, h0, w_out_bf16, g_ffn, w_rt, b_rt):
    n_pad = h0.shape[0]
    row = lambda w: pl.BlockSpec((OUT_TM, w), lambda m: (m, 0))
    full = lambda a, b: pl.BlockSpec((a, b), lambda m: (0, 0))
    return pl.pallas_call(
        _outproj_kernel,
        grid=(n_pad // OUT_TM,),
        in_specs=[row(D_LRU), row(D_ATT), row(D_MODEL), full(D_MODEL, D_MODEL), full(1, D_MODEL),
                  full(D_MODEL, LANES), full(1, LANES)],
        out_specs=[row(D_MODEL), row(D_MODEL // 2), row(LANES)],
        out_shape=[
            jax.ShapeDtypeStruct((n_pad, D_MODEL), F32),
            jax.ShapeDtypeStruct((n_pad, D_MODEL // 2), jnp.uint32),
            jax.ShapeDtypeStruct((n_pad, LANES), F32),
        ],
        compiler_params=pltpu.CompilerParams(
            dimension_semantics=("arbitrary",), vmem_limit_bytes=VMEM_LIMIT),
        name="out_proj",
    )(y_lru, y_att, h0, w_out_bf16, g_ffn, w_rt, b_rt)


def _route_kernel(lg_ref, idx_ref, gate_ref, cnt_ref, carry_scr):
    @pl.when(pl.program_id(0) == 0)
    def _():
        carry_scr[...] = jnp.zeros_like(carry_scr)

    idx, gates, carry = _route_tile(lg_ref[...], carry_scr[...])
    idx_ref[...] = idx
    gate_ref[...] = gates
    carry_scr[...] = carry
    cnt_ref[...] = carry


def _route(logits):
    n_pad = logits.shape[0]
    row = pl.BlockSpec((TT, LANES), lambda m: (m, 0))
    return pl.pallas_call(
        _route_kernel,
        grid=(n_pad // TT,),
        in_specs=[row],
        out_specs=[row, row, pl.BlockSpec((1, LANES), lambda m: (0, 0))],
        out_shape=[
            jax.ShapeDtypeStruct((n_pad, LANES), jnp.int32),
            jax.ShapeDtypeStruct((n_pad, LANES), F32),
            jax.ShapeDtypeStruct((1, LANES), F32),
        ],
        scratch_shapes=[pltpu.VMEM((1, LANES), F32)],
        compiler_params=pltpu.CompilerParams(
            dimension_semantics=("arbitrary",), vmem_limit_bytes=VMEM_LIMIT),
        name="route",
    )(logits)


def _dest_kernel(idx_ref, ps_ref, o_ref):
    idx = idx_ref[...]
    lane = lax.broadcasted_iota(jnp.int32, idx.shape, 1)
    ps = ps_ref[...]
    pick = lambda e: jnp.sum(jnp.where(lane == e, ps, 0.0), axis=-1, keepdims=True).astype(jnp.int32)
    d0 = pick(idx[:, 0:1]) + idx[:, 2:3]
    d1 = pick(idx[:, 1:2]) + idx[:, 3:4]
    o_ref[...] = jnp.where(lane == 0, d0, jnp.where(lane == 1, d1, 0))


def _dest(idx, pstart_row):
    n_pad = idx.shape[0]
    row = pl.BlockSpec((TT, LANES), lambda m: (m, 0))
    return pl.pallas_call(
        _dest_kernel,
        grid=(n_pad // TT,),
        in_specs=[row, pl.BlockSpec((1, LANES), lambda m: (0, 0))],
        out_specs=row,
        out_shape=jax.ShapeDtypeStruct((n_pad, LANES), jnp.int32),
        compiler_params=pltpu.CompilerParams(
            dimension_semantics=("arbitrary",), vmem_limit_bytes=VMEM_LIMIT),
        name="dest",
    )(idx, pstart_row)


def _row_copy(src, s_row, dst, d_row, sem):
    return pltpu.make_async_copy(src.at[pl.ds(s_row, 1)], dst.at[pl.ds(d_row, 1)], sem)


def _dispatch_kernel(d0_ref, d1_ref, pe_ref, u_ref, xb_ref, zero_scr, sem, zsem):
    step = pl.program_id(0)
    base = step * TT

    def zero_block(e):
        start = pl.multiple_of(pe_ref[e] - DISPATCH_BLK, DISPATCH_BLK)
        return pltpu.make_async_copy(zero_scr, xb_ref.at[pl.ds(start, DISPATCH_BLK)], zsem)

    @pl.when(step == 0)
    def _():
        zero_scr[...] = jnp.zeros_like(zero_scr)

        def nonempty(e):
            return pe_ref[e] > jnp.where(e == 0, 0, pe_ref[jnp.maximum(e - 1, 0)])

        def zstart(e, carry):
            @pl.when(nonempty(e))
            def _():
                zero_block(e).start()
            return carry

        def zwait(e, carry):
            @pl.when(nonempty(e))
            def _():
                zero_block(e).wait()
            return carry

        lax.fori_loop(0, N_EXPERTS, zstart, 0)
        lax.fori_loop(0, N_EXPERTS, zwait, 0)

        def tail_block(i):
            start = pl.multiple_of(i * DISPATCH_BLK, DISPATCH_BLK)
            return pltpu.make_async_copy(zero_scr, xb_ref.at[pl.ds(start, DISPATCH_BLK)], zsem)

        def tstart(i, carry):
            tail_block(i).start()
            return carry

        def twait(i, carry):
            tail_block(i).wait()
            return carry

        n_used = pe_ref[N_EXPERTS - 1] // DISPATCH_BLK
        n_blocks = xb_ref.shape[0] // DISPATCH_BLK
        lax.fori_loop(n_used, n_blocks, tstart, 0)
        lax.fori_loop(n_used, n_blocks, twait, 0)

    def issue(r, carry):
        _row_copy(u_ref, r, xb_ref, d0_ref[base + r], sem.at[0]).start()
        _row_copy(u_ref, r, xb_ref, d1_ref[base + r], sem.at[1]).start()
        return carry

    for r in range(TT):
        issue(r, 0)
    for k in range(2):
        pltpu.make_async_copy(u_ref, xb_ref.at[pl.ds(0, TT)], sem.at[k]).wait()


def _dispatch(d0, d1, pends, u2, rows):
    n_pad, width = u2.shape
    return pl.pallas_call(
        _dispatch_kernel,
        grid_spec=pltpu.PrefetchScalarGridSpec(
            num_scalar_prefetch=3,
            grid=(n_pad // TT,),
            in_specs=[pl.BlockSpec((TT, width), lambda m, d0, d1, pe: (m, 0))],
            out_specs=pl.BlockSpec(memory_space=pl.ANY),
            scratch_shapes=[
                pltpu.VMEM((DISPATCH_BLK, width), u2.dtype),
                pltpu.SemaphoreType.DMA((2,)),
                pltpu.SemaphoreType.DMA(()),
            ],
        ),
        out_shape=jax.ShapeDtypeStruct((rows, width), u2.dtype),
        compiler_params=pltpu.CompilerParams(
            dimension_semantics=("arbitrary",), vmem_limit_bytes=VMEM_LIMIT),
        name="dispatch",
    )(d0, d1, pends, u2)


WEIGHT_UNITS = 8
WEIGHT_DEPTH = 4


def _stream_weights(i, tgt_ref, eseq_ref, ntot_ref, w_hbm, stages, wbfs, sem, done_scr):
    def slab(u, k):
        rows = stages[k].shape[1]
        return pl.ds(pl.multiple_of((u % WEIGHT_UNITS) * rows, rows), rows)

    def unit_copies(u):
        e = eseq_ref[u // WEIGHT_UNITS]
        b = u % WEIGHT_DEPTH
        return [pltpu.make_async_copy(w.at[e, slab(u, k), :], stages[k].at[b], sem.at[b, k])
                for k, w in enumerate(w_hbm)]

    @pl.when(i == 0)
    def _():
        done_scr[0] = 0
        for u in range(WEIGHT_DEPTH):
            for c in unit_copies(u):
                c.start()

    def body(u, carry):
        for c in unit_copies(u):
            c.wait()
        slot = (u // WEIGHT_UNITS) % 2
        b = u % WEIGHT_DEPTH
        for k in range(len(w_hbm)):
            wbfs[k][slot, slab(u, k), :] = stages[k][b].astype(BF16)

        @pl.when(u + WEIGHT_DEPTH < ntot_ref[0])
        def _():
            for c in unit_copies(u + WEIGHT_DEPTH):
                c.start()

        return carry

    lax.fori_loop(done_scr[0], tgt_ref[i], body, 0)
    done_scr[0] = tgt_ref[i]


def _expert_kernel(tgt_ref, eseq_ref, ntot_ref, slot_ref, nu_ref, x_ref, wg_hbm, wu_hbm, wd_hbm, y_ref,
                   sg, su, sd, bg, bu, bd, sem, done_scr):
    i = pl.program_id(0)
    _stream_weights(i, tgt_ref, eseq_ref, ntot_ref, (wg_hbm, wu_hbm, wd_hbm), (sg, su, sd), (bg, bu, bd),
                    sem, done_scr)
    used = i < nu_ref[0]

    @pl.when(used)
    def _():
        slot = slot_ref[i]
        x_lo, x_hi = _unpack_bf16_pairs(x_ref[...])
        half = D_MODEL // 2
        up = lambda w: (jnp.dot(x_lo, w[slot, 0:half, :], preferred_element_type=F32)
                        + jnp.dot(x_hi, w[slot, half:, :], preferred_element_type=F32))
        g = up(bg)
        u = up(bu)
        a = (jax.nn.silu(g) * u).astype(BF16)
        y_ref[...] = jnp.dot(a, bd[slot], preferred_element_type=F32)

    @pl.when(jnp.logical_not(used))
    def _():
        y_ref[...] = jnp.zeros_like(y_ref)


def _experts(sched, x_buf, w_gate, w_up, w_down):
    rows = x_buf.shape[0]
    n_blocks = rows // DISPATCH_BLK
    blk_map = lambda i, tg, es, nt, sl, nu: (jnp.minimum(i, jnp.maximum(nu[0] - 1, 0)), 0)
    out_map = lambda i, tg, es, nt, sl, nu: (i, 0)
    hbm = pl.BlockSpec(memory_space=pl.ANY)
    stage = lambda k_dim, n_dim: pltpu.VMEM((WEIGHT_DEPTH, k_dim // WEIGHT_UNITS, n_dim), F32)
    resident = lambda k_dim, n_dim: pltpu.VMEM((2, k_dim, n_dim), BF16)
    return pl.pallas_call(
        _expert_kernel,
        grid_spec=pltpu.PrefetchScalarGridSpec(
            num_scalar_prefetch=5,
            grid=(n_blocks,),
            in_specs=[pl.BlockSpec((DISPATCH_BLK, x_buf.shape[1]), blk_map), hbm, hbm, hbm],
            out_specs=pl.BlockSpec((DISPATCH_BLK, D_MODEL), out_map),
            scratch_shapes=[
                stage(D_MODEL, D_EXPERT), stage(D_MODEL, D_EXPERT), stage(D_EXPERT, D_MODEL),
                resident(D_MODEL, D_EXPERT), resident(D_MODEL, D_EXPERT), resident(D_EXPERT, D_MODEL),
                pltpu.SemaphoreType.DMA((WEIGHT_DEPTH, 3)),
                pltpu.SMEM((1,), jnp.int32),
            ],
        ),
        out_shape=jax.ShapeDtypeStruct((rows, D_MODEL), F32),
        compiler_params=pltpu.CompilerParams(
            dimension_semantics=("arbitrary",), vmem_limit_bytes=VMEM_LIMIT),
        name="experts",
    )(*sched, x_buf, w_gate, w_up, w_down)


def _expert_schedule(padded, pstarts, n_blocks):
    nonempty = padded > 0
    seq_of_expert = jnp.cumsum(nonempty.astype(jnp.int32)) - 1
    n_seq = jnp.sum(nonempty.astype(jnp.int32))
    experts = jnp.arange(N_EXPERTS, dtype=jnp.int32)
    is_kth = jnp.logical_and(nonempty[None, :], seq_of_expert[None, :] == experts[:, None])
    eseq = jnp.sum(jnp.where(is_kth, experts[None, :], 0), axis=1)
    pends = pstarts + padded
    n_used = pends[-1] // DISPATCH_BLK
    blk = jnp.arange(n_blocks, dtype=jnp.int32)
    blk_c = jnp.minimum(blk, jnp.maximum(n_used - 1, 0))
    blk_expert = jnp.minimum(
        jnp.sum((pends[None, :] <= (blk_c * DISPATCH_BLK)[:, None]).astype(jnp.int32), axis=1), N_EXPERTS - 1)
    onehot = (blk_expert[:, None] == experts[None, :]).astype(jnp.int32)
    pick = lambda v: jnp.sum(onehot * v[None, :], axis=1)
    q = pick(seq_of_expert)
    j = blk_c - pick(pstarts) // DISPATCH_BLK
    n = jnp.maximum(pick(padded) // DISPATCH_BLK, 1)
    n_total = n_seq * WEIGHT_UNITS
    tgt = jnp.minimum(WEIGHT_UNITS * (q + 1) + (WEIGHT_UNITS * j) // n, n_total)
    tgt = jnp.where(blk < n_used, tgt, n_total)
    i32 = lambda a: a.astype(jnp.int32)
    return i32(tgt), eseq, i32(n_total)[None], i32(q % 2), i32(n_used)[None]


COMBINE_TM = 512


def _combine_kernel(t_pad, d0_ref, d1_ref, h_hbm, gate_hbm, g_ref, yb_ref, o_ref,
                    h_scr, gate_scr, y0_scr, y1_scr, sem):
    n_j = pl.num_programs(1)
    step = pl.program_id(0) * n_j + pl.program_id(1)
    n_steps = pl.num_programs(0) * n_j

    def base_row(s):
        return pl.multiple_of((s // n_j) * t_pad + N_META + (s % n_j) * COMBINE_TM, 8)

    def tile_copies(s, slot):
        base = base_row(s)
        return [
            pltpu.make_async_copy(h_hbm.at[pl.ds(base, COMBINE_TM)], h_scr.at[slot], sem.at[slot, 2]),
            pltpu.make_async_copy(gate_hbm.at[pl.ds(base, COMBINE_TM)], gate_scr.at[slot], sem.at[slot, 3]),
        ]

    def fetch(s, slot):
        base = base_row(s)
        for c in tile_copies(s, slot):
            c.start()

        def issue(r, carry):
            _row_copy(yb_ref, d0_ref[base + r], y0_scr.at[slot], r, sem.at[slot, 0]).start()
            _row_copy(yb_ref, d1_ref[base + r], y1_scr.at[slot], r, sem.at[slot, 1]).start()
            return carry

        for r in range(COMBINE_TM):
            issue(r, 0)

    slot = step % 2

    @pl.when(step == 0)
    def _():
        fetch(step, slot)

    @pl.when(step + 1 < n_steps)
    def _():
        fetch(step + 1, 1 - slot)

    pltpu.make_async_copy(yb_ref.at[pl.ds(0, COMBINE_TM)], y0_scr.at[slot], sem.at[slot, 0]).wait()
    pltpu.make_async_copy(yb_ref.at[pl.ds(0, COMBINE_TM)], y1_scr.at[slot], sem.at[slot, 1]).wait()
    for c in tile_copies(step, slot):
        c.wait()

    gate = gate_scr[slot]
    h = h_scr[slot] + gate[:, 0:1] * y0_scr[slot] + gate[:, 1:2] * y1_scr[slot]
    o_ref[...] = _rmsnorm(h, g_ref[...])


def _combine(d0, d1, h1, gates, g_final, y_buf, batch, seq):
    assert seq % COMBINE_TM == 0 and N_META % 8 == 0
    tile = lambda w: pltpu.VMEM((2, COMBINE_TM, w), F32)
    return pl.pallas_call(
        functools.partial(_combine_kernel, h1.shape[0] // batch),
        grid_spec=pltpu.PrefetchScalarGridSpec(
            num_scalar_prefetch=2,
            grid=(batch, seq // COMBINE_TM),
            in_specs=[
                pl.BlockSpec(memory_space=pl.ANY),
                pl.BlockSpec(memory_space=pl.ANY),
                pl.BlockSpec((1, D_MODEL), lambda b, j, d0, d1: (0, 0)),
                pl.BlockSpec(memory_space=pl.ANY),
            ],
            out_specs=pl.BlockSpec((None, COMBINE_TM, D_MODEL), lambda b, j, d0, d1: (b, j, 0)),
            scratch_shapes=[tile(D_MODEL), tile(LANES), tile(D_MODEL), tile(D_MODEL),
                            pltpu.SemaphoreType.DMA((2, 4))],
        ),
        out_shape=jax.ShapeDtypeStruct((batch, seq, D_MODEL), F32),
        compiler_params=pltpu.CompilerParams(
            dimension_semantics=("arbitrary", "arbitrary"), vmem_limit_bytes=VMEM_LIMIT),
        name="combine",
    )(d0, d1, h1, gates, g_final, y_buf)


def _block_diag(w):
    per = LRU_CB // LRU_BLOCK_DIM
    w4 = w.reshape(LRU_BLOCKS // per, per, LRU_BLOCK_DIM, LRU_BLOCK_DIM)
    eye = jnp.eye(per, dtype=w.dtype)
    bd = jnp.einsum('cpij,pq->cpiqj', w4, eye)
    return bd.reshape(LRU_BLOCKS // per, LRU_CB, LRU_CB)


def kernel(x, meta_tokens, g_mix, w_in, conv_w, conv_b, w_rgate, b_rgate, w_igate, b_igate, lru_L, lambda_q1, lambda_k1, lambda_q2, lambda_k2, subln_g, rel_bias, w_out, g_ffn, w_group, b_group, w_router, b_router, w_gate, w_up, w_down, g_final):
    batch, seq, _ = x.shape
    t_real = N_META + seq
    t_pad = _round_up(t_real, TT)
    n_pad = batch * t_pad

    h0, proj_lru, proj_qkv = _in_proj(x, meta_tokens.astype(x.dtype), g_mix[0][None],
                                      w_in[0].astype(BF16), t_pad)

    y_lru = _lru(proj_lru, conv_w[0], conv_b[0][None],
                 _block_diag(w_rgate[0]).astype(BF16), b_rgate[0].reshape(1, D_LRU),
                 _block_diag(w_igate[0]).astype(BF16), b_igate[0].reshape(1, D_LRU),
                 lru_L[0][None], batch, t_pad)

    lam_params = jnp.stack([lambda_q1[0], lambda_k1[0], lambda_q2[0], lambda_k2[0]])
    y_att = _attention(proj_qkv, rel_bias, lam_params, subln_g[0][None], batch, t_pad)

    w_rt = jnp.concatenate([w_group[0], w_router[0],
                            jnp.zeros((D_MODEL, LANES - N_GROUPS - N_EXPERTS), F32)], axis=1)
    b_rt = jnp.concatenate([b_group[0], b_router[0],
                            jnp.zeros((LANES - N_GROUPS - N_EXPERTS,), F32)])[None]
    h1, u2, logits = _out_proj(y_lru, y_att, h0, w_out[0].astype(BF16), g_ffn[0][None], w_rt, b_rt)

    idx, gates, counts = _route(logits)

    cnt = counts[0, N_GROUPS:N_GROUPS + N_EXPERTS].astype(jnp.int32)
    padded = (cnt + DISPATCH_BLK - 1) // DISPATCH_BLK * DISPATCH_BLK
    pends = jnp.cumsum(padded)
    pstarts = pends - padded
    dest = _dest(idx, jnp.pad(pstarts.astype(F32), (0, LANES - N_EXPERTS))[None])
    d0 = dest[:, 0]
    d1 = dest[:, 1]
    n_blocks = -(-(2 * n_pad + N_EXPERTS * (DISPATCH_BLK - 1)) // DISPATCH_BLK)
    rows = n_blocks * DISPATCH_BLK
    sched = _expert_schedule(padded, pstarts, n_blocks)

    x_buf = _dispatch(d0, d1, pends.astype(jnp.int32), u2, rows)
    y_buf = _experts(sched, x_buf, w_gate[0], w_up[0], w_down[0])
    return _combine(d0, d1, h1, gates, g_final[None], y_buf, batch, seq)
```

```python
import functools
import math

import numpy as np
import jax
import jax.numpy as jnp
from jax import lax
from jax.experimental import pallas as pl
from jax.experimental.pallas import tpu as pltpu

D_MODEL = 2048
N_META = 16
D_LRU = 1024
D_ATT = 1024
LRU_BLOCKS = 16
LRU_BLOCK_DIM = 64
CONV_WIDTH = 4
LRU_C = 8.0
N_HEADS = 8
V_HEAD_DIM = 128
QK_HEAD_DIM = 64
N_BUCKETS = 32
MAX_DISTANCE = 128
N_GROUPS = 4
EXPERTS_PER_GROUP = 8
N_EXPERTS = 32
D_EXPERT = 1024
D_IN_PROJ = 2 * D_LRU + 3 * D_ATT
EPS = 1e-6
LAMBDA_INIT = 0.8 - 0.6 * math.exp(-0.3 * 0)
LOG2_E = math.log2(math.e)

F32 = jnp.float32
BF16 = jnp.bfloat16

LANES = 128
SUBLANES = 8
MXU_DIM = 256
TT = 768
PROJ_TN = 1024
OUT_TM = TT // 2
ATT_SUB = MXU_DIM
ATT_LOOKAHEAD = 4
ATT_GROUP = 4
VT_ROWS = V_HEAD_DIM + 16
LRU_CB = MXU_DIM
DISPATCH_BLK = 256
VMEM_LIMIT = 56 * 1024 * 1024
NEG_BIG = -1e30


def _round_up(a, b):
    return -(-a // b) * b


def _pack_bf16_pairs(x):
    n = x.shape[1] // 2
    bits = lambda v: pltpu.bitcast(v.astype(BF16).astype(F32), jnp.uint32)
    return lax.shift_right_logical(bits(x[:, :n]), jnp.uint32(16)) | bits(x[:, n:])


def _unpack_bf16_pairs(w):
    lo = pltpu.bitcast(lax.shift_left(w, jnp.uint32(16)), F32)
    hi = pltpu.bitcast(w & jnp.uint32(0xFFFF0000), F32)
    return lo.astype(BF16), hi.astype(BF16)


def _rmsnorm(x, g):
    ms = jnp.mean(x * x, axis=-1, keepdims=True)
    return x * lax.rsqrt(ms + EPS) * g


def _inproj_kernel(n_t, seq, x_hbm, meta_ref, g_ref, w_ref, h0_ref, lru_ref, qkv_ref, xbuf, u_scr, sem):
    m = pl.program_id(0)
    n = pl.program_id(1)
    n_m = pl.num_programs(0)
    last_rows = seq + N_META - (n_t - 1) * TT

    def tile_copy(mm, slot):
        b = mm // n_t
        t = mm % n_t
        row0 = pl.multiple_of(t * TT - N_META, SUBLANES)
        return [
            (t == 0, pltpu.make_async_copy(x_hbm.at[b, pl.ds(0, TT - N_META)],
                                           xbuf.at[slot, pl.ds(N_META, TT - N_META)], sem.at[slot])),
            (jnp.logical_and(t > 0, t < n_t - 1),
             pltpu.make_async_copy(x_hbm.at[b, pl.ds(row0, TT)], xbuf.at[slot], sem.at[slot])),
            (t == n_t - 1, pltpu.make_async_copy(x_hbm.at[b, pl.ds(row0, last_rows)],
                                                 xbuf.at[slot, pl.ds(0, last_rows)], sem.at[slot])),
        ]

    def start(mm, slot):
        for cond, copy in tile_copy(mm, slot):
            @pl.when(cond)
            def _(copy=copy):
                copy.start()

    def wait(mm, slot):
        for cond, copy in tile_copy(mm, slot):
            @pl.when(cond)
            def _(copy=copy):
                copy.wait()

    slot = m % 2

    @pl.when(jnp.logical_and(m == 0, n == 0))
    def _():
        start(m, slot)

    @pl.when(jnp.logical_and(n == 1, m + 1 < n_m))
    def _():
        start(m + 1, 1 - slot)

    @pl.when(n == 0)
    def _():
        wait(m, slot)
        t = m % n_t

        @pl.when(t == 0)
        def _():
            xbuf[slot, 0:N_META, :] = meta_ref[...]

        @pl.when(t == n_t - 1)
        def _():
            xbuf[slot, last_rows:TT, :] = jnp.zeros((TT - last_rows, D_MODEL), F32)

        h0 = xbuf[slot]
        h0_ref[...] = h0
        u_scr[...] = _rmsnorm(h0, g_ref[...]).astype(BF16)

    y = jnp.dot(u_scr[...], w_ref[...], preferred_element_type=F32)

    @pl.when(n < 2)
    def _():
        lru_ref[...] = y

    @pl.when(n >= 2)
    def _():
        qkv_ref[...] = y.astype(BF16)


def _in_proj(x, meta_tokens, g_mix, w_in_bf16, t_pad):
    batch, seq, _ = x.shape
    n_t = t_pad // TT
    n_pad = batch * t_pad
    n_col = D_IN_PROJ // PROJ_TN
    assert N_META % SUBLANES == 0 and 0 < seq + N_META - (n_t - 1) * TT <= TT
    return pl.pallas_call(
        functools.partial(_inproj_kernel, n_t, seq),
        grid=(n_pad // TT, n_col),
        in_specs=[
            pl.BlockSpec(memory_space=pl.ANY),
            pl.BlockSpec((N_META, D_MODEL), lambda m, n: (0, 0)),
            pl.BlockSpec((1, D_MODEL), lambda m, n: (0, 0)),
            pl.BlockSpec((D_MODEL, PROJ_TN), lambda m, n: (0, n)),
        ],
        out_specs=[
            pl.BlockSpec((TT, D_MODEL), lambda m, n: (m, 0)),
            pl.BlockSpec((TT, PROJ_TN), lambda m, n: (m, jnp.minimum(n, 1))),
            pl.BlockSpec((TT, PROJ_TN), lambda m, n: (m, jnp.maximum(n - 2, 0))),
        ],
        out_shape=[
            jax.ShapeDtypeStruct((n_pad, D_MODEL), F32),
            jax.ShapeDtypeStruct((n_pad, 2 * D_LRU), F32),
            jax.ShapeDtypeStruct((n_pad, 3 * D_ATT), BF16),
        ],
        scratch_shapes=[
            pltpu.VMEM((2, TT, D_MODEL), F32),
            pltpu.VMEM((TT, D_MODEL), BF16),
            pltpu.SemaphoreType.DMA((2,)),
        ],
        compiler_params=pltpu.CompilerParams(
            dimension_semantics=("arbitrary", "arbitrary"), vmem_limit_bytes=VMEM_LIMIT),
        name="in_proj",
    )(x, meta_tokens, g_mix, w_in_bf16)


def _lru_kernel(x_ref, gate_ref, cw_ref, cb_ref, wr_ref, br_ref, wi_ref, bi_ref, lam_ref,
                y_ref, xpad_scr, h_scr):
    t = pl.program_id(2)

    @pl.when(t == 0)
    def _():
        xpad_scr[0:SUBLANES, :] = jnp.zeros((SUBLANES, LRU_CB), F32)
        h_scr[...] = jnp.zeros_like(h_scr)

    x = x_ref[...]
    xpad_scr[SUBLANES:, :] = x
    cw = cw_ref[...]
    xc = cb_ref[...] + x * cw[CONV_WIDTH - 1:CONV_WIDTH]
    for k in range(CONV_WIDTH - 1):
        shift = CONV_WIDTH - 1 - k
        xc = xc + xpad_scr[pl.ds(SUBLANES - shift, TT), :] * cw[k:k + 1]
    xpad_scr[0:SUBLANES, :] = x[TT - SUBLANES:TT]

    xb = xc.astype(BF16)
    r = jax.nn.sigmoid(jnp.dot(xb, wr_ref[...], preferred_element_type=F32) + br_ref[...])
    i = jax.nn.sigmoid(jnp.dot(xb, wi_ref[...], preferred_element_type=F32) + bi_ref[...])
    lam = lam_ref[...]
    log_sig = jnp.minimum(lam, 0.0) - jnp.log(1.0 + jnp.exp(-jnp.abs(lam)))
    log_a = (LRU_C * r) * log_sig
    a = jnp.exp(log_a)
    one_m_a2 = 1.0 - a * a
    root = jnp.where(one_m_a2 > 0.0, one_m_a2 * lax.rsqrt(one_m_a2), 0.0)
    b = root * (i * xc)

    row = lax.broadcasted_iota(jnp.int32, (SUBLANES, LRU_CB), 0)
    carry = h_scr[...]
    groups = []
    for g in range(TT // SUBLANES):
        rows = slice(g * SUBLANES, (g + 1) * SUBLANES)
        ag, bg = a[rows], b[rows]
        s = 1
        while s < SUBLANES:
            keep = row >= s
            a_sh = jnp.where(keep, pltpu.roll(ag, s, 0), 1.0)
            b_sh = jnp.where(keep, pltpu.roll(bg, s, 0), 0.0)
            bg = ag * b_sh + bg
            ag = ag * a_sh
            s *= 2
        hg = bg + ag * carry
        groups.append(hg)
        carry = jnp.broadcast_to(hg[SUBLANES - 1:SUBLANES], hg.shape)
    h_scr[...] = carry
    h = jnp.concatenate(groups, axis=0)
    y_ref[...] = (h * jax.nn.gelu(gate_ref[...])).astype(BF16)


def _lru(proj_lru, conv_w, conv_b, wr_bd, b_r, wi_bd, b_i, lru_l, batch, t_pad):
    n_pad = proj_lru.shape[0]
    n_t = t_pad // TT
    n_c = D_LRU // LRU_CB
    vec = lambda: pl.BlockSpec((1, LRU_CB), lambda b, c, t: (0, c))
    return pl.pallas_call(
        _lru_kernel,
        grid=(batch, n_c, n_t),
        in_specs=[
            pl.BlockSpec((TT, LRU_CB), lambda b, c, t: (b * n_t + t, c)),
            pl.BlockSpec((TT, LRU_CB), lambda b, c, t: (b * n_t + t, n_c + c)),
            pl.BlockSpec((CONV_WIDTH, LRU_CB), lambda b, c, t: (0, c)),
            vec(),
            pl.BlockSpec((None, LRU_CB, LRU_CB), lambda b, c, t: (c, 0, 0)),
            vec(),
            pl.BlockSpec((None, LRU_CB, LRU_CB), lambda b, c, t: (c, 0, 0)),
            vec(),
            vec(),
        ],
        out_specs=pl.BlockSpec((TT, LRU_CB), lambda b, c, t: (b * n_t + t, c)),
        out_shape=jax.ShapeDtypeStruct((n_pad, D_LRU), BF16),
        scratch_shapes=[pltpu.VMEM((SUBLANES + TT, LRU_CB), F32), pltpu.VMEM((SUBLANES, LRU_CB), F32)],
        compiler_params=pltpu.CompilerParams(
            dimension_semantics=("arbitrary", "arbitrary", "arbitrary"),
            vmem_limit_bytes=VMEM_LIMIT),
        name="rglru",
    )(proj_lru, proj_lru, conv_w, conv_b, wr_bd, b_r, wi_bd, b_i, lru_l)


def _bucket_thresholds():
    max_exact = N_BUCKETS // 2
    n = np.arange(0, MAX_DISTANCE + 1)
    nf = np.maximum(n, 1).astype(np.float32)
    large = max_exact + (np.log(nf / np.float32(max_exact)) / np.float32(math.log(MAX_DISTANCE / max_exact))
                         * np.float32(N_BUCKETS - max_exact)).astype(np.int32)
    large = np.minimum(large, N_BUCKETS - 1)
    bucket = np.where(n < max_exact, n, large)
    return [int(np.argmax(bucket >= k)) for k in range(1, N_BUCKETS)]


_BUCKET_THR = _bucket_thresholds()


def _attn_kernel(rb_ref, q_ref, k_ref, v_ref, lam_ref, g_ref, o_ref, bias_scr, vt_scr, m_scr, acc_scr):
    h = pl.program_id(0)
    b = pl.program_id(1)
    qi = pl.program_id(2)
    tk = TT
    n_kv = vt_scr.shape[0]

    @pl.when(jnp.logical_and(b == 0, qi == 0))
    def _():
        k_pos = lax.broadcasted_iota(jnp.int32, (tk, tk), 0)
        q_pos = lax.broadcasted_iota(jnp.int32, (tk, tk), 1)
        far = rb_ref[N_BUCKETS - 1, h]
        for d in range(2):
            rel = q_pos - k_pos + d * tk
            val = jnp.full((tk, tk), rb_ref[0, h] - far, F32)
            for kk, thr in enumerate(_BUCKET_THR):
                val = jnp.where(rel >= thr, rb_ref[kk + 1, h] - far, val)
            val = val * LOG2_E
            if d == 0:
                val = jnp.where(rel >= 0, val, -jnp.inf)
            bias_scr[d] = val

    @pl.when(qi == 0)
    def _():
        for t in range(n_kv):
            vt_scr[t, 0:V_HEAD_DIM, :] = v_ref[t * tk:(t + 1) * tk, :].astype(F32).T.astype(BF16)
            vt_scr[t, V_HEAD_DIM:VT_ROWS, :] = jnp.ones((VT_ROWS - V_HEAD_DIM, tk), BF16)

    q = q_ref[...]
    lane = lax.broadcasted_iota(jnp.int32, q.shape, 1)
    scale = (QK_HEAD_DIM ** -0.5) * LOG2_E
    qf = q.astype(F32) * scale
    qs = jnp.concatenate([jnp.where(lane < QK_HEAD_DIM, qf, 0.0),
                          jnp.where(lane >= QK_HEAD_DIM, qf, 0.0)], axis=0).astype(BF16)

    m_scr[...] = jnp.full(m_scr.shape, NEG_BIG, F32)
    acc_scr[...] = jnp.zeros_like(acc_scr)

    def tile_group(tiles):
        n_sub = 2 * tk // ATT_SUB
        stages = [(t, c) for t in range(len(tiles)) for c in range(n_sub)]

        def n_keys(t, c):
            return (c * ATT_SUB) % tk + ATT_SUB if tiles[t][1] == 0 else tk

        def scores(t, c):
            j, bias_idx = tiles[t]
            nk = n_keys(t, c)
            kt = k_ref[pl.ds(pl.multiple_of(j * tk, tk), nk), :]
            cols = slice(c * ATT_SUB, (c + 1) * ATT_SUB)
            s = lax.dot_general(kt, qs[cols], (((1,), (1,)), ((), ())), preferred_element_type=F32)
            if bias_idx is not None:
                q0 = (c * ATT_SUB) % tk
                s = s + bias_scr[bias_idx, 0:nk, q0:q0 + ATT_SUB]
            return s

        def compact(t):
            return tiles[t][1] != 0

        def produce(t, c):
            s = scores(t, c)
            if not compact(t):
                return s
            cols = slice(c * ATT_SUB, (c + 1) * ATT_SUB)
            m_prev = m_scr[:, cols]
            m_new = jnp.maximum(m_prev, jnp.max(s, axis=0, keepdims=True))
            m_scr[:, cols] = m_new
            return (s - m_prev).astype(BF16), m_prev, m_new

        pending = [produce(*st) for st in stages[:ATT_LOOKAHEAD]]
        for n, (t, c) in enumerate(stages):
            cols = slice(c * ATT_SUB, (c + 1) * ATT_SUB)
            item = pending.pop(0)
            if n + ATT_LOOKAHEAD < len(stages):
                pending.append(produce(*stages[n + ATT_LOOKAHEAD]))
            if compact(t):
                d, m_prev, m_new = item
                p = jnp.exp2(d - (m_new - m_prev).astype(BF16))
            else:
                m_prev = m_scr[:, cols]
                m_new = jnp.maximum(m_prev, jnp.max(item, axis=0, keepdims=True))
                m_scr[:, cols] = m_new
                p = jnp.exp2(item - m_new).astype(BF16)
            alpha = jnp.exp2(m_prev - m_new)
            vt = vt_scr[tiles[t][0], :, 0:n_keys(t, c)]
            acc_scr[:, cols] = alpha * acc_scr[:, cols] + jnp.dot(vt, p, preferred_element_type=F32)

    @pl.when(qi >= 1)
    def _():
        tile_group([(qi, 0), (qi - 1, 1)])

    @pl.when(qi == 0)
    def _():
        tile_group([(qi, 0)])

    n_far = jnp.maximum(qi - 1, 0)

    def far_group(i, carry):
        tile_group([(ATT_GROUP * i + t, None) for t in range(ATT_GROUP)])
        return carry

    lax.fori_loop(0, n_far // ATT_GROUP, far_group, 0)
    done = n_far - n_far % ATT_GROUP
    piece = ATT_GROUP // 2
    while piece >= 1:
        @pl.when((n_far % (2 * piece)) >= piece)
        def _(done=done, piece=piece):
            tile_group([(done + t, None) for t in range(piece)])
        done = done + jnp.where((n_far % (2 * piece)) >= piece, piece, 0)
        piece //= 2

    lam_p = lam_ref[...]
    lam = (jnp.exp(jnp.sum(lam_p[0:1] * lam_p[1:2], axis=-1, keepdims=True))
           - jnp.exp(jnp.sum(lam_p[2:3] * lam_p[3:4], axis=-1, keepdims=True)) + LAMBDA_INIT)
    inv = 1.0 / acc_scr[V_HEAD_DIM:V_HEAD_DIM + 1, :]
    o_t = (acc_scr[0:V_HEAD_DIM, 0:tk] * inv[:, :tk]
           - acc_scr[0:V_HEAD_DIM, tk:] * (lam * inv[:, tk:]))
    ms = jnp.mean(o_t * o_t, axis=0, keepdims=True)
    o_n = (o_t * lax.rsqrt(ms + EPS)).T
    o_ref[...] = (o_n * g_ref[...] * (1.0 - LAMBDA_INIT)).astype(BF16)


def _attention(qkv, rel_bias, lam_params, subln_g, batch, t_pad):
    n_q = t_pad // TT
    qkv3 = qkv.reshape(batch, t_pad, 3 * D_ATT)
    out = pl.pallas_call(
        _attn_kernel,
        grid_spec=pltpu.PrefetchScalarGridSpec(
            num_scalar_prefetch=0,
            grid=(N_HEADS, batch, n_q),
            in_specs=[
                pl.BlockSpec(memory_space=pltpu.SMEM),
                pl.BlockSpec((None, TT, V_HEAD_DIM), lambda h, b, q: (b, q, h)),
                pl.BlockSpec((None, t_pad, V_HEAD_DIM), lambda h, b, q: (b, 0, N_HEADS + h)),
                pl.BlockSpec((None, t_pad, V_HEAD_DIM), lambda h, b, q: (b, 0, 2 * N_HEADS + h)),
                pl.BlockSpec((4, QK_HEAD_DIM), lambda h, b, q: (0, 0)),
                pl.BlockSpec((1, V_HEAD_DIM), lambda h, b, q: (0, 0)),
            ],
            out_specs=pl.BlockSpec((None, TT, V_HEAD_DIM), lambda h, b, q: (b, q, h)),
            scratch_shapes=[
                pltpu.VMEM((2, TT, TT), F32),
                pltpu.VMEM((n_q, VT_ROWS, TT), BF16),
                pltpu.VMEM((1, 2 * TT), F32),
                pltpu.VMEM((VT_ROWS, 2 * TT), F32),
            ],
        ),
        out_shape=jax.ShapeDtypeStruct((batch, t_pad, D_ATT), BF16),
        compiler_params=pltpu.CompilerParams(
            dimension_semantics=("arbitrary", "arbitrary", "arbitrary"),
            vmem_limit_bytes=VMEM_LIMIT),
        name="diff_attention",
    )(rel_bias, qkv3, qkv3, qkv3, lam_params, subln_g)
    return out.reshape(batch * t_pad, D_ATT)


def _route_tile(lg, carry):
    rows = lg.shape[0]
    lane = lax.broadcasted_iota(jnp.int32, lg.shape, 1)
    first = lambda mask: jnp.min(jnp.where(mask, lane, LANES), axis=-1, keepdims=True)

    is_g = lane < N_GROUPS
    gl = jnp.where(is_g, lg, -jnp.inf)
    gmax = jnp.max(gl, axis=-1, keepdims=True)
    grp = first(gl == gmax)
    gsum = jnp.sum(jnp.where(is_g, jnp.exp(gl - gmax), 0.0), axis=-1, keepdims=True)
    p_grp = 1.0 / gsum

    lane_e = lane - N_GROUPS
    in_grp = (lane_e >= 0) & (lane_e < N_EXPERTS) & ((lane_e // EXPERTS_PER_GROUP) == grp)
    el = jnp.where(in_grp, lg, -jnp.inf)
    v1 = jnp.max(el, axis=-1, keepdims=True)
    i1 = first(el == v1)
    el2 = jnp.where(lane == i1, -jnp.inf, el)
    v2 = jnp.max(el2, axis=-1, keepdims=True)
    i2 = first(el2 == v2)
    e2 = jnp.exp(v2 - v1)
    den = 1.0 + e2
    g0 = p_grp * (1.0 / den)
    g1 = p_grp * (e2 / den)

    hit1 = lane == i1
    hit2 = lane == i2
    onehot = jnp.where(hit1 | hit2, 1.0, 0.0)
    rr = lax.broadcasted_iota(jnp.int32, (rows, rows), 0)
    cc = lax.broadcasted_iota(jnp.int32, (rows, rows), 1)
    tri = jnp.where(cc < rr, 1.0, 0.0).astype(BF16)
    rank = jnp.dot(tri, onehot.astype(BF16), preferred_element_type=F32) + carry
    r0 = jnp.sum(jnp.where(hit1, rank, 0.0), axis=-1, keepdims=True).astype(jnp.int32)
    r1 = jnp.sum(jnp.where(hit2, rank, 0.0), axis=-1, keepdims=True).astype(jnp.int32)

    idx = jnp.where(lane == 0, i1 - N_GROUPS,
                    jnp.where(lane == 1, i2 - N_GROUPS,
                              jnp.where(lane == 2, r0, jnp.where(lane == 3, r1, 0))))
    gates = jnp.where(lane == 0, g0, jnp.where(lane == 1, g1, 0.0))
    return idx, gates, carry + jnp.sum(onehot, axis=0, keepdims=True)


def _outproj_kernel(yl_ref, ya_ref, h0_ref, w_ref, g_ref, wrt_ref, brt_ref, h1_ref, u_ref, lg_ref):
    acc = jnp.dot(yl_ref[...], w_ref[0:D_LRU, :], preferred_element_type=F32)
    acc = acc + jnp.dot(ya_ref[...], w_ref[D_LRU:, :], preferred_element_type=F32)
    h1 = h0_ref[...] + acc
    h1_ref[...] = h1
    u = _rmsnorm(h1, g_ref[...])
    u_ref[...] = _pack_bf16_pairs(u)
    u_hi = u.astype(BF16)
    u_lo = (u - u_hi.astype(F32)).astype(BF16)
    w = wrt_ref[...]
    w_hi = w.astype(BF16)
    w_lo = (w - w_hi.astype(F32)).astype(BF16)
    parts = jnp.dot(jnp.concatenate([u_hi, u_lo], axis=0), jnp.concatenate([w_hi, w_lo], axis=1),
                    preferred_element_type=F32)
    lg = (parts[:OUT_TM, :LANES] + parts[:OUT_TM, LANES:]) + (parts[OUT_TM:, :LANES] + parts[OUT_TM:, LANES:])
    lg_ref[...] = lg + brt_ref[...]


def _out_proj(y_lru, y_att, h0, w_out_bf16, g_ffn, w_rt, b_rt):
    n_pad = h0.shape[0]
    row = lambda w: pl.BlockSpec((OUT_TM, w), lambda m: (m, 0))
    full = lambda a, b: pl.BlockSpec((a, b), lambda m: (0, 0))
    return pl.pallas_call(
        _outproj_kernel,
        grid=(n_pad // OUT_TM,),
        in_specs=[row(D_LRU), row(D_ATT), row(D_MODEL), full(D_MODEL, D_MODEL), full(1, D_MODEL),
                  full(D_MODEL, LANES), full(1, LANES)],
        out_specs=[row(D_MODEL), row(D_MODEL // 2), row(LANES)],
        out_shape=[
            jax.ShapeDtypeStruct((n_pad, D_MODEL), F32),
            jax.ShapeDtypeStruct((n_pad, D_MODEL // 2), jnp.uint32),
            jax.ShapeDtypeStruct((n_pad, LANES), F32),
        ],
        compiler_params=pltpu.CompilerParams(
            dimension_semantics=("arbitrary",), vmem_limit_bytes=VMEM_LIMIT),
        name="out_proj",
    )(y_lru, y_att, h0, w_out_bf16, g_ffn, w_rt, b_rt)


def _route_kernel(lg_ref, idx_ref, gate_ref, cnt_ref, carry_scr):
    @pl.when(pl.program_id(0) == 0)
    def _():
        carry_scr[...] = jnp.zeros_like(carry_scr)

    idx, gates, carry = _route_tile(lg_ref[...], carry_scr[...])
    idx_ref[...] = idx
    gate_ref[...] = gates
    carry_scr[...] = carry
    cnt_ref[...] = carry


def _route(logits):
    n_pad = logits.shape[0]
    row = pl.BlockSpec((TT, LANES), lambda m: (m, 0))
    return pl.pallas_call(
        _route_kernel,
        grid=(n_pad // TT,),
        in_specs=[row],
        out_specs=[row, row, pl.BlockSpec((1, LANES), lambda m: (0, 0))],
        out_shape=[
            jax.ShapeDtypeStruct((n_pad, LANES), jnp.int32),
            jax.ShapeDtypeStruct((n_pad, LANES), F32),
            jax.ShapeDtypeStruct((1, LANES), F32),
        ],
        scratch_shapes=[pltpu.VMEM((1, LANES), F32)],
        compiler_params=pltpu.CompilerParams(
            dimension_semantics=("arbitrary",), vmem_limit_bytes=VMEM_LIMIT),
        name="route",
    )(logits)


def _dest_kernel(idx_ref, ps_ref, o_ref):
    idx = idx_ref[...]
    lane = lax.broadcasted_iota(jnp.int32, idx.shape, 1)
    ps = ps_ref[...]
    pick = lambda e: jnp.sum(jnp.where(lane == e, ps, 0.0), axis=-1, keepdims=True).astype(jnp.int32)
    d0 = pick(idx[:, 0:1]) + idx[:, 2:3]
    d1 = pick(idx[:, 1:2]) + idx[:, 3:4]
    o_ref[...] = jnp.where(lane == 0, d0, jnp.where(lane == 1, d1, 0))


def _dest(idx, pstart_row):
    n_pad = idx.shape[0]
    row = pl.BlockSpec((TT, LANES), lambda m: (m, 0))
    return pl.pallas_call(
        _dest_kernel,
        grid=(n_pad // TT,),
        in_specs=[row, pl.BlockSpec((1, LANES), lambda m: (0, 0))],
        out_specs=row,
        out_shape=jax.ShapeDtypeStruct((n_pad, LANES), jnp.int32),
        compiler_params=pltpu.CompilerParams(
            dimension_semantics=("arbitrary",), vmem_limit_bytes=VMEM_LIMIT),
        name="dest",
    )(idx, pstart_row)


def _row_copy(src, s_row, dst, d_row, sem):
    return pltpu.make_async_copy(src.at[pl.ds(s_row, 1)], dst.at[pl.ds(d_row, 1)], sem)


def _dispatch_kernel(d0_ref, d1_ref, pe_ref, u_ref, xb_ref, zero_scr, sem, zsem):
    step = pl.program_id(0)
    base = step * TT

    def zero_block(e):
        start = pl.multiple_of(pe_ref[e] - DISPATCH_BLK, DISPATCH_BLK)
        return pltpu.make_async_copy(zero_scr, xb_ref.at[pl.ds(start, DISPATCH_BLK)], zsem)

    @pl.when(step == 0)
    def _():
        zero_scr[...] = jnp.zeros_like(zero_scr)

        def nonempty(e):
            return pe_ref[e] > jnp.where(e == 0, 0, pe_ref[jnp.maximum(e - 1, 0)])

        def zstart(e, carry):
            @pl.when(nonempty(e))
            def _():
                zero_block(e).start()
            return carry

        def zwait(e, carry):
            @pl.when(nonempty(e))
            def _():
                zero_block(e).wait()
            return carry

        lax.fori_loop(0, N_EXPERTS, zstart, 0)
        lax.fori_loop(0, N_EXPERTS, zwait, 0)

        def tail_block(i):
            start = pl.multiple_of(i * DISPATCH_BLK, DISPATCH_BLK)
            return pltpu.make_async_copy(zero_scr, xb_ref.at[pl.ds(start, DISPATCH_BLK)], zsem)

        def tstart(i, carry):
            tail_block(i).start()
            return carry

        def twait(i, carry):
            tail_block(i).wait()
            return carry

        n_used = pe_ref[N_EXPERTS - 1] // DISPATCH_BLK
        n_blocks = xb_ref.shape[0] // DISPATCH_BLK
        lax.fori_loop(n_used, n_blocks, tstart, 0)
        lax.fori_loop(n_used, n_blocks, twait, 0)

    def issue(r, carry):
        _row_copy(u_ref, r, xb_ref, d0_ref[base + r], sem.at[0]).start(priority=0)
        _row_copy(u_ref, r, xb_ref, d1_ref[base + r], sem.at[1]).start(priority=1)
        return carry

    for r in range(TT):
        issue(r, 0)
    for k in range(2):
        pltpu.make_async_copy(u_ref, xb_ref.at[pl.ds(0, TT)], sem.at[k]).wait()


def _dispatch(d0, d1, pends, u2, rows):
    n_pad, width = u2.shape
    return pl.pallas_call(
        _dispatch_kernel,
        grid_spec=pltpu.PrefetchScalarGridSpec(
            num_scalar_prefetch=3,
            grid=(n_pad // TT,),
            in_specs=[pl.BlockSpec((TT, width), lambda m, d0, d1, pe: (m, 0))],
            out_specs=pl.BlockSpec(memory_space=pl.ANY),
            scratch_shapes=[
                pltpu.VMEM((DISPATCH_BLK, width), u2.dtype),
                pltpu.SemaphoreType.DMA((2,)),
                pltpu.SemaphoreType.DMA(()),
            ],
        ),
        out_shape=jax.ShapeDtypeStruct((rows, width), u2.dtype),
        compiler_params=pltpu.CompilerParams(
            dimension_semantics=("arbitrary",), vmem_limit_bytes=VMEM_LIMIT),
        name="dispatch",
    )(d0, d1, pends, u2)


WEIGHT_UNITS = 8
WEIGHT_DEPTH = 4


def _stream_weights(i, tgt_ref, eseq_ref, ntot_ref, w_hbm, stages, wbfs, sem, done_scr):
    def slab(u, k):
        rows = stages[k].shape[1]
        return pl.ds(pl.multiple_of((u % WEIGHT_UNITS) * rows, rows), rows)

    def unit_copies(u):
        e = eseq_ref[u // WEIGHT_UNITS]
        b = u % WEIGHT_DEPTH
        return [pltpu.make_async_copy(w.at[e, slab(u, k), :], stages[k].at[b], sem.at[b, k])
                for k, w in enumerate(w_hbm)]

    @pl.when(i == 0)
    def _():
        done_scr[0] = 0
        for u in range(WEIGHT_DEPTH):
            for c in unit_copies(u):
                c.start()

    def body(u, carry):
        for c in unit_copies(u):
            c.wait()
        slot = (u // WEIGHT_UNITS) % 2
        b = u % WEIGHT_DEPTH
        for k in range(len(w_hbm)):
            wbfs[k][slot, slab(u, k), :] = stages[k][b].astype(BF16)

        @pl.when(u + WEIGHT_DEPTH < ntot_ref[0])
        def _():
            for c in unit_copies(u + WEIGHT_DEPTH):
                c.start()

        return carry

    lax.fori_loop(done_scr[0], tgt_ref[i], body, 0)
    done_scr[0] = tgt_ref[i]


def _expert_kernel(tgt_ref, eseq_ref, ntot_ref, slot_ref, nu_ref, x_ref, wg_hbm, wu_hbm, wd_hbm, y_ref,
                   sg, su, sd, bg, bu, bd, sem, done_scr):
    i = pl.program_id(0)
    _stream_weights(i, tgt_ref, eseq_ref, ntot_ref, (wg_hbm, wu_hbm, wd_hbm), (sg, su, sd), (bg, bu, bd),
                    sem, done_scr)
    used = i < nu_ref[0]

    @pl.when(used)
    def _():
        slot = slot_ref[i]
        x_lo, x_hi = _unpack_bf16_pairs(x_ref[...])
        half = D_MODEL // 2
        up = lambda w: (jnp.dot(x_lo, w[slot, 0:half, :], preferred_element_type=F32)
                        + jnp.dot(x_hi, w[slot, half:, :], preferred_element_type=F32))
        g = up(bg)
        u = up(bu)
        a = (jax.nn.silu(g) * u).astype(BF16)
        y_ref[...] = jnp.dot(a, bd[slot], preferred_element_type=F32)

    @pl.when(jnp.logical_not(used))
    def _():
        y_ref[...] = jnp.zeros_like(y_ref)


def _experts(sched, x_buf, w_gate, w_up, w_down):
    rows = x_buf.shape[0]
    n_blocks = rows // DISPATCH_BLK
    blk_map = lambda i, tg, es, nt, sl, nu: (jnp.minimum(i, jnp.maximum(nu[0] - 1, 0)), 0)
    out_map = lambda i, tg, es, nt, sl, nu: (i, 0)
    hbm = pl.BlockSpec(memory_space=pl.ANY)
    stage = lambda k_dim, n_dim: pltpu.VMEM((WEIGHT_DEPTH, k_dim // WEIGHT_UNITS, n_dim), F32)
    resident = lambda k_dim, n_dim: pltpu.VMEM((2, k_dim, n_dim), BF16)
    return pl.pallas_call(
        _expert_kernel,
        grid_spec=pltpu.PrefetchScalarGridSpec(
            num_scalar_prefetch=5,
            grid=(n_blocks,),
            in_specs=[pl.BlockSpec((DISPATCH_BLK, x_buf.shape[1]), blk_map), hbm, hbm, hbm],
            out_specs=pl.BlockSpec((DISPATCH_BLK, D_MODEL), out_map),
            scratch_shapes=[
                stage(D_MODEL, D_EXPERT), stage(D_MODEL, D_EXPERT), stage(D_EXPERT, D_MODEL),
                resident(D_MODEL, D_EXPERT), resident(D_MODEL, D_EXPERT), resident(D_EXPERT, D_MODEL),
                pltpu.SemaphoreType.DMA((WEIGHT_DEPTH, 3)),
                pltpu.SMEM((1,), jnp.int32),
            ],
        ),
        out_shape=jax.ShapeDtypeStruct((rows, D_MODEL), F32),
        compiler_params=pltpu.CompilerParams(
            dimension_semantics=("arbitrary",), vmem_limit_bytes=VMEM_LIMIT),
        name="experts",
    )(*sched, x_buf, w_gate, w_up, w_down)


def _expert_schedule(padded, pstarts, n_blocks):
    nonempty = padded > 0
    seq_of_expert = jnp.cumsum(nonempty.astype(jnp.int32)) - 1
    n_seq = jnp.sum(nonempty.astype(jnp.int32))
    experts = jnp.arange(N_EXPERTS, dtype=jnp.int32)
    is_kth = jnp.logical_and(nonempty[None, :], seq_of_expert[None, :] == experts[:, None])
    eseq = jnp.sum(jnp.where(is_kth, experts[None, :], 0), axis=1)
    pends = pstarts + padded
    n_used = pends[-1] // DISPATCH_BLK
    blk = jnp.arange(n_blocks, dtype=jnp.int32)
    blk_c = jnp.minimum(blk, jnp.maximum(n_used - 1, 0))
    blk_expert = jnp.minimum(
        jnp.sum((pends[None, :] <= (blk_c * DISPATCH_BLK)[:, None]).astype(jnp.int32), axis=1), N_EXPERTS - 1)
    onehot = (blk_expert[:, None] == experts[None, :]).astype(jnp.int32)
    pick = lambda v: jnp.sum(onehot * v[None, :], axis=1)
    q = pick(seq_of_expert)
    j = blk_c - pick(pstarts) // DISPATCH_BLK
    n = jnp.maximum(pick(padded) // DISPATCH_BLK, 1)
    n_total = n_seq * WEIGHT_UNITS
    tgt = jnp.minimum(WEIGHT_UNITS * (q + 1) + (WEIGHT_UNITS * j) // n, n_total)
    tgt = jnp.where(blk < n_used, tgt, n_total)
    i32 = lambda a: a.astype(jnp.int32)
    return i32(tgt), eseq, i32(n_total)[None], i32(q % 2), i32(n_used)[None]


COMBINE_TM = 512


def _combine_kernel(t_pad, d0_ref, d1_ref, h_hbm, gate_hbm, g_ref, yb_ref, o_ref,
                    h_scr, gate_scr, y0_scr, y1_scr, sem):
    n_j = pl.num_programs(1)
    step = pl.program_id(0) * n_j + pl.program_id(1)
    n_steps = pl.num_programs(0) * n_j

    def base_row(s):
        return pl.multiple_of((s // n_j) * t_pad + N_META + (s % n_j) * COMBINE_TM, 8)

    def tile_copies(s, slot):
        base = base_row(s)
        return [
            pltpu.make_async_copy(h_hbm.at[pl.ds(base, COMBINE_TM)], h_scr.at[slot], sem.at[slot, 2]),
            pltpu.make_async_copy(gate_hbm.at[pl.ds(base, COMBINE_TM)], gate_scr.at[slot], sem.at[slot, 3]),
        ]

    def fetch(s, slot):
        base = base_row(s)
        for c in tile_copies(s, slot):
            c.start()

        def issue(r, carry):
            _row_copy(yb_ref, d0_ref[base + r], y0_scr.at[slot], r, sem.at[slot, 0]).start(priority=0)
            _row_copy(yb_ref, d1_ref[base + r], y1_scr.at[slot], r, sem.at[slot, 1]).start(priority=1)
            return carry

        for r in range(COMBINE_TM):
            issue(r, 0)

    slot = step % 2

    @pl.when(step == 0)
    def _():
        fetch(step, slot)

    @pl.when(step + 1 < n_steps)
    def _():
        fetch(step + 1, 1 - slot)

    pltpu.make_async_copy(yb_ref.at[pl.ds(0, COMBINE_TM)], y0_scr.at[slot], sem.at[slot, 0]).wait()
    pltpu.make_async_copy(yb_ref.at[pl.ds(0, COMBINE_TM)], y1_scr.at[slot], sem.at[slot, 1]).wait()
    for c in tile_copies(step, slot):
        c.wait()

    gate = gate_scr[slot]
    h = h_scr[slot] + gate[:, 0:1] * y0_scr[slot] + gate[:, 1:2] * y1_scr[slot]
    o_ref[...] = _rmsnorm(h, g_ref[...])


def _combine(d0, d1, h1, gates, g_final, y_buf, batch, seq):
    assert seq % COMBINE_TM == 0 and N_META % 8 == 0
    tile = lambda w: pltpu.VMEM((2, COMBINE_TM, w), F32)
    return pl.pallas_call(
        functools.partial(_combine_kernel, h1.shape[0] // batch),
        grid_spec=pltpu.PrefetchScalarGridSpec(
            num_scalar_prefetch=2,
            grid=(batch, seq // COMBINE_TM),
            in_specs=[
                pl.BlockSpec(memory_space=pl.ANY),
                pl.BlockSpec(memory_space=pl.ANY),
                pl.BlockSpec((1, D_MODEL), lambda b, j, d0, d1: (0, 0)),
                pl.BlockSpec(memory_space=pl.ANY),
            ],
            out_specs=pl.BlockSpec((None, COMBINE_TM, D_MODEL), lambda b, j, d0, d1: (b, j, 0)),
            scratch_shapes=[tile(D_MODEL), tile(LANES), tile(D_MODEL), tile(D_MODEL),
                            pltpu.SemaphoreType.DMA((2, 4))],
        ),
        out_shape=jax.ShapeDtypeStruct((batch, seq, D_MODEL), F32),
        compiler_params=pltpu.CompilerParams(
            dimension_semantics=("arbitrary", "arbitrary"), vmem_limit_bytes=VMEM_LIMIT),
        name="combine",
    )(d0, d1, h1, gates, g_final, y_buf)


def _block_diag(w):
    per = LRU_CB // LRU_BLOCK_DIM
    w4 = w.reshape(LRU_BLOCKS // per, per, LRU_BLOCK_DIM, LRU_BLOCK_DIM)
    eye = jnp.eye(per, dtype=w.dtype)
    bd = jnp.einsum('cpij,pq->cpiqj', w4, eye)
    return bd.reshape(LRU_BLOCKS // per, LRU_CB, LRU_CB)


def kernel(x, meta_tokens, g_mix, w_in, conv_w, conv_b, w_rgate, b_rgate, w_igate, b_igate, lru_L, lambda_q1, lambda_k1, lambda_q2, lambda_k2, subln_g, rel_bias, w_out, g_ffn, w_group, b_group, w_router, b_router, w_gate, w_up, w_down, g_final):
    batch, seq, _ = x.shape
    t_real = N_META + seq
    t_pad = _round_up(t_real, TT)
    n_pad = batch * t_pad

    h0, proj_lru, proj_qkv = _in_proj(x, meta_tokens.astype(x.dtype), g_mix[0][None],
                                      w_in[0].astype(BF16), t_pad)

    y_lru = _lru(proj_lru, conv_w[0], conv_b[0][None],
                 _block_diag(w_rgate[0]).astype(BF16), b_rgate[0].reshape(1, D_LRU),
                 _block_diag(w_igate[0]).astype(BF16), b_igate[0].reshape(1, D_LRU),
                 lru_L[0][None], batch, t_pad)

    lam_params = jnp.stack([lambda_q1[0], lambda_k1[0], lambda_q2[0], lambda_k2[0]])
    y_att = _attention(proj_qkv, rel_bias, lam_params, subln_g[0][None], batch, t_pad)

    w_rt = jnp.concatenate([w_group[0], w_router[0],
                            jnp.zeros((D_MODEL, LANES - N_GROUPS - N_EXPERTS), F32)], axis=1)
    b_rt = jnp.concatenate([b_group[0], b_router[0],
                            jnp.zeros((LANES - N_GROUPS - N_EXPERTS,), F32)])[None]
    h1, u2, logits = _out_proj(y_lru, y_att, h0, w_out[0].astype(BF16), g_ffn[0][None], w_rt, b_rt)

    idx, gates, counts = _route(logits)

    cnt = counts[0, N_GROUPS:N_GROUPS + N_EXPERTS].astype(jnp.int32)
    padded = (cnt + DISPATCH_BLK - 1) // DISPATCH_BLK * DISPATCH_BLK
    pends = jnp.cumsum(padded)
    pstarts = pends - padded
    dest = _dest(idx, jnp.pad(pstarts.astype(F32), (0, LANES - N_EXPERTS))[None])
    d0 = dest[:, 0]
    d1 = dest[:, 1]
    n_blocks = -(-(2 * n_pad + N_EXPERTS * (DISPATCH_BLK - 1)) // DISPATCH_BLK)
    rows = n_blocks * DISPATCH_BLK
    sched = _expert_schedule(padded, pstarts, n_blocks)

    x_buf = _dispatch(d0, d1, pends.astype(jnp.int32), u2, rows)
    y_buf = _experts(sched, x_buf, w_gate[0], w_up[0], w_down[0])
    return _combine(d0, d1, h1, gates, g_final[None], y_buf, batch, seq)
```

```python
import functools
import math

import numpy as np
import jax
import jax.numpy as jnp
from jax import lax
from jax.experimental import pallas as pl
from jax.experimental.pallas import tpu as pltpu

D_MODEL = 2048
N_META = 16
D_LRU = 1024
D_ATT = 1024
LRU_BLOCKS = 16
LRU_BLOCK_DIM = 64
CONV_WIDTH = 4
LRU_C = 8.0
N_HEADS = 8
V_HEAD_DIM = 128
QK_HEAD_DIM = 64
N_BUCKETS = 32
MAX_DISTANCE = 128
N_GROUPS = 4
EXPERTS_PER_GROUP = 8
N_EXPERTS = 32
D_EXPERT = 1024
D_IN_PROJ = 2 * D_LRU + 3 * D_ATT
EPS = 1e-6
LAMBDA_INIT = 0.8 - 0.6 * math.exp(-0.3 * 0)
LOG2_E = math.log2(math.e)

F32 = jnp.float32
BF16 = jnp.bfloat16

LANES = 128
SUBLANES = 8
MXU_DIM = 256
TT = 768
PROJ_TN = 1024
OUT_TM = TT // 2
ATT_SUB = MXU_DIM
ATT_LOOKAHEAD = 4
ATT_GROUP = 4
VT_ROWS = V_HEAD_DIM + 16
LRU_CB = MXU_DIM
DISPATCH_BLK = 256
VMEM_LIMIT = 56 * 1024 * 1024
NEG_BIG = -1e30


def _round_up(a, b):
    return -(-a // b) * b


def _pack_bf16_pairs(x):
    n = x.shape[1] // 2
    bits = lambda v: pltpu.bitcast(v.astype(BF16).astype(F32), jnp.uint32)
    return lax.shift_right_logical(bits(x[:, :n]), jnp.uint32(16)) | bits(x[:, n:])


def _unpack_bf16_pairs(w):
    lo = pltpu.bitcast(lax.shift_left(w, jnp.uint32(16)), F32)
    hi = pltpu.bitcast(w & jnp.uint32(0xFFFF0000), F32)
    return lo.astype(BF16), hi.astype(BF16)


def _rmsnorm(x, g):
    ms = jnp.mean(x * x, axis=-1, keepdims=True)
    return x * lax.rsqrt(ms + EPS) * g


def _inproj_kernel(n_t, seq, x_hbm, meta_ref, g_ref, w_ref, h0_ref, lru_ref, qkv_ref, xbuf, u_scr, sem):
    m = pl.program_id(0)
    n = pl.program_id(1)
    n_m = pl.num_programs(0)
    last_rows = seq + N_META - (n_t - 1) * TT

    def tile_copy(mm, slot):
        b = mm // n_t
        t = mm % n_t
        row0 = pl.multiple_of(t * TT - N_META, SUBLANES)
        return [
            (t == 0, pltpu.make_async_copy(x_hbm.at[b, pl.ds(0, TT - N_META)],
                                           xbuf.at[slot, pl.ds(N_META, TT - N_META)], sem.at[slot])),
            (jnp.logical_and(t > 0, t < n_t - 1),
             pltpu.make_async_copy(x_hbm.at[b, pl.ds(row0, TT)], xbuf.at[slot], sem.at[slot])),
            (t == n_t - 1, pltpu.make_async_copy(x_hbm.at[b, pl.ds(row0, last_rows)],
                                                 xbuf.at[slot, pl.ds(0, last_rows)], sem.at[slot])),
        ]

    def start(mm, slot):
        for cond, copy in tile_copy(mm, slot):
            @pl.when(cond)
            def _(copy=copy):
                copy.start()

    def wait(mm, slot):
        for cond, copy in tile_copy(mm, slot):
            @pl.when(cond)
            def _(copy=copy):
                copy.wait()

    slot = m % 2

    @pl.when(jnp.logical_and(m == 0, n == 0))
    def _():
        start(m, slot)

    @pl.when(jnp.logical_and(n == 1, m + 1 < n_m))
    def _():
        start(m + 1, 1 - slot)

    @pl.when(n == 0)
    def _():
        wait(m, slot)
        t = m % n_t

        @pl.when(t == 0)
        def _():
            xbuf[slot, 0:N_META, :] = meta_ref[...]

        @pl.when(t == n_t - 1)
        def _():
            xbuf[slot, last_rows:TT, :] = jnp.zeros((TT - last_rows, D_MODEL), F32)

        h0 = xbuf[slot]
        h0_ref[...] = h0
        u_scr[...] = _rmsnorm(h0, g_ref[...]).astype(BF16)

    y = jnp.dot(u_scr[...], w_ref[...], preferred_element_type=F32)

    @pl.when(n < 2)
    def _():
        lru_ref[...] = y

    @pl.when(n >= 2)
    def _():
        qkv_ref[...] = y.astype(BF16)


def _in_proj(x, meta_tokens, g_mix, w_in_bf16, t_pad):
    batch, seq, _ = x.shape
    n_t = t_pad // TT
    n_pad = batch * t_pad
    n_col = D_IN_PROJ // PROJ_TN
    assert N_META % SUBLANES == 0 and 0 < seq + N_META - (n_t - 1) * TT <= TT
    return pl.pallas_call(
        functools.partial(_inproj_kernel, n_t, seq),
        grid=(n_pad // TT, n_col),
        in_specs=[
            pl.BlockSpec(memory_space=pl.ANY),
            pl.BlockSpec((N_META, D_MODEL), lambda m, n: (0, 0)),
            pl.BlockSpec((1, D_MODEL), lambda m, n: (0, 0)),
            pl.BlockSpec((D_MODEL, PROJ_TN), lambda m, n: (0, n)),
        ],
        out_specs=[
            pl.BlockSpec((TT, D_MODEL), lambda m, n: (m, 0)),
            pl.BlockSpec((TT, PROJ_TN), lambda m, n: (m, jnp.minimum(n, 1))),
            pl.BlockSpec((TT, PROJ_TN), lambda m, n: (m, jnp.maximum(n - 2, 0))),
        ],
        out_shape=[
            jax.ShapeDtypeStruct((n_pad, D_MODEL), F32),
            jax.ShapeDtypeStruct((n_pad, 2 * D_LRU), F32),
            jax.ShapeDtypeStruct((n_pad, 3 * D_ATT), BF16),
        ],
        scratch_shapes=[
            pltpu.VMEM((2, TT, D_MODEL), F32),
            pltpu.VMEM((TT, D_MODEL), BF16),
            pltpu.SemaphoreType.DMA((2,)),
        ],
        compiler_params=pltpu.CompilerParams(
            dimension_semantics=("arbitrary", "arbitrary"), vmem_limit_bytes=VMEM_LIMIT),
        name="in_proj",
    )(x, meta_tokens, g_mix, w_in_bf16)


def _lru_kernel(x_ref, gate_ref, cw_ref, cb_ref, wr_ref, br_ref, wi_ref, bi_ref, lam_ref,
                y_ref, xpad_scr, h_scr):
    t = pl.program_id(2)

    @pl.when(t == 0)
    def _():
        xpad_scr[0:SUBLANES, :] = jnp.zeros((SUBLANES, LRU_CB), F32)
        h_scr[...] = jnp.zeros_like(h_scr)

    x = x_ref[...]
    xpad_scr[SUBLANES:, :] = x
    cw = cw_ref[...]
    xc = cb_ref[...] + x * cw[CONV_WIDTH - 1:CONV_WIDTH]
    for k in range(CONV_WIDTH - 1):
        shift = CONV_WIDTH - 1 - k
        xc = xc + xpad_scr[pl.ds(SUBLANES - shift, TT), :] * cw[k:k + 1]
    xpad_scr[0:SUBLANES, :] = x[TT - SUBLANES:TT]

    xb = xc.astype(BF16)
    r = jax.nn.sigmoid(jnp.dot(xb, wr_ref[...], preferred_element_type=F32) + br_ref[...])
    i = jax.nn.sigmoid(jnp.dot(xb, wi_ref[...], preferred_element_type=F32) + bi_ref[...])
    lam = lam_ref[...]
    log_sig = jnp.minimum(lam, 0.0) - jnp.log(1.0 + jnp.exp(-jnp.abs(lam)))
    log_a = (LRU_C * r) * log_sig
    a = jnp.exp(log_a)
    one_m_a2 = 1.0 - a * a
    root = jnp.where(one_m_a2 > 0.0, one_m_a2 * lax.rsqrt(one_m_a2), 0.0)
    b = root * (i * xc)

    row = lax.broadcasted_iota(jnp.int32, (SUBLANES, LRU_CB), 0)
    carry = h_scr[...]
    groups = []
    for g in range(TT // SUBLANES):
        rows = slice(g * SUBLANES, (g + 1) * SUBLANES)
        ag, bg = a[rows], b[rows]
        s = 1
        while s < SUBLANES:
            keep = row >= s
            a_sh = jnp.where(keep, pltpu.roll(ag, s, 0), 1.0)
            b_sh = jnp.where(keep, pltpu.roll(bg, s, 0), 0.0)
            bg = ag * b_sh + bg
            ag = ag * a_sh
            s *= 2
        hg = bg + ag * carry
        groups.append(hg)
        carry = jnp.broadcast_to(hg[SUBLANES - 1:SUBLANES], hg.shape)
    h_scr[...] = carry
    h = jnp.concatenate(groups, axis=0)
    y_ref[...] = (h * jax.nn.gelu(gate_ref[...])).astype(BF16)


def _lru(proj_lru, conv_w, conv_b, wr_bd, b_r, wi_bd, b_i, lru_l, batch, t_pad):
    n_pad = proj_lru.shape[0]
    n_t = t_pad // TT
    n_c = D_LRU // LRU_CB
    vec = lambda: pl.BlockSpec((1, LRU_CB), lambda b, c, t: (0, c))
    return pl.pallas_call(
        _lru_kernel,
        grid=(batch, n_c, n_t),
        in_specs=[
            pl.BlockSpec((TT, LRU_CB), lambda b, c, t: (b * n_t + t, c)),
            pl.BlockSpec((TT, LRU_CB), lambda b, c, t: (b * n_t + t, n_c + c)),
            pl.BlockSpec((CONV_WIDTH, LRU_CB), lambda b, c, t: (0, c)),
            vec(),
            pl.BlockSpec((None, LRU_CB, LRU_CB), lambda b, c, t: (c, 0, 0)),
            vec(),
            pl.BlockSpec((None, LRU_CB, LRU_CB), lambda b, c, t: (c, 0, 0)),
            vec(),
            vec(),
        ],
        out_specs=pl.BlockSpec((TT, LRU_CB), lambda b, c, t: (b * n_t + t, c)),
        out_shape=jax.ShapeDtypeStruct((n_pad, D_LRU), BF16),
        scratch_shapes=[pltpu.VMEM((SUBLANES + TT, LRU_CB), F32), pltpu.VMEM((SUBLANES, LRU_CB), F32)],
        compiler_params=pltpu.CompilerParams(
            dimension_semantics=("arbitrary", "arbitrary", "arbitrary"),
            vmem_limit_bytes=VMEM_LIMIT),
        name="rglru",
    )(proj_lru, proj_lru, conv_w, conv_b, wr_bd, b_r, wi_bd, b_i, lru_l)


def _bucket_thresholds():
    max_exact = N_BUCKETS // 2
    n = np.arange(0, MAX_DISTANCE + 1)
    nf = np.maximum(n, 1).astype(np.float32)
    large = max_exact + (np.log(nf / np.float32(max_exact)) / np.float32(math.log(MAX_DISTANCE / max_exact))
                         * np.float32(N_BUCKETS - max_exact)).astype(np.int32)
    large = np.minimum(large, N_BUCKETS - 1)
    bucket = np.where(n < max_exact, n, large)
    return [int(np.argmax(bucket >= k)) for k in range(1, N_BUCKETS)]


_BUCKET_THR = _bucket_thresholds()


def _attn_kernel(rb_ref, q_ref, k_ref, v_ref, lam_ref, g_ref, o_ref, bias_scr, vt_scr, m_scr, acc_scr):
    h = pl.program_id(0)
    b = pl.program_id(1)
    qi = pl.program_id(2)
    tk = TT
    n_kv = vt_scr.shape[0]

    @pl.when(jnp.logical_and(b == 0, qi == 0))
    def _():
        k_pos = lax.broadcasted_iota(jnp.int32, (tk, tk), 0)
        q_pos = lax.broadcasted_iota(jnp.int32, (tk, tk), 1)
        far = rb_ref[N_BUCKETS - 1, h]
        for d in range(2):
            rel = q_pos - k_pos + d * tk
            val = jnp.full((tk, tk), rb_ref[0, h] - far, F32)
            for kk, thr in enumerate(_BUCKET_THR):
                val = jnp.where(rel >= thr, rb_ref[kk + 1, h] - far, val)
            val = val * LOG2_E
            if d == 0:
                val = jnp.where(rel >= 0, val, -jnp.inf)
            bias_scr[d] = val

    @pl.when(qi == 0)
    def _():
        for t in range(n_kv):
            vt_scr[t, 0:V_HEAD_DIM, :] = v_ref[t * tk:(t + 1) * tk, :].astype(F32).T.astype(BF16)
            vt_scr[t, V_HEAD_DIM:VT_ROWS, :] = jnp.ones((VT_ROWS - V_HEAD_DIM, tk), BF16)

    q = q_ref[...]
    lane = lax.broadcasted_iota(jnp.int32, q.shape, 1)
    scale = (QK_HEAD_DIM ** -0.5) * LOG2_E
    qf = q.astype(F32) * scale
    qs = jnp.concatenate([jnp.where(lane < QK_HEAD_DIM, qf, 0.0),
                          jnp.where(lane >= QK_HEAD_DIM, qf, 0.0)], axis=0).astype(BF16)

    m_scr[...] = jnp.full(m_scr.shape, NEG_BIG, F32)
    acc_scr[...] = jnp.zeros_like(acc_scr)

    def tile_group(tiles):
        n_sub = 2 * tk // ATT_SUB
        stages = [(t, c) for t in range(len(tiles)) for c in range(n_sub)]

        def n_keys(t, c):
            return (c * ATT_SUB) % tk + ATT_SUB if tiles[t][1] == 0 else tk

        def scores(t, c):
            j, bias_idx = tiles[t]
            nk = n_keys(t, c)
            kt = k_ref[pl.ds(pl.multiple_of(j * tk, tk), nk), :]
            cols = slice(c * ATT_SUB, (c + 1) * ATT_SUB)
            s = lax.dot_general(kt, qs[cols], (((1,), (1,)), ((), ())), preferred_element_type=F32)
            if bias_idx is not None:
                q0 = (c * ATT_SUB) % tk
                s = s + bias_scr[bias_idx, 0:nk, q0:q0 + ATT_SUB]
            return s

        def compact(t):
            return tiles[t][1] != 0

        def produce(t, c):
            s = scores(t, c)
            if not compact(t):
                return s
            cols = slice(c * ATT_SUB, (c + 1) * ATT_SUB)
            m_prev = m_scr[:, cols]
            m_new = jnp.maximum(m_prev, jnp.max(s, axis=0, keepdims=True))
            m_scr[:, cols] = m_new
            return (s - m_prev).astype(BF16), m_prev, m_new

        pending = [produce(*st) for st in stages[:ATT_LOOKAHEAD]]
        for n, (t, c) in enumerate(stages):
            cols = slice(c * ATT_SUB, (c + 1) * ATT_SUB)
            item = pending.pop(0)
            if n + ATT_LOOKAHEAD < len(stages):
                pending.append(produce(*stages[n + ATT_LOOKAHEAD]))
            if compact(t):
                d, m_prev, m_new = item
                p = jnp.exp2(d - (m_new - m_prev).astype(BF16))
            else:
                m_prev = m_scr[:, cols]
                m_new = jnp.maximum(m_prev, jnp.max(item, axis=0, keepdims=True))
                m_scr[:, cols] = m_new
                p = jnp.exp2(item - m_new).astype(BF16)
            alpha = jnp.exp2(m_prev - m_new)
            vt = vt_scr[tiles[t][0], :, 0:n_keys(t, c)]
            acc_scr[:, cols] = alpha * acc_scr[:, cols] + jnp.dot(vt, p, preferred_element_type=F32)

    @pl.when(qi >= 1)
    def _():
        tile_group([(qi, 0), (qi - 1, 1)])

    @pl.when(qi == 0)
    def _():
        tile_group([(qi, 0)])

    n_far = jnp.maximum(qi - 1, 0)

    def far_group(i, carry):
        tile_group([(ATT_GROUP * i + t, None) for t in range(ATT_GROUP)])
        return carry

    lax.fori_loop(0, n_far // ATT_GROUP, far_group, 0)
    done = n_far - n_far % ATT_GROUP
    piece = ATT_GROUP // 2
    while piece >= 1:
        @pl.when((n_far % (2 * piece)) >= piece)
        def _(done=done, piece=piece):
            tile_group([(done + t, None) for t in range(piece)])
        done = done + jnp.where((n_far % (2 * piece)) >= piece, piece, 0)
        piece //= 2

    lam_p = lam_ref[...]
    lam = (jnp.exp(jnp.sum(lam_p[0:1] * lam_p[1:2], axis=-1, keepdims=True))
           - jnp.exp(jnp.sum(lam_p[2:3] * lam_p[3:4], axis=-1, keepdims=True)) + LAMBDA_INIT)
    inv = 1.0 / acc_scr[V_HEAD_DIM:V_HEAD_DIM + 1, :]
    o_t = (acc_scr[0:V_HEAD_DIM, 0:tk] * inv[:, :tk]
           - acc_scr[0:V_HEAD_DIM, tk:] * (lam * inv[:, tk:]))
    ms = jnp.mean(o_t * o_t, axis=0, keepdims=True)
    o_n = (o_t * lax.rsqrt(ms + EPS)).T
    o_ref[...] = (o_n * g_ref[...] * (1.0 - LAMBDA_INIT)).astype(BF16)


def _attention(qkv, rel_bias, lam_params, subln_g, batch, t_pad):
    n_q = t_pad // TT
    qkv3 = qkv.reshape(batch, t_pad, 3 * D_ATT)
    out = pl.pallas_call(
        _attn_kernel,
        grid_spec=pltpu.PrefetchScalarGridSpec(
            num_scalar_prefetch=0,
            grid=(N_HEADS, batch, n_q),
            in_specs=[
                pl.BlockSpec(memory_space=pltpu.SMEM),
                pl.BlockSpec((None, TT, V_HEAD_DIM), lambda h, b, q: (b, q, h)),
                pl.BlockSpec((None, t_pad, V_HEAD_DIM), lambda h, b, q: (b, 0, N_HEADS + h)),
                pl.BlockSpec((None, t_pad, V_HEAD_DIM), lambda h, b, q: (b, 0, 2 * N_HEADS + h)),
                pl.BlockSpec((4, QK_HEAD_DIM), lambda h, b, q: (0, 0)),
                pl.BlockSpec((1, V_HEAD_DIM), lambda h, b, q: (0, 0)),
            ],
            out_specs=pl.BlockSpec((None, TT, V_HEAD_DIM), lambda h, b, q: (b, q, h)),
            scratch_shapes=[
                pltpu.VMEM((2, TT, TT), F32),
                pltpu.VMEM((n_q, VT_ROWS, TT), BF16),
                pltpu.VMEM((1, 2 * TT), F32),
                pltpu.VMEM((VT_ROWS, 2 * TT), F32),
            ],
        ),
        out_shape=jax.ShapeDtypeStruct((batch, t_pad, D_ATT), BF16),
        compiler_params=pltpu.CompilerParams(
            dimension_semantics=("arbitrary", "arbitrary", "arbitrary"),
            vmem_limit_bytes=VMEM_LIMIT),
        name="diff_attention",
    )(rel_bias, qkv3, qkv3, qkv3, lam_params, subln_g)
    return out.reshape(batch * t_pad, D_ATT)


def _route_tile(lg, carry):
    rows = lg.shape[0]
    lane = lax.broadcasted_iota(jnp.int32, lg.shape, 1)
    first = lambda mask: jnp.min(jnp.where(mask, lane, LANES), axis=-1, keepdims=True)

    is_g = lane < N_GROUPS
    gl = jnp.where(is_g, lg, -jnp.inf)
    gmax = jnp.max(gl, axis=-1, keepdims=True)
    grp = first(gl == gmax)
    gsum = jnp.sum(jnp.where(is_g, jnp.exp(gl - gmax), 0.0), axis=-1, keepdims=True)
    p_grp = 1.0 / gsum

    lane_e = lane - N_GROUPS
    in_grp = (lane_e >= 0) & (lane_e < N_EXPERTS) & ((lane_e // EXPERTS_PER_GROUP) == grp)
    el = jnp.where(in_grp, lg, -jnp.inf)
    v1 = jnp.max(el, axis=-1, keepdims=True)
    i1 = first(el == v1)
    el2 = jnp.where(lane == i1, -jnp.inf, el)
    v2 = jnp.max(el2, axis=-1, keepdims=True)
    i2 = first(el2 == v2)
    e2 = jnp.exp(v2 - v1)
    den = 1.0 + e2
    g0 = p_grp * (1.0 / den)
    g1 = p_grp * (e2 / den)

    hit1 = lane == i1
    hit2 = lane == i2
    onehot = jnp.where(hit1 | hit2, 1.0, 0.0)
    rr = lax.broadcasted_iota(jnp.int32, (rows, rows), 0)
    cc = lax.broadcasted_iota(jnp.int32, (rows, rows), 1)
    tri = jnp.where(cc < rr, 1.0, 0.0).astype(BF16)
    rank = jnp.dot(tri, onehot.astype(BF16), preferred_element_type=F32) + carry
    r0 = jnp.sum(jnp.where(hit1, rank, 0.0), axis=-1, keepdims=True).astype(jnp.int32)
    r1 = jnp.sum(jnp.where(hit2, rank, 0.0), axis=-1, keepdims=True).astype(jnp.int32)

    idx = jnp.where(lane == 0, i1 - N_GROUPS,
                    jnp.where(lane == 1, i2 - N_GROUPS,
                              jnp.where(lane == 2, r0, jnp.where(lane == 3, r1, 0))))
    gates = jnp.where(lane == 0, g0, jnp.where(lane == 1, g1, 0.0))
    return idx, gates, carry + jnp.sum(onehot, axis=0, keepdims=True)


def _outproj_kernel(yl_ref, ya_ref, h0_ref, w_ref, g_ref, wrt_ref, brt_ref, h1_ref, u_ref, lg_ref):
    acc = jnp.dot(yl_ref[...], w_ref[0:D_LRU, :], preferred_element_type=F32)
    acc = acc + jnp.dot(ya_ref[...], w_ref[D_LRU:, :], preferred_element_type=F32)
    h1 = h0_ref[...] + acc
    h1_ref[...] = h1
    u = _rmsnorm(h1, g_ref[...])
    u_ref[...] = _pack_bf16_pairs(u)
    u_hi = u.astype(BF16)
    u_lo = (u - u_hi.astype(F32)).astype(BF16)
    w = wrt_ref[...]
    w_hi = w.astype(BF16)
    w_lo = (w - w_hi.astype(F32)).astype(BF16)
    parts = jnp.dot(jnp.concatenate([u_hi, u_lo], axis=0), jnp.concatenate([w_hi, w_lo], axis=1),
                    preferred_element_type=F32)
    lg = (parts[:OUT_TM, :LANES] + parts[:OUT_TM, LANES:]) + (parts[OUT_TM:, :LANES] + parts[OUT_TM:, LANES:])
    lg_ref[...] = lg + brt_ref[...]


def _out_proj(y_lru, y_att, h0, w_out_bf16, g_ffn, w_rt, b_rt):
    n_pad = h0.shape[0]
    row = lambda w: pl.BlockSpec((OUT_TM, w), lambda m: (m, 0))
    full = lambda a, b: pl.BlockSpec((a, b), lambda m: (0, 0))
    return pl.pallas_call(
        _outproj_kernel,
        grid=(n_pad // OUT_TM,),
        in_specs=[row(D_LRU), row(D_ATT), row(D_MODEL), full(D_MODEL, D_MODEL), full(1, D_MODEL),
                  full(D_MODEL, LANES), full(1, LANES)],
        out_specs=[row(D_MODEL), row(D_MODEL // 2), row(LANES)],
        out_shape=[
            jax.ShapeDtypeStruct((n_pad, D_MODEL), F32),
            jax.ShapeDtypeStruct((n_pad, D_MODEL // 2), jnp.uint32),
            jax.ShapeDtypeStruct((n_pad, LANES), F32),
        ],
        compiler_params=pltpu.CompilerParams(
            dimension_semantics=("arbitrary",), vmem_limit_bytes=VMEM_LIMIT),
        name="out_proj",
    )(y_lru, y_att, h0, w_out_bf16, g_ffn, w_rt, b_rt)


def _route_kernel(lg_ref, idx_ref, gate_ref, cnt_ref, carry_scr):
    @pl.when(pl.program_id(0) == 0)
    def _():
        carry_scr[...] = jnp.zeros_like(carry_scr)

    idx, gates, carry = _route_tile(lg_ref[...], carry_scr[...])
    idx_ref[...] = idx
    gate_ref[...] = gates
    carry_scr[...] = carry
    cnt_ref[...] = carry


def _route(logits):
    n_pad = logits.shape[0]
    row = pl.BlockSpec((TT, LANES), lambda m: (m, 0))
    return pl.pallas_call(
        _route_kernel,
        grid=(n_pad // TT,),
        in_specs=[row],
        out_specs=[row, row, pl.BlockSpec((1, LANES), lambda m: (0, 0))],
        out_shape=[
            jax.ShapeDtypeStruct((n_pad, LANES), jnp.int32),
            jax.ShapeDtypeStruct((n_pad, LANES), F32),
            jax.ShapeDtypeStruct((1, LANES), F32),
        ],
        scratch_shapes=[pltpu.VMEM((1, LANES), F32)],
        compiler_params=pltpu.CompilerParams(
            dimension_semantics=("arbitrary",), vmem_limit_bytes=VMEM_LIMIT),
        name="route",
    )(logits)


def _dest_kernel(idx_ref, ps_ref, o_ref):
    idx = idx_ref[...]
    lane = lax.broadcasted_iota(jnp.int32, idx.shape, 1)
    ps = ps_ref[...]
    pick = lambda e: jnp.sum(jnp.where(lane == e, ps, 0.0), axis=-1, keepdims=True).astype(jnp.int32)
    d0 = pick(idx[:, 0:1]) + idx[:, 2:3]
    d1 = pick(idx[:, 1:2]) + idx[:, 3:4]
    o_ref[...] = jnp.where(lane == 0, d0, jnp.where(lane == 1, d1, 0))


def _dest(idx, pstart_row):
    n_pad = idx.shape[0]
    row = pl.BlockSpec((TT, LANES), lambda m: (m, 0))
    return pl.pallas_call(
        _dest_kernel,
        grid=(n_pad // TT,),
        in_specs=[row, pl.BlockSpec((1, LANES), lambda m: (0, 0))],
        out_specs=row,
        out_shape=jax.ShapeDtypeStruct((n_pad, LANES), jnp.int32),
        compiler_params=pltpu.CompilerParams(
            dimension_semantics=("arbitrary",), vmem_limit_bytes=VMEM_LIMIT),
        name="dest",
    )(idx, pstart_row)


def _row_copy(src, s_row, dst, d_row, sem):
    return pltpu.make_async_copy(src.at[pl.ds(s_row, 1)], dst.at[pl.ds(d_row, 1)], sem)


def _dispatch_kernel(d0_ref, d1_ref, pe_ref, u_ref, xb_ref, zero_scr, sem, zsem):
    step = pl.program_id(0)
    base = step * TT

    def zero_block(e):
        start = pl.multiple_of(pe_ref[e] - DISPATCH_BLK, DISPATCH_BLK)
        return pltpu.make_async_copy(zero_scr, xb_ref.at[pl.ds(start, DISPATCH_BLK)], zsem)

    @pl.when(step == 0)
    def _():
        zero_scr[...] = jnp.zeros_like(zero_scr)

        def nonempty(e):
            return pe_ref[e] > jnp.where(e == 0, 0, pe_ref[jnp.maximum(e - 1, 0)])

        def zstart(e, carry):
            @pl.when(nonempty(e))
            def _():
                zero_block(e).start()
            return carry

        def zwait(e, carry):
            @pl.when(nonempty(e))
            def _():
                zero_block(e).wait()
            return carry

        lax.fori_loop(0, N_EXPERTS, zstart, 0)
        lax.fori_loop(0, N_EXPERTS, zwait, 0)

        def tail_block(i):
            start = pl.multiple_of(i * DISPATCH_BLK, DISPATCH_BLK)
            return pltpu.make_async_copy(zero_scr, xb_ref.at[pl.ds(start, DISPATCH_BLK)], zsem)

        def tstart(i, carry):
            tail_block(i).start()
            return carry

        def twait(i, carry):
            tail_block(i).wait()
            return carry

        n_used = pe_ref[N_EXPERTS - 1] // DISPATCH_BLK
        n_blocks = xb_ref.shape[0] // DISPATCH_BLK
        lax.fori_loop(n_used, n_blocks, tstart, 0)
        lax.fori_loop(n_used, n_blocks, twait, 0)

    def issue(r, carry):
        _row_copy(u_ref, r, xb_ref, d0_ref[base + r], sem.at[0]).start(priority=0)
        _row_copy(u_ref, r, xb_ref, d1_ref[base + r], sem.at[1]).start(priority=1)
        return carry

    for r in range(TT):
        issue(r, 0)
    for k in range(2):
        pltpu.make_async_copy(u_ref, xb_ref.at[pl.ds(0, TT)], sem.at[k]).wait()


def _dispatch(d0, d1, pends, u2, rows):
    n_pad, width = u2.shape
    return pl.pallas_call(
        _dispatch_kernel,
        grid_spec=pltpu.PrefetchScalarGridSpec(
            num_scalar_prefetch=3,
            grid=(n_pad // TT,),
            in_specs=[pl.BlockSpec((TT, width), lambda m, d0, d1, pe: (m, 0))],
            out_specs=pl.BlockSpec(memory_space=pl.ANY),
            scratch_shapes=[
                pltpu.VMEM((DISPATCH_BLK, width), u2.dtype),
                pltpu.SemaphoreType.DMA((2,)),
                pltpu.SemaphoreType.DMA(()),
            ],
        ),
        out_shape=jax.ShapeDtypeStruct((rows, width), u2.dtype),
        compiler_params=pltpu.CompilerParams(
            dimension_semantics=("arbitrary",), vmem_limit_bytes=VMEM_LIMIT),
        name="dispatch",
    )(d0, d1, pends, u2)


WEIGHT_UNITS = 8
WEIGHT_DEPTH = 4


def _stream_weights(i, tgt_ref, eseq_ref, ntot_ref, w_hbm, stages, wbfs, sem, done_scr):
    def slab(u, k):
        rows = stages[k].shape[1]
        return pl.ds(pl.multiple_of((u % WEIGHT_UNITS) * rows, rows), rows)

    def unit_copies(u):
        e = eseq_ref[u // WEIGHT_UNITS]
        b = u % WEIGHT_DEPTH
        return [pltpu.make_async_copy(w.at[e, slab(u, k), :], stages[k].at[b], sem.at[b, k])
                for k, w in enumerate(w_hbm)]

    @pl.when(i == 0)
    def _():
        done_scr[0] = 0
        for u in range(WEIGHT_DEPTH):
            for c in unit_copies(u):
                c.start()

    def body(u, carry):
        for c in unit_copies(u):
            c.wait()
        slot = (u // WEIGHT_UNITS) % 2
        b = u % WEIGHT_DEPTH
        for k in range(len(w_hbm)):
            wbfs[k][slot, slab(u, k), :] = stages[k][b].astype(BF16)

        @pl.when(u + WEIGHT_DEPTH < ntot_ref[0])
        def _():
            for c in unit_copies(u + WEIGHT_DEPTH):
                c.start()

        return carry

    lax.fori_loop(done_scr[0], tgt_ref[i], body, 0)
    done_scr[0] = tgt_ref[i]


def _expert_kernel(tgt_ref, eseq_ref, ntot_ref, slot_ref, nu_ref, x_ref, wg_hbm, wu_hbm, wd_hbm, y_ref,
                   sg, su, sd, bg, bu, bd, sem, done_scr):
    i = pl.program_id(0)
    _stream_weights(i, tgt_ref, eseq_ref, ntot_ref, (wg_hbm, wu_hbm, wd_hbm), (sg, su, sd), (bg, bu, bd),
                    sem, done_scr)
    used = i < nu_ref[0]

    @pl.when(used)
    def _():
        slot = slot_ref[i]
        x_lo, x_hi = _unpack_bf16_pairs(x_ref[...])
        half = D_MODEL // 2
        up = lambda w: (jnp.dot(x_lo, w[slot, 0:half, :], preferred_element_type=F32)
                        + jnp.dot(x_hi, w[slot, half:, :], preferred_element_type=F32))
        g = up(bg)
        u = up(bu)
        a = (jax.nn.silu(g) * u).astype(BF16)
        y_ref[...] = jnp.dot(a, bd[slot], preferred_element_type=F32)

    @pl.when(jnp.logical_not(used))
    def _():
        y_ref[...] = jnp.zeros_like(y_ref)


def _experts(sched, x_buf, w_gate, w_up, w_down):
    rows = x_buf.shape[0]
    n_blocks = rows // DISPATCH_BLK
    blk_map = lambda i, tg, es, nt, sl, nu: (jnp.minimum(i, jnp.maximum(nu[0] - 1, 0)), 0)
    out_map = lambda i, tg, es, nt, sl, nu: (i, 0)
    hbm = pl.BlockSpec(memory_space=pl.ANY)
    stage = lambda k_dim, n_dim: pltpu.VMEM((WEIGHT_DEPTH, k_dim // WEIGHT_UNITS, n_dim), F32)
    resident = lambda k_dim, n_dim: pltpu.VMEM((2, k_dim, n_dim), BF16)
    return pl.pallas_call(
        _expert_kernel,
        grid_spec=pltpu.PrefetchScalarGridSpec(
            num_scalar_prefetch=5,
            grid=(n_blocks,),
            in_specs=[pl.BlockSpec((DISPATCH_BLK, x_buf.shape[1]), blk_map), hbm, hbm, hbm],
            out_specs=pl.BlockSpec((DISPATCH_BLK, D_MODEL), out_map),
            scratch_shapes=[
                stage(D_MODEL, D_EXPERT), stage(D_MODEL, D_EXPERT), stage(D_EXPERT, D_MODEL),
                resident(D_MODEL, D_EXPERT), resident(D_MODEL, D_EXPERT), resident(D_EXPERT, D_MODEL),
                pltpu.SemaphoreType.DMA((WEIGHT_DEPTH, 3)),
                pltpu.SMEM((1,), jnp.int32),
            ],
        ),
        out_shape=jax.ShapeDtypeStruct((rows, D_MODEL), F32),
        compiler_params=pltpu.CompilerParams(
            dimension_semantics=("arbitrary",), vmem_limit_bytes=VMEM_LIMIT),
        name="experts",
    )(*sched, x_buf, w_gate, w_up, w_down)


def _expert_schedule(padded, pstarts, n_blocks):
    nonempty = padded > 0
    seq_of_expert = jnp.cumsum(nonempty.astype(jnp.int32)) - 1
    n_seq = jnp.sum(nonempty.astype(jnp.int32))
    experts = jnp.arange(N_EXPERTS, dtype=jnp.int32)
    is_kth = jnp.logical_and(nonempty[None, :], seq_of_expert[None, :] == experts[:, None])
    eseq = jnp.sum(jnp.where(is_kth, experts[None, :], 0), axis=1)
    pends = pstarts + padded
    n_used = pends[-1] // DISPATCH_BLK
    blk = jnp.arange(n_blocks, dtype=jnp.int32)
    blk_c = jnp.minimum(blk, jnp.maximum(n_used - 1, 0))
    blk_expert = jnp.minimum(
        jnp.sum((pends[None, :] <= (blk_c * DISPATCH_BLK)[:, None]).astype(jnp.int32), axis=1), N_EXPERTS - 1)
    onehot = (blk_expert[:, None] == experts[None, :]).astype(jnp.int32)
    pick = lambda v: jnp.sum(onehot * v[None, :], axis=1)
    q = pick(seq_of_expert)
    j = blk_c - pick(pstarts) // DISPATCH_BLK
    n = jnp.maximum(pick(padded) // DISPATCH_BLK, 1)
    n_total = n_seq * WEIGHT_UNITS
    tgt = jnp.minimum(WEIGHT_UNITS * (q + 1) + (WEIGHT_UNITS * j) // n, n_total)
    tgt = jnp.where(blk < n_used, tgt, n_total)
    i32 = lambda a: a.astype(jnp.int32)
    return i32(tgt), eseq, i32(n_total)[None], i32(q % 2), i32(n_used)[None]


COMBINE_TM = 512


def _combine_kernel(t_pad, d0_ref, d1_ref, h_hbm, gate_hbm, g_ref, yb_ref, o_ref,
                    h_scr, gate_scr, y0_scr, y1_scr, sem):
    n_j = pl.num_programs(1)
    step = pl.program_id(0) * n_j + pl.program_id(1)
    n_steps = pl.num_programs(0) * n_j

    def base_row(s):
        return pl.multiple_of((s // n_j) * t_pad + N_META + (s % n_j) * COMBINE_TM, 8)

    def tile_copies(s, slot):
        base = base_row(s)
        return [
            pltpu.make_async_copy(h_hbm.at[pl.ds(base, COMBINE_TM)], h_scr.at[slot], sem.at[slot, 2]),
            pltpu.make_async_copy(gate_hbm.at[pl.ds(base, COMBINE_TM)], gate_scr.at[slot], sem.at[slot, 3]),
        ]

    def fetch(s, slot):
        base = base_row(s)
        for c in tile_copies(s, slot):
            c.start(priority=1)

        def issue(r, carry):
            _row_copy(yb_ref, d0_ref[base + r], y0_scr.at[slot], r, sem.at[slot, 0]).start(priority=0)
            _row_copy(yb_ref, d1_ref[base + r], y1_scr.at[slot], r, sem.at[slot, 1]).start(priority=0)
            return carry

        for r in range(COMBINE_TM):
            issue(r, 0)

    slot = step % 2

    @pl.when(step == 0)
    def _():
        fetch(step, slot)

    @pl.when(step + 1 < n_steps)
    def _():
        fetch(step + 1, 1 - slot)

    pltpu.make_async_copy(yb_ref.at[pl.ds(0, COMBINE_TM)], y0_scr.at[slot], sem.at[slot, 0]).wait()
    pltpu.make_async_copy(yb_ref.at[pl.ds(0, COMBINE_TM)], y1_scr.at[slot], sem.at[slot, 1]).wait()
    for c in tile_copies(step, slot):
        c.wait()

    gate = gate_scr[slot]
    h = h_scr[slot] + gate[:, 0:1] * y0_scr[slot] + gate[:, 1:2] * y1_scr[slot]
    o_ref[...] = _rmsnorm(h, g_ref[...])


def _combine(d0, d1, h1, gates, g_final, y_buf, batch, seq):
    assert seq % COMBINE_TM == 0 and N_META % 8 == 0
    tile = lambda w: pltpu.VMEM((2, COMBINE_TM, w), F32)
    return pl.pallas_call(
        functools.partial(_combine_kernel, h1.shape[0] // batch),
        grid_spec=pltpu.PrefetchScalarGridSpec(
            num_scalar_prefetch=2,
            grid=(batch, seq // COMBINE_TM),
            in_specs=[
                pl.BlockSpec(memory_space=pl.ANY),
                pl.BlockSpec(memory_space=pl.ANY),
                pl.BlockSpec((1, D_MODEL), lambda b, j, d0, d1: (0, 0)),
                pl.BlockSpec(memory_space=pl.ANY),
            ],
            out_specs=pl.BlockSpec((None, COMBINE_TM, D_MODEL), lambda b, j, d0, d1: (b, j, 0)),
            scratch_shapes=[tile(D_MODEL), tile(LANES), tile(D_MODEL), tile(D_MODEL),
                            pltpu.SemaphoreType.DMA((2, 4))],
        ),
        out_shape=jax.ShapeDtypeStruct((batch, seq, D_MODEL), F32),
        compiler_params=pltpu.CompilerParams(
            dimension_semantics=("arbitrary", "arbitrary"), vmem_limit_bytes=VMEM_LIMIT),
        name="combine",
    )(d0, d1, h1, gates, g_final, y_buf)


def _block_diag(w):
    per = LRU_CB // LRU_BLOCK_DIM
    w4 = w.reshape(LRU_BLOCKS // per, per, LRU_BLOCK_DIM, LRU_BLOCK_DIM)
    eye = jnp.eye(per, dtype=w.dtype)
    bd = jnp.einsum('cpij,pq->cpiqj', w4, eye)
    return bd.reshape(LRU_BLOCKS // per, LRU_CB, LRU_CB)


def kernel(x, meta_tokens, g_mix, w_in, conv_w, conv_b, w_rgate, b_rgate, w_igate, b_igate, lru_L, lambda_q1, lambda_k1, lambda_q2, lambda_k2, subln_g, rel_bias, w_out, g_ffn, w_group, b_group, w_router, b_router, w_gate, w_up, w_down, g_final):
    batch, seq, _ = x.shape
    t_real = N_META + seq
    t_pad = _round_up(t_real, TT)
    n_pad = batch * t_pad

    h0, proj_lru, proj_qkv = _in_proj(x, meta_tokens.astype(x.dtype), g_mix[0][None],
                                      w_in[0].astype(BF16), t_pad)

    y_lru = _lru(proj_lru, conv_w[0], conv_b[0][None],
                 _block_diag(w_rgate[0]).astype(BF16), b_rgate[0].reshape(1, D_LRU),
                 _block_diag(w_igate[0]).astype(BF16), b_igate[0].reshape(1, D_LRU),
                 lru_L[0][None], batch, t_pad)

    lam_params = jnp.stack([lambda_q1[0], lambda_k1[0], lambda_q2[0], lambda_k2[0]])
    y_att = _attention(proj_qkv, rel_bias, lam_params, subln_g[0][None], batch, t_pad)

    w_rt = jnp.concatenate([w_group[0], w_router[0],
                            jnp.zeros((D_MODEL, LANES - N_GROUPS - N_EXPERTS), F32)], axis=1)
    b_rt = jnp.concatenate([b_group[0], b_router[0],
                            jnp.zeros((LANES - N_GROUPS - N_EXPERTS,), F32)])[None]
    h1, u2, logits = _out_proj(y_lru, y_att, h0, w_out[0].astype(BF16), g_ffn[0][None], w_rt, b_rt)

    idx, gates, counts = _route(logits)

    cnt = counts[0, N_GROUPS:N_GROUPS + N_EXPERTS].astype(jnp.int32)
    padded = (cnt + DISPATCH_BLK - 1) // DISPATCH_BLK * DISPATCH_BLK
    pends = jnp.cumsum(padded)
    pstarts = pends - padded
    dest = _dest(idx, jnp.pad(pstarts.astype(F32), (0, LANES - N_EXPERTS))[None])
    d0 = dest[:, 0]
    d1 = dest[:, 1]
    n_blocks = -(-(2 * n_pad + N_EXPERTS * (DISPATCH_BLK - 1)) // DISPATCH_BLK)
    rows = n_blocks * DISPATCH_BLK
    sched = _expert_schedule(padded, pstarts, n_blocks)

    x_buf = _dispatch(d0, d1, pends.astype(jnp.int32), u2, rows)
    y_buf = _experts(sched, x_buf, w_gate[0], w_up[0], w_down[0])
    return _combine(d0, d1, h1, gates, g_final[None], y_buf, batch, seq)
```
